```python
import math
import jax, jax.numpy as jnp
from jax import lax
import numpy as np

D_MODEL = 2048
BATCH = 16
SEQ = 256
DEPTH = 1
DEC_BATCH = 4
DEC_SEQ = 2048
PAST_LEN = 512

GRID_W = 64
D_MIX = D_MODEL
D_HY = D_MIX // 2
HY_ORDER = 2
HY_SHORT = 3
HY_BANDS = 16
HY_EMB_DIM = 1 + 2 * HY_BANDS
HY_FILTER_HIDDEN = 64
HY_SIN_FREQ = 1.0
HY_WINDOW_SHIFT = 0.05
HY_DECAY_MIN = 3.07
HY_DECAY_MAX = 15.35
D_GDN = D_MIX - D_HY
GDN_HEADS = 8
GDN_DK = D_GDN // GDN_HEADS
GDN_DV = D_GDN // GDN_HEADS
GDN_SHORT = 3
GDN_CHUNK = 64
W_IN_COLS = 3 * D_HY + 4 * D_GDN + 4 * GDN_HEADS
D_FF = 5632
N_MOD = 9
RMS_EPS = 1e-6
POS_BASE = 10000.0

kernel_name = 'hyena_gdn_macaron_prefix_flow_step'


def rms_norm(x, g):
    xf = x.astype(jnp.float32)
    y = xf * lax.rsqrt(jnp.mean(xf * xf, axis=-1, keepdims=True) + RMS_EPS)
    return (y * g.astype(jnp.float32)).astype(x.dtype)


def l2_norm(x):
    return x * lax.rsqrt(jnp.sum(x * x, axis=-1, keepdims=True) + RMS_EPS)


def adaln(x, g, shift, scale):
    return rms_norm(x, g) * (1.0 + scale) + shift


def swiglu(h, wg, wu, wd):
    return (jax.nn.silu(h @ wg) * (h @ wu)) @ wd


def dw_conv_centred(u, w):
    k = w.shape[0]
    p = k // 2
    n = u.shape[1]
    up = jnp.pad(u, ((0, 0), (p, p), (0, 0)))
    out = up[:, 0:n, :] * w[0]
    for j in range(1, k):
        out = out + up[:, j:j + n, :] * w[j]
    return out


def grid_pos_embed(n_tok, dim):
    rows = n_tok // GRID_W
    r = jnp.broadcast_to(jnp.arange(rows, dtype=jnp.float32)[:, None], (rows, GRID_W)).reshape(-1)
    col = jnp.broadcast_to(jnp.arange(GRID_W, dtype=jnp.float32)[None, :], (rows, GRID_W)).reshape(-1)
    quarter = dim // 4
    omega = 1.0 / (POS_BASE ** (jnp.arange(quarter, dtype=jnp.float32) / quarter))
    ar = r[:, None] * omega[None]
    ac = col[:, None] * omega[None]
    return jnp.concatenate([jnp.sin(ar), jnp.cos(ar), jnp.sin(ac), jnp.cos(ac)], axis=-1)


def hyena_filter_spectra(n_tok, w1, b1, w2, b2, w3, decay):
    f32 = jnp.float32
    idx = jnp.arange(n_tok, dtype=f32)
    t = idx / (n_tok - 1)
    bands = jnp.arange(1, HY_BANDS + 1, dtype=f32)
    ang = (2.0 * math.pi / n_tok) * idx[:, None] * bands[None, :]
    feats = jnp.concatenate([t[:, None], jnp.cos(ang), jnp.sin(ang)], axis=-1)
    h = jnp.sin(HY_SIN_FREQ * (feats @ w1.astype(f32) + b1.astype(f32)))
    h = jnp.sin(HY_SIN_FREQ * (h @ w2.astype(f32) + b2.astype(f32)))
    h = (h @ w3.astype(f32)).reshape(n_tok, 2, HY_ORDER, D_HY)
    window = jnp.exp(-t[:, None, None] * jnp.abs(decay.astype(f32))[None]) + HY_WINDOW_SHIFT
    h = h * window[:, None]
    h_fwd = h[:, 0]
    h_bwd = h[:, 1]
    taps = jnp.concatenate([h_fwd, jnp.zeros_like(h_fwd[:1]), h_bwd[1:][::-1]], axis=0)
    taps = taps / jnp.sum(jnp.abs(taps), axis=0, keepdims=True)
    return jnp.fft.rfft(taps, axis=0)


def fft_long_conv(u, spec, d_bias):
    n = u.shape[1]
    uf = jnp.fft.rfft(u, n=2 * n, axis=1)
    y = jnp.fft.irfft(uf * spec[None], n=2 * n, axis=1)[:, :n]
    return y + u * d_bias


def hyena_mixer(proj, conv_w, w1, b1, w2, b2, w3, decay, d_bias):
    n = proj.shape[1]
    pc = dw_conv_centred(proj, conv_w).astype(jnp.float32)
    v, x1, x2 = jnp.split(pc, 3, axis=-1)
    spec = hyena_filter_spectra(n, w1, b1, w2, b2, w3, decay)
    d_bias = d_bias.astype(jnp.float32)
    z = x1 * fft_long_conv(v, spec[:, 0], d_bias[0])
    z = x2 * fft_long_conv(z, spec[:, 1], d_bias[1])
    return z


def chunk_gated_delta(q, k, v, g, beta, s0):
    b, n, nh, dk = q.shape
    c = GDN_CHUNK
    nc = n // c

    def chunks(t):
        t = t.reshape((b, nc, c) + t.shape[2:])
        return jnp.moveaxis(t, 3, 1)

    q, k, v, g, beta = chunks(q), chunks(k), chunks(v), chunks(g), chunks(beta)
    g_cum = jnp.cumsum(g, axis=-1)
    diff = g_cum[..., :, None] - g_cum[..., None, :]
    causal = jnp.tril(jnp.ones((c, c), dtype=bool))
    strict = jnp.tril(jnp.ones((c, c), dtype=bool), -1)
    decay = jnp.where(causal, jnp.exp(jnp.where(causal, diff, 0.0)), 0.0)
    k_beta = k * beta[..., None]
    a_low = jnp.where(strict, jnp.einsum('bhnid,bhnjd->bhnij', k_beta, k) * decay, 0.0)
    t_mat = a_low + jnp.eye(c, dtype=jnp.float32)
    rhs = jnp.concatenate([v * beta[..., None], k_beta * jnp.exp(g_cum)[..., None]], axis=-1)
    sol = lax.linalg.triangular_solve(t_mat, rhs, left_side=True, lower=True, unit_diagonal=True)
    dv = v.shape[-1]
    u = sol[..., :dv]
    w = sol[..., dv:]
    attn = jnp.einsum('bhnid,bhnjd->bhnij', q, k) * decay
    g_last = g_cum[..., -1]
    k_dec = k * jnp.exp(g_last[..., None] - g_cum)[..., None]
    q_dec = q * jnp.exp(g_cum)[..., None]
    xs = tuple(jnp.moveaxis(t, 2, 0) for t in (q_dec, k_dec, u, w, attn, g_last))

    def step(s, inp):
        q_d, k_d, u_i, w_i, a_i, gl = inp
        v_new = u_i - jnp.einsum('bhck,bhkv->bhcv', w_i, s)
        o = jnp.einsum('bhck,bhkv->bhcv', q_d, s) + jnp.einsum('bhij,bhjv->bhiv', a_i, v_new)
        s = s * jnp.exp(gl)[..., None, None] + jnp.einsum('bhck,bhcv->bhkv', k_d, v_new)
        return s, o

    s_fin, o = lax.scan(step, s0, xs)
    o = jnp.transpose(o, (1, 0, 3, 2, 4)).reshape(b, n, nh, dv)
    return o, s_fin


def gdn_mixer(qkv, z, ab, conv_w, a_log, dt_bias, o_norm, s0):
    f32 = jnp.float32
    b, n, _ = qkv.shape
    qkv = jax.nn.silu(dw_conv_centred(qkv, conv_w)).astype(f32)
    q, k, v = jnp.split(qkv, 3, axis=-1)
    q = l2_norm(q.reshape(b, n, GDN_HEADS, GDN_DK)) * (GDN_DK ** -0.5)
    k = l2_norm(k.reshape(b, n, GDN_HEADS, GDN_DK))
    v = v.reshape(b, n, GDN_HEADS, GDN_DV)
    ab = ab.astype(f32).reshape(b, n, 2, 2, GDN_HEADS)
    g = -jnp.exp(a_log.astype(f32)) * jax.nn.softplus(ab[:, :, :, 0] + dt_bias.astype(f32))
    beta = jax.nn.sigmoid(ab[:, :, :, 1])
    o_f, s_f = chunk_gated_delta(q, k, v, g[:, :, 0], beta[:, :, 0], s0[:, 0])
    o_b, s_b = chunk_gated_delta(q[:, ::-1], k[:, ::-1], v[:, ::-1], g[:, ::-1, 1], beta[:, ::-1, 1], s0[:, 1])
    o = o_f + o_b[:, ::-1]
    o = rms_norm(o, o_norm) * jax.nn.silu(z.astype(f32).reshape(b, n, GDN_HEADS, GDN_DV))
    return o.reshape(b, n, D_GDN), jnp.stack([s_f, s_b], axis=1)


def trunk_layer(x, mod, s0, norm_pre, norm_post, ffn_wg, ffn_wu, ffn_wd, w_in, w_out,
                hy_conv_w, hy_f_w1, hy_f_b1, hy_f_w2, hy_f_b2, hy_f_w3, hy_decay, hy_bias, hy_out_norm,
                gdn_conv_w, gdn_a_log, gdn_dt_bias, gdn_o_norm):
    h = adaln(x, norm_pre[0], mod[:, :, 0], mod[:, :, 1])
    x = x + 0.5 * mod[:, :, 2] * rms_norm(swiglu(h, ffn_wg[0], ffn_wu[0], ffn_wd[0]), norm_post[0])

    h = adaln(x, norm_pre[1], mod[:, :, 3], mod[:, :, 4])
    proj = h @ w_in
    o0 = 3 * D_HY
    o1 = o0 + 3 * D_GDN
    o2 = o1 + D_GDN
    y_hy = rms_norm(hyena_mixer(proj[..., :o0], hy_conv_w, hy_f_w1, hy_f_b1, hy_f_w2, hy_f_b2,
                                hy_f_w3, hy_decay, hy_bias), hy_out_norm).astype(x.dtype)
    y_gdn, s_fin = gdn_mixer(proj[..., o0:o1], proj[..., o1:o2], proj[..., o2:], gdn_conv_w,
                             gdn_a_log, gdn_dt_bias, gdn_o_norm, s0)
    y = jnp.concatenate([y_hy, y_gdn.astype(x.dtype)], axis=-1) @ w_out
    x = x + mod[:, :, 5] * rms_norm(y, norm_post[1])

    h = adaln(x, norm_pre[2], mod[:, :, 6], mod[:, :, 7])
    x = x + 0.5 * mod[:, :, 8] * rms_norm(swiglu(h, ffn_wg[1], ffn_wu[1], ffn_wd[1]), norm_post[2])
    return x, s_fin


def setup_inputs(seed: int = 0) -> dict:
    key = jax.random.key(seed)
    ks = jax.random.split(key, 32)
    f32 = jnp.float32

    def nrm(k, shape, s):
        return jax.random.normal(k, shape, f32) * s

    dt = jnp.exp(jax.random.uniform(ks[25], (DEPTH, 2, GDN_HEADS), f32, math.log(1e-3), math.log(1e-1)))
    return {
        'x_prompt': nrm(ks[0], (BATCH, SEQ, D_MODEL), 1.0),
        'x_sample': nrm(ks[1], (DEC_BATCH, DEC_SEQ, D_MODEL), 1.0),
        'state_gdn': nrm(ks[2], (DEC_BATCH, DEPTH, 2, GDN_HEADS, GDN_DK, GDN_DV), GDN_DK ** -0.5),
        'c': nrm(ks[3], (DEC_BATCH, D_MODEL), 1.0),
        'c_ctx': nrm(ks[4], (D_MODEL,), 1.0),
        'ada_w': nrm(ks[5], (DEPTH, D_MODEL, N_MOD * D_MODEL), 0.5 * D_MODEL ** -0.5),
        'ada_b': nrm(ks[6], (DEPTH, N_MOD * D_MODEL), 0.02),
        'norm_pre': 1.0 + nrm(ks[7], (DEPTH, 3, D_MODEL), 0.02),
        'norm_post': 1.0 + nrm(ks[8], (DEPTH, 3, D_MODEL), 0.02),
        'ffn_wg': nrm(ks[9], (DEPTH, 2, D_MODEL, D_FF), D_MODEL ** -0.5),
        'ffn_wu': nrm(ks[10], (DEPTH, 2, D_MODEL, D_FF), D_MODEL ** -0.5),
        'ffn_wd': nrm(ks[11], (DEPTH, 2, D_FF, D_MODEL), D_FF ** -0.5),
        'w_in': nrm(ks[12], (DEPTH, D_MODEL, W_IN_COLS), D_MODEL ** -0.5),
        'w_out': nrm(ks[13], (DEPTH, D_MIX, D_MODEL), D_MIX ** -0.5),
        'hy_conv_w': nrm(ks[14], (DEPTH, HY_SHORT, 3 * D_HY), HY_SHORT ** -0.5),
        'hy_f_w1': nrm(ks[15], (DEPTH, HY_EMB_DIM, HY_FILTER_HIDDEN), HY_EMB_DIM ** -0.5),
        'hy_f_b1': nrm(ks[16], (DEPTH, HY_FILTER_HIDDEN), 0.1),
        'hy_f_w2': nrm(ks[17], (DEPTH, HY_FILTER_HIDDEN, HY_FILTER_HIDDEN), HY_FILTER_HIDDEN ** -0.5),
        'hy_f_b2': nrm(ks[18], (DEPTH, HY_FILTER_HIDDEN), 0.1),
        'hy_f_w3': nrm(ks[19], (DEPTH, HY_FILTER_HIDDEN, 2 * HY_ORDER * D_HY), HY_FILTER_HIDDEN ** -0.5),
        'hy_decay': jax.random.uniform(ks[20], (DEPTH, HY_ORDER, D_HY), f32, HY_DECAY_MIN, HY_DECAY_MAX),
        'hy_bias': nrm(ks[21], (DEPTH, HY_ORDER, D_HY), 1.0),
        'hy_out_norm': 1.0 + nrm(ks[22], (DEPTH, D_HY), 0.02),
        'gdn_conv_w': nrm(ks[23], (DEPTH, GDN_SHORT, 3 * D_GDN), GDN_SHORT ** -0.5),
        'gdn_a_log': jnp.log(jax.random.uniform(ks[24], (DEPTH, 2, GDN_HEADS), f32, 1.0, 16.0)),
        'gdn_dt_bias': dt + jnp.log(-jnp.expm1(-dt)),
        'gdn_o_norm': 1.0 + nrm(ks[26], (DEPTH, GDN_DV), 0.02),
    }


def reference(x_prompt, x_sample, state_gdn, c, c_ctx, ada_w, ada_b, norm_pre, norm_post,
              ffn_wg, ffn_wu, ffn_wd, w_in, w_out, hy_conv_w, hy_f_w1, hy_f_b1, hy_f_w2, hy_f_b2,
              hy_f_w3, hy_decay, hy_bias, hy_out_norm, gdn_conv_w, gdn_a_log, gdn_dt_bias, gdn_o_norm):
    n_ctx_req = x_prompt.shape[0]
    n_lat_req = x_sample.shape[0]
    xp = x_prompt
    xs = x_sample + grid_pos_embed(x_sample.shape[1], D_MODEL).astype(x_sample.dtype)[None]
    ctx_states = []
    for l in range(DEPTH):
        mod_ctx = (jax.nn.silu(c_ctx)[None] @ ada_w[l] + ada_b[l]).reshape(1, 1, N_MOD, D_MODEL)
        mod_lat = (jax.nn.silu(c) @ ada_w[l] + ada_b[l]).reshape(n_lat_req, 1, N_MOD, D_MODEL)
        layer_w = (norm_pre[l], norm_post[l], ffn_wg[l], ffn_wu[l], ffn_wd[l], w_in[l], w_out[l],
                   hy_conv_w[l], hy_f_w1[l], hy_f_b1[l], hy_f_w2[l], hy_f_b2[l], hy_f_w3[l], hy_decay[l],
                   hy_bias[l], hy_out_norm[l], gdn_conv_w[l], gdn_a_log[l], gdn_dt_bias[l], gdn_o_norm[l])
        s0_ctx = jnp.zeros((n_ctx_req, 2, GDN_HEADS, GDN_DK, GDN_DV), jnp.float32)
        xp, s_ctx = trunk_layer(xp, mod_ctx, s0_ctx, *layer_w)
        xs, _ = trunk_layer(xs, mod_lat, state_gdn[:, l].astype(jnp.float32), *layer_w)
        ctx_states.append(s_ctx)
    new_state_gdn = jnp.stack(ctx_states, axis=1)
    return (xp, xs, new_state_gdn)
```

```python
import functools
import math

import numpy as np
import jax
import jax.numpy as jnp
from jax import lax
from jax.experimental import pallas as pl
from jax.experimental.pallas import tpu as pltpu

F32 = jnp.float32
BF16 = jnp.bfloat16

N_MOD = 9
RMS_EPS = 1e-6
GRID_W = 64
POS_BASE = 10000.0
HY_BANDS = 16
HY_SIN_FREQ = 1.0
HY_WINDOW_SHIFT = 0.05
GDN_HEADS = 8
GDN_CHUNK = 128
NEWTON_STEPS = 1
VMEM_LIMIT = 56 * 1024 * 1024


def _cparams(sem):
    return pltpu.CompilerParams(dimension_semantics=sem, vmem_limit_bytes=VMEM_LIMIT)


def _dot(a, b):
    return jnp.dot(a, b, preferred_element_type=F32)


def _dot_nt(a, b):
    return lax.dot_general(a, b, (((1,), (1,)), ((), ())), preferred_element_type=F32)


def _dot_tn(a, b):
    return lax.dot_general(a, b, (((0,), (0,)), ((), ())), preferred_element_type=F32)


def _rms(x, g):
    ms = jnp.mean(x * x, axis=-1, keepdims=True)
    return x * lax.rsqrt(ms + RMS_EPS) * g


def _silu(x):
    return x * jax.nn.sigmoid(x)


def _group_of_tile(i, tm, n_ctx_rows, rows_per_req):
    nct = n_ctx_rows // tm
    tpr = rows_per_req // tm
    return jnp.where(i < nct, 0, 1 + jnp.maximum(i - nct, 0) // tpr)


def _mod_body(c_ref, w_ref, b_ref, o_ref):
    s = _silu(c_ref[...]).astype(BF16)
    o_ref[...] = _dot(s, w_ref[...].astype(BF16)) + b_ref[...]


def _mod_call(cvec, ada_w, ada_b):
    g, d = cvec.shape
    n = ada_w.shape[1]
    tn = 1024
    return pl.pallas_call(
        _mod_body,
        grid=(n // tn,),
        in_specs=[pl.BlockSpec((g, d), lambda j: (0, 0)),
                  pl.BlockSpec((d, tn), lambda j: (0, j)),
                  pl.BlockSpec((1, tn), lambda j: (0, j))],
        out_specs=pl.BlockSpec((g, tn), lambda j: (0, j)),
        out_shape=jax.ShapeDtypeStruct((g, n), F32),
        compiler_params=_cparams(("arbitrary",)),
        name="mod_table",
    )(cvec, ada_w, ada_b.reshape(1, n))


def _ffn_body(x_ref, mod_ref, npre_ref, npost_ref, wg_ref, wu_ref, wd_ref, o_ref, h_ref, acc_ref, *, mi):
    j = pl.program_id(1)

    @pl.when(j == 0)
    def _():
        shift = mod_ref[0, mi:mi + 1, :]
        scale = mod_ref[0, mi + 1:mi + 2, :]
        h = _rms(x_ref[...], npre_ref[...]) * (1.0 + scale) + shift
        h_ref[...] = h.astype(BF16)
        acc_ref[...] = jnp.zeros_like(acc_ref)

    h = h_ref[...]
    g = _dot(h, wg_ref[...])
    u = _dot(h, wu_ref[...])
    a = (_silu(g) * u).astype(BF16)
    acc_ref[...] += _dot(a, wd_ref[...])

    @pl.when(j == pl.num_programs(1) - 1)
    def _():
        gate = mod_ref[0, mi + 2:mi + 3, :]
        o_ref[...] = x_ref[...] + 0.5 * gate * _rms(acc_ref[...], npost_ref[...])


def _ffn_call(x, mod, npre, npost, wg, wu, wd, *, mi, n_ctx_rows, rows_per_req, tm=512, tf=512):
    m, d = x.shape
    ff = wg.shape[1]
    grp = functools.partial(_group_of_tile, tm=tm, n_ctx_rows=n_ctx_rows, rows_per_req=rows_per_req)
    return pl.pallas_call(
        functools.partial(_ffn_body, mi=mi),
        grid=(m // tm, ff // tf),
        in_specs=[pl.BlockSpec((tm, d), lambda i, j: (i, 0)),
                  pl.BlockSpec((1, N_MOD, d), lambda i, j: (grp(i), 0, 0)),
                  pl.BlockSpec((1, d), lambda i, j: (0, 0)),
                  pl.BlockSpec((1, d), lambda i, j: (0, 0)),
                  pl.BlockSpec((d, tf), lambda i, j: (0, j)),
                  pl.BlockSpec((d, tf), lambda i, j: (0, j)),
                  pl.BlockSpec((tf, d), lambda i, j: (j, 0))],
        out_specs=pl.BlockSpec((tm, d), lambda i, j: (i, 0)),
        out_shape=jax.ShapeDtypeStruct((m, d), F32),
        scratch_shapes=[pltpu.VMEM((tm, d), BF16), pltpu.VMEM((tm, d), F32)],
        compiler_params=_cparams(("parallel", "arbitrary")),
        name="ffn",
    )(x, mod, npre.reshape(1, d), npost.reshape(1, d), wg, wu, wd)


def _inproj_body(x_ref, mod_ref, npre_ref, w_ref, o_ref, h_ref, *, mi):
    @pl.when(pl.program_id(1) == 0)
    def _():
        shift = mod_ref[0, mi:mi + 1, :]
        scale = mod_ref[0, mi + 1:mi + 2, :]
        h = _rms(x_ref[...], npre_ref[...]) * (1.0 + scale) + shift
        h_ref[...] = h.astype(BF16)

    o_ref[...] = _dot(h_ref[...], w_ref[...])


def _inproj_call(x, mod, npre, w, *, mi, n_ctx_rows, rows_per_req, tm=1024, tn=512):
    m, d = x.shape
    n = w.shape[1]
    grp = functools.partial(_group_of_tile, tm=tm, n_ctx_rows=n_ctx_rows, rows_per_req=rows_per_req)
    return pl.pallas_call(
        functools.partial(_inproj_body, mi=mi),
        grid=(m // tm, n // tn),
        in_specs=[pl.BlockSpec((tm, d), lambda i, j: (i, 0)),
                  pl.BlockSpec((1, N_MOD, d), lambda i, j: (grp(i), 0, 0)),
                  pl.BlockSpec((1, d), lambda i, j: (0, 0)),
                  pl.BlockSpec((d, tn), lambda i, j: (0, j))],
        out_specs=pl.BlockSpec((tm, tn), lambda i, j: (i, j)),
        out_shape=jax.ShapeDtypeStruct((m, n), F32),
        scratch_shapes=[pltpu.VMEM((tm, d), BF16)],
        compiler_params=_cparams(("parallel", "arbitrary")),
        name="in_proj",
    )(x, mod, npre.reshape(1, d), w)


def _conv3(u, w):
    n = u.shape[0]
    row = lax.broadcasted_iota(jnp.int32, u.shape, 0)
    prev = jnp.where(row == 0, 0.0, pltpu.roll(u, 1, 0))
    nxt = jnp.where(row == n - 1, 0.0, pltpu.roll(u, n - 1, 0))
    return prev * w[0:1, :] + u * w[1:2, :] + nxt * w[2:3, :]


@functools.lru_cache(maxsize=None)
def _dft_tables_np(n_tok, r):
    big = 2 * n_tok
    k = np.arange(n_tok, dtype=np.int64)
    t = np.arange(n_tok, dtype=np.int64)
    ang = ((k[:, None] * t[None, :]) % big).astype(np.float64) * (2.0 * math.pi / big)
    cosm = np.cos(ang)
    sinm = np.sin(ang)
    sinm[0, :] = 1.0 - 2.0 * (t % 2)
    nj = n_tok // r
    fr = np.concatenate([cosm.reshape(nj, r, n_tok), sinm.reshape(nj, r, n_tok)], axis=1)
    gr = np.ascontiguousarray(np.transpose(fr, (0, 2, 1)))
    return fr.astype(np.float32), gr.astype(np.float32)


def _dft_tables(n_tok, r):
    fr, gr = _dft_tables_np(n_tok, r)
    return jnp.asarray(fr).astype(BF16), jnp.asarray(gr).astype(BF16)


def _spectra_body(feat_ref, w1_ref, b1_ref, w2_ref, b2_ref, w3f_ref, w3b_ref, dec_ref, f_ref,
                  ha_ref, hb_ref, hc_ref, e_ref, o_ref, *, n_tok, r):
    j = pl.program_id(1)

    @pl.when(j == 0)
    def _():
        feats = feat_ref[...]
        h = jnp.sin(HY_SIN_FREQ * (_dot(feats.astype(BF16), w1_ref[...].astype(BF16)) + b1_ref[...]))
        h = jnp.sin(HY_SIN_FREQ * (_dot(h.astype(BF16), w2_ref[...].astype(BF16)) + b2_ref[...]))
        hb16 = h.astype(BF16)
        t = feats[:, 0:1]
        window = jnp.exp(-t * jnp.abs(dec_ref[...])) + HY_WINDOW_SHIFT
        fwd = _dot(hb16, w3f_ref[...].astype(BF16)) * window
        bwd = _dot(hb16, w3b_ref[...].astype(BF16)) * window
        row = lax.broadcasted_iota(jnp.int32, bwd.shape, 0)
        bwd = jnp.where(row == 0, 0.0, bwd)
        norm = jnp.sum(jnp.abs(fwd), axis=0, keepdims=True) + jnp.sum(jnp.abs(bwd), axis=0, keepdims=True)
        fwd = fwd / norm
        bwd = bwd / norm
        e_ref[...] = (fwd + bwd).astype(BF16)
        o_ref[...] = (fwd - bwd).astype(BF16)

    fb = f_ref[0]
    e = e_ref[...]
    p = _dot(fb[:r], e)
    q = _dot(fb[r:], o_ref[...])
    alt = _dot(fb[r:r + 8], e)[0:1, :]
    k = j * r + lax.broadcasted_iota(jnp.int32, p.shape, 0)
    big = 2.0 * n_tok
    wk = jnp.where(k == 0, 1.0 / big, 2.0 / big)
    ha = wk * p
    ha_ref[...] = ha
    hb_ref[...] = jnp.where(k == 0, 0.0, -wk * q)
    hc_ref[...] = jnp.where(k == 0, alt * (1.0 / big), ha)


def _spectra_call(n_tok, w1, b1, w2, b2, w3, decay, fr, *, r, cn=512):
    f32 = jnp.float32
    order, c = decay.shape
    oc = order * c
    hid = w2.shape[0]
    emb = w1.shape[0]
    embp = 128
    idx = jnp.arange(n_tok, dtype=f32)
    t = idx / (n_tok - 1)
    bands = jnp.arange(1, HY_BANDS + 1, dtype=f32)
    ang = (2.0 * math.pi / n_tok) * idx[:, None] * bands[None, :]
    feats = jnp.concatenate([t[:, None], jnp.cos(ang), jnp.sin(ang)], axis=-1)
    feats = jnp.pad(feats, ((0, 0), (0, embp - emb)))
    w1p = jnp.pad(w1, ((0, embp - emb), (0, 0)))
    nj = n_tok // r
    ncb = oc // cn
    out = jax.ShapeDtypeStruct((n_tok, oc), F32)
    return pl.pallas_call(
        functools.partial(_spectra_body, n_tok=n_tok, r=r),
        grid=(ncb, nj),
        in_specs=[pl.BlockSpec((n_tok, embp), lambda cb, j: (0, 0)),
                  pl.BlockSpec((embp, hid), lambda cb, j: (0, 0)),
                  pl.BlockSpec((1, hid), lambda cb, j: (0, 0)),
                  pl.BlockSpec((hid, hid), lambda cb, j: (0, 0)),
                  pl.BlockSpec((1, hid), lambda cb, j: (0, 0)),
                  pl.BlockSpec((hid, cn), lambda cb, j: (0, cb)),
                  pl.BlockSpec((hid, cn), lambda cb, j: (0, ncb + cb)),
                  pl.BlockSpec((1, cn), lambda cb, j: (0, cb)),
                  pl.BlockSpec((1, 2 * r, n_tok), lambda cb, j: (j, 0, 0))],
        out_specs=[pl.BlockSpec((r, cn), lambda cb, j: (j, cb))] * 3,
        out_shape=[out, out, out],
        scratch_shapes=[pltpu.VMEM((n_tok, cn), BF16), pltpu.VMEM((n_tok, cn), BF16)],
        compiler_params=_cparams(("parallel", "arbitrary")),
        name="hyena_spectra",
    )(feats, w1p, b1.reshape(1, hid), w2, b2.reshape(1, hid), w3, w3, decay.reshape(1, oc), fr)


def _hyena_body(v_ref, x1_ref, x2_ref, wv_ref, wx1_ref, wx2_ref, bias_ref, f_ref, g_ref,
                ha_ref, hb_ref, hc_ref, o_ref, uf_ref, ub_ref, acc_ref, *, r, nj):
    j = pl.program_id(2)

    @pl.when(j == 0)
    def _():
        v = _conv3(v_ref[...], wv_ref[...])
        uf_ref[...] = v
        ub_ref[...] = v.astype(BF16)
        acc_ref[...] = jnp.zeros_like(acc_ref)

    x = _dot(f_ref[0], ub_ref[...])
    p = x[:r]
    q = x[r:]
    hb = hb_ref[...]
    top = p * ha_ref[...] + q * hb
    bot = q * hc_ref[...] - p * hb
    z = jnp.concatenate([top, bot], axis=0).astype(BF16)
    acc_ref[...] += _dot(g_ref[0], z)

    @pl.when(j == nj - 1)
    def _():
        x1 = _conv3(x1_ref[...], wx1_ref[...])
        z1 = x1 * (acc_ref[...] + uf_ref[...] * bias_ref[0:1, :])
        uf_ref[...] = z1
        ub_ref[...] = z1.astype(BF16)
        acc_ref[...] = jnp.zeros_like(acc_ref)

    @pl.when(j == 2 * nj - 1)
    def _():
        x2 = _conv3(x2_ref[...], wx2_ref[...])
        o_ref[...] = x2 * (acc_ref[...] + uf_ref[...] * bias_ref[1:2, :])


def _hyena_call(proj, conv_w, bias, fr, gr, ha, hb, hc, *, n_seq, n_tok, row_blk0, c, r, cn=512):
    nj = n_tok // r
    ncb = c // cn
    single = pl.Buffered(1)
    seq_spec = lambda off: pl.BlockSpec((n_tok, cn), lambda b, cb, j: (row_blk0 + b, off * ncb + cb),
                                        pipeline_mode=single)
    w_spec = lambda off: pl.BlockSpec((3, cn), lambda b, cb, j: (0, off * ncb + cb))
    h_spec = pl.BlockSpec((r, cn), lambda b, cb, j: (j % nj, (j // nj) * ncb + cb))
    return pl.pallas_call(
        functools.partial(_hyena_body, r=r, nj=nj),
        grid=(n_seq, ncb, 2 * nj),
        in_specs=[seq_spec(0), seq_spec(1), seq_spec(2), w_spec(0), w_spec(1), w_spec(2),
                  pl.BlockSpec((2, cn), lambda b, cb, j: (0, cb)),
                  pl.BlockSpec((1, 2 * r, n_tok), lambda b, cb, j: (j % nj, 0, 0)),
                  pl.BlockSpec((1, n_tok, 2 * r), lambda b, cb, j: (j % nj, 0, 0)),
                  h_spec, h_spec, h_spec],
        out_specs=pl.BlockSpec((n_tok, cn), lambda b, cb, j: (b, cb)),
        out_shape=jax.ShapeDtypeStruct((n_seq * n_tok, c), F32),
        scratch_shapes=[pltpu.VMEM((n_tok, cn), F32), pltpu.VMEM((n_tok, cn), BF16),
                        pltpu.VMEM((n_tok, cn), F32)],
        compiler_params=_cparams(("parallel", "parallel", "arbitrary")),
        name="hyena_conv",
    )(proj, proj, proj, conv_w, conv_w, conv_w, bias, fr, gr, ha, hb, hc)


def _gdn_prep_body(x_ref, w_ref, o_ref, *, heads_per_blk, dk, n_qk_blk):
    cb = pl.program_id(1)
    y = _silu(_conv3(x_ref[...], w_ref[...]))

    @pl.when(cb >= n_qk_blk)
    def _():
        o_ref[...] = y

    @pl.when(cb < n_qk_blk)
    def _():
        scale = jnp.where(cb < n_qk_blk // 2, dk ** -0.5, 1.0)
        for h in range(heads_per_blk):
            yh = y[:, h * dk:(h + 1) * dk]
            inv = lax.rsqrt(jnp.sum(yh * yh, axis=-1, keepdims=True) + RMS_EPS)
            o_ref[:, h * dk:(h + 1) * dk] = yh * (inv * scale)


def _gdn_prep_call(proj, conv_w, *, n_seq, n_tok, row_blk0, col0, d_gdn, dk, cn=512):
    ncb = 3 * d_gdn // cn
    cb0 = col0 // cn
    return pl.pallas_call(
        functools.partial(_gdn_prep_body, heads_per_blk=cn // dk, dk=dk, n_qk_blk=2 * d_gdn // cn),
        grid=(n_seq, ncb),
        in_specs=[pl.BlockSpec((n_tok, cn), lambda b, cb: (row_blk0 + b, cb0 + cb)),
                  pl.BlockSpec((3, cn), lambda b, cb: (0, cb))],
        out_specs=pl.BlockSpec((n_tok, cn), lambda b, cb: (b, cb)),
        out_shape=jax.ShapeDtypeStruct((n_seq * n_tok, 3 * d_gdn), F32),
        compiler_params=_cparams(("parallel", "parallel")),
        name="gdn_prep",
    )(proj, conv_w)


def _split3(x):
    hi = x.astype(BF16)
    r1 = x - hi.astype(F32)
    mid = r1.astype(BF16)
    lo = (r1 - mid.astype(F32)).astype(BF16)
    return hi, mid, lo


def _gdn_body(q_ref, k_ref, v_ref, ab_ref, abt_ref, av_ref, dtv_ref, avt_ref, dtvt_ref, s0_ref,
              o_ref, sfin_ref, s_ref, *, rev, heads, dk, chunk, col0):
    n = pl.program_id(1)

    @pl.when(n == 0)
    def _():
        s_ref[...] = s0_ref[0, 0]

    ci = lax.broadcasted_iota(jnp.int32, (chunk, chunk), 0)
    cj = lax.broadcasted_iota(jnp.int32, (chunk, chunk), 1)
    if rev:
        incl = cj >= ci
        strict = cj > ci
    else:
        incl = cj <= ci
        strict = cj < ci
    eye = (ci == cj).astype(F32)
    diag_blk = (ci >> 3) == (cj >> 3)
    merge_blks = [((ci >> s) ^ (cj >> s)) == 1 for s in range(3, int(math.log2(chunk)))]
    ones_incl = jnp.where(incl, 1.0, 0.0).astype(BF16)
    ones_incl_t = jnp.where(ci >= cj if rev else ci <= cj, 1.0, 0.0).astype(BF16)

    ab = ab_ref[...]
    g_col = -av_ref[...] * jax.nn.softplus(ab + dtv_ref[...])
    beta_col = jax.nn.sigmoid(ab)
    abt = abt_ref[...]
    g_row = -avt_ref[...] * jax.nn.softplus(abt + dtvt_ref[...])

    h3 = _split3(g_col)
    gc_col = _dot(ones_incl, h3[0]) + _dot(ones_incl, h3[1]) + _dot(ones_incl, h3[2])
    r3 = _split3(g_row)
    gc_row = _dot(r3[0], ones_incl_t) + _dot(r3[1], ones_incl_t) + _dot(r3[2], ones_incl_t)
    gl_all = jnp.sum(g_col, axis=0, keepdims=True)

    for h in range(heads):
        cg = col0 + h
        cb = col0 + heads + h
        sl = slice(h * dk, (h + 1) * dk)
        q = q_ref[:, sl]
        k = k_ref[:, sl]
        v = v_ref[:, sl]
        gc = gc_col[:, cg:cg + 1]
        gr = gc_row[cg:cg + 1, :]
        beta = beta_col[:, cb:cb + 1]
        gl = gl_all[:, cg:cg + 1]

        decay = jnp.where(incl, jnp.exp(jnp.where(incl, gc - gr, 0.0)), 0.0)
        k_beta = k * beta
        kb16 = k.astype(BF16)
        a_low = jnp.where(strict, _dot_nt(k_beta.astype(BF16), kb16) * decay, 0.0)
        attn = _dot_nt(q.astype(BF16), kb16) * decay

        d = jnp.where(diag_blk, a_low, 0.0)
        d2 = _dot(d.astype(BF16), d.astype(BF16))
        d4 = _dot(d2.astype(BF16), d2.astype(BF16))
        x = eye - d
        x = x + _dot(x.astype(BF16), d2.astype(BF16))
        x = x + _dot(x.astype(BF16), d4.astype(BF16))
        for off_blk in merge_blks:
            x16 = x.astype(BF16)
            t = _dot(jnp.where(off_blk, a_low, 0.0).astype(BF16), x16)
            x = x - _dot(x16, t.astype(BF16))
        a_hi = a_low.astype(BF16)
        a_lo = (a_low - a_hi.astype(F32)).astype(BF16)
        for _ in range(NEWTON_STEPS):
            x0 = x.astype(BF16)
            x0f = x0.astype(F32)
            res = eye - x0f - (_dot(a_hi, x0) + _dot(a_lo, x0))
            x = x0f + _dot(x0, res.astype(BF16))
        e_gc = jnp.exp(gc)
        rhs = jnp.concatenate([v * beta, k_beta * e_gc], axis=-1).astype(BF16)
        sol = _dot(x.astype(BF16), rhs)
        u = sol[:, :dk]
        w = sol[:, dk:]

        k_dec = k * jnp.exp(gl - gc)
        q_dec = q * e_gc
        s = s_ref[h]
        s16 = s.astype(BF16)
        v_new = u - _dot(w.astype(BF16), s16)
        vn16 = v_new.astype(BF16)
        o = _dot(q_dec.astype(BF16), s16) + _dot(attn.astype(BF16), vn16)
        o_ref[:, sl] = o
        s_ref[h] = s * jnp.exp(gl) + _dot_tn(k_dec.astype(BF16), vn16)

    @pl.when(n == pl.num_programs(1) - 1)
    def _():
        sfin_ref[0] = s_ref[...]


def _gdn_call(qkv, ab_src, abt, av, dtv, avt, dtvt, s0, *, rev, n_seq, n_tok, ab_row_blk0, ab_col_blk,
              heads, dk, chunk):
    nc = n_tok // chunk
    d = heads * dk
    dr = 1 if rev else 0
    cidx = (lambda n: nc - 1 - n) if rev else (lambda n: n)
    qkv_spec = lambda which: pl.BlockSpec((chunk, d), lambda b, n: (b * nc + cidx(n), which))
    small = lambda shape: pl.BlockSpec(shape, lambda b, n: (0, 0))
    st_spec = pl.BlockSpec((1, 1, heads, dk, dk), lambda b, n: (b, dr, 0, 0, 0))
    return pl.pallas_call(
        functools.partial(_gdn_body, rev=rev, heads=heads, dk=dk, chunk=chunk, col0=dr * 2 * heads),
        grid=(n_seq, nc),
        in_specs=[qkv_spec(0), qkv_spec(1), qkv_spec(2),
                  pl.BlockSpec((chunk, 128), lambda b, n: (ab_row_blk0 + b * nc + cidx(n), ab_col_blk)),
                  pl.BlockSpec((4 * heads, chunk), lambda b, n: (0, b * nc + cidx(n))),
                  small((1, 128)), small((1, 128)), small((4 * heads, 1)), small((4 * heads, 1)),
                  st_spec],
        out_specs=[pl.BlockSpec((chunk, d), lambda b, n: (b * nc + cidx(n), 0)),
                   pl.BlockSpec((1, heads, dk, dk), lambda b, n: (b, 0, 0, 0))],
        out_shape=[jax.ShapeDtypeStruct((n_seq * n_tok, d), F32),
                   jax.ShapeDtypeStruct((n_seq, heads, dk, dk), F32)],
        scratch_shapes=[pltpu.VMEM((heads, dk, dk), F32)],
        compiler_params=_cparams(("parallel", "arbitrary")),
        name="gdn_bwd" if rev else "gdn_fwd",
    )(qkv, qkv, qkv, ab_src, abt, av, dtv, avt, dtvt, s0)


def _outproj_body(x_ref, mod_ref, zhy_ref, of_ref, ob_ref, zg_ref, ghy_ref, go_ref, npost_ref, w_ref,
                  o_ref, *, mi, heads, dk):
    yhy = _rms(zhy_ref[...], ghy_ref[...]).astype(BF16)
    o = of_ref[...] + ob_ref[...]
    zg = zg_ref[...]
    parts = [yhy]
    for h in range(heads):
        sl = slice(h * dk, (h + 1) * dk)
        parts.append((_rms(o[:, sl], go_ref[...]) * _silu(zg[:, sl])).astype(BF16))
    y = _dot(jnp.concatenate(parts, axis=-1), w_ref[...])
    gate = mod_ref[0, mi:mi + 1, :]
    o_ref[...] = x_ref[...] + gate * _rms(y, npost_ref[...])


def _outproj_call(x, mod, zhy, o_f, o_b, proj, ghy, go, npost, w, *, mi, zg_col_blk, n_ctx_rows, rows_per_req,
                  heads, dk, tm=512):
    m, d = x.shape
    c = zhy.shape[1]
    dg = heads * dk
    grp = functools.partial(_group_of_tile, tm=tm, n_ctx_rows=n_ctx_rows, rows_per_req=rows_per_req)
    row = lambda width: pl.BlockSpec((tm, width), lambda i: (i, 0))
    return pl.pallas_call(
        functools.partial(_outproj_body, mi=mi, heads=heads, dk=dk),
        grid=(m // tm,),
        in_specs=[row(d),
                  pl.BlockSpec((1, N_MOD, d), lambda i: (grp(i), 0, 0)),
                  row(c), row(dg), row(dg),
                  pl.BlockSpec((tm, dg), lambda i: (i, zg_col_blk)),
                  pl.BlockSpec((1, c), lambda i: (0, 0)),
                  pl.BlockSpec((1, dk), lambda i: (0, 0)),
                  pl.BlockSpec((1, d), lambda i: (0, 0)),
                  pl.BlockSpec((c + dg, d), lambda i: (0, 0))],
        out_specs=row(d),
        out_shape=jax.ShapeDtypeStruct((m, d), F32),
        compiler_params=_cparams(("parallel",)),
        name="out_proj",
    )(x, mod, zhy, o_f, o_b, proj, ghy.reshape(1, c), go.reshape(1, dk), npost.reshape(1, d), w)


def _grid_pos_embed(n_tok, dim):
    rows = n_tok // GRID_W
    r = jnp.broadcast_to(jnp.arange(rows, dtype=F32)[:, None], (rows, GRID_W)).reshape(-1)
    col = jnp.broadcast_to(jnp.arange(GRID_W, dtype=F32)[None, :], (rows, GRID_W)).reshape(-1)
    quarter = dim // 4
    omega = 1.0 / (POS_BASE ** (jnp.arange(quarter, dtype=F32) / quarter))
    ar = r[:, None] * omega[None]
    ac = col[:, None] * omega[None]
    return jnp.concatenate([jnp.sin(ar), jnp.cos(ar), jnp.sin(ac), jnp.cos(ac)], axis=-1)


def _mixer(proj, *, groups, hy_conv_w, hy_f, hy_decay, hy_bias, gdn_conv_w, gdn_a_log, gdn_dt_bias,
           c_hy, d_gdn, states):
    heads = GDN_HEADS
    dk = d_gdn // heads
    o0 = 3 * c_hy
    o2 = o0 + 4 * d_gdn
    ab_col_blk = o2 // 128

    a_exp = jnp.exp(gdn_a_log.astype(F32))
    zeros = jnp.zeros_like(a_exp)
    av32 = jnp.stack([a_exp, zeros], axis=1).reshape(-1)
    dtv32 = jnp.stack([gdn_dt_bias.astype(F32), zeros], axis=1).reshape(-1)
    pad = 128 - 4 * heads
    av = jnp.pad(av32, (0, pad)).reshape(1, 128)
    dtv = jnp.pad(dtv32, (0, pad)).reshape(1, 128)
    avt = av32.reshape(4 * heads, 1)
    dtvt = dtv32.reshape(4 * heads, 1)

    zs, ofs, obs, sfs = [], [], [], []
    for (n_seq, n_tok, row0), s0 in zip(groups, states):
        r = min(256, n_tok)
        fr, gr = _dft_tables(n_tok, r)
        ha, hb, hc = _spectra_call(n_tok, *hy_f, hy_decay, fr, r=r)
        zs.append(_hyena_call(proj, hy_conv_w, hy_bias, fr, gr, ha, hb, hc, n_seq=n_seq, n_tok=n_tok,
                              row_blk0=row0 // n_tok, c=c_hy, r=r))
        qkv = _gdn_prep_call(proj, gdn_conv_w, n_seq=n_seq, n_tok=n_tok, row_blk0=row0 // n_tok, col0=o0,
                             d_gdn=d_gdn, dk=dk)
        rows = n_seq * n_tok
        abt = jnp.transpose(lax.slice(proj, (row0, o2), (row0 + rows, o2 + 4 * heads)))
        common = dict(n_seq=n_seq, n_tok=n_tok, ab_row_blk0=row0 // GDN_CHUNK, ab_col_blk=ab_col_blk,
                      heads=heads, dk=dk, chunk=GDN_CHUNK)
        o_f, s_f = _gdn_call(qkv, proj, abt, av, dtv, avt, dtvt, s0, rev=False, **common)
        o_b, s_b = _gdn_call(qkv, proj, abt, av, dtv, avt, dtvt, s0, rev=True, **common)
        ofs.append(o_f)
        obs.append(o_b)
        sfs.append(jnp.stack([s_f, s_b], axis=1))
    return jnp.concatenate(zs, 0), jnp.concatenate(ofs, 0), jnp.concatenate(obs, 0), sfs


def kernel(x_prompt, x_sample, state_gdn, c, c_ctx, ada_w, ada_b, norm_pre, norm_post, ffn_wg, ffn_wu, ffn_wd,
           w_in, w_out, hy_conv_w, hy_f_w1, hy_f_b1, hy_f_w2, hy_f_b2, hy_f_w3, hy_decay, hy_bias, hy_out_norm,
           gdn_conv_w, gdn_a_log, gdn_dt_bias, gdn_o_norm):
    nb, seq, d = x_prompt.shape
    db, dseq, _ = x_sample.shape
    depth = ada_w.shape[0]
    c_hy = hy_decay.shape[-1]
    d_gdn = gdn_conv_w.shape[-1] // 3
    heads = GDN_HEADS
    dk = d_gdn // heads
    n_ctx_rows = nb * seq
    grp = dict(n_ctx_rows=n_ctx_rows, rows_per_req=dseq)

    xs = x_sample + _grid_pos_embed(dseq, d).astype(x_sample.dtype)[None]
    x = jnp.concatenate([x_prompt.reshape(n_ctx_rows, d), xs.reshape(db * dseq, d)], axis=0)
    cvec = jnp.concatenate([c_ctx[None], c, jnp.zeros((8 - 1 - db, d), F32)], axis=0)

    ctx_states = []
    for l in range(depth):
        mod = _mod_call(cvec, ada_w[l], ada_b[l]).reshape(8, N_MOD, d)
        wg = ffn_wg[l].astype(BF16)
        wu = ffn_wu[l].astype(BF16)
        wd = ffn_wd[l].astype(BF16)
        ncols = w_in.shape[-1]
        ncols_p = -(-(ncols + 96) // 512) * 512
        w_in_p = jnp.pad(w_in[l], ((0, 0), (0, ncols_p - ncols))).astype(BF16)

        x = _ffn_call(x, mod, norm_pre[l, 0], norm_post[l, 0], wg[0], wu[0], wd[0], mi=0, **grp)

        proj = _inproj_call(x, mod, norm_pre[l, 1], w_in_p, mi=3, **grp)
        s0_ctx = jnp.zeros((nb, 2, heads, dk, dk), F32)
        s0_lat = state_gdn[:, l].astype(F32)
        zhy, o_f, o_b, sfs = _mixer(
            proj, groups=[(nb, seq, 0), (db, dseq, n_ctx_rows)], hy_conv_w=hy_conv_w[l],
            hy_f=(hy_f_w1[l], hy_f_b1[l], hy_f_w2[l], hy_f_b2[l], hy_f_w3[l]), hy_decay=hy_decay[l],
            hy_bias=hy_bias[l], gdn_conv_w=gdn_conv_w[l], gdn_a_log=gdn_a_log[l], gdn_dt_bias=gdn_dt_bias[l],
            c_hy=c_hy, d_gdn=d_gdn, states=[s0_ctx, s0_lat])
        ctx_states.append(sfs[0])
        x = _outproj_call(x, mod, zhy, o_f, o_b, proj, hy_out_norm[l], gdn_o_norm[l], norm_post[l, 1],
                          w_out[l].astype(BF16), mi=5, zg_col_blk=(3 * c_hy + 3 * d_gdn) // d_gdn, heads=heads,
                          dk=dk, **grp)

        x = _ffn_call(x, mod, norm_pre[l, 2], norm_post[l, 2], wg[1], wu[1], wd[1], mi=6, **grp)

    y_prompt = x[:n_ctx_rows].reshape(nb, seq, d)
    y_sample = x[n_ctx_rows:].reshape(db, dseq, d)
    return y_prompt, y_sample, jnp.stack(ctx_states, axis=1)
```

```python
import functools
import math

import numpy as np
import jax
import jax.numpy as jnp
from jax import lax
from jax.experimental import pallas as pl
from jax.experimental.pallas import tpu as pltpu

F32 = jnp.float32
BF16 = jnp.bfloat16

N_MOD = 9
RMS_EPS = 1e-6
GRID_W = 64
POS_BASE = 10000.0
HY_BANDS = 16
HY_SIN_FREQ = 1.0
HY_WINDOW_SHIFT = 0.05
GDN_HEADS = 8
GDN_CHUNK = 128
NEWTON_STEPS = 1
VMEM_LIMIT = 56 * 1024 * 1024


def _cparams(sem):
    return pltpu.CompilerParams(dimension_semantics=sem, vmem_limit_bytes=VMEM_LIMIT)


def _dot(a, b):
    return jnp.dot(a, b, preferred_element_type=F32)


def _dot_nt(a, b):
    return lax.dot_general(a, b, (((1,), (1,)), ((), ())), preferred_element_type=F32)


def _dot_tn(a, b):
    return lax.dot_general(a, b, (((0,), (0,)), ((), ())), preferred_element_type=F32)


def _rms(x, g):
    ms = jnp.mean(x * x, axis=-1, keepdims=True)
    return x * lax.rsqrt(ms + RMS_EPS) * g


def _silu(x):
    return x * jax.nn.sigmoid(x)


def _group_of_tile(i, tm, n_ctx_rows, rows_per_req):
    nct = n_ctx_rows // tm
    tpr = rows_per_req // tm
    return jnp.where(i < nct, 0, 1 + jnp.maximum(i - nct, 0) // tpr)


def _mod_body(c_ref, w_ref, b_ref, o_ref):
    s = _silu(c_ref[...]).astype(BF16)
    o_ref[...] = _dot(s, w_ref[...].astype(BF16)) + b_ref[...]


def _mod_call(cvec, ada_w, ada_b):
    g, d = cvec.shape
    n = ada_w.shape[1]
    tn = 1024
    return pl.pallas_call(
        _mod_body,
        grid=(n // tn,),
        in_specs=[pl.BlockSpec((g, d), lambda j: (0, 0)),
                  pl.BlockSpec((d, tn), lambda j: (0, j)),
                  pl.BlockSpec((1, tn), lambda j: (0, j))],
        out_specs=pl.BlockSpec((g, tn), lambda j: (0, j)),
        out_shape=jax.ShapeDtypeStruct((g, n), F32),
        compiler_params=_cparams(("arbitrary",)),
        name="mod_table",
    )(cvec, ada_w, ada_b.reshape(1, n))


def _ffn_body(x_ref, mod_ref, npre_ref, npost_ref, wg_ref, wu_ref, wd_ref, o_ref, h_ref, acc_ref, *, mi):
    j = pl.program_id(1)

    @pl.when(j == 0)
    def _():
        shift = mod_ref[0, mi:mi + 1, :]
        scale = mod_ref[0, mi + 1:mi + 2, :]
        h = _rms(x_ref[...], npre_ref[...]) * (1.0 + scale) + shift
        h_ref[...] = h.astype(BF16)
        acc_ref[...] = jnp.zeros_like(acc_ref)

    h = h_ref[...]
    g = _dot(h, wg_ref[...])
    u = _dot(h, wu_ref[...])
    a = (_silu(g) * u).astype(BF16)
    acc_ref[...] += _dot(a, wd_ref[...])

    @pl.when(j == pl.num_programs(1) - 1)
    def _():
        gate = mod_ref[0, mi + 2:mi + 3, :]
        o_ref[...] = x_ref[...] + 0.5 * gate * _rms(acc_ref[...], npost_ref[...])


def _ffn_call(x, mod, npre, npost, wg, wu, wd, *, mi, n_ctx_rows, rows_per_req, tm=512, tf=512):
    m, d = x.shape
    ff = wg.shape[1]
    grp = functools.partial(_group_of_tile, tm=tm, n_ctx_rows=n_ctx_rows, rows_per_req=rows_per_req)
    return pl.pallas_call(
        functools.partial(_ffn_body, mi=mi),
        grid=(m // tm, ff // tf),
        in_specs=[pl.BlockSpec((tm, d), lambda i, j: (i, 0)),
                  pl.BlockSpec((1, N_MOD, d), lambda i, j: (grp(i), 0, 0)),
                  pl.BlockSpec((1, d), lambda i, j: (0, 0)),
                  pl.BlockSpec((1, d), lambda i, j: (0, 0)),
                  pl.BlockSpec((d, tf), lambda i, j: (0, j)),
                  pl.BlockSpec((d, tf), lambda i, j: (0, j)),
                  pl.BlockSpec((tf, d), lambda i, j: (j, 0))],
        out_specs=pl.BlockSpec((tm, d), lambda i, j: (i, 0)),
        out_shape=jax.ShapeDtypeStruct((m, d), F32),
        scratch_shapes=[pltpu.VMEM((tm, d), BF16), pltpu.VMEM((tm, d), F32)],
        compiler_params=_cparams(("parallel", "arbitrary")),
        name="ffn",
    )(x, mod, npre.reshape(1, d), npost.reshape(1, d), wg, wu, wd)


def _inproj_body(x_ref, mod_ref, npre_ref, w_ref, o_ref, h_ref, *, mi):
    @pl.when(pl.program_id(1) == 0)
    def _():
        shift = mod_ref[0, mi:mi + 1, :]
        scale = mod_ref[0, mi + 1:mi + 2, :]
        h = _rms(x_ref[...], npre_ref[...]) * (1.0 + scale) + shift
        h_ref[...] = h.astype(BF16)

    o_ref[...] = _dot(h_ref[...], w_ref[...])


def _inproj_call(x, mod, npre, w, *, mi, n_ctx_rows, rows_per_req, tm=1024, tn=512):
    m, d = x.shape
    n = w.shape[1]
    grp = functools.partial(_group_of_tile, tm=tm, n_ctx_rows=n_ctx_rows, rows_per_req=rows_per_req)
    return pl.pallas_call(
        functools.partial(_inproj_body, mi=mi),
        grid=(m // tm, n // tn),
        in_specs=[pl.BlockSpec((tm, d), lambda i, j: (i, 0)),
                  pl.BlockSpec((1, N_MOD, d), lambda i, j: (grp(i), 0, 0)),
                  pl.BlockSpec((1, d), lambda i, j: (0, 0)),
                  pl.BlockSpec((d, tn), lambda i, j: (0, j))],
        out_specs=pl.BlockSpec((tm, tn), lambda i, j: (i, j)),
        out_shape=jax.ShapeDtypeStruct((m, n), F32),
        scratch_shapes=[pltpu.VMEM((tm, d), BF16)],
        compiler_params=_cparams(("parallel", "arbitrary")),
        name="in_proj",
    )(x, mod, npre.reshape(1, d), w)


def _conv3(u, w):
    n = u.shape[0]
    row = lax.broadcasted_iota(jnp.int32, u.shape, 0)
    prev = jnp.where(row == 0, 0.0, pltpu.roll(u, 1, 0))
    nxt = jnp.where(row == n - 1, 0.0, pltpu.roll(u, n - 1, 0))
    return prev * w[0:1, :] + u * w[1:2, :] + nxt * w[2:3, :]


@functools.lru_cache(maxsize=None)
def _dft_tables_np(n_tok, r):
    big = 2 * n_tok
    k = np.arange(n_tok, dtype=np.int64)
    t = np.arange(n_tok, dtype=np.int64)
    ang = ((k[:, None] * t[None, :]) % big).astype(np.float64) * (2.0 * math.pi / big)
    cosm = np.cos(ang)
    sinm = np.sin(ang)
    sinm[0, :] = 1.0 - 2.0 * (t % 2)
    nj = n_tok // r
    fr = np.concatenate([cosm.reshape(nj, r, n_tok), sinm.reshape(nj, r, n_tok)], axis=1)
    gr = np.ascontiguousarray(np.transpose(fr, (0, 2, 1)))
    return fr.astype(np.float32), gr.astype(np.float32)


def _dft_tables(n_tok, r):
    fr, gr = _dft_tables_np(n_tok, r)
    return jnp.asarray(fr).astype(BF16), jnp.asarray(gr).astype(BF16)


def _spectra_body(feat_ref, w1_ref, b1_ref, w2_ref, b2_ref, w3f_ref, w3b_ref, dec_ref, f_ref,
                  ha_ref, hb_ref, hc_ref, e_ref, o_ref, *, n_tok, r):
    j = pl.program_id(1)

    @pl.when(j == 0)
    def _():
        feats = feat_ref[...]
        h = jnp.sin(HY_SIN_FREQ * (_dot(feats.astype(BF16), w1_ref[...].astype(BF16)) + b1_ref[...]))
        h = jnp.sin(HY_SIN_FREQ * (_dot(h.astype(BF16), w2_ref[...].astype(BF16)) + b2_ref[...]))
        hb16 = h.astype(BF16)
        t = feats[:, 0:1]
        window = jnp.exp(-t * jnp.abs(dec_ref[...])) + HY_WINDOW_SHIFT
        fwd = _dot(hb16, w3f_ref[...].astype(BF16)) * window
        bwd = _dot(hb16, w3b_ref[...].astype(BF16)) * window
        row = lax.broadcasted_iota(jnp.int32, bwd.shape, 0)
        bwd = jnp.where(row == 0, 0.0, bwd)
        norm = jnp.sum(jnp.abs(fwd), axis=0, keepdims=True) + jnp.sum(jnp.abs(bwd), axis=0, keepdims=True)
        fwd = fwd / norm
        bwd = bwd / norm
        e_ref[...] = (fwd + bwd).astype(BF16)
        o_ref[...] = (fwd - bwd).astype(BF16)

    fb = f_ref[0]
    e = e_ref[...]
    p = _dot(fb[:r], e)
    q = _dot(fb[r:], o_ref[...])
    alt = _dot(fb[r:r + 8], e)[0:1, :]
    k = j * r + lax.broadcasted_iota(jnp.int32, p.shape, 0)
    big = 2.0 * n_tok
    wk = jnp.where(k == 0, 1.0 / big, 2.0 / big)
    ha = wk * p
    ha_ref[...] = ha
    hb_ref[...] = jnp.where(k == 0, 0.0, -wk * q)
    hc_ref[...] = jnp.where(k == 0, alt * (1.0 / big), ha)


def _spectra_call(n_tok, w1, b1, w2, b2, w3, decay, fr, *, r, cn=512):
    f32 = jnp.float32
    order, c = decay.shape
    oc = order * c
    hid = w2.shape[0]
    emb = w1.shape[0]
    embp = 128
    idx = jnp.arange(n_tok, dtype=f32)
    t = idx / (n_tok - 1)
    bands = jnp.arange(1, HY_BANDS + 1, dtype=f32)
    ang = (2.0 * math.pi / n_tok) * idx[:, None] * bands[None, :]
    feats = jnp.concatenate([t[:, None], jnp.cos(ang), jnp.sin(ang)], axis=-1)
    feats = jnp.pad(feats, ((0, 0), (0, embp - emb)))
    w1p = jnp.pad(w1, ((0, embp - emb), (0, 0)))
    nj = n_tok // r
    ncb = oc // cn
    out = jax.ShapeDtypeStruct((n_tok, oc), F32)
    return pl.pallas_call(
        functools.partial(_spectra_body, n_tok=n_tok, r=r),
        grid=(ncb, nj),
        in_specs=[pl.BlockSpec((n_tok, embp), lambda cb, j: (0, 0)),
                  pl.BlockSpec((embp, hid), lambda cb, j: (0, 0)),
                  pl.BlockSpec((1, hid), lambda cb, j: (0, 0)),
                  pl.BlockSpec((hid, hid), lambda cb, j: (0, 0)),
                  pl.BlockSpec((1, hid), lambda cb, j: (0, 0)),
                  pl.BlockSpec((hid, cn), lambda cb, j: (0, cb)),
                  pl.BlockSpec((hid, cn), lambda cb, j: (0, ncb + cb)),
                  pl.BlockSpec((1, cn), lambda cb, j: (0, cb)),
                  pl.BlockSpec((1, 2 * r, n_tok), lambda cb, j: (j, 0, 0))],
        out_specs=[pl.BlockSpec((r, cn), lambda cb, j: (j, cb))] * 3,
        out_shape=[out, out, out],
        scratch_shapes=[pltpu.VMEM((n_tok, cn), BF16), pltpu.VMEM((n_tok, cn), BF16)],
        compiler_params=_cparams(("parallel", "arbitrary")),
        name="hyena_spectra",
    )(feats, w1p, b1.reshape(1, hid), w2, b2.reshape(1, hid), w3, w3, decay.reshape(1, oc), fr)


def _hyena_body(v_ref, x1_ref, x2_ref, wv_ref, wx1_ref, wx2_ref, bias_ref, f_ref, g_ref,
                ha_ref, hb_ref, hc_ref, o_ref, uf_ref, ub_ref, acc_ref, *, r, nj):
    j = pl.program_id(2)

    @pl.when(j == 0)
    def _():
        v = _conv3(v_ref[...], wv_ref[...])
        uf_ref[...] = v
        ub_ref[...] = v.astype(BF16)
        acc_ref[...] = jnp.zeros_like(acc_ref)

    x = _dot(f_ref[0], ub_ref[...])
    p = x[:r]
    q = x[r:]
    hb = hb_ref[...]
    top = p * ha_ref[...] + q * hb
    bot = q * hc_ref[...] - p * hb
    z = jnp.concatenate([top, bot], axis=0).astype(BF16)
    acc_ref[...] += _dot(g_ref[0], z)

    @pl.when(j == nj - 1)
    def _():
        x1 = _conv3(x1_ref[...], wx1_ref[...])
        z1 = x1 * (acc_ref[...] + uf_ref[...] * bias_ref[0:1, :])
        uf_ref[...] = z1
        ub_ref[...] = z1.astype(BF16)
        acc_ref[...] = jnp.zeros_like(acc_ref)

    @pl.when(j == 2 * nj - 1)
    def _():
        x2 = _conv3(x2_ref[...], wx2_ref[...])
        o_ref[...] = x2 * (acc_ref[...] + uf_ref[...] * bias_ref[1:2, :])


def _hyena_call(proj, conv_w, bias, fr, gr, ha, hb, hc, *, n_seq, n_tok, row_blk0, c, r, cn=512):
    nj = n_tok // r
    ncb = c // cn
    single = pl.Buffered(1)
    seq_spec = lambda off: pl.BlockSpec((n_tok, cn), lambda b, cb, j: (row_blk0 + b, off * ncb + cb),
                                        pipeline_mode=single)
    w_spec = lambda off: pl.BlockSpec((3, cn), lambda b, cb, j: (0, off * ncb + cb))
    h_spec = pl.BlockSpec((r, cn), lambda b, cb, j: (j % nj, (j // nj) * ncb + cb))
    return pl.pallas_call(
        functools.partial(_hyena_body, r=r, nj=nj),
        grid=(n_seq, ncb, 2 * nj),
        in_specs=[seq_spec(0), seq_spec(1), seq_spec(2), w_spec(0), w_spec(1), w_spec(2),
                  pl.BlockSpec((2, cn), lambda b, cb, j: (0, cb)),
                  pl.BlockSpec((1, 2 * r, n_tok), lambda b, cb, j: (j % nj, 0, 0)),
                  pl.BlockSpec((1, n_tok, 2 * r), lambda b, cb, j: (j % nj, 0, 0)),
                  h_spec, h_spec, h_spec],
        out_specs=pl.BlockSpec((n_tok, cn), lambda b, cb, j: (b, cb)),
        out_shape=jax.ShapeDtypeStruct((n_seq * n_tok, c), F32),
        scratch_shapes=[pltpu.VMEM((n_tok, cn), F32), pltpu.VMEM((n_tok, cn), BF16),
                        pltpu.VMEM((n_tok, cn), F32)],
        compiler_params=_cparams(("parallel", "parallel", "arbitrary")),
        name="hyena_conv",
    )(proj, proj, proj, conv_w, conv_w, conv_w, bias, fr, gr, ha, hb, hc)


def _gdn_prep_body(x_ref, w_ref, o_ref, *, heads_per_blk, dk, n_qk_blk):
    cb = pl.program_id(1)
    y = _silu(_conv3(x_ref[...], w_ref[...]))

    @pl.when(cb >= n_qk_blk)
    def _():
        o_ref[...] = y

    @pl.when(cb < n_qk_blk)
    def _():
        scale = jnp.where(cb < n_qk_blk // 2, dk ** -0.5, 1.0)
        for h in range(heads_per_blk):
            yh = y[:, h * dk:(h + 1) * dk]
            inv = lax.rsqrt(jnp.sum(yh * yh, axis=-1, keepdims=True) + RMS_EPS)
            o_ref[:, h * dk:(h + 1) * dk] = yh * (inv * scale)


def _gdn_prep_call(proj, conv_w, *, n_seq, n_tok, row_blk0, col0, d_gdn, dk, cn=512):
    ncb = 3 * d_gdn // cn
    cb0 = col0 // cn
    return pl.pallas_call(
        functools.partial(_gdn_prep_body, heads_per_blk=cn // dk, dk=dk, n_qk_blk=2 * d_gdn // cn),
        grid=(n_seq, ncb),
        in_specs=[pl.BlockSpec((n_tok, cn), lambda b, cb: (row_blk0 + b, cb0 + cb)),
                  pl.BlockSpec((3, cn), lambda b, cb: (0, cb))],
        out_specs=pl.BlockSpec((n_tok, cn), lambda b, cb: (b, cb)),
        out_shape=jax.ShapeDtypeStruct((n_seq * n_tok, 3 * d_gdn), F32),
        compiler_params=_cparams(("parallel", "parallel")),
        name="gdn_prep",
    )(proj, conv_w)


def _split3(x):
    hi = x.astype(BF16)
    r1 = x - hi.astype(F32)
    mid = r1.astype(BF16)
    lo = (r1 - mid.astype(F32)).astype(BF16)
    return hi, mid, lo


def _gdn_body(qf_ref, kf_ref, vf_ref, abf_ref, abtf_ref, qb_ref, kb_ref, vb_ref, abb_ref, abtb_ref,
              av_ref, dtv_ref, avt_ref, dtvt_ref, s0_ref, of_ref, ob_ref, sfin_ref, s_ref,
              *, heads, dk, chunk):
    n = pl.program_id(1)

    @pl.when(n == 0)
    def _():
        s_ref[...] = s0_ref[0]

    ci = lax.broadcasted_iota(jnp.int32, (chunk, chunk), 0)
    cj = lax.broadcasted_iota(jnp.int32, (chunk, chunk), 1)
    eye = (ci == cj).astype(F32)
    diag_blk = (ci >> 3) == (cj >> 3)
    merge_blks = [((ci >> s) ^ (cj >> s)) == 1 for s in range(3, int(math.log2(chunk)))]
    lower = cj <= ci
    upper = cj >= ci

    units = []
    for dr, (q_ref, k_ref, v_ref, ab_ref, abt_ref, o_ref) in enumerate(
            ((qf_ref, kf_ref, vf_ref, abf_ref, abtf_ref, of_ref),
             (qb_ref, kb_ref, vb_ref, abb_ref, abtb_ref, ob_ref))):
        incl = upper if dr else lower
        strict = (cj > ci) if dr else (cj < ci)
        ones_incl = jnp.where(incl, 1.0, 0.0).astype(BF16)
        ones_incl_t = jnp.where(lower if dr else upper, 1.0, 0.0).astype(BF16)
        ab = ab_ref[...]
        g_col = -av_ref[...] * jax.nn.softplus(ab + dtv_ref[...])
        beta_col = jax.nn.sigmoid(ab)
        abt = abt_ref[...]
        g_row = -avt_ref[...] * jax.nn.softplus(abt + dtvt_ref[...])
        h3 = _split3(g_col)
        gc_col = _dot(ones_incl, h3[0]) + _dot(ones_incl, h3[1]) + _dot(ones_incl, h3[2])
        r3 = _split3(g_row)
        gc_row = _dot(r3[0], ones_incl_t) + _dot(r3[1], ones_incl_t) + _dot(r3[2], ones_incl_t)
        gl_all = jnp.sum(g_col, axis=0, keepdims=True)
        for h in range(heads):
            cg = dr * 2 * heads + h
            cb = cg + heads
            sl = slice(h * dk, (h + 1) * dk)
            units.append(dict(dr=dr, h=h, sl=sl, q_ref=q_ref, k_ref=k_ref, v_ref=v_ref, o_ref=o_ref,
                              incl=incl, strict=strict, gc=gc_col[:, cg:cg + 1], gr=gc_row[cg:cg + 1, :],
                              beta=beta_col[:, cb:cb + 1], gl=gl_all[:, cg:cg + 1]))

    def stage(fn):
        return [fn(u) for u in units]

    bf = lambda t: t.astype(BF16)
    k16 = stage(lambda u: bf(u["k_ref"][:, u["sl"]]))
    decay = stage(lambda u: jnp.where(u["incl"], jnp.exp(jnp.where(u["incl"], u["gc"] - u["gr"], 0.0)), 0.0))
    kbeta = stage(lambda u: u["k_ref"][:, u["sl"]] * u["beta"])
    a_low = [jnp.where(u["strict"], _dot_nt(bf(kb), k) * dc, 0.0)
             for u, kb, k, dc in zip(units, kbeta, k16, decay)]
    attn = [bf(_dot_nt(bf(u["q_ref"][:, u["sl"]]), k) * dc) for u, k, dc in zip(units, k16, decay)]

    d = [jnp.where(diag_blk, a, 0.0) for a in a_low]
    d2 = [_dot(bf(t), bf(t)) for t in d]
    x = [eye - t for t in d]
    x = [t + _dot(bf(t), bf(p)) for t, p in zip(x, d2)]
    d4 = [_dot(bf(p), bf(p)) for p in d2]
    x = [t + _dot(bf(t), bf(p)) for t, p in zip(x, d4)]
    for off_blk in merge_blks:
        x16 = [bf(t) for t in x]
        tmp = [_dot(bf(jnp.where(off_blk, a, 0.0)), t16) for a, t16 in zip(a_low, x16)]
        x = [t - _dot(t16, bf(m)) for t, t16, m in zip(x, x16, tmp)]
    a_hi = [bf(a) for a in a_low]
    a_lo = [bf(a - hi.astype(F32)) for a, hi in zip(a_low, a_hi)]
    for _ in range(NEWTON_STEPS):
        x0 = [bf(t) for t in x]
        res = [eye - t0.astype(F32) - (_dot(hi, t0) + _dot(lo, t0)) for t0, hi, lo in zip(x0, a_hi, a_lo)]
        x = [t0.astype(F32) + _dot(t0, bf(r)) for t0, r in zip(x0, res)]

    e_gc = stage(lambda u: jnp.exp(u["gc"]))
    rhs = [bf(jnp.concatenate([u["v_ref"][:, u["sl"]] * u["beta"], kb * e], axis=-1))
           for u, kb, e in zip(units, kbeta, e_gc)]
    sol = [_dot(bf(t), r) for t, r in zip(x, rhs)]

    s16 = stage(lambda u: bf(s_ref[u["dr"], u["h"]]))
    v_new = [so[:, :dk] - _dot(bf(so[:, dk:]), s) for so, s in zip(sol, s16)]
    vn16 = [bf(t) for t in v_new]
    o = [_dot(bf(u["q_ref"][:, u["sl"]] * e), s) + _dot(at, vn)
         for u, e, s, at, vn in zip(units, e_gc, s16, attn, vn16)]
    for u, t in zip(units, o):
        u["o_ref"][:, u["sl"]] = t
    s_new = [s_ref[u["dr"], u["h"]] * jnp.exp(u["gl"])
             + _dot_tn(bf(u["k_ref"][:, u["sl"]] * jnp.exp(u["gl"] - u["gc"])), vn)
             for u, vn in zip(units, vn16)]
    for u, t in zip(units, s_new):
        s_ref[u["dr"], u["h"]] = t

    @pl.when(n == pl.num_programs(1) - 1)
    def _():
        sfin_ref[0] = s_ref[...]


def _gdn_call(qkv, ab_src, abt, av, dtv, avt, dtvt, s0, *, n_seq, n_tok, ab_row_blk0, ab_col_blk,
              heads, dk, chunk):
    nc = n_tok // chunk
    d = heads * dk
    fwd = lambda b, n: b * nc + n
    bwd = lambda b, n: b * nc + nc - 1 - n
    qkv_spec = lambda idx, which: pl.BlockSpec((chunk, d), lambda b, n: (idx(b, n), which))
    ab_spec = lambda idx: pl.BlockSpec((chunk, 128), lambda b, n: (ab_row_blk0 + idx(b, n), ab_col_blk))
    abt_spec = lambda idx: pl.BlockSpec((4 * heads, chunk), lambda b, n: (0, idx(b, n)))
    small = lambda shape: pl.BlockSpec(shape, lambda b, n: (0, 0))
    st_spec = pl.BlockSpec((1, 2, heads, dk, dk), lambda b, n: (b, 0, 0, 0, 0))
    o_shape = jax.ShapeDtypeStruct((n_seq * n_tok, d), F32)
    return pl.pallas_call(
        functools.partial(_gdn_body, heads=heads, dk=dk, chunk=chunk),
        grid=(n_seq, nc),
        in_specs=[qkv_spec(fwd, 0), qkv_spec(fwd, 1), qkv_spec(fwd, 2), ab_spec(fwd), abt_spec(fwd),
                  qkv_spec(bwd, 0), qkv_spec(bwd, 1), qkv_spec(bwd, 2), ab_spec(bwd), abt_spec(bwd),
                  small((1, 128)), small((1, 128)), small((4 * heads, 1)), small((4 * heads, 1)),
                  st_spec],
        out_specs=[pl.BlockSpec((chunk, d), lambda b, n: (fwd(b, n), 0)),
                   pl.BlockSpec((chunk, d), lambda b, n: (bwd(b, n), 0)),
                   st_spec],
        out_shape=[o_shape, o_shape, jax.ShapeDtypeStruct((n_seq, 2, heads, dk, dk), F32)],
        scratch_shapes=[pltpu.VMEM((2, heads, dk, dk), F32)],
        compiler_params=_cparams(("parallel", "arbitrary")),
        name="gdn_scan",
    )(qkv, qkv, qkv, ab_src, abt, qkv, qkv, qkv, ab_src, abt, av, dtv, avt, dtvt, s0)


def _outproj_body(x_ref, mod_ref, zhy_ref, of_ref, ob_ref, zg_ref, ghy_ref, go_ref, npost_ref, w_ref,
                  o_ref, *, mi, heads, dk):
    yhy = _rms(zhy_ref[...], ghy_ref[...]).astype(BF16)
    o = of_ref[...] + ob_ref[...]
    zg = zg_ref[...]
    parts = [yhy]
    for h in range(heads):
        sl = slice(h * dk, (h + 1) * dk)
        parts.append((_rms(o[:, sl], go_ref[...]) * _silu(zg[:, sl])).astype(BF16))
    y = _dot(jnp.concatenate(parts, axis=-1), w_ref[...])
    gate = mod_ref[0, mi:mi + 1, :]
    o_ref[...] = x_ref[...] + gate * _rms(y, npost_ref[...])


def _outproj_call(x, mod, zhy, o_f, o_b, proj, ghy, go, npost, w, *, mi, zg_col_blk, n_ctx_rows, rows_per_req,
                  heads, dk, tm=512):
    m, d = x.shape
    c = zhy.shape[1]
    dg = heads * dk
    grp = functools.partial(_group_of_tile, tm=tm, n_ctx_rows=n_ctx_rows, rows_per_req=rows_per_req)
    row = lambda width: pl.BlockSpec((tm, width), lambda i: (i, 0))
    return pl.pallas_call(
        functools.partial(_outproj_body, mi=mi, heads=heads, dk=dk),
        grid=(m // tm,),
        in_specs=[row(d),
                  pl.BlockSpec((1, N_MOD, d), lambda i: (grp(i), 0, 0)),
                  row(c), row(dg), row(dg),
                  pl.BlockSpec((tm, dg), lambda i: (i, zg_col_blk)),
                  pl.BlockSpec((1, c), lambda i: (0, 0)),
                  pl.BlockSpec((1, dk), lambda i: (0, 0)),
                  pl.BlockSpec((1, d), lambda i: (0, 0)),
                  pl.BlockSpec((c + dg, d), lambda i: (0, 0))],
        out_specs=row(d),
        out_shape=jax.ShapeDtypeStruct((m, d), F32),
        compiler_params=_cparams(("parallel",)),
        name="out_proj",
    )(x, mod, zhy, o_f, o_b, proj, ghy.reshape(1, c), go.reshape(1, dk), npost.reshape(1, d), w)


def _grid_pos_embed(n_tok, dim):
    rows = n_tok // GRID_W
    r = jnp.broadcast_to(jnp.arange(rows, dtype=F32)[:, None], (rows, GRID_W)).reshape(-1)
    col = jnp.broadcast_to(jnp.arange(GRID_W, dtype=F32)[None, :], (rows, GRID_W)).reshape(-1)
    quarter = dim // 4
    omega = 1.0 / (POS_BASE ** (jnp.arange(quarter, dtype=F32) / quarter))
    ar = r[:, None] * omega[None]
    ac = col[:, None] * omega[None]
    return jnp.concatenate([jnp.sin(ar), jnp.cos(ar), jnp.sin(ac), jnp.cos(ac)], axis=-1)


def _mixer(proj, *, groups, hy_conv_w, hy_f, hy_decay, hy_bias, gdn_conv_w, gdn_a_log, gdn_dt_bias,
           c_hy, d_gdn, states):
    heads = GDN_HEADS
    dk = d_gdn // heads
    o0 = 3 * c_hy
    o2 = o0 + 4 * d_gdn
    ab_col_blk = o2 // 128

    a_exp = jnp.exp(gdn_a_log.astype(F32))
    zeros = jnp.zeros_like(a_exp)
    av32 = jnp.stack([a_exp, zeros], axis=1).reshape(-1)
    dtv32 = jnp.stack([gdn_dt_bias.astype(F32), zeros], axis=1).reshape(-1)
    pad = 128 - 4 * heads
    av = jnp.pad(av32, (0, pad)).reshape(1, 128)
    dtv = jnp.pad(dtv32, (0, pad)).reshape(1, 128)
    avt = av32.reshape(4 * heads, 1)
    dtvt = dtv32.reshape(4 * heads, 1)

    zs, ofs, obs, sfs = [], [], [], []
    for (n_seq, n_tok, row0), s0 in zip(groups, states):
        r = min(256, n_tok)
        fr, gr = _dft_tables(n_tok, r)
        ha, hb, hc = _spectra_call(n_tok, *hy_f, hy_decay, fr, r=r)
        zs.append(_hyena_call(proj, hy_conv_w, hy_bias, fr, gr, ha, hb, hc, n_seq=n_seq, n_tok=n_tok,
                              row_blk0=row0 // n_tok, c=c_hy, r=r))
        qkv = _gdn_prep_call(proj, gdn_conv_w, n_seq=n_seq, n_tok=n_tok, row_blk0=row0 // n_tok, col0=o0,
                             d_gdn=d_gdn, dk=dk)
        rows = n_seq * n_tok
        abt = jnp.transpose(lax.slice(proj, (row0, o2), (row0 + rows, o2 + 4 * heads)))
        o_f, o_b, s_fin = _gdn_call(qkv, proj, abt, av, dtv, avt, dtvt, s0, n_seq=n_seq, n_tok=n_tok,
                                    ab_row_blk0=row0 // GDN_CHUNK, ab_col_blk=ab_col_blk, heads=heads, dk=dk,
                                    chunk=GDN_CHUNK)
        ofs.append(o_f)
        obs.append(o_b)
        sfs.append(s_fin)
    return jnp.concatenate(zs, 0), jnp.concatenate(ofs, 0), jnp.concatenate(obs, 0), sfs


def kernel(x_prompt, x_sample, state_gdn, c, c_ctx, ada_w, ada_b, norm_pre, norm_post, ffn_wg, ffn_wu, ffn_wd,
           w_in, w_out, hy_conv_w, hy_f_w1, hy_f_b1, hy_f_w2, hy_f_b2, hy_f_w3, hy_decay, hy_bias, hy_out_norm,
           gdn_conv_w, gdn_a_log, gdn_dt_bias, gdn_o_norm):
    nb, seq, d = x_prompt.shape
    db, dseq, _ = x_sample.shape
    depth = ada_w.shape[0]
    c_hy = hy_decay.shape[-1]
    d_gdn = gdn_conv_w.shape[-1] // 3
    heads = GDN_HEADS
    dk = d_gdn // heads
    n_ctx_rows = nb * seq
    grp = dict(n_ctx_rows=n_ctx_rows, rows_per_req=dseq)

    xs = x_sample + _grid_pos_embed(dseq, d).astype(x_sample.dtype)[None]
    x = jnp.concatenate([x_prompt.reshape(n_ctx_rows, d), xs.reshape(db * dseq, d)], axis=0)
    cvec = jnp.concatenate([c_ctx[None], c, jnp.zeros((8 - 1 - db, d), F32)], axis=0)

    ctx_states = []
    for l in range(depth):
        mod = _mod_call(cvec, ada_w[l], ada_b[l]).reshape(8, N_MOD, d)
        wg = ffn_wg[l].astype(BF16)
        wu = ffn_wu[l].astype(BF16)
        wd = ffn_wd[l].astype(BF16)
        ncols = w_in.shape[-1]
        ncols_p = -(-(ncols + 96) // 512) * 512
        w_in_p = jnp.pad(w_in[l], ((0, 0), (0, ncols_p - ncols))).astype(BF16)

        x = _ffn_call(x, mod, norm_pre[l, 0], norm_post[l, 0], wg[0], wu[0], wd[0], mi=0, **grp)

        proj = _inproj_call(x, mod, norm_pre[l, 1], w_in_p, mi=3, **grp)
        s0_ctx = jnp.zeros((nb, 2, heads, dk, dk), F32)
        s0_lat = state_gdn[:, l].astype(F32)
        zhy, o_f, o_b, sfs = _mixer(
            proj, groups=[(nb, seq, 0), (db, dseq, n_ctx_rows)], hy_conv_w=hy_conv_w[l],
            hy_f=(hy_f_w1[l], hy_f_b1[l], hy_f_w2[l], hy_f_b2[l], hy_f_w3[l]), hy_decay=hy_decay[l],
            hy_bias=hy_bias[l], gdn_conv_w=gdn_conv_w[l], gdn_a_log=gdn_a_log[l], gdn_dt_bias=gdn_dt_bias[l],
            c_hy=c_hy, d_gdn=d_gdn, states=[s0_ctx, s0_lat])
        ctx_states.append(sfs[0])
        x = _outproj_call(x, mod, zhy, o_f, o_b, proj, hy_out_norm[l], gdn_o_norm[l], norm_post[l, 1],
                          w_out[l].astype(BF16), mi=5, zg_col_blk=(3 * c_hy + 3 * d_gdn) // d_gdn, heads=heads,
                          dk=dk, **grp)

        x = _ffn_call(x, mod, norm_pre[l, 2], norm_post[l, 2], wg[1], wu[1], wd[1], mi=6, **grp)

    y_prompt = x[:n_ctx_rows].reshape(nb, seq, d)
    y_sample = x[n_ctx_rows:].reshape(db, dseq, d)
    return y_prompt, y_sample, jnp.stack(ctx_states, axis=1)
```

```python
import functools
import math

import numpy as np
import jax
import jax.numpy as jnp
from jax import lax
from jax.experimental import pallas as pl
from jax.experimental.pallas import tpu as pltpu

F32 = jnp.float32
BF16 = jnp.bfloat16

N_MOD = 9
RMS_EPS = 1e-6
GRID_W = 64
POS_BASE = 10000.0
HY_BANDS = 16
HY_SIN_FREQ = 1.0
HY_WINDOW_SHIFT = 0.05
GDN_HEADS = 8
GDN_CHUNK = 128
NEWTON_STEPS = 1
VMEM_LIMIT = 56 * 1024 * 1024


def _cparams(sem):
    return pltpu.CompilerParams(dimension_semantics=sem, vmem_limit_bytes=VMEM_LIMIT)


def _dot(a, b):
    return jnp.dot(a, b, preferred_element_type=F32)


def _dot_nt(a, b):
    return lax.dot_general(a, b, (((1,), (1,)), ((), ())), preferred_element_type=F32)


def _dot_tn(a, b):
    return lax.dot_general(a, b, (((0,), (0,)), ((), ())), preferred_element_type=F32)


def _rms(x, g):
    ms = jnp.mean(x * x, axis=-1, keepdims=True)
    return x * lax.rsqrt(ms + RMS_EPS) * g


def _silu(x):
    return x * jax.nn.sigmoid(x)


def _group_of_tile(i, tm, n_ctx_rows, rows_per_req):
    nct = n_ctx_rows // tm
    tpr = rows_per_req // tm
    return jnp.where(i < nct, 0, 1 + jnp.maximum(i - nct, 0) // tpr)


def _mod_body(c_ref, w_ref, b_ref, o_ref):
    s = _silu(c_ref[...]).astype(BF16)
    o_ref[...] = _dot(s, w_ref[...].astype(BF16)) + b_ref[...]


def _mod_call(cvec, ada_w, ada_b):
    g, d = cvec.shape
    n = ada_w.shape[1]
    tn = 1024
    return pl.pallas_call(
        _mod_body,
        grid=(n // tn,),
        in_specs=[pl.BlockSpec((g, d), lambda j: (0, 0)),
                  pl.BlockSpec((d, tn), lambda j: (0, j)),
                  pl.BlockSpec((1, tn), lambda j: (0, j))],
        out_specs=pl.BlockSpec((g, tn), lambda j: (0, j)),
        out_shape=jax.ShapeDtypeStruct((g, n), F32),
        compiler_params=_cparams(("arbitrary",)),
        name="mod_table",
    )(cvec, ada_w, ada_b.reshape(1, n))


def _ffn_body(*refs, mi, has_pos, has_base):
    refs = list(refs)
    x_ref = refs.pop(0)
    pos_ref = refs.pop(0) if has_pos else None
    if has_base:
        refs.pop(0)
    mod_ref, npre_ref, npost_ref, wg_ref, wu_ref, wd_ref, o_ref, h_ref, acc_ref = refs
    j = pl.program_id(1)

    def x_in():
        return x_ref[...] + pos_ref[...] if has_pos else x_ref[...]

    @pl.when(j == 0)
    def _():
        shift = mod_ref[0, mi:mi + 1, :]
        scale = mod_ref[0, mi + 1:mi + 2, :]
        h = _rms(x_in(), npre_ref[...]) * (1.0 + scale) + shift
        h_ref[...] = h.astype(BF16)
        acc_ref[...] = jnp.zeros_like(acc_ref)

    h = h_ref[...]
    g = _dot(h, wg_ref[...])
    u = _dot(h, wu_ref[...])
    a = (_silu(g) * u).astype(BF16)
    acc_ref[...] += _dot(a, wd_ref[...])

    @pl.when(j == pl.num_programs(1) - 1)
    def _():
        gate = mod_ref[0, mi + 2:mi + 3, :]
        o_ref[...] = x_in() + 0.5 * gate * _rms(acc_ref[...], npost_ref[...])


def _ffn_call(x, mod, npre, npost, wg, wu, wd, *, mi, n_ctx_rows, rows_per_req, x_row0=0, n_rows=None,
              g_row0=0, out_rows=None, out_row0=0, pos=None, base=None, tm=512, tf=512):
    d = x.shape[1]
    n_rows = x.shape[0] if n_rows is None else n_rows
    out_rows = n_rows if out_rows is None else out_rows
    ff = wg.shape[1]
    grp = functools.partial(_group_of_tile, tm=tm, n_ctx_rows=n_ctx_rows, rows_per_req=rows_per_req)
    xb0, gb0, ob0, tpr = x_row0 // tm, g_row0 // tm, out_row0 // tm, rows_per_req // tm
    const = lambda shape: pl.BlockSpec(shape, lambda i, j: (0, 0))
    in_specs = [pl.BlockSpec((tm, d), lambda i, j: (xb0 + i, 0))]
    args = [x]
    if pos is not None:
        in_specs.append(pl.BlockSpec((tm, d), lambda i, j: (i % tpr, 0)))
        args.append(pos)
    aliases = {}
    if base is not None:
        aliases = {len(args): 0}
        in_specs.append(pl.BlockSpec(memory_space=pl.ANY))
        args.append(base)
    in_specs += [pl.BlockSpec((1, N_MOD, d), lambda i, j: (grp(gb0 + i), 0, 0)),
                 const((1, d)), const((1, d)),
                 pl.BlockSpec((d, tf), lambda i, j: (0, j)),
                 pl.BlockSpec((d, tf), lambda i, j: (0, j)),
                 pl.BlockSpec((tf, d), lambda i, j: (j, 0))]
    args += [mod, npre.reshape(1, d), npost.reshape(1, d), wg, wu, wd]
    return pl.pallas_call(
        functools.partial(_ffn_body, mi=mi, has_pos=pos is not None, has_base=base is not None),
        grid=(n_rows // tm, ff // tf),
        in_specs=in_specs,
        out_specs=pl.BlockSpec((tm, d), lambda i, j: (ob0 + i, 0)),
        out_shape=jax.ShapeDtypeStruct((out_rows, d), F32),
        scratch_shapes=[pltpu.VMEM((tm, d), BF16), pltpu.VMEM((tm, d), F32)],
        input_output_aliases=aliases,
        compiler_params=_cparams(("parallel", "arbitrary")),
        name="ffn",
    )(*args)


def _inproj_body(x_ref, mod_ref, npre_ref, w_ref, o_ref, h_ref, *, mi):
    @pl.when(pl.program_id(1) == 0)
    def _():
        shift = mod_ref[0, mi:mi + 1, :]
        scale = mod_ref[0, mi + 1:mi + 2, :]
        h = _rms(x_ref[...], npre_ref[...]) * (1.0 + scale) + shift
        h_ref[...] = h.astype(BF16)

    o_ref[...] = _dot(h_ref[...], w_ref[...])


def _inproj_call(x, mod, npre, w, *, mi, n_ctx_rows, rows_per_req, tm=1024, tn=512):
    m, d = x.shape
    n = w.shape[1]
    grp = functools.partial(_group_of_tile, tm=tm, n_ctx_rows=n_ctx_rows, rows_per_req=rows_per_req)
    return pl.pallas_call(
        functools.partial(_inproj_body, mi=mi),
        grid=(m // tm, n // tn),
        in_specs=[pl.BlockSpec((tm, d), lambda i, j: (i, 0)),
                  pl.BlockSpec((1, N_MOD, d), lambda i, j: (grp(i), 0, 0)),
                  pl.BlockSpec((1, d), lambda i, j: (0, 0)),
                  pl.BlockSpec((d, tn), lambda i, j: (0, j))],
        out_specs=pl.BlockSpec((tm, tn), lambda i, j: (i, j)),
        out_shape=jax.ShapeDtypeStruct((m, n), F32),
        scratch_shapes=[pltpu.VMEM((tm, d), BF16)],
        compiler_params=_cparams(("parallel", "arbitrary")),
        name="in_proj",
    )(x, mod, npre.reshape(1, d), w)


def _conv3(u, w):
    n = u.shape[0]
    row = lax.broadcasted_iota(jnp.int32, u.shape, 0)
    prev = jnp.where(row == 0, 0.0, pltpu.roll(u, 1, 0))
    nxt = jnp.where(row == n - 1, 0.0, pltpu.roll(u, n - 1, 0))
    return prev * w[0:1, :] + u * w[1:2, :] + nxt * w[2:3, :]


@functools.lru_cache(maxsize=None)
def _dft_tables_np(n_tok, r):
    big = 2 * n_tok
    k = np.arange(n_tok, dtype=np.int64)
    t = np.arange(n_tok, dtype=np.int64)
    ang = ((k[:, None] * t[None, :]) % big).astype(np.float64) * (2.0 * math.pi / big)
    cosm = np.cos(ang)
    sinm = np.sin(ang)
    sinm[0, :] = 1.0 - 2.0 * (t % 2)
    nj = n_tok // r
    fr = np.concatenate([cosm.reshape(nj, r, n_tok), sinm.reshape(nj, r, n_tok)], axis=1)
    gr = np.ascontiguousarray(np.transpose(fr, (0, 2, 1)))
    return fr.astype(np.float32), gr.astype(np.float32)


def _dft_tables(n_tok, r):
    fr, gr = _dft_tables_np(n_tok, r)
    return jnp.asarray(fr).astype(BF16), jnp.asarray(gr).astype(BF16)


def _spectra_body(feat_ref, w1_ref, b1_ref, w2_ref, b2_ref, w3f_ref, w3b_ref, dec_ref, f_ref,
                  ha_ref, hb_ref, hc_ref, e_ref, o_ref, *, n_tok, r):
    j = pl.program_id(1)

    @pl.when(j == 0)
    def _():
        feats = feat_ref[...]
        h = jnp.sin(HY_SIN_FREQ * (_dot(feats.astype(BF16), w1_ref[...].astype(BF16)) + b1_ref[...]))
        h = jnp.sin(HY_SIN_FREQ * (_dot(h.astype(BF16), w2_ref[...].astype(BF16)) + b2_ref[...]))
        hb16 = h.astype(BF16)
        t = feats[:, 0:1]
        window = jnp.exp(-t * jnp.abs(dec_ref[...])) + HY_WINDOW_SHIFT
        fwd = _dot(hb16, w3f_ref[...].astype(BF16)) * window
        bwd = _dot(hb16, w3b_ref[...].astype(BF16)) * window
        row = lax.broadcasted_iota(jnp.int32, bwd.shape, 0)
        bwd = jnp.where(row == 0, 0.0, bwd)
        norm = jnp.sum(jnp.abs(fwd), axis=0, keepdims=True) + jnp.sum(jnp.abs(bwd), axis=0, keepdims=True)
        fwd = fwd / norm
        bwd = bwd / norm
        e_ref[...] = (fwd + bwd).astype(BF16)
        o_ref[...] = (fwd - bwd).astype(BF16)

    fb = f_ref[0]
    e = e_ref[...]
    p = _dot(fb[:r], e)
    q = _dot(fb[r:], o_ref[...])
    alt = _dot(fb[r:r + 8], e)[0:1, :]
    k = j * r + lax.broadcasted_iota(jnp.int32, p.shape, 0)
    big = 2.0 * n_tok
    wk = jnp.where(k == 0, 1.0 / big, 2.0 / big)
    ha = wk * p
    ha_ref[...] = ha
    hb_ref[...] = jnp.where(k == 0, 0.0, -wk * q)
    hc_ref[...] = jnp.where(k == 0, alt * (1.0 / big), ha)


def _spectra_call(n_tok, w1, b1, w2, b2, w3, decay, fr, *, r, cn=512):
    f32 = jnp.float32
    order, c = decay.shape
    oc = order * c
    hid = w2.shape[0]
    emb = w1.shape[0]
    embp = 128
    idx = jnp.arange(n_tok, dtype=f32)
    t = idx / (n_tok - 1)
    bands = jnp.arange(1, HY_BANDS + 1, dtype=f32)
    ang = (2.0 * math.pi / n_tok) * idx[:, None] * bands[None, :]
    feats = jnp.concatenate([t[:, None], jnp.cos(ang), jnp.sin(ang)], axis=-1)
    feats = jnp.pad(feats, ((0, 0), (0, embp - emb)))
    w1p = jnp.pad(w1, ((0, embp - emb), (0, 0)))
    nj = n_tok // r
    ncb = oc // cn
    out = jax.ShapeDtypeStruct((n_tok, oc), F32)
    return pl.pallas_call(
        functools.partial(_spectra_body, n_tok=n_tok, r=r),
        grid=(ncb, nj),
        in_specs=[pl.BlockSpec((n_tok, embp), lambda cb, j: (0, 0)),
                  pl.BlockSpec((embp, hid), lambda cb, j: (0, 0)),
                  pl.BlockSpec((1, hid), lambda cb, j: (0, 0)),
                  pl.BlockSpec((hid, hid), lambda cb, j: (0, 0)),
                  pl.BlockSpec((1, hid), lambda cb, j: (0, 0)),
                  pl.BlockSpec((hid, cn), lambda cb, j: (0, cb)),
                  pl.BlockSpec((hid, cn), lambda cb, j: (0, ncb + cb)),
                  pl.BlockSpec((1, cn), lambda cb, j: (0, cb)),
                  pl.BlockSpec((1, 2 * r, n_tok), lambda cb, j: (j, 0, 0))],
        out_specs=[pl.BlockSpec((r, cn), lambda cb, j: (j, cb))] * 3,
        out_shape=[out, out, out],
        scratch_shapes=[pltpu.VMEM((n_tok, cn), BF16), pltpu.VMEM((n_tok, cn), BF16)],
        compiler_params=_cparams(("parallel", "arbitrary")),
        name="hyena_spectra",
    )(feats, w1p, b1.reshape(1, hid), w2, b2.reshape(1, hid), w3, w3, decay.reshape(1, oc), fr)


def _hyena_body(*refs, r, nj, has_base):
    refs = list(refs)
    if has_base:
        refs.pop(0)
    (v_ref, x1_ref, x2_ref, wv_ref, wx1_ref, wx2_ref, bias_ref, f_ref, g_ref,
     ha_ref, hb_ref, hc_ref, o_ref, uf_ref, ub_ref, acc_ref) = refs
    j = pl.program_id(2)

    @pl.when(j == 0)
    def _():
        v = _conv3(v_ref[...], wv_ref[...])
        uf_ref[...] = v
        ub_ref[...] = v.astype(BF16)
        acc_ref[...] = jnp.zeros_like(acc_ref)

    x = _dot(f_ref[0], ub_ref[...])
    p = x[:r]
    q = x[r:]
    hb = hb_ref[...]
    top = p * ha_ref[...] + q * hb
    bot = q * hc_ref[...] - p * hb
    z = jnp.concatenate([top, bot], axis=0).astype(BF16)
    acc_ref[...] += _dot(g_ref[0], z)

    @pl.when(j == nj - 1)
    def _():
        x1 = _conv3(x1_ref[...], wx1_ref[...])
        z1 = x1 * (acc_ref[...] + uf_ref[...] * bias_ref[0:1, :])
        uf_ref[...] = z1
        ub_ref[...] = z1.astype(BF16)
        acc_ref[...] = jnp.zeros_like(acc_ref)

    @pl.when(j == 2 * nj - 1)
    def _():
        x2 = _conv3(x2_ref[...], wx2_ref[...])
        o_ref[...] = x2 * (acc_ref[...] + uf_ref[...] * bias_ref[1:2, :])


def _hyena_call(proj, conv_w, bias, fr, gr, ha, hb, hc, *, n_seq, n_tok, row0, out_rows, base, c, r, cn=512):
    nj = n_tok // r
    ncb = c // cn
    rb0 = row0 // n_tok
    single = pl.Buffered(1)
    seq_spec = lambda off: pl.BlockSpec((n_tok, cn), lambda b, cb, j: (rb0 + b, off * ncb + cb),
                                        pipeline_mode=single)
    w_spec = lambda off: pl.BlockSpec((3, cn), lambda b, cb, j: (0, off * ncb + cb))
    h_spec = pl.BlockSpec((r, cn), lambda b, cb, j: (j % nj, (j // nj) * ncb + cb))
    in_specs = [seq_spec(0), seq_spec(1), seq_spec(2), w_spec(0), w_spec(1), w_spec(2),
                pl.BlockSpec((2, cn), lambda b, cb, j: (0, cb)),
                pl.BlockSpec((1, 2 * r, n_tok), lambda b, cb, j: (j % nj, 0, 0)),
                pl.BlockSpec((1, n_tok, 2 * r), lambda b, cb, j: (j % nj, 0, 0)),
                h_spec, h_spec, h_spec]
    args = [proj, proj, proj, conv_w, conv_w, conv_w, bias, fr, gr, ha, hb, hc]
    aliases = {}
    if base is not None:
        in_specs.insert(0, pl.BlockSpec(memory_space=pl.ANY))
        args.insert(0, base)
        aliases = {0: 0}
    return pl.pallas_call(
        functools.partial(_hyena_body, r=r, nj=nj, has_base=base is not None),
        grid=(n_seq, ncb, 2 * nj),
        in_specs=in_specs,
        out_specs=pl.BlockSpec((n_tok, cn), lambda b, cb, j: (rb0 + b, cb)),
        out_shape=jax.ShapeDtypeStruct((out_rows, c), F32),
        scratch_shapes=[pltpu.VMEM((n_tok, cn), F32), pltpu.VMEM((n_tok, cn), BF16),
                        pltpu.VMEM((n_tok, cn), F32)],
        input_output_aliases=aliases,
        compiler_params=_cparams(("parallel", "parallel", "arbitrary")),
        name="hyena_conv",
    )(*args)


def _conv3_rows(x_ref, w, r0, nr):
    n = x_ref.shape[0]
    lo = max(r0 - 8, 0)
    hi = min(r0 + nr + 8, n)
    slab = x_ref[lo:hi, :]
    prev = pltpu.roll(slab, 1, 0)[r0 - lo:r0 - lo + nr]
    nxt = pltpu.roll(slab, hi - lo - 1, 0)[r0 - lo:r0 - lo + nr]
    mid = slab[r0 - lo:r0 - lo + nr]
    row = lax.broadcasted_iota(jnp.int32, mid.shape, 0)
    if r0 == 0:
        prev = jnp.where(row == 0, 0.0, prev)
    if r0 + nr == n:
        nxt = jnp.where(row == nr - 1, 0.0, nxt)
    return prev * w[0:1, :] + mid * w[1:2, :] + nxt * w[2:3, :]


def _gdn_prep_body(x_ref, w_ref, o_ref, *, heads_per_blk, dk, n_qk_blk, rows):
    cb = pl.program_id(1)
    n = x_ref.shape[0]
    w = w_ref[...]
    is_v = cb >= n_qk_blk
    scale = jnp.where(cb < n_qk_blk // 2, dk ** -0.5, 1.0)
    for r0 in range(0, n, rows):
        y = _silu(_conv3_rows(x_ref, w, r0, rows))
        for h in range(heads_per_blk):
            yh = y[:, h * dk:(h + 1) * dk]
            inv = lax.rsqrt(jnp.sum(yh * yh, axis=-1, keepdims=True) + RMS_EPS) * scale
            o_ref[r0:r0 + rows, h * dk:(h + 1) * dk] = yh * jnp.where(is_v, 1.0, inv)


def _gdn_prep_call(proj, conv_w, *, n_seq, n_tok, row_blk0, col0, d_gdn, dk, cn=512):
    ncb = 3 * d_gdn // cn
    cb0 = col0 // cn
    return pl.pallas_call(
        functools.partial(_gdn_prep_body, heads_per_blk=cn // dk, dk=dk, n_qk_blk=2 * d_gdn // cn,
                          rows=min(64, n_tok)),
        grid=(n_seq, ncb),
        in_specs=[pl.BlockSpec((n_tok, cn), lambda b, cb: (row_blk0 + b, cb0 + cb)),
                  pl.BlockSpec((3, cn), lambda b, cb: (0, cb))],
        out_specs=pl.BlockSpec((n_tok, cn), lambda b, cb: (b, cb)),
        out_shape=jax.ShapeDtypeStruct((n_seq * n_tok, 3 * d_gdn), F32),
        compiler_params=_cparams(("parallel", "parallel")),
        name="gdn_prep",
    )(proj, conv_w)


def _split3(x):
    hi = x.astype(BF16)
    r1 = x - hi.astype(F32)
    mid = r1.astype(BF16)
    lo = (r1 - mid.astype(F32)).astype(BF16)
    return hi, mid, lo


def _gdn_body(*refs, heads, dk, chunk, has_base, has_s0):
    refs = list(refs)
    if has_base:
        del refs[:2]
    (qf_ref, kf_ref, vf_ref, abf_ref, abtf_ref, qb_ref, kb_ref, vb_ref, abb_ref, abtb_ref,
     av_ref, dtv_ref, avt_ref, dtvt_ref) = refs[:14]
    s0_ref = refs[14] if has_s0 else None
    of_ref, ob_ref, sfin_ref, s_ref = refs[-4:]
    n = pl.program_id(1)

    @pl.when(n == 0)
    def _():
        s_ref[...] = s0_ref[0] if has_s0 else jnp.zeros_like(s_ref)

    ci = lax.broadcasted_iota(jnp.int32, (chunk, chunk), 0)
    cj = lax.broadcasted_iota(jnp.int32, (chunk, chunk), 1)
    eye = (ci == cj).astype(F32)
    diag_blk = (ci >> 3) == (cj >> 3)
    merge_blks = [((ci >> s) ^ (cj >> s)) == 1 for s in range(3, int(math.log2(chunk)))]
    lower = cj <= ci
    upper = cj >= ci

    units = []
    for dr, (q_ref, k_ref, v_ref, ab_ref, abt_ref, o_ref) in enumerate(
            ((qf_ref, kf_ref, vf_ref, abf_ref, abtf_ref, of_ref),
             (qb_ref, kb_ref, vb_ref, abb_ref, abtb_ref, ob_ref))):
        incl = upper if dr else lower
        strict = (cj > ci) if dr else (cj < ci)
        ones_incl = jnp.where(incl, 1.0, 0.0).astype(BF16)
        ones_incl_t = jnp.where(lower if dr else upper, 1.0, 0.0).astype(BF16)
        ab = ab_ref[...]
        g_col = -av_ref[...] * jax.nn.softplus(ab + dtv_ref[...])
        beta_col = jax.nn.sigmoid(ab)
        abt = abt_ref[...]
        g_row = -avt_ref[...] * jax.nn.softplus(abt + dtvt_ref[...])
        h3 = _split3(g_col)
        gc_col = _dot(ones_incl, h3[0]) + _dot(ones_incl, h3[1]) + _dot(ones_incl, h3[2])
        r3 = _split3(g_row)
        gc_row = _dot(r3[0], ones_incl_t) + _dot(r3[1], ones_incl_t) + _dot(r3[2], ones_incl_t)
        gl_all = jnp.sum(g_col, axis=0, keepdims=True)
        for h in range(heads):
            cg = dr * 2 * heads + h
            cb = cg + heads
            sl = slice(h * dk, (h + 1) * dk)
            units.append(dict(dr=dr, h=h, sl=sl, q_ref=q_ref, k_ref=k_ref, v_ref=v_ref, o_ref=o_ref,
                              incl=incl, strict=strict, gc=gc_col[:, cg:cg + 1], gr=gc_row[cg:cg + 1, :],
                              beta=beta_col[:, cb:cb + 1], gl=gl_all[:, cg:cg + 1]))

    def stage(fn):
        return [fn(u) for u in units]

    bf = lambda t: t.astype(BF16)
    k16 = stage(lambda u: bf(u["k_ref"][:, u["sl"]]))
    decay = stage(lambda u: jnp.where(u["incl"], jnp.exp(jnp.where(u["incl"], u["gc"] - u["gr"], 0.0)), 0.0))
    kbeta = stage(lambda u: u["k_ref"][:, u["sl"]] * u["beta"])
    a_low = [jnp.where(u["strict"], _dot_nt(bf(kb), k) * dc, 0.0)
             for u, kb, k, dc in zip(units, kbeta, k16, decay)]
    attn = [bf(_dot_nt(bf(u["q_ref"][:, u["sl"]]), k) * dc) for u, k, dc in zip(units, k16, decay)]

    d = [jnp.where(diag_blk, a, 0.0) for a in a_low]
    d2 = [_dot(bf(t), bf(t)) for t in d]
    x = [eye - t for t in d]
    x = [t + _dot(bf(t), bf(p)) for t, p in zip(x, d2)]
    d4 = [_dot(bf(p), bf(p)) for p in d2]
    x = [t + _dot(bf(t), bf(p)) for t, p in zip(x, d4)]
    for off_blk in merge_blks:
        x16 = [bf(t) for t in x]
        tmp = [_dot(bf(jnp.where(off_blk, a, 0.0)), t16) for a, t16 in zip(a_low, x16)]
        x = [t - _dot(t16, bf(m)) for t, t16, m in zip(x, x16, tmp)]
    a_hi = [bf(a) for a in a_low]
    a_lo = [bf(a - hi.astype(F32)) for a, hi in zip(a_low, a_hi)]
    for _ in range(NEWTON_STEPS):
        x0 = [bf(t) for t in x]
        res = [eye - t0.astype(F32) - (_dot(hi, t0) + _dot(lo, t0)) for t0, hi, lo in zip(x0, a_hi, a_lo)]
        x = [t0.astype(F32) + _dot(t0, bf(r)) for t0, r in zip(x0, res)]

    e_gc = stage(lambda u: jnp.exp(u["gc"]))
    rhs = [bf(jnp.concatenate([u["v_ref"][:, u["sl"]] * u["beta"], kb * e], axis=-1))
           for u, kb, e in zip(units, kbeta, e_gc)]
    sol = [_dot(bf(t), r) for t, r in zip(x, rhs)]

    s16 = stage(lambda u: bf(s_ref[u["dr"], u["h"]]))
    v_new = [so[:, :dk] - _dot(bf(so[:, dk:]), s) for so, s in zip(sol, s16)]
    vn16 = [bf(t) for t in v_new]
    o = [_dot(bf(u["q_ref"][:, u["sl"]] * e), s) + _dot(at, vn)
         for u, e, s, at, vn in zip(units, e_gc, s16, attn, vn16)]
    for u, t in zip(units, o):
        u["o_ref"][:, u["sl"]] = t
    s_new = [s_ref[u["dr"], u["h"]] * jnp.exp(u["gl"])
             + _dot_tn(bf(u["k_ref"][:, u["sl"]] * jnp.exp(u["gl"] - u["gc"])), vn)
             for u, vn in zip(units, vn16)]
    for u, t in zip(units, s_new):
        s_ref[u["dr"], u["h"]] = t

    @pl.when(n == pl.num_programs(1) - 1)
    def _():
        sfin_ref[0] = s_ref[...]


def _gdn_call(qkv, ab_src, abt, av, dtv, avt, dtvt, s0, *, n_seq, n_tok, row0, out_rows, bases, ab_col_blk,
              heads, dk, chunk):
    nc = n_tok // chunk
    d = heads * dk
    rb0 = row0 // chunk
    fwd = lambda b, n: b * nc + n
    bwd = lambda b, n: b * nc + nc - 1 - n
    qkv_spec = lambda idx, which: pl.BlockSpec((chunk, d), lambda b, n: (idx(b, n), which))
    ab_spec = lambda idx: pl.BlockSpec((chunk, 128), lambda b, n: (rb0 + idx(b, n), ab_col_blk))
    abt_spec = lambda idx: pl.BlockSpec((4 * heads, chunk), lambda b, n: (0, idx(b, n)))
    small = lambda shape: pl.BlockSpec(shape, lambda b, n: (0, 0))
    st_spec = pl.BlockSpec((1, 2, heads, dk, dk), lambda b, n: (b, 0, 0, 0, 0))
    o_shape = jax.ShapeDtypeStruct((out_rows, d), F32)
    in_specs = [qkv_spec(fwd, 0), qkv_spec(fwd, 1), qkv_spec(fwd, 2), ab_spec(fwd), abt_spec(fwd),
                qkv_spec(bwd, 0), qkv_spec(bwd, 1), qkv_spec(bwd, 2), ab_spec(bwd), abt_spec(bwd),
                small((1, 128)), small((1, 128)), small((4 * heads, 1)), small((4 * heads, 1))]
    args = [qkv, qkv, qkv, ab_src, abt, qkv, qkv, qkv, ab_src, abt, av, dtv, avt, dtvt]
    if s0 is not None:
        in_specs.append(st_spec)
        args.append(s0)
    aliases = {}
    if bases is not None:
        in_specs = [pl.BlockSpec(memory_space=pl.ANY)] * 2 + in_specs
        args = list(bases) + args
        aliases = {0: 0, 1: 1}
    return pl.pallas_call(
        functools.partial(_gdn_body, heads=heads, dk=dk, chunk=chunk, has_base=bases is not None,
                          has_s0=s0 is not None),
        grid=(n_seq, nc),
        in_specs=in_specs,
        out_specs=[pl.BlockSpec((chunk, d), lambda b, n: (rb0 + fwd(b, n), 0)),
                   pl.BlockSpec((chunk, d), lambda b, n: (rb0 + bwd(b, n), 0)),
                   st_spec],
        out_shape=[o_shape, o_shape, jax.ShapeDtypeStruct((n_seq, 2, heads, dk, dk), F32)],
        scratch_shapes=[pltpu.VMEM((2, heads, dk, dk), F32)],
        input_output_aliases=aliases,
        compiler_params=_cparams(("parallel", "arbitrary")),
        name="gdn_scan",
    )(*args)


def _outproj_body(x_ref, mod_ref, zhy_ref, of_ref, ob_ref, zg_ref, ghy_ref, go_ref, npost_ref, w_ref,
                  o_ref, *, mi, heads, dk):
    yhy = _rms(zhy_ref[...], ghy_ref[...]).astype(BF16)
    o = of_ref[...] + ob_ref[...]
    zg = zg_ref[...]
    parts = [yhy]
    for h in range(heads):
        sl = slice(h * dk, (h + 1) * dk)
        parts.append((_rms(o[:, sl], go_ref[...]) * _silu(zg[:, sl])).astype(BF16))
    y = _dot(jnp.concatenate(parts, axis=-1), w_ref[...])
    gate = mod_ref[0, mi:mi + 1, :]
    o_ref[...] = x_ref[...] + gate * _rms(y, npost_ref[...])


def _outproj_call(x, mod, zhy, o_f, o_b, proj, ghy, go, npost, w, *, mi, zg_col_blk, n_ctx_rows, rows_per_req,
                  heads, dk, tm=512):
    m, d = x.shape
    c = zhy.shape[1]
    dg = heads * dk
    grp = functools.partial(_group_of_tile, tm=tm, n_ctx_rows=n_ctx_rows, rows_per_req=rows_per_req)
    row = lambda width: pl.BlockSpec((tm, width), lambda i: (i, 0))
    return pl.pallas_call(
        functools.partial(_outproj_body, mi=mi, heads=heads, dk=dk),
        grid=(m // tm,),
        in_specs=[row(d),
                  pl.BlockSpec((1, N_MOD, d), lambda i: (grp(i), 0, 0)),
                  row(c), row(dg), row(dg),
                  pl.BlockSpec((tm, dg), lambda i: (i, zg_col_blk)),
                  pl.BlockSpec((1, c), lambda i: (0, 0)),
                  pl.BlockSpec((1, dk), lambda i: (0, 0)),
                  pl.BlockSpec((1, d), lambda i: (0, 0)),
                  pl.BlockSpec((c + dg, d), lambda i: (0, 0))],
        out_specs=row(d),
        out_shape=jax.ShapeDtypeStruct((m, d), F32),
        compiler_params=_cparams(("parallel",)),
        name="out_proj",
    )(x, mod, zhy, o_f, o_b, proj, ghy.reshape(1, c), go.reshape(1, dk), npost.reshape(1, d), w)


def _grid_pos_embed(n_tok, dim):
    rows = n_tok // GRID_W
    r = jnp.broadcast_to(jnp.arange(rows, dtype=F32)[:, None], (rows, GRID_W)).reshape(-1)
    col = jnp.broadcast_to(jnp.arange(GRID_W, dtype=F32)[None, :], (rows, GRID_W)).reshape(-1)
    quarter = dim // 4
    omega = 1.0 / (POS_BASE ** (jnp.arange(quarter, dtype=F32) / quarter))
    ar = r[:, None] * omega[None]
    ac = col[:, None] * omega[None]
    return jnp.concatenate([jnp.sin(ar), jnp.cos(ar), jnp.sin(ac), jnp.cos(ac)], axis=-1)


def _mixer(proj, *, groups, hy_conv_w, hy_f, hy_decay, hy_bias, gdn_conv_w, gdn_a_log, gdn_dt_bias,
           c_hy, d_gdn, states):
    heads = GDN_HEADS
    dk = d_gdn // heads
    o0 = 3 * c_hy
    o2 = o0 + 4 * d_gdn
    ab_col_blk = o2 // 128

    a_exp = jnp.exp(gdn_a_log.astype(F32))
    zeros = jnp.zeros_like(a_exp)
    av32 = jnp.stack([a_exp, zeros], axis=1).reshape(-1)
    dtv32 = jnp.stack([gdn_dt_bias.astype(F32), zeros], axis=1).reshape(-1)
    pad = 128 - 4 * heads
    av = jnp.pad(av32, (0, pad)).reshape(1, 128)
    dtv = jnp.pad(dtv32, (0, pad)).reshape(1, 128)
    avt = av32.reshape(4 * heads, 1)
    dtvt = dtv32.reshape(4 * heads, 1)

    all_rows = proj.shape[0]
    zhy, o_fb, sfs = None, None, []
    for (n_seq, n_tok, row0), s0 in zip(groups, states):
        r = min(256, n_tok)
        fr, gr = _dft_tables(n_tok, r)
        ha, hb, hc = _spectra_call(n_tok, *hy_f, hy_decay, fr, r=r)
        zhy = _hyena_call(proj, hy_conv_w, hy_bias, fr, gr, ha, hb, hc, n_seq=n_seq, n_tok=n_tok, row0=row0,
                          out_rows=all_rows, base=zhy, c=c_hy, r=r)
        qkv = _gdn_prep_call(proj, gdn_conv_w, n_seq=n_seq, n_tok=n_tok, row_blk0=row0 // n_tok, col0=o0,
                             d_gdn=d_gdn, dk=dk)
        rows = n_seq * n_tok
        abt = jnp.transpose(lax.slice(proj, (row0, o2), (row0 + rows, o2 + 4 * heads)))
        o_f, o_b, s_fin = _gdn_call(qkv, proj, abt, av, dtv, avt, dtvt, s0, n_seq=n_seq, n_tok=n_tok, row0=row0,
                                    out_rows=all_rows, bases=o_fb, ab_col_blk=ab_col_blk, heads=heads, dk=dk,
                                    chunk=GDN_CHUNK)
        o_fb = (o_f, o_b)
        sfs.append(s_fin)
    return zhy, o_fb[0], o_fb[1], sfs


def kernel(x_prompt, x_sample, state_gdn, c, c_ctx, ada_w, ada_b, norm_pre, norm_post, ffn_wg, ffn_wu, ffn_wd,
           w_in, w_out, hy_conv_w, hy_f_w1, hy_f_b1, hy_f_w2, hy_f_b2, hy_f_w3, hy_decay, hy_bias, hy_out_norm,
           gdn_conv_w, gdn_a_log, gdn_dt_bias, gdn_o_norm):
    nb, seq, d = x_prompt.shape
    db, dseq, _ = x_sample.shape
    depth = ada_w.shape[0]
    c_hy = hy_decay.shape[-1]
    d_gdn = gdn_conv_w.shape[-1] // 3
    heads = GDN_HEADS
    dk = d_gdn // heads
    n_ctx_rows = nb * seq
    grp = dict(n_ctx_rows=n_ctx_rows, rows_per_req=dseq)

    n_lat_rows = db * dseq
    n_rows = n_ctx_rows + n_lat_rows
    pos = _grid_pos_embed(dseq, d).astype(x_sample.dtype)
    cvec = jnp.concatenate([c_ctx[None], c, jnp.zeros((8 - 1 - db, d), F32)], axis=0)
    bf16 = lambda w: w.astype(BF16)

    x = None
    ctx_states = []
    for l in range(depth):
        mod = _mod_call(cvec, ada_w[l], ada_b[l]).reshape(8, N_MOD, d)
        ncols = w_in.shape[-1]
        ncols_p = -(-(ncols + 96) // 512) * 512
        w_in_p = jnp.pad(bf16(w_in[l]), ((0, 0), (0, ncols_p - ncols)))

        ffn0 = (mod, norm_pre[l, 0], norm_post[l, 0], bf16(ffn_wg[l, 0]), bf16(ffn_wu[l, 0]), bf16(ffn_wd[l, 0]))
        if l == 0:
            x = _ffn_call(x_prompt.reshape(n_ctx_rows, d), *ffn0, mi=0, out_rows=n_rows, **grp)
            x = _ffn_call(x_sample.reshape(n_lat_rows, d), *ffn0, mi=0, g_row0=n_ctx_rows, out_rows=n_rows,
                          out_row0=n_ctx_rows, pos=pos, base=x, **grp)
        else:
            x = _ffn_call(x, *ffn0, mi=0, **grp)

        proj = _inproj_call(x, mod, norm_pre[l, 1], w_in_p, mi=3, **grp)
        zhy, o_f, o_b, sfs = _mixer(
            proj, groups=[(nb, seq, 0), (db, dseq, n_ctx_rows)], hy_conv_w=hy_conv_w[l],
            hy_f=(hy_f_w1[l], hy_f_b1[l], hy_f_w2[l], hy_f_b2[l], hy_f_w3[l]), hy_decay=hy_decay[l],
            hy_bias=hy_bias[l], gdn_conv_w=gdn_conv_w[l], gdn_a_log=gdn_a_log[l], gdn_dt_bias=gdn_dt_bias[l],
            c_hy=c_hy, d_gdn=d_gdn, states=[None, state_gdn[:, l].astype(F32)])
        ctx_states.append(sfs[0])
        x = _outproj_call(x, mod, zhy, o_f, o_b, proj, hy_out_norm[l], gdn_o_norm[l], norm_post[l, 1],
                          bf16(w_out[l]), mi=5, zg_col_blk=(3 * c_hy + 3 * d_gdn) // d_gdn, heads=heads,
                          dk=dk, **grp)

        ffn1 = (mod, norm_pre[l, 2], norm_post[l, 2], bf16(ffn_wg[l, 1]), bf16(ffn_wu[l, 1]), bf16(ffn_wd[l, 1]))
        if l == depth - 1:
            y_prompt = _ffn_call(x, *ffn1, mi=6, n_rows=n_ctx_rows, **grp)
            y_sample = _ffn_call(x, *ffn1, mi=6, x_row0=n_ctx_rows, n_rows=n_lat_rows, g_row0=n_ctx_rows, **grp)
        else:
            x = _ffn_call(x, *ffn1, mi=6, **grp)

    new_state = ctx_states[0][:, None] if depth == 1 else jnp.stack(ctx_states, axis=1)
    return y_prompt.reshape(nb, seq, d), y_sample.reshape(db, dseq, d), new_state
```

```python
import functools
import math

import numpy as np
import jax
import jax.numpy as jnp
from jax import lax
from jax.experimental import pallas as pl
from jax.experimental.pallas import tpu as pltpu

F32 = jnp.float32
BF16 = jnp.bfloat16

N_MOD = 9
RMS_EPS = 1e-6
GRID_W = 64
POS_BASE = 10000.0
HY_BANDS = 16
HY_SIN_FREQ = 1.0
HY_WINDOW_SHIFT = 0.05
GDN_HEADS = 8
GDN_CHUNK = 128
NEWTON_STEPS = 1
VMEM_LIMIT = 56 * 1024 * 1024


def _cparams(sem):
    return pltpu.CompilerParams(dimension_semantics=sem, vmem_limit_bytes=VMEM_LIMIT)


def _dot(a, b):
    return jnp.dot(a, b, preferred_element_type=F32)


def _dot_nt(a, b):
    return lax.dot_general(a, b, (((1,), (1,)), ((), ())), preferred_element_type=F32)


def _dot_tn(a, b):
    return lax.dot_general(a, b, (((0,), (0,)), ((), ())), preferred_element_type=F32)


def _rms(x, g):
    ms = jnp.mean(x * x, axis=-1, keepdims=True)
    return x * lax.rsqrt(ms + RMS_EPS) * g


def _silu(x):
    return x * jax.nn.sigmoid(x)


def _group_of_tile(i, tm, n_ctx_rows, rows_per_req):
    nct = n_ctx_rows // tm
    tpr = rows_per_req // tm
    return jnp.where(i < nct, 0, 1 + jnp.maximum(i - nct, 0) // tpr)


def _mod_body(c_ref, w_ref, b_ref, o_ref):
    s = _silu(c_ref[...]).astype(BF16)
    o_ref[...] = _dot(s, w_ref[...].astype(BF16)) + b_ref[...]


def _mod_call(cvec, ada_w, ada_b):
    g, d = cvec.shape
    n = ada_w.shape[1]
    tn = 1024
    return pl.pallas_call(
        _mod_body,
        grid=(n // tn,),
        in_specs=[pl.BlockSpec((g, d), lambda j: (0, 0)),
                  pl.BlockSpec((d, tn), lambda j: (0, j)),
                  pl.BlockSpec((1, tn), lambda j: (0, j))],
        out_specs=pl.BlockSpec((g, tn), lambda j: (0, j)),
        out_shape=jax.ShapeDtypeStruct((g, n), F32),
        compiler_params=_cparams(("arbitrary",)),
        name="mod_table",
    )(cvec, ada_w, ada_b.reshape(1, n))


def _ffn_body(*refs, mi, has_pos, has_base):
    refs = list(refs)
    x_ref = refs.pop(0)
    pos_ref = refs.pop(0) if has_pos else None
    if has_base:
        refs.pop(0)
    mod_ref, npre_ref, npost_ref, wg_ref, wu_ref, wd_ref, o_ref, h_ref, acc_ref = refs
    j = pl.program_id(1)

    def x_in():
        return x_ref[...] + pos_ref[...] if has_pos else x_ref[...]

    @pl.when(j == 0)
    def _():
        shift = mod_ref[0, mi:mi + 1, :]
        scale = mod_ref[0, mi + 1:mi + 2, :]
        h = _rms(x_in(), npre_ref[...]) * (1.0 + scale) + shift
        h_ref[...] = h.astype(BF16)
        acc_ref[...] = jnp.zeros_like(acc_ref)

    h = h_ref[...]
    g = _dot(h, wg_ref[...])
    u = _dot(h, wu_ref[...])
    a = (_silu(g) * u).astype(BF16)
    acc_ref[...] += _dot(a, wd_ref[...])

    @pl.when(j == pl.num_programs(1) - 1)
    def _():
        gate = mod_ref[0, mi + 2:mi + 3, :]
        o_ref[...] = x_in() + 0.5 * gate * _rms(acc_ref[...], npost_ref[...])


def _ffn_call(x, mod, npre, npost, wg, wu, wd, wsel, *, mi, n_ctx_rows, rows_per_req, x_row0=0, n_rows=None,
              g_row0=0, out_rows=None, out_row0=0, pos=None, base=None, tm=512, tf=512):
    d = x.shape[1]
    n_rows = x.shape[0] if n_rows is None else n_rows
    out_rows = n_rows if out_rows is None else out_rows
    ff = wg.shape[-1]
    wl, wk = wsel
    grp = functools.partial(_group_of_tile, tm=tm, n_ctx_rows=n_ctx_rows, rows_per_req=rows_per_req)
    xb0, gb0, ob0, tpr = x_row0 // tm, g_row0 // tm, out_row0 // tm, rows_per_req // tm
    const = lambda shape: pl.BlockSpec(shape, lambda i, j: (0, 0))
    in_specs = [pl.BlockSpec((tm, d), lambda i, j: (xb0 + i, 0))]
    args = [x]
    if pos is not None:
        in_specs.append(pl.BlockSpec((tm, d), lambda i, j: (i % tpr, 0)))
        args.append(pos)
    aliases = {}
    if base is not None:
        aliases = {len(args): 0}
        in_specs.append(pl.BlockSpec(memory_space=pl.ANY))
        args.append(base)
    in_specs += [pl.BlockSpec((1, N_MOD, d), lambda i, j: (grp(gb0 + i), 0, 0)),
                 const((1, d)), const((1, d)),
                 pl.BlockSpec((None, None, d, tf), lambda i, j: (wl, wk, 0, j)),
                 pl.BlockSpec((None, None, d, tf), lambda i, j: (wl, wk, 0, j)),
                 pl.BlockSpec((None, None, tf, d), lambda i, j: (wl, wk, j, 0))]
    args += [mod, npre.reshape(1, d), npost.reshape(1, d), wg, wu, wd]
    return pl.pallas_call(
        functools.partial(_ffn_body, mi=mi, has_pos=pos is not None, has_base=base is not None),
        grid=(n_rows // tm, ff // tf),
        in_specs=in_specs,
        out_specs=pl.BlockSpec((tm, d), lambda i, j: (ob0 + i, 0)),
        out_shape=jax.ShapeDtypeStruct((out_rows, d), F32),
        scratch_shapes=[pltpu.VMEM((tm, d), BF16), pltpu.VMEM((tm, d), F32)],
        input_output_aliases=aliases,
        compiler_params=_cparams(("parallel", "arbitrary")),
        name="ffn",
    )(*args)


def _inproj_body(x_ref, mod_ref, npre_ref, w_ref, wt_ref, o_ref, ot_ref, h_ref, *, mi, n_main):
    j = pl.program_id(1)

    @pl.when(j == 0)
    def _():
        shift = mod_ref[0, mi:mi + 1, :]
        scale = mod_ref[0, mi + 1:mi + 2, :]
        h = _rms(x_ref[...], npre_ref[...]) * (1.0 + scale) + shift
        h_ref[...] = h.astype(BF16)

    @pl.when(j < n_main)
    def _():
        o_ref[...] = _dot(h_ref[...], w_ref[...].astype(BF16))

    @pl.when(j == n_main)
    def _():
        ot_ref[...] = _dot(h_ref[...], wt_ref[...].astype(BF16))


def _inproj_call(x, mod, npre, w_in, layer, w_tail, *, mi, n_main_cols, n_ctx_rows, rows_per_req, tm=1024,
                 tn=512):
    m, d = x.shape
    n_main = n_main_cols // tn
    nt = w_tail.shape[1]
    grp = functools.partial(_group_of_tile, tm=tm, n_ctx_rows=n_ctx_rows, rows_per_req=rows_per_req)
    jm = lambda j: jnp.minimum(j, n_main - 1)
    return pl.pallas_call(
        functools.partial(_inproj_body, mi=mi, n_main=n_main),
        grid=(m // tm, n_main + 1),
        in_specs=[pl.BlockSpec((tm, d), lambda i, j: (i, 0)),
                  pl.BlockSpec((1, N_MOD, d), lambda i, j: (grp(i), 0, 0)),
                  pl.BlockSpec((1, d), lambda i, j: (0, 0)),
                  pl.BlockSpec((None, d, tn), lambda i, j: (layer, 0, jm(j))),
                  pl.BlockSpec((d, nt), lambda i, j: (0, 0))],
        out_specs=[pl.BlockSpec((tm, tn), lambda i, j: (i, jm(j))),
                   pl.BlockSpec((tm, nt), lambda i, j: (i, 0))],
        out_shape=[jax.ShapeDtypeStruct((m, n_main_cols), F32), jax.ShapeDtypeStruct((m, nt), F32)],
        scratch_shapes=[pltpu.VMEM((tm, d), BF16)],
        compiler_params=_cparams(("parallel", "arbitrary")),
        name="in_proj",
    )(x, mod, npre.reshape(1, d), w_in, w_tail)


def _conv3(u, w):
    n = u.shape[0]
    row = lax.broadcasted_iota(jnp.int32, u.shape, 0)
    prev = jnp.where(row == 0, 0.0, pltpu.roll(u, 1, 0))
    nxt = jnp.where(row == n - 1, 0.0, pltpu.roll(u, n - 1, 0))
    return prev * w[0:1, :] + u * w[1:2, :] + nxt * w[2:3, :]


@functools.lru_cache(maxsize=None)
def _dft_tables_np(n_tok, r):
    big = 2 * n_tok
    k = np.arange(n_tok, dtype=np.int64)
    t = np.arange(n_tok, dtype=np.int64)
    ang = ((k[:, None] * t[None, :]) % big).astype(np.float64) * (2.0 * math.pi / big)
    cosm = np.cos(ang)
    sinm = np.sin(ang)
    sinm[0, :] = 1.0 - 2.0 * (t % 2)
    nj = n_tok // r
    fr = np.concatenate([cosm.reshape(nj, r, n_tok), sinm.reshape(nj, r, n_tok)], axis=1)
    gr = np.ascontiguousarray(np.transpose(fr, (0, 2, 1)))
    return fr.astype(np.float32), gr.astype(np.float32)


def _dft_tables(n_tok, r):
    fr, gr = _dft_tables_np(n_tok, r)
    return jnp.asarray(fr).astype(BF16), jnp.asarray(gr).astype(BF16)


def _spectra_body(featt_ref, t_ref, w1t_ref, b1_ref, w2t_ref, b2_ref, w3f_ref, w3b_ref, dec_ref, f_ref,
                  ha_ref, hb_ref, hc_ref, e_ref, o_ref, h_ref, *, n_tok, r):
    j = pl.program_id(1)

    @pl.when((pl.program_id(0) == 0) & (j == 0))
    def _():
        h = jnp.sin(HY_SIN_FREQ * (_dot(w1t_ref[...].astype(BF16), featt_ref[...].astype(BF16)) + b1_ref[...]))
        h = jnp.sin(HY_SIN_FREQ * (_dot(w2t_ref[...].astype(BF16), h.astype(BF16)) + b2_ref[...]))
        h_ref[...] = h.astype(BF16)

    @pl.when(j == 0)
    def _():
        hb16 = h_ref[...]
        window = jnp.exp(-t_ref[...] * jnp.abs(dec_ref[...])) + HY_WINDOW_SHIFT
        fwd = _dot_tn(hb16, w3f_ref[...].astype(BF16)) * window
        bwd = _dot_tn(hb16, w3b_ref[...].astype(BF16)) * window
        row = lax.broadcasted_iota(jnp.int32, bwd.shape, 0)
        bwd = jnp.where(row == 0, 0.0, bwd)
        norm = jnp.sum(jnp.abs(fwd), axis=0, keepdims=True) + jnp.sum(jnp.abs(bwd), axis=0, keepdims=True)
        fwd = fwd / norm
        bwd = bwd / norm
        e_ref[...] = (fwd + bwd).astype(BF16)
        o_ref[...] = (fwd - bwd).astype(BF16)

    fb = f_ref[0]
    e = e_ref[...]
    p = _dot(fb[:r], e)
    q = _dot(fb[r:], o_ref[...])
    alt = _dot(fb[r:r + 8], e)[0:1, :]
    k = j * r + lax.broadcasted_iota(jnp.int32, p.shape, 0)
    big = 2.0 * n_tok
    wk = jnp.where(k == 0, 1.0 / big, 2.0 / big)
    ha = wk * p
    ha_ref[...] = ha
    hb_ref[...] = jnp.where(k == 0, 0.0, -wk * q)
    hc_ref[...] = jnp.where(k == 0, alt * (1.0 / big), ha)


def _spectra_call(n_tok, w1, b1, w2, b2, w3, decay, fr, *, r, cn=512):
    f32 = jnp.float32
    order, c = decay.shape
    oc = order * c
    hid = w2.shape[0]
    emb = w1.shape[0]
    embp = 128
    idx = jnp.arange(n_tok, dtype=f32)
    t = idx / (n_tok - 1)
    bands = jnp.arange(1, HY_BANDS + 1, dtype=f32)
    ang = (2.0 * math.pi / n_tok) * idx[:, None] * bands[None, :]
    feats = jnp.concatenate([t[:, None], jnp.cos(ang), jnp.sin(ang)], axis=-1)
    featt = jnp.pad(feats, ((0, 0), (0, embp - emb))).T
    w1t = jnp.pad(w1, ((0, embp - emb), (0, 0))).T
    nj = n_tok // r
    ncb = oc // cn
    out = jax.ShapeDtypeStruct((n_tok, oc), F32)
    const = lambda shape: pl.BlockSpec(shape, lambda cb, j: (0, 0))
    return pl.pallas_call(
        functools.partial(_spectra_body, n_tok=n_tok, r=r),
        grid=(ncb, nj),
        in_specs=[const((embp, n_tok)), const((n_tok, 1)), const((hid, embp)), const((hid, 1)),
                  const((hid, hid)), const((hid, 1)),
                  pl.BlockSpec((hid, cn), lambda cb, j: (0, cb)),
                  pl.BlockSpec((hid, cn), lambda cb, j: (0, ncb + cb)),
                  pl.BlockSpec((1, cn), lambda cb, j: (0, cb)),
                  pl.BlockSpec((1, 2 * r, n_tok), lambda cb, j: (j, 0, 0))],
        out_specs=[pl.BlockSpec((r, cn), lambda cb, j: (j, cb))] * 3,
        out_shape=[out, out, out],
        scratch_shapes=[pltpu.VMEM((n_tok, cn), BF16), pltpu.VMEM((n_tok, cn), BF16),
                        pltpu.VMEM((hid, n_tok), BF16)],
        compiler_params=_cparams(("arbitrary", "arbitrary")),
        name="hyena_spectra",
    )(featt, t[:, None], w1t, b1.reshape(hid, 1), w2.T, b2.reshape(hid, 1), w3, w3, decay.reshape(1, oc), fr)


def _hyena_body(*refs, r, nj, has_base):
    refs = list(refs)
    if has_base:
        refs.pop(0)
    (v_ref, x1_ref, x2_ref, wv_ref, wx1_ref, wx2_ref, bias_ref, f_ref, g_ref,
     ha_ref, hb_ref, hc_ref, o_ref, uf_ref, ub_ref, acc_ref) = refs
    j = pl.program_id(2)

    @pl.when(j == 0)
    def _():
        v = _conv3(v_ref[...], wv_ref[...])
        uf_ref[...] = v
        ub_ref[...] = v.astype(BF16)
        acc_ref[...] = jnp.zeros_like(acc_ref)

    x = _dot(f_ref[0], ub_ref[...])
    p = x[:r]
    q = x[r:]
    hb = hb_ref[...]
    top = p * ha_ref[...] + q * hb
    bot = q * hc_ref[...] - p * hb
    z = jnp.concatenate([top, bot], axis=0).astype(BF16)
    acc_ref[...] += _dot(g_ref[0], z)

    @pl.when(j == nj - 1)
    def _():
        x1 = _conv3(x1_ref[...], wx1_ref[...])
        z1 = x1 * (acc_ref[...] + uf_ref[...] * bias_ref[0:1, :])
        uf_ref[...] = z1
        ub_ref[...] = z1.astype(BF16)
        acc_ref[...] = jnp.zeros_like(acc_ref)

    @pl.when(j == 2 * nj - 1)
    def _():
        x2 = _conv3(x2_ref[...], wx2_ref[...])
        o_ref[...] = x2 * (acc_ref[...] + uf_ref[...] * bias_ref[1:2, :])


def _hyena_call(proj, conv_w, bias, fr, gr, ha, hb, hc, *, n_seq, n_tok, row0, out_rows, base, c, r, cn=512):
    nj = n_tok // r
    ncb = c // cn
    rb0 = row0 // n_tok
    single = pl.Buffered(1)
    seq_spec = lambda off: pl.BlockSpec((n_tok, cn), lambda b, cb, j: (rb0 + b, off * ncb + cb),
                                        pipeline_mode=single)
    w_spec = lambda off: pl.BlockSpec((3, cn), lambda b, cb, j: (0, off * ncb + cb))
    h_spec = pl.BlockSpec((r, cn), lambda b, cb, j: (j % nj, (j // nj) * ncb + cb))
    in_specs = [seq_spec(0), seq_spec(1), seq_spec(2), w_spec(0), w_spec(1), w_spec(2),
                pl.BlockSpec((2, cn), lambda b, cb, j: (0, cb)),
                pl.BlockSpec((1, 2 * r, n_tok), lambda b, cb, j: (j % nj, 0, 0)),
                pl.BlockSpec((1, n_tok, 2 * r), lambda b, cb, j: (j % nj, 0, 0)),
                h_spec, h_spec, h_spec]
    args = [proj, proj, proj, conv_w, conv_w, conv_w, bias, fr, gr, ha, hb, hc]
    aliases = {}
    if base is not None:
        in_specs.insert(0, pl.BlockSpec(memory_space=pl.ANY))
        args.insert(0, base)
        aliases = {0: 0}
    return pl.pallas_call(
        functools.partial(_hyena_body, r=r, nj=nj, has_base=base is not None),
        grid=(n_seq, ncb, 2 * nj),
        in_specs=in_specs,
        out_specs=pl.BlockSpec((n_tok, cn), lambda b, cb, j: (rb0 + b, cb)),
        out_shape=jax.ShapeDtypeStruct((out_rows, c), F32),
        scratch_shapes=[pltpu.VMEM((n_tok, cn), F32), pltpu.VMEM((n_tok, cn), BF16),
                        pltpu.VMEM((n_tok, cn), F32)],
        input_output_aliases=aliases,
        compiler_params=_cparams(("parallel", "parallel", "arbitrary")),
        name="hyena_conv",
    )(*args)


def _conv3_rows(x_ref, w, r0, nr, seq_len):
    n = x_ref.shape[0]
    lo = max(r0 - 8, 0)
    hi = min(r0 + nr + 8, n)
    slab = x_ref[lo:hi, :]
    prev = pltpu.roll(slab, 1, 0)[r0 - lo:r0 - lo + nr]
    nxt = pltpu.roll(slab, hi - lo - 1, 0)[r0 - lo:r0 - lo + nr]
    mid = slab[r0 - lo:r0 - lo + nr]
    row = lax.broadcasted_iota(jnp.int32, mid.shape, 0)
    if r0 % seq_len == 0:
        prev = jnp.where(row == 0, 0.0, prev)
    if (r0 + nr) % seq_len == 0:
        nxt = jnp.where(row == nr - 1, 0.0, nxt)
    return prev * w[0:1, :] + mid * w[1:2, :] + nxt * w[2:3, :]


def _gdn_prep_body(x_ref, w_ref, o_ref, *, heads_per_blk, dk, n_qk_blk, rows, seq_len):
    cb = pl.program_id(1)
    n = x_ref.shape[0]
    w = w_ref[...]
    is_v = cb >= n_qk_blk
    scale = jnp.where(cb < n_qk_blk // 2, dk ** -0.5, 1.0)
    for r0 in range(0, n, rows):
        y = _silu(_conv3_rows(x_ref, w, r0, rows, seq_len))
        for h in range(heads_per_blk):
            yh = y[:, h * dk:(h + 1) * dk]
            inv = lax.rsqrt(jnp.sum(yh * yh, axis=-1, keepdims=True) + RMS_EPS) * scale
            o_ref[r0:r0 + rows, h * dk:(h + 1) * dk] = yh * jnp.where(is_v, 1.0, inv)


def _gdn_prep_call(proj, conv_w, *, n_seq, n_tok, row0, col0, d_gdn, dk, cn=512, blk_rows=2048):
    ncb = 3 * d_gdn // cn
    cb0 = col0 // cn
    rows = n_seq * n_tok
    blk_rows = min(blk_rows, rows)
    rb0 = row0 // blk_rows
    return pl.pallas_call(
        functools.partial(_gdn_prep_body, heads_per_blk=cn // dk, dk=dk, n_qk_blk=2 * d_gdn // cn,
                          rows=min(64, n_tok), seq_len=n_tok),
        grid=(rows // blk_rows, ncb),
        in_specs=[pl.BlockSpec((blk_rows, cn), lambda b, cb: (rb0 + b, cb0 + cb)),
                  pl.BlockSpec((3, cn), lambda b, cb: (0, cb))],
        out_specs=pl.BlockSpec((blk_rows, cn), lambda b, cb: (b, cb)),
        out_shape=jax.ShapeDtypeStruct((rows, 3 * d_gdn), F32),
        compiler_params=_cparams(("parallel", "parallel")),
        name="gdn_prep",
    )(proj, conv_w)


def _split3(x):
    hi = x.astype(BF16)
    r1 = x - hi.astype(F32)
    mid = r1.astype(BF16)
    lo = (r1 - mid.astype(F32)).astype(BF16)
    return hi, mid, lo


def _gdn_body(*refs, heads, dk, chunk, has_base, has_s0):
    refs = list(refs)
    if has_base:
        del refs[:2]
    (qf_ref, kf_ref, vf_ref, abf_ref, abtf_ref, qb_ref, kb_ref, vb_ref, abb_ref, abtb_ref,
     av_ref, dtv_ref, avt_ref, dtvt_ref) = refs[:14]
    s0_ref = refs[14] if has_s0 else None
    of_ref, ob_ref, sfin_ref, s_ref = refs[-4:]
    n = pl.program_id(1)

    @pl.when(n == 0)
    def _():
        s_ref[...] = s0_ref[0] if has_s0 else jnp.zeros_like(s_ref)

    ci = lax.broadcasted_iota(jnp.int32, (chunk, chunk), 0)
    cj = lax.broadcasted_iota(jnp.int32, (chunk, chunk), 1)
    eye = (ci == cj).astype(F32)
    diag_blk = (ci >> 3) == (cj >> 3)
    merge_blks = [((ci >> s) ^ (cj >> s)) == 1 for s in range(3, int(math.log2(chunk)))]
    lower = cj <= ci
    upper = cj >= ci

    units = []
    for dr, (q_ref, k_ref, v_ref, ab_ref, abt_ref, o_ref) in enumerate(
            ((qf_ref, kf_ref, vf_ref, abf_ref, abtf_ref, of_ref),
             (qb_ref, kb_ref, vb_ref, abb_ref, abtb_ref, ob_ref))):
        incl = upper if dr else lower
        strict = (cj > ci) if dr else (cj < ci)
        ones_incl = jnp.where(incl, 1.0, 0.0).astype(BF16)
        ones_incl_t = jnp.where(lower if dr else upper, 1.0, 0.0).astype(BF16)
        ab = ab_ref[...]
        g_col = -av_ref[...] * jax.nn.softplus(ab + dtv_ref[...])
        beta_col = jax.nn.sigmoid(ab)
        abt = abt_ref[...]
        g_row = -avt_ref[...] * jax.nn.softplus(abt + dtvt_ref[...])
        h3 = _split3(g_col)
        gc_col = _dot(ones_incl, h3[0]) + _dot(ones_incl, h3[1]) + _dot(ones_incl, h3[2])
        r3 = _split3(g_row)
        gc_row = _dot(r3[0], ones_incl_t) + _dot(r3[1], ones_incl_t) + _dot(r3[2], ones_incl_t)
        gl_all = jnp.sum(g_col, axis=0, keepdims=True)
        for h in range(heads):
            cg = dr * 2 * heads + h
            cb = cg + heads
            sl = slice(h * dk, (h + 1) * dk)
            units.append(dict(dr=dr, h=h, sl=sl, q_ref=q_ref, k_ref=k_ref, v_ref=v_ref, o_ref=o_ref,
                              incl=incl, strict=strict, gc=gc_col[:, cg:cg + 1], gr=gc_row[cg:cg + 1, :],
                              beta=beta_col[:, cb:cb + 1], gl=gl_all[:, cg:cg + 1]))

    def stage(fn):
        return [fn(u) for u in units]

    bf = lambda t: t.astype(BF16)
    k16 = stage(lambda u: bf(u["k_ref"][:, u["sl"]]))
    decay = stage(lambda u: jnp.where(u["incl"], jnp.exp(jnp.where(u["incl"], u["gc"] - u["gr"], 0.0)), 0.0))
    kbeta = stage(lambda u: u["k_ref"][:, u["sl"]] * u["beta"])
    a_low = [jnp.where(u["strict"], _dot_nt(bf(kb), k) * dc, 0.0)
             for u, kb, k, dc in zip(units, kbeta, k16, decay)]
    attn = [bf(_dot_nt(bf(u["q_ref"][:, u["sl"]]), k) * dc) for u, k, dc in zip(units, k16, decay)]

    d = [jnp.where(diag_blk, a, 0.0) for a in a_low]
    d2 = [_dot(bf(t), bf(t)) for t in d]
    x = [eye - t for t in d]
    x = [t + _dot(bf(t), bf(p)) for t, p in zip(x, d2)]
    d4 = [_dot(bf(p), bf(p)) for p in d2]
    x = [t + _dot(bf(t), bf(p)) for t, p in zip(x, d4)]
    for off_blk in merge_blks:
        x16 = [bf(t) for t in x]
        tmp = [_dot(bf(jnp.where(off_blk, a, 0.0)), t16) for a, t16 in zip(a_low, x16)]
        x = [t - _dot(t16, bf(m)) for t, t16, m in zip(x, x16, tmp)]
    a_hi = [bf(a) for a in a_low]
    a_lo = [bf(a - hi.astype(F32)) for a, hi in zip(a_low, a_hi)]
    for _ in range(NEWTON_STEPS):
        x0 = [bf(t) for t in x]
        res = [eye - t0.astype(F32) - (_dot(hi, t0) + _dot(lo, t0)) for t0, hi, lo in zip(x0, a_hi, a_lo)]
        x = [t0.astype(F32) + _dot(t0, bf(r)) for t0, r in zip(x0, res)]

    e_gc = stage(lambda u: jnp.exp(u["gc"]))
    rhs = [bf(jnp.concatenate([u["v_ref"][:, u["sl"]] * u["beta"], kb * e], axis=-1))
           for u, kb, e in zip(units, kbeta, e_gc)]
    sol = [_dot(bf(t), r) for t, r in zip(x, rhs)]

    s16 = stage(lambda u: bf(s_ref[u["dr"], u["h"]]))
    v_new = [so[:, :dk] - _dot(bf(so[:, dk:]), s) for so, s in zip(sol, s16)]
    vn16 = [bf(t) for t in v_new]
    o = [_dot(bf(u["q_ref"][:, u["sl"]] * e), s) + _dot(at, vn)
         for u, e, s, at, vn in zip(units, e_gc, s16, attn, vn16)]
    for u, t in zip(units, o):
        u["o_ref"][:, u["sl"]] = t
    s_new = [s_ref[u["dr"], u["h"]] * jnp.exp(u["gl"])
             + _dot_tn(bf(u["k_ref"][:, u["sl"]] * jnp.exp(u["gl"] - u["gc"])), vn)
             for u, vn in zip(units, vn16)]
    for u, t in zip(units, s_new):
        s_ref[u["dr"], u["h"]] = t

    @pl.when(n == pl.num_programs(1) - 1)
    def _():
        sfin_ref[0] = s_ref[...]


def _gdn_call(qkv, ab_src, abt, av, dtv, avt, dtvt, s0, *, n_seq, n_tok, row0, out_rows, bases, ab_col_blk,
              heads, dk, chunk):
    nc = n_tok // chunk
    d = heads * dk
    rb0 = row0 // chunk
    fwd = lambda b, n: b * nc + n
    bwd = lambda b, n: b * nc + nc - 1 - n
    qkv_spec = lambda idx, which: pl.BlockSpec((chunk, d), lambda b, n: (idx(b, n), which))
    ab_spec = lambda idx: pl.BlockSpec((chunk, 128), lambda b, n: (rb0 + idx(b, n), ab_col_blk))
    abt_spec = lambda idx: pl.BlockSpec((4 * heads, chunk), lambda b, n: (0, idx(b, n)))
    small = lambda shape: pl.BlockSpec(shape, lambda b, n: (0, 0))
    st_spec = pl.BlockSpec((1, 2, heads, dk, dk), lambda b, n: (b, 0, 0, 0, 0))
    o_shape = jax.ShapeDtypeStruct((out_rows, d), F32)
    in_specs = [qkv_spec(fwd, 0), qkv_spec(fwd, 1), qkv_spec(fwd, 2), ab_spec(fwd), abt_spec(fwd),
                qkv_spec(bwd, 0), qkv_spec(bwd, 1), qkv_spec(bwd, 2), ab_spec(bwd), abt_spec(bwd),
                small((1, 128)), small((1, 128)), small((4 * heads, 1)), small((4 * heads, 1))]
    args = [qkv, qkv, qkv, ab_src, abt, qkv, qkv, qkv, ab_src, abt, av, dtv, avt, dtvt]
    if s0 is not None:
        in_specs.append(st_spec)
        args.append(s0)
    aliases = {}
    if bases is not None:
        in_specs = [pl.BlockSpec(memory_space=pl.ANY)] * 2 + in_specs
        args = list(bases) + args
        aliases = {0: 0, 1: 1}
    return pl.pallas_call(
        functools.partial(_gdn_body, heads=heads, dk=dk, chunk=chunk, has_base=bases is not None,
                          has_s0=s0 is not None),
        grid=(n_seq, nc),
        in_specs=in_specs,
        out_specs=[pl.BlockSpec((chunk, d), lambda b, n: (rb0 + fwd(b, n), 0)),
                   pl.BlockSpec((chunk, d), lambda b, n: (rb0 + bwd(b, n), 0)),
                   st_spec],
        out_shape=[o_shape, o_shape, jax.ShapeDtypeStruct((n_seq, 2, heads, dk, dk), F32)],
        scratch_shapes=[pltpu.VMEM((2, heads, dk, dk), F32)],
        input_output_aliases=aliases,
        compiler_params=_cparams(("parallel", "arbitrary")),
        name="gdn_scan",
    )(*args)


def _outproj_body(x_ref, mod_ref, zhy_ref, of_ref, ob_ref, zg_ref, ghy_ref, go_ref, npost_ref, w_ref,
                  o_ref, *, mi, heads, dk):
    yhy = _rms(zhy_ref[...], ghy_ref[...]).astype(BF16)
    o = of_ref[...] + ob_ref[...]
    zg = zg_ref[...]
    parts = [yhy]
    for h in range(heads):
        sl = slice(h * dk, (h + 1) * dk)
        parts.append((_rms(o[:, sl], go_ref[...]) * _silu(zg[:, sl])).astype(BF16))
    y = _dot(jnp.concatenate(parts, axis=-1), w_ref[...])
    gate = mod_ref[0, mi:mi + 1, :]
    o_ref[...] = x_ref[...] + gate * _rms(y, npost_ref[...])


def _outproj_call(x, mod, zhy, o_f, o_b, proj, ghy, go, npost, w, *, mi, zg_col_blk, n_ctx_rows, rows_per_req,
                  heads, dk, tm=512):
    m, d = x.shape
    c = zhy.shape[1]
    dg = heads * dk
    grp = functools.partial(_group_of_tile, tm=tm, n_ctx_rows=n_ctx_rows, rows_per_req=rows_per_req)
    row = lambda width: pl.BlockSpec((tm, width), lambda i: (i, 0))
    return pl.pallas_call(
        functools.partial(_outproj_body, mi=mi, heads=heads, dk=dk),
        grid=(m // tm,),
        in_specs=[row(d),
                  pl.BlockSpec((1, N_MOD, d), lambda i: (grp(i), 0, 0)),
                  row(c), row(dg), row(dg),
                  pl.BlockSpec((tm, dg), lambda i: (i, zg_col_blk)),
                  pl.BlockSpec((1, c), lambda i: (0, 0)),
                  pl.BlockSpec((1, dk), lambda i: (0, 0)),
                  pl.BlockSpec((1, d), lambda i: (0, 0)),
                  pl.BlockSpec((c + dg, d), lambda i: (0, 0))],
        out_specs=row(d),
        out_shape=jax.ShapeDtypeStruct((m, d), F32),
        compiler_params=_cparams(("parallel",)),
        name="out_proj",
    )(x, mod, zhy, o_f, o_b, proj, ghy.reshape(1, c), go.reshape(1, dk), npost.reshape(1, d), w)


@functools.lru_cache(maxsize=None)
def _grid_pos_embed_np(n_tok, dim):
    rows = n_tok // GRID_W
    r = np.repeat(np.arange(rows, dtype=np.float64), GRID_W)
    col = np.tile(np.arange(GRID_W, dtype=np.float64), rows)
    quarter = dim // 4
    omega = 1.0 / (POS_BASE ** (np.arange(quarter, dtype=np.float64) / quarter))
    ar = r[:, None] * omega[None]
    ac = col[:, None] * omega[None]
    return np.concatenate([np.sin(ar), np.cos(ar), np.sin(ac), np.cos(ac)], axis=-1).astype(np.float32)


def _mixer(proj, proj_ab, *, groups, hy_conv_w, hy_f, hy_decay, hy_bias, gdn_conv_w, gdn_a_log, gdn_dt_bias,
           c_hy, d_gdn, states):
    heads = GDN_HEADS
    dk = d_gdn // heads
    o0 = 3 * c_hy

    a_exp = jnp.exp(gdn_a_log.astype(F32))
    zeros = jnp.zeros_like(a_exp)
    av32 = jnp.stack([a_exp, zeros], axis=1).reshape(-1)
    dtv32 = jnp.stack([gdn_dt_bias.astype(F32), zeros], axis=1).reshape(-1)
    pad = 128 - 4 * heads
    av = jnp.pad(av32, (0, pad)).reshape(1, 128)
    dtv = jnp.pad(dtv32, (0, pad)).reshape(1, 128)
    avt = av32.reshape(4 * heads, 1)
    dtvt = dtv32.reshape(4 * heads, 1)
    abt_all = jnp.transpose(proj_ab[:, :4 * heads])

    all_rows = proj.shape[0]
    zhy, o_fb, sfs = None, None, []
    for (n_seq, n_tok, row0), s0 in zip(groups, states):
        r = min(256, n_tok)
        fr, gr = _dft_tables(n_tok, r)
        ha, hb, hc = _spectra_call(n_tok, *hy_f, hy_decay, fr, r=r)
        zhy = _hyena_call(proj, hy_conv_w, hy_bias, fr, gr, ha, hb, hc, n_seq=n_seq, n_tok=n_tok, row0=row0,
                          out_rows=all_rows, base=zhy, c=c_hy, r=r)
        qkv = _gdn_prep_call(proj, gdn_conv_w, n_seq=n_seq, n_tok=n_tok, row0=row0, col0=o0, d_gdn=d_gdn, dk=dk)
        abt = lax.slice(abt_all, (0, row0), (4 * heads, row0 + n_seq * n_tok))
        o_f, o_b, s_fin = _gdn_call(qkv, proj_ab, abt, av, dtv, avt, dtvt, s0, n_seq=n_seq, n_tok=n_tok,
                                    row0=row0, out_rows=all_rows, bases=o_fb, ab_col_blk=0, heads=heads, dk=dk,
                                    chunk=GDN_CHUNK)
        o_fb = (o_f, o_b)
        sfs.append(s_fin)
    return zhy, o_fb[0], o_fb[1], sfs


def kernel(x_prompt, x_sample, state_gdn, c, c_ctx, ada_w, ada_b, norm_pre, norm_post, ffn_wg, ffn_wu, ffn_wd,
           w_in, w_out, hy_conv_w, hy_f_w1, hy_f_b1, hy_f_w2, hy_f_b2, hy_f_w3, hy_decay, hy_bias, hy_out_norm,
           gdn_conv_w, gdn_a_log, gdn_dt_bias, gdn_o_norm):
    nb, seq, d = x_prompt.shape
    db, dseq, _ = x_sample.shape
    depth = ada_w.shape[0]
    c_hy = hy_decay.shape[-1]
    d_gdn = gdn_conv_w.shape[-1] // 3
    heads = GDN_HEADS
    dk = d_gdn // heads
    n_ctx_rows = nb * seq
    grp = dict(n_ctx_rows=n_ctx_rows, rows_per_req=dseq)

    n_lat_rows = db * dseq
    n_rows = n_ctx_rows + n_lat_rows
    pos = jnp.asarray(_grid_pos_embed_np(dseq, d)).astype(x_sample.dtype)
    cvec = jnp.concatenate([c_ctx[None], c, jnp.zeros((8 - 1 - db, d), F32)], axis=0)
    wg, wu, wd = ffn_wg.astype(BF16), ffn_wu.astype(BF16), ffn_wd.astype(BF16)
    n_main_cols = 3 * c_hy + 4 * d_gdn
    n_tail = w_in.shape[-1] - n_main_cols

    x = None
    ctx_states = []
    for l in range(depth):
        mod = _mod_call(cvec, ada_w[l], ada_b[l]).reshape(8, N_MOD, d)
        w_tail = jnp.pad(w_in[l, :, n_main_cols:], ((0, 0), (0, 128 - n_tail)))

        ffn0 = (mod, norm_pre[l, 0], norm_post[l, 0], wg, wu, wd, (l, 0))
        if l == 0:
            x = _ffn_call(x_prompt.reshape(n_ctx_rows, d), *ffn0, mi=0, out_rows=n_rows, **grp)
            x = _ffn_call(x_sample.reshape(n_lat_rows, d), *ffn0, mi=0, g_row0=n_ctx_rows, out_rows=n_rows,
                          out_row0=n_ctx_rows, pos=pos, base=x, **grp)
        else:
            x = _ffn_call(x, *ffn0, mi=0, **grp)

        proj, proj_ab = _inproj_call(x, mod, norm_pre[l, 1], w_in, l, w_tail, mi=3, n_main_cols=n_main_cols, **grp)
        zhy, o_f, o_b, sfs = _mixer(
            proj, proj_ab, groups=[(nb, seq, 0), (db, dseq, n_ctx_rows)], hy_conv_w=hy_conv_w[l],
            hy_f=(hy_f_w1[l], hy_f_b1[l], hy_f_w2[l], hy_f_b2[l], hy_f_w3[l]), hy_decay=hy_decay[l],
            hy_bias=hy_bias[l], gdn_conv_w=gdn_conv_w[l], gdn_a_log=gdn_a_log[l], gdn_dt_bias=gdn_dt_bias[l],
            c_hy=c_hy, d_gdn=d_gdn, states=[None, state_gdn[:, l].astype(F32)])
        ctx_states.append(sfs[0])
        x = _outproj_call(x, mod, zhy, o_f, o_b, proj, hy_out_norm[l], gdn_o_norm[l], norm_post[l, 1],
                          w_out[l].astype(BF16), mi=5, zg_col_blk=(3 * c_hy + 3 * d_gdn) // d_gdn, heads=heads,
                          dk=dk, **grp)

        ffn1 = (mod, norm_pre[l, 2], norm_post[l, 2], wg, wu, wd, (l, 1))
        if l == depth - 1:
            y_prompt = _ffn_call(x, *ffn1, mi=6, n_rows=n_ctx_rows, **grp)
            y_sample = _ffn_call(x, *ffn1, mi=6, x_row0=n_ctx_rows, n_rows=n_lat_rows, g_row0=n_ctx_rows, **grp)
        else:
            x = _ffn_call(x, *ffn1, mi=6, **grp)

    new_state = ctx_states[0][:, None] if depth == 1 else jnp.stack(ctx_states, axis=1)
    return y_prompt.reshape(nb, seq, d), y_sample.reshape(db, dseq, d), new_state
```

```python
import functools
import math

import numpy as np
import jax
import jax.numpy as jnp
from jax import lax
from jax.experimental import pallas as pl
from jax.experimental.pallas import tpu as pltpu

F32 = jnp.float32
BF16 = jnp.bfloat16

N_MOD = 9
RMS_EPS = 1e-6
GRID_W = 64
POS_BASE = 10000.0
HY_BANDS = 16
HY_SIN_FREQ = 1.0
HY_WINDOW_SHIFT = 0.05
GDN_HEADS = 8
GDN_CHUNK = 128
VMEM_LIMIT = 56 * 1024 * 1024


def _cparams(sem):
    return pltpu.CompilerParams(dimension_semantics=sem, vmem_limit_bytes=VMEM_LIMIT)


def _dot(a, b):
    return jnp.dot(a, b, preferred_element_type=F32)


def _dot_nt(a, b):
    return lax.dot_general(a, b, (((1,), (1,)), ((), ())), preferred_element_type=F32)


def _dot_tn(a, b):
    return lax.dot_general(a, b, (((0,), (0,)), ((), ())), preferred_element_type=F32)


def _rms(x, g):
    ms = jnp.mean(x * x, axis=-1, keepdims=True)
    return x * lax.rsqrt(ms + RMS_EPS) * g


def _silu(x):
    return x * jax.nn.sigmoid(x)


def _group_of_tile(i, tm, n_ctx_rows, rows_per_req):
    nct = n_ctx_rows // tm
    tpr = rows_per_req // tm
    return jnp.where(i < nct, 0, 1 + jnp.maximum(i - nct, 0) // tpr)


def _mod_body(c_ref, w_ref, b_ref, o_ref):
    s = _silu(c_ref[...]).astype(BF16)
    o_ref[...] = _dot(s, w_ref[...].astype(BF16)) + b_ref[...]


def _mod_call(cvec, ada_w, ada_b):
    g, d = cvec.shape
    n = ada_w.shape[1]
    tn = 1024
    return pl.pallas_call(
        _mod_body,
        grid=(n // tn,),
        in_specs=[pl.BlockSpec((g, d), lambda j: (0, 0)),
                  pl.BlockSpec((d, tn), lambda j: (0, j)),
                  pl.BlockSpec((1, tn), lambda j: (0, j))],
        out_specs=pl.BlockSpec((g, tn), lambda j: (0, j)),
        out_shape=jax.ShapeDtypeStruct((g, n), F32),
        compiler_params=_cparams(("arbitrary",)),
        name="mod_table",
    )(cvec, ada_w, ada_b.reshape(1, n))


def _ffn_body(*refs, mi, has_pos, has_base):
    refs = list(refs)
    x_ref = refs.pop(0)
    pos_ref = refs.pop(0) if has_pos else None
    if has_base:
        refs.pop(0)
    mod_ref, npre_ref, npost_ref, wg_ref, wu_ref, wd_ref, o_ref, h_ref, acc_ref = refs
    j = pl.program_id(1)

    def x_in():
        return x_ref[...] + pos_ref[...] if has_pos else x_ref[...]

    @pl.when(j == 0)
    def _():
        shift = mod_ref[0, mi:mi + 1, :]
        scale = mod_ref[0, mi + 1:mi + 2, :]
        h = _rms(x_in(), npre_ref[...]) * (1.0 + scale) + shift
        h_ref[...] = h.astype(BF16)
        acc_ref[...] = jnp.zeros_like(acc_ref)

    h = h_ref[...]
    g = _dot(h, wg_ref[...])
    u = _dot(h, wu_ref[...])
    a = (_silu(g) * u).astype(BF16)
    acc_ref[...] += _dot(a, wd_ref[...])

    @pl.when(j == pl.num_programs(1) - 1)
    def _():
        gate = mod_ref[0, mi + 2:mi + 3, :]
        o_ref[...] = x_in() + 0.5 * gate * _rms(acc_ref[...], npost_ref[...])


def _ffn_call(x, mod, npre, npost, wg, wu, wd, wsel, *, mi, n_ctx_rows, rows_per_req, x_row0=0, n_rows=None,
              g_row0=0, out_rows=None, out_row0=0, pos=None, base=None, tm=512, tf=512):
    d = x.shape[1]
    n_rows = x.shape[0] if n_rows is None else n_rows
    out_rows = n_rows if out_rows is None else out_rows
    ff = wg.shape[-1]
    wl, wk = wsel
    grp = functools.partial(_group_of_tile, tm=tm, n_ctx_rows=n_ctx_rows, rows_per_req=rows_per_req)
    xb0, gb0, ob0, tpr = x_row0 // tm, g_row0 // tm, out_row0 // tm, rows_per_req // tm
    const = lambda shape: pl.BlockSpec(shape, lambda i, j: (0, 0))
    in_specs = [pl.BlockSpec((tm, d), lambda i, j: (xb0 + i, 0))]
    args = [x]
    if pos is not None:
        in_specs.append(pl.BlockSpec((tm, d), lambda i, j: (i % tpr, 0)))
        args.append(pos)
    aliases = {}
    if base is not None:
        aliases = {len(args): 0}
        in_specs.append(pl.BlockSpec(memory_space=pl.ANY))
        args.append(base)
    in_specs += [pl.BlockSpec((1, N_MOD, d), lambda i, j: (grp(gb0 + i), 0, 0)),
                 const((1, d)), const((1, d)),
                 pl.BlockSpec((None, None, d, tf), lambda i, j: (wl, wk, 0, j)),
                 pl.BlockSpec((None, None, d, tf), lambda i, j: (wl, wk, 0, j)),
                 pl.BlockSpec((None, None, tf, d), lambda i, j: (wl, wk, j, 0))]
    args += [mod, npre.reshape(1, d), npost.reshape(1, d), wg, wu, wd]
    return pl.pallas_call(
        functools.partial(_ffn_body, mi=mi, has_pos=pos is not None, has_base=base is not None),
        grid=(n_rows // tm, ff // tf),
        in_specs=in_specs,
        out_specs=pl.BlockSpec((tm, d), lambda i, j: (ob0 + i, 0)),
        out_shape=jax.ShapeDtypeStruct((out_rows, d), F32),
        scratch_shapes=[pltpu.VMEM((tm, d), BF16), pltpu.VMEM((tm, d), F32)],
        input_output_aliases=aliases,
        compiler_params=_cparams(("parallel", "arbitrary")),
        name="ffn",
    )(*args)


def _inproj_body(x_ref, mod_ref, npre_ref, w_ref, wt_ref, o_ref, ot_ref, h_ref, *, mi, n_main):
    j = pl.program_id(1)

    @pl.when(j == 0)
    def _():
        shift = mod_ref[0, mi:mi + 1, :]
        scale = mod_ref[0, mi + 1:mi + 2, :]
        h = _rms(x_ref[...], npre_ref[...]) * (1.0 + scale) + shift
        h_ref[...] = h.astype(BF16)

    @pl.when(j < n_main)
    def _():
        o_ref[...] = _dot_nt(h_ref[...], w_ref[...])

    @pl.when(j == n_main)
    def _():
        ot_ref[...] = _dot_nt(h_ref[...], wt_ref[...])


def _inproj_call(x, mod, npre, w_in_t, layer, w_tail_t, *, mi, n_main_cols, n_ctx_rows, rows_per_req, tm=1024,
                 tn=512):
    m, d = x.shape
    n_main = n_main_cols // tn
    nt = w_tail_t.shape[0]
    grp = functools.partial(_group_of_tile, tm=tm, n_ctx_rows=n_ctx_rows, rows_per_req=rows_per_req)
    jm = lambda j: jnp.minimum(j, n_main - 1)
    return pl.pallas_call(
        functools.partial(_inproj_body, mi=mi, n_main=n_main),
        grid=(m // tm, n_main + 1),
        in_specs=[pl.BlockSpec((tm, d), lambda i, j: (i, 0)),
                  pl.BlockSpec((1, N_MOD, d), lambda i, j: (grp(i), 0, 0)),
                  pl.BlockSpec((1, d), lambda i, j: (0, 0)),
                  pl.BlockSpec((None, tn, d), lambda i, j: (layer, jm(j), 0)),
                  pl.BlockSpec((nt, d), lambda i, j: (0, 0))],
        out_specs=[pl.BlockSpec((tm, tn), lambda i, j: (i, jm(j))),
                   pl.BlockSpec((tm, nt), lambda i, j: (i, 0))],
        out_shape=[jax.ShapeDtypeStruct((m, n_main_cols), F32), jax.ShapeDtypeStruct((m, nt), F32)],
        scratch_shapes=[pltpu.VMEM((tm, d), BF16)],
        compiler_params=_cparams(("parallel", "arbitrary")),
        name="in_proj",
    )(x, mod, npre.reshape(1, d), w_in_t, w_tail_t)


def _conv3_rows(x_ref, w, r0, nr, seq_len):
    zeros = jnp.zeros((8, x_ref.shape[1]), x_ref.dtype)
    top = zeros if r0 % seq_len == 0 else x_ref[r0 - 8:r0, :]
    mid = x_ref[r0:r0 + nr, :]
    bot = zeros if (r0 + nr) % seq_len == 0 else x_ref[r0 + nr:r0 + nr + 8, :]
    return _conv3_slab(top, mid, bot, w)


def _conv3_rows_dyn(x_ref, w, r0, nr, seq_len):
    n = x_ref.shape[0]
    top = x_ref[pl.ds(pl.multiple_of(jnp.maximum(r0 - 8, 0), 8), 8), :]
    mid = x_ref[pl.ds(pl.multiple_of(r0, 8), nr), :]
    bot = x_ref[pl.ds(pl.multiple_of(jnp.minimum(r0 + nr, n - 8), 8), 8), :]
    top = jnp.where(r0 % seq_len == 0, 0.0, top)
    bot = jnp.where((r0 + nr) % seq_len == 0, 0.0, bot)
    return _conv3_slab(top, mid, bot, w)


def _conv3_slab(top, mid, bot, w):
    nr = mid.shape[0]
    slab = jnp.concatenate([top, mid, bot], axis=0)
    prev = pltpu.roll(slab, 1, 0)[8:8 + nr]
    nxt = pltpu.roll(slab, nr + 15, 0)[8:8 + nr]
    return prev * w[0:1, :] + mid * w[1:2, :] + nxt * w[2:3, :]


@functools.lru_cache(maxsize=None)
def _dft_tables_np(n_tok, r):
    big = 2 * n_tok
    k = np.arange(n_tok, dtype=np.int64)
    t = np.arange(n_tok, dtype=np.int64)
    ang = ((k[:, None] * t[None, :]) % big).astype(np.float64) * (2.0 * math.pi / big)
    cosm = np.cos(ang)
    sinm = np.sin(ang)
    sinm[0, :] = 1.0 - 2.0 * (t % 2)
    nj = n_tok // r
    fr = np.concatenate([cosm.reshape(nj, r, n_tok), sinm.reshape(nj, r, n_tok)], axis=1)
    g_full = fr.reshape(big, n_tok).T
    gr = np.concatenate([g_full[:, :n_tok].reshape(nj, r, n_tok), g_full[:, n_tok:].reshape(nj, r, n_tok)], axis=1)
    return np.concatenate([fr, gr], axis=0).astype(np.float32)


def _dft_tables(n_tok, r):
    return jnp.asarray(_dft_tables_np(n_tok, r)).astype(BF16)


def _spectra_body(featt_ref, t_ref, w1t_ref, b1_ref, w2t_ref, b2_ref, w3f_ref, w3b_ref, dec_ref, f_ref,
                  ha_ref, hb_ref, hc_ref, e_ref, o_ref, h_ref, *, n_tok, r):
    j = pl.program_id(1)

    @pl.when((pl.program_id(0) == 0) & (j == 0))
    def _():
        h = jnp.sin(HY_SIN_FREQ * (_dot(w1t_ref[...].astype(BF16), featt_ref[...].astype(BF16)) + b1_ref[...]))
        h = jnp.sin(HY_SIN_FREQ * (_dot(w2t_ref[...].astype(BF16), h.astype(BF16)) + b2_ref[...]))
        h_ref[...] = h.astype(BF16)

    @pl.when(j == 0)
    def _():
        hb16 = h_ref[...]
        window = jnp.exp(-t_ref[...] * jnp.abs(dec_ref[...])) + HY_WINDOW_SHIFT
        fwd = _dot_tn(hb16, w3f_ref[...].astype(BF16)) * window
        bwd = _dot_tn(hb16, w3b_ref[...].astype(BF16)) * window
        row = lax.broadcasted_iota(jnp.int32, bwd.shape, 0)
        bwd = jnp.where(row == 0, 0.0, bwd)
        norm = jnp.sum(jnp.abs(fwd), axis=0, keepdims=True) + jnp.sum(jnp.abs(bwd), axis=0, keepdims=True)
        fwd = fwd / norm
        bwd = bwd / norm
        e_ref[...] = (fwd + bwd).astype(BF16)
        o_ref[...] = (fwd - bwd).astype(BF16)

    fb = f_ref[0]
    e = e_ref[...]
    p = _dot(fb[:r], e)
    q = _dot(fb[r:], o_ref[...])
    alt = _dot(fb[r:r + 8], e)[0:1, :]
    k = j * r + lax.broadcasted_iota(jnp.int32, p.shape, 0)
    big = 2.0 * n_tok
    wk = jnp.where(k == 0, 1.0 / big, 2.0 / big)
    ha = wk * p
    ha_ref[...] = ha
    hb_ref[...] = jnp.where(k == 0, 0.0, -wk * q)
    hc_ref[...] = jnp.where(k == 0, alt * (1.0 / big), ha)


def _spectra_call(n_tok, w1, b1, w2, b2, w3, decay, fr, *, r, cn=512):
    f32 = jnp.float32
    order, c = decay.shape
    oc = order * c
    hid = w2.shape[0]
    emb = w1.shape[0]
    embp = 128
    idx = jnp.arange(n_tok, dtype=f32)
    t = idx / (n_tok - 1)
    bands = jnp.arange(1, HY_BANDS + 1, dtype=f32)
    ang = (2.0 * math.pi / n_tok) * idx[:, None] * bands[None, :]
    feats = jnp.concatenate([t[:, None], jnp.cos(ang), jnp.sin(ang)], axis=-1)
    featt = jnp.pad(feats, ((0, 0), (0, embp - emb))).T
    w1t = jnp.pad(w1, ((0, embp - emb), (0, 0))).T
    nj = n_tok // r
    ncb = oc // cn
    out = jax.ShapeDtypeStruct((n_tok, oc), F32)
    const = lambda shape: pl.BlockSpec(shape, lambda cb, j: (0, 0))
    return pl.pallas_call(
        functools.partial(_spectra_body, n_tok=n_tok, r=r),
        grid=(ncb, nj),
        in_specs=[const((embp, n_tok)), const((n_tok, 1)), const((hid, embp)), const((hid, 1)),
                  const((hid, hid)), const((hid, 1)),
                  pl.BlockSpec((hid, cn), lambda cb, j: (0, cb)),
                  pl.BlockSpec((hid, cn), lambda cb, j: (0, ncb + cb)),
                  pl.BlockSpec((1, cn), lambda cb, j: (0, cb)),
                  pl.BlockSpec((1, 2 * r, n_tok), lambda cb, j: (j, 0, 0))],
        out_specs=[pl.BlockSpec((r, cn), lambda cb, j: (j, cb))] * 3,
        out_shape=[out, out, out],
        scratch_shapes=[pltpu.VMEM((n_tok, cn), BF16), pltpu.VMEM((n_tok, cn), BF16),
                        pltpu.VMEM((hid, n_tok), BF16)],
        compiler_params=_cparams(("arbitrary", "arbitrary")),
        name="hyena_spectra",
    )(featt, t[:, None], w1t, b1.reshape(hid, 1), w2.T, b2.reshape(hid, 1), w3, w3, decay.reshape(1, oc), fr)


def _hyena_body(*refs, r, nj, seq_len, n_sub, has_base):
    refs = list(refs)
    if has_base:
        refs.pop(0)
    (v_ref, x1_ref, x2_ref, wv_ref, wx1_ref, wx2_ref, bias_ref, t_ref, ha_ref, hb_ref, hc_ref,
     o_ref, ub_ref, u2f_ref, z_ref) = refs
    t = pl.program_id(2)
    order = t // (2 * nj)
    tt = t % (2 * nj)
    sub = min(128, r)

    @pl.when(t == 0)
    def _():
        w = wv_ref[...]
        for r0 in range(0, n_sub * seq_len, 64):
            ub_ref[0, r0:r0 + 64, :] = _conv3_rows(v_ref, w, r0, 64, seq_len).astype(BF16)

    @pl.when(tt < nj)
    def _():
        tb = t_ref[0]
        ha, hb, hc = ha_ref[...], hb_ref[...], hc_ref[...]
        z0 = pl.multiple_of(tt * (2 * r), 2 * r)
        for s in range(n_sub):
            x = _dot(tb, ub_ref[order, s * seq_len:(s + 1) * seq_len, :])
            p = x[:r]
            q = x[r:]
            z_ref[s, pl.ds(z0, 2 * r), :] = jnp.concatenate([p * ha + q * hb, q * hc - p * hb],
                                                             axis=0).astype(BF16)

    def inverse_rows(s):
        tb = t_ref[0]
        return _dot(tb[:r], z_ref[s, :seq_len, :]) + _dot(tb[r:], z_ref[s, seq_len:, :])

    r_blk = (tt - nj) * r

    @pl.when((tt >= nj) & (order == 0))
    def _():
        wv, wx1 = wv_ref[...], wx1_ref[...]
        for s in range(n_sub):
            y = inverse_rows(s)
            for c0 in range(0, r, sub):
                r0 = s * seq_len + r_blk + c0
                u1 = _conv3_rows_dyn(v_ref, wv, r0, sub, seq_len)
                z1 = _conv3_rows_dyn(x1_ref, wx1, r0, sub, seq_len) * (y[c0:c0 + sub] + u1 * bias_ref[0:1, :])
                u2f_ref[pl.ds(pl.multiple_of(r0, 8), sub), :] = z1
                ub_ref[1, pl.ds(pl.multiple_of(r0, 8), sub), :] = z1.astype(BF16)

    @pl.when((tt >= nj) & (order == 1))
    def _():
        wx2 = wx2_ref[...]
        for s in range(n_sub):
            y = inverse_rows(s)
            for c0 in range(0, r, sub):
                r0 = s * seq_len + r_blk + c0
                u2 = u2f_ref[pl.ds(pl.multiple_of(r0, 8), sub), :]
                o_ref[pl.ds(pl.multiple_of(r0, 8), sub), :] = (
                    _conv3_rows_dyn(x2_ref, wx2, r0, sub, seq_len) * (y[c0:c0 + sub] + u2 * bias_ref[1:2, :]))


def _hyena_call(proj, conv_w, bias, tables, ha, hb, hc, *, n_seq, n_tok, row0, out_rows, base, c, r, n_sub,
                cn=512):
    nj = n_tok // r
    ncb = c // cn
    blk = n_sub * n_tok
    rb0 = row0 // blk
    steps = 4 * nj
    single = pl.Buffered(1)
    seq_spec = lambda off: pl.BlockSpec((blk, cn), lambda b, cb, t: (rb0 + b, off * ncb + cb),
                                        pipeline_mode=single)
    w_spec = lambda off: pl.BlockSpec((3, cn), lambda b, cb, t: (0, off * ncb + cb))
    h_spec = pl.BlockSpec((r, cn), lambda b, cb, t: (jnp.minimum(t % (2 * nj), nj - 1), (t // (2 * nj)) * ncb + cb))
    in_specs = [seq_spec(0), seq_spec(1), seq_spec(2), w_spec(0), w_spec(1), w_spec(2),
                pl.BlockSpec((2, cn), lambda b, cb, t: (0, cb)),
                pl.BlockSpec((1, 2 * r, n_tok), lambda b, cb, t: (t % (2 * nj), 0, 0)),
                h_spec, h_spec, h_spec]
    args = [proj, proj, proj, conv_w, conv_w, conv_w, bias, tables, ha, hb, hc]
    aliases = {}
    if base is not None:
        in_specs.insert(0, pl.BlockSpec(memory_space=pl.ANY))
        args.insert(0, base)
        aliases = {0: 0}
    return pl.pallas_call(
        functools.partial(_hyena_body, r=r, nj=nj, seq_len=n_tok, n_sub=n_sub, has_base=base is not None),
        grid=(n_seq // n_sub, ncb, steps),
        in_specs=in_specs,
        out_specs=pl.BlockSpec((blk, cn), lambda b, cb, t: (rb0 + b, cb)),
        out_shape=jax.ShapeDtypeStruct((out_rows, c), F32),
        scratch_shapes=[pltpu.VMEM((2, blk, cn), BF16), pltpu.VMEM((blk, cn), F32),
                        pltpu.VMEM((n_sub, 2 * n_tok, cn), BF16)],
        input_output_aliases=aliases,
        compiler_params=_cparams(("parallel", "parallel", "arbitrary")),
        name="hyena_conv",
    )(*args)


def _gdn_prep_body(x_ref, w_ref, o_ref, *, heads_per_blk, dk, n_qk_blk, rows, seq_len):
    cb = pl.program_id(1)
    n = x_ref.shape[0]
    w = w_ref[...]
    is_v = cb >= n_qk_blk
    scale = jnp.where(cb < n_qk_blk // 2, dk ** -0.5, 1.0)
    for r0 in range(0, n, rows):
        y = _silu(_conv3_rows(x_ref, w, r0, rows, seq_len))
        for h in range(heads_per_blk):
            yh = y[:, h * dk:(h + 1) * dk]
            inv = lax.rsqrt(jnp.sum(yh * yh, axis=-1, keepdims=True) + RMS_EPS) * scale
            o_ref[r0:r0 + rows, h * dk:(h + 1) * dk] = yh * jnp.where(is_v, 1.0, inv)


def _gdn_prep_call(proj, conv_w, *, n_seq, n_tok, row0, col0, d_gdn, dk, cn=512, blk_rows=2048):
    ncb = 3 * d_gdn // cn
    cb0 = col0 // cn
    rows = n_seq * n_tok
    blk_rows = min(blk_rows, rows)
    rb0 = row0 // blk_rows
    return pl.pallas_call(
        functools.partial(_gdn_prep_body, heads_per_blk=cn // dk, dk=dk, n_qk_blk=2 * d_gdn // cn,
                          rows=min(64, n_tok), seq_len=n_tok),
        grid=(rows // blk_rows, ncb),
        in_specs=[pl.BlockSpec((blk_rows, cn), lambda b, cb: (rb0 + b, cb0 + cb)),
                  pl.BlockSpec((3, cn), lambda b, cb: (0, cb))],
        out_specs=pl.BlockSpec((blk_rows, cn), lambda b, cb: (b, cb)),
        out_shape=jax.ShapeDtypeStruct((rows, 3 * d_gdn), F32),
        compiler_params=_cparams(("parallel", "parallel")),
        name="gdn_prep",
    )(proj, conv_w)


def _split3(x):
    hi = x.astype(BF16)
    r1 = x - hi.astype(F32)
    mid = r1.astype(BF16)
    lo = (r1 - mid.astype(F32)).astype(BF16)
    return hi, mid, lo


def _gdn_body(*refs, heads, dk, chunk, has_base, has_s0):
    refs = list(refs)
    if has_base:
        del refs[:2]
    (qf_ref, kf_ref, vf_ref, abf_ref, abtf_ref, qb_ref, kb_ref, vb_ref, abb_ref, abtb_ref,
     av_ref, dtv_ref, avt_ref, dtvt_ref) = refs[:14]
    s0_ref = refs[14] if has_s0 else None
    of_ref, ob_ref, sfin_ref, s_ref = refs[-4:]
    n = pl.program_id(1)

    @pl.when(n == 0)
    def _():
        s_ref[...] = s0_ref[0] if has_s0 else jnp.zeros_like(s_ref)

    ci = lax.broadcasted_iota(jnp.int32, (chunk, chunk), 0)
    cj = lax.broadcasted_iota(jnp.int32, (chunk, chunk), 1)
    eye = (ci == cj).astype(F32)
    diag_blk = (ci >> 3) == (cj >> 3)
    merge_blks = [((ci >> s) ^ (cj >> s)) == 1 for s in range(3, int(math.log2(chunk)))]
    lower = cj <= ci
    upper = cj >= ci

    units = []
    for dr, (q_ref, k_ref, v_ref, ab_ref, abt_ref, o_ref) in enumerate(
            ((qf_ref, kf_ref, vf_ref, abf_ref, abtf_ref, of_ref),
             (qb_ref, kb_ref, vb_ref, abb_ref, abtb_ref, ob_ref))):
        incl = upper if dr else lower
        strict = (cj > ci) if dr else (cj < ci)
        ones_incl = jnp.where(incl, 1.0, 0.0).astype(BF16)
        ones_incl_t = jnp.where(lower if dr else upper, 1.0, 0.0).astype(BF16)
        ab = ab_ref[...]
        g_col = -av_ref[...] * jax.nn.softplus(ab + dtv_ref[...])
        beta_col = jax.nn.sigmoid(ab)
        abt = abt_ref[...]
        g_row = -avt_ref[...] * jax.nn.softplus(abt + dtvt_ref[...])
        h3 = _split3(g_col)
        gc_col = _dot(ones_incl, h3[0]) + _dot(ones_incl, h3[1]) + _dot(ones_incl, h3[2])
        r3 = _split3(g_row)
        gc_row = _dot(r3[0], ones_incl_t) + _dot(r3[1], ones_incl_t) + _dot(r3[2], ones_incl_t)
        gl_all = jnp.sum(g_col, axis=0, keepdims=True)
        for h in range(heads):
            cg = dr * 2 * heads + h
            cb = cg + heads
            sl = slice(h * dk, (h + 1) * dk)
            units.append(dict(dr=dr, h=h, sl=sl, q_ref=q_ref, k_ref=k_ref, v_ref=v_ref, o_ref=o_ref,
                              incl=incl, strict=strict, gc=gc_col[:, cg:cg + 1], gr=gc_row[cg:cg + 1, :],
                              beta=beta_col[:, cb:cb + 1], gl=gl_all[:, cg:cg + 1]))

    def stage(fn):
        return [fn(u) for u in units]

    bf = lambda t: t.astype(BF16)
    k16 = stage(lambda u: bf(u["k_ref"][:, u["sl"]]))
    decay = stage(lambda u: jnp.where(u["incl"], jnp.exp(jnp.where(u["incl"], u["gc"] - u["gr"], 0.0)), 0.0))
    kbeta = stage(lambda u: u["k_ref"][:, u["sl"]] * u["beta"])
    a_low = [jnp.where(u["strict"], _dot_nt(bf(kb), k) * dc, 0.0)
             for u, kb, k, dc in zip(units, kbeta, k16, decay)]
    attn = [bf(_dot_nt(bf(u["q_ref"][:, u["sl"]]), k) * dc) for u, k, dc in zip(units, k16, decay)]

    d = [jnp.where(diag_blk, a, 0.0) for a in a_low]
    d2 = [_dot(bf(t), bf(t)) for t in d]
    x = [eye - t for t in d]
    x = [t + _dot(bf(t), bf(p)) for t, p in zip(x, d2)]
    d4 = [_dot(bf(p), bf(p)) for p in d2]
    x = [t + _dot(bf(t), bf(p)) for t, p in zip(x, d4)]
    for off_blk in merge_blks:
        x16 = [bf(t) for t in x]
        tmp = [_dot(bf(jnp.where(off_blk, a, 0.0)), t16) for a, t16 in zip(a_low, x16)]
        x = [t - _dot(t16, bf(m)) for t, t16, m in zip(x, x16, tmp)]

    e_gc = stage(lambda u: jnp.exp(u["gc"]))
    rhs = [bf(jnp.concatenate([u["v_ref"][:, u["sl"]] * u["beta"], kb * e], axis=-1))
           for u, kb, e in zip(units, kbeta, e_gc)]
    sol = [_dot(bf(t), r) for t, r in zip(x, rhs)]

    s16 = stage(lambda u: bf(s_ref[u["dr"], u["h"]]))
    v_new = [so[:, :dk] - _dot(bf(so[:, dk:]), s) for so, s in zip(sol, s16)]
    vn16 = [bf(t) for t in v_new]
    o = [_dot(bf(u["q_ref"][:, u["sl"]] * e), s) + _dot(at, vn)
         for u, e, s, at, vn in zip(units, e_gc, s16, attn, vn16)]
    for u, t in zip(units, o):
        u["o_ref"][:, u["sl"]] = t
    s_new = [s_ref[u["dr"], u["h"]] * jnp.exp(u["gl"])
             + _dot_tn(bf(u["k_ref"][:, u["sl"]] * jnp.exp(u["gl"] - u["gc"])), vn)
             for u, vn in zip(units, vn16)]
    for u, t in zip(units, s_new):
        s_ref[u["dr"], u["h"]] = t

    @pl.when(n == pl.num_programs(1) - 1)
    def _():
        sfin_ref[0] = s_ref[...]


def _gdn_call(qkv, ab_src, abt, av, dtv, avt, dtvt, s0, *, n_seq, n_tok, row0, out_rows, bases, ab_col_blk,
              heads, dk, chunk):
    nc = n_tok // chunk
    d = heads * dk
    rb0 = row0 // chunk
    fwd = lambda b, n: b * nc + n
    bwd = lambda b, n: b * nc + nc - 1 - n
    qkv_spec = lambda idx, which: pl.BlockSpec((chunk, d), lambda b, n: (idx(b, n), which))
    ab_spec = lambda idx: pl.BlockSpec((chunk, 128), lambda b, n: (rb0 + idx(b, n), ab_col_blk))
    abt_spec = lambda idx: pl.BlockSpec((4 * heads, chunk), lambda b, n: (0, idx(b, n)))
    small = lambda shape: pl.BlockSpec(shape, lambda b, n: (0, 0))
    st_spec = pl.BlockSpec((1, 2, heads, dk, dk), lambda b, n: (b, 0, 0, 0, 0))
    o_shape = jax.ShapeDtypeStruct((out_rows, d), F32)
    in_specs = [qkv_spec(fwd, 0), qkv_spec(fwd, 1), qkv_spec(fwd, 2), ab_spec(fwd), abt_spec(fwd),
                qkv_spec(bwd, 0), qkv_spec(bwd, 1), qkv_spec(bwd, 2), ab_spec(bwd), abt_spec(bwd),
                small((1, 128)), small((1, 128)), small((4 * heads, 1)), small((4 * heads, 1))]
    args = [qkv, qkv, qkv, ab_src, abt, qkv, qkv, qkv, ab_src, abt, av, dtv, avt, dtvt]
    if s0 is not None:
        in_specs.append(st_spec)
        args.append(s0)
    aliases = {}
    if bases is not None:
        in_specs = [pl.BlockSpec(memory_space=pl.ANY)] * 2 + in_specs
        args = list(bases) + args
        aliases = {0: 0, 1: 1}
    return pl.pallas_call(
        functools.partial(_gdn_body, heads=heads, dk=dk, chunk=chunk, has_base=bases is not None,
                          has_s0=s0 is not None),
        grid=(n_seq, nc),
        in_specs=in_specs,
        out_specs=[pl.BlockSpec((chunk, d), lambda b, n: (rb0 + fwd(b, n), 0)),
                   pl.BlockSpec((chunk, d), lambda b, n: (rb0 + bwd(b, n), 0)),
                   st_spec],
        out_shape=[o_shape, o_shape, jax.ShapeDtypeStruct((n_seq, 2, heads, dk, dk), F32)],
        scratch_shapes=[pltpu.VMEM((2, heads, dk, dk), F32)],
        input_output_aliases=aliases,
        compiler_params=_cparams(("parallel", "arbitrary")),
        name="gdn_scan",
    )(*args)


def _outproj_body(x_ref, mod_ref, zhy_ref, of_ref, ob_ref, zg_ref, ghy_ref, go_ref, npost_ref, w_ref,
                  o_ref, *, mi, heads, dk):
    yhy = _rms(zhy_ref[...], ghy_ref[...]).astype(BF16)
    o = of_ref[...] + ob_ref[...]
    zg = zg_ref[...]
    parts = [yhy]
    for h in range(heads):
        sl = slice(h * dk, (h + 1) * dk)
        parts.append((_rms(o[:, sl], go_ref[...]) * _silu(zg[:, sl])).astype(BF16))
    y = _dot(jnp.concatenate(parts, axis=-1), w_ref[...])
    gate = mod_ref[0, mi:mi + 1, :]
    o_ref[...] = x_ref[...] + gate * _rms(y, npost_ref[...])


def _outproj_call(x, mod, zhy, o_f, o_b, proj, ghy, go, npost, w, *, mi, zg_col_blk, n_ctx_rows, rows_per_req,
                  heads, dk, tm=512):
    m, d = x.shape
    c = zhy.shape[1]
    dg = heads * dk
    grp = functools.partial(_group_of_tile, tm=tm, n_ctx_rows=n_ctx_rows, rows_per_req=rows_per_req)
    row = lambda width: pl.BlockSpec((tm, width), lambda i: (i, 0))
    return pl.pallas_call(
        functools.partial(_outproj_body, mi=mi, heads=heads, dk=dk),
        grid=(m // tm,),
        in_specs=[row(d),
                  pl.BlockSpec((1, N_MOD, d), lambda i: (grp(i), 0, 0)),
                  row(c), row(dg), row(dg),
                  pl.BlockSpec((tm, dg), lambda i: (i, zg_col_blk)),
                  pl.BlockSpec((1, c), lambda i: (0, 0)),
                  pl.BlockSpec((1, dk), lambda i: (0, 0)),
                  pl.BlockSpec((1, d), lambda i: (0, 0)),
                  pl.BlockSpec((c + dg, d), lambda i: (0, 0))],
        out_specs=row(d),
        out_shape=jax.ShapeDtypeStruct((m, d), F32),
        compiler_params=_cparams(("parallel",)),
        name="out_proj",
    )(x, mod, zhy, o_f, o_b, proj, ghy.reshape(1, c), go.reshape(1, dk), npost.reshape(1, d), w)


@functools.lru_cache(maxsize=None)
def _grid_pos_embed_np(n_tok, dim):
    rows = n_tok // GRID_W
    r = np.repeat(np.arange(rows, dtype=np.float64), GRID_W)
    col = np.tile(np.arange(GRID_W, dtype=np.float64), rows)
    quarter = dim // 4
    omega = 1.0 / (POS_BASE ** (np.arange(quarter, dtype=np.float64) / quarter))
    ar = r[:, None] * omega[None]
    ac = col[:, None] * omega[None]
    return np.concatenate([np.sin(ar), np.cos(ar), np.sin(ac), np.cos(ac)], axis=-1).astype(np.float32)


def _mixer(proj, proj_ab, *, groups, hy_conv_w, hy_f, hy_decay, hy_bias, gdn_conv_w, gdn_a_log, gdn_dt_bias,
           c_hy, d_gdn, states):
    heads = GDN_HEADS
    dk = d_gdn // heads
    o0 = 3 * c_hy

    a_exp = jnp.exp(gdn_a_log.astype(F32))
    zeros = jnp.zeros_like(a_exp)
    av32 = jnp.stack([a_exp, zeros], axis=1).reshape(-1)
    dtv32 = jnp.stack([gdn_dt_bias.astype(F32), zeros], axis=1).reshape(-1)
    pad = 128 - 4 * heads
    av = jnp.pad(av32, (0, pad)).reshape(1, 128)
    dtv = jnp.pad(dtv32, (0, pad)).reshape(1, 128)
    avt = av32.reshape(4 * heads, 1)
    dtvt = dtv32.reshape(4 * heads, 1)
    abt_all = jnp.transpose(proj_ab[:, :4 * heads])

    all_rows = proj.shape[0]
    zhy, o_fb, sfs = None, None, []
    for (n_seq, n_tok, row0), s0 in zip(groups, states):
        r = min(512, n_tok)
        tables = _dft_tables(n_tok, r)
        ha, hb, hc = _spectra_call(n_tok, *hy_f, hy_decay, tables, r=r)
        zhy = _hyena_call(proj, hy_conv_w, hy_bias, tables, ha, hb, hc, n_seq=n_seq, n_tok=n_tok, row0=row0,
                          out_rows=all_rows, base=zhy, c=c_hy, r=r, n_sub=max(1, min(n_seq, 1024 // n_tok)))
        qkv = _gdn_prep_call(proj, gdn_conv_w, n_seq=n_seq, n_tok=n_tok, row0=row0, col0=o0, d_gdn=d_gdn, dk=dk)
        abt = lax.slice(abt_all, (0, row0), (4 * heads, row0 + n_seq * n_tok))
        o_f, o_b, s_fin = _gdn_call(qkv, proj_ab, abt, av, dtv, avt, dtvt, s0, n_seq=n_seq, n_tok=n_tok,
                                    row0=row0, out_rows=all_rows, bases=o_fb, ab_col_blk=0, heads=heads, dk=dk,
                                    chunk=GDN_CHUNK)
        o_fb = (o_f, o_b)
        sfs.append(s_fin)
    return zhy, o_fb[0], o_fb[1], sfs


def kernel(x_prompt, x_sample, state_gdn, c, c_ctx, ada_w, ada_b, norm_pre, norm_post, ffn_wg, ffn_wu, ffn_wd,
           w_in, w_out, hy_conv_w, hy_f_w1, hy_f_b1, hy_f_w2, hy_f_b2, hy_f_w3, hy_decay, hy_bias, hy_out_norm,
           gdn_conv_w, gdn_a_log, gdn_dt_bias, gdn_o_norm):
    nb, seq, d = x_prompt.shape
    db, dseq, _ = x_sample.shape
    depth = ada_w.shape[0]
    c_hy = hy_decay.shape[-1]
    d_gdn = gdn_conv_w.shape[-1] // 3
    heads = GDN_HEADS
    dk = d_gdn // heads
    n_ctx_rows = nb * seq
    grp = dict(n_ctx_rows=n_ctx_rows, rows_per_req=dseq)

    n_lat_rows = db * dseq
    n_rows = n_ctx_rows + n_lat_rows
    pos = jnp.asarray(_grid_pos_embed_np(dseq, d)).astype(x_sample.dtype)
    cvec = jnp.concatenate([c_ctx[None], c, jnp.zeros((8 - 1 - db, d), F32)], axis=0)
    wg, wu, wd = ffn_wg.astype(BF16), ffn_wu.astype(BF16), ffn_wd.astype(BF16)
    n_main_cols = 3 * c_hy + 4 * d_gdn
    n_tail = w_in.shape[-1] - n_main_cols
    w_in_t = jnp.swapaxes(w_in, 1, 2).astype(BF16)

    x = None
    ctx_states = []
    for l in range(depth):
        mod = _mod_call(cvec, ada_w[l], ada_b[l]).reshape(8, N_MOD, d)
        w_tail_t = jnp.pad(w_in_t[l, n_main_cols:], ((0, 128 - n_tail), (0, 0)))

        ffn0 = (mod, norm_pre[l, 0], norm_post[l, 0], wg, wu, wd, (l, 0))
        if l == 0:
            x = _ffn_call(x_prompt.reshape(n_ctx_rows, d), *ffn0, mi=0, out_rows=n_rows, **grp)
            x = _ffn_call(x_sample.reshape(n_lat_rows, d), *ffn0, mi=0, g_row0=n_ctx_rows, out_rows=n_rows,
                          out_row0=n_ctx_rows, pos=pos, base=x, **grp)
        else:
            x = _ffn_call(x, *ffn0, mi=0, **grp)

        proj, proj_ab = _inproj_call(x, mod, norm_pre[l, 1], w_in_t, l, w_tail_t, mi=3, n_main_cols=n_main_cols,
                                     **grp)
        zhy, o_f, o_b, sfs = _mixer(
            proj, proj_ab, groups=[(nb, seq, 0), (db, dseq, n_ctx_rows)], hy_conv_w=hy_conv_w[l],
            hy_f=(hy_f_w1[l], hy_f_b1[l], hy_f_w2[l], hy_f_b2[l], hy_f_w3[l]), hy_decay=hy_decay[l],
            hy_bias=hy_bias[l], gdn_conv_w=gdn_conv_w[l], gdn_a_log=gdn_a_log[l], gdn_dt_bias=gdn_dt_bias[l],
            c_hy=c_hy, d_gdn=d_gdn, states=[None, state_gdn[:, l].astype(F32)])
        ctx_states.append(sfs[0])
        x = _outproj_call(x, mod, zhy, o_f, o_b, proj, hy_out_norm[l], gdn_o_norm[l], norm_post[l, 1],
                          w_out[l].astype(BF16), mi=5, zg_col_blk=(3 * c_hy + 3 * d_gdn) // d_gdn, heads=heads,
                          dk=dk, **grp)

        ffn1 = (mod, norm_pre[l, 2], norm_post[l, 2], wg, wu, wd, (l, 1))
        if l == depth - 1:
            y_prompt = _ffn_call(x, *ffn1, mi=6, n_rows=n_ctx_rows, **grp)
            y_sample = _ffn_call(x, *ffn1, mi=6, x_row0=n_ctx_rows, n_rows=n_lat_rows, g_row0=n_ctx_rows, **grp)
        else:
            x = _ffn_call(x, *ffn1, mi=6, **grp)

    new_state = ctx_states[0][:, None] if depth == 1 else jnp.stack(ctx_states, axis=1)
    return y_prompt.reshape(nb, seq, d), y_sample.reshape(db, dseq, d), new_state
```

```python
import functools
import math

import numpy as np
import jax
import jax.numpy as jnp
from jax import lax
from jax.experimental import pallas as pl
from jax.experimental.pallas import tpu as pltpu

F32 = jnp.float32
BF16 = jnp.bfloat16

N_MOD = 9
RMS_EPS = 1e-6
GRID_W = 64
POS_BASE = 10000.0
HY_BANDS = 16
HY_SIN_FREQ = 1.0
HY_WINDOW_SHIFT = 0.05
GDN_HEADS = 8
GDN_CHUNK = 128
VMEM_LIMIT = 56 * 1024 * 1024
ROW_CHUNK = 16


def _cparams(sem):
    return pltpu.CompilerParams(dimension_semantics=sem, vmem_limit_bytes=VMEM_LIMIT)


def _dot(a, b):
    return jnp.dot(a, b, preferred_element_type=F32)


def _dot_nt(a, b):
    return lax.dot_general(a, b, (((1,), (1,)), ((), ())), preferred_element_type=F32)


def _dot_tn(a, b):
    return lax.dot_general(a, b, (((0,), (0,)), ((), ())), preferred_element_type=F32)


def _rms(x, g):
    ms = jnp.mean(x * x, axis=-1, keepdims=True)
    return x * lax.rsqrt(ms + RMS_EPS) * g


def _silu(x):
    return x * jax.nn.sigmoid(x)


def _for_row_chunks(n_rows, fn):
    def body(c, carry):
        fn(pl.ds(pl.multiple_of(c * ROW_CHUNK, ROW_CHUNK), ROW_CHUNK))
        return carry

    lax.fori_loop(0, n_rows // ROW_CHUNK, body, 0, unroll=8)


def _mod_spec(d, tm, g0, rows_per_req):
    tpr = rows_per_req // tm
    return pl.BlockSpec((1, N_MOD, d), lambda i, *_: (g0 + i // tpr, 0, 0))


def _mod_body(c_ref, w_ref, b_ref, o_ref):
    s = _silu(c_ref[...]).astype(BF16)
    o_ref[...] = _dot(s, w_ref[...].astype(BF16)) + b_ref[...]


def _mod_call(cvec, ada_w, ada_b):
    g, d = cvec.shape
    n = ada_w.shape[1]
    tn = 1024
    return pl.pallas_call(
        _mod_body,
        grid=(n // tn,),
        in_specs=[pl.BlockSpec((g, d), lambda j: (0, 0)),
                  pl.BlockSpec((d, tn), lambda j: (0, j)),
                  pl.BlockSpec((1, tn), lambda j: (0, j))],
        out_specs=pl.BlockSpec((g, tn), lambda j: (0, j)),
        out_shape=jax.ShapeDtypeStruct((g, n), F32),
        compiler_params=_cparams(("arbitrary",)),
        name="mod_table",
    )(cvec, ada_w, ada_b.reshape(1, n))


def _ffn_body(*refs, mi, has_pos):
    refs = list(refs)
    x_ref = refs.pop(0)
    pos_ref = refs.pop(0) if has_pos else None
    mod_ref, npre_ref, npost_ref, wg_ref, wu_ref, wd_ref, o_ref, h_ref, acc_ref = refs
    j = pl.program_id(1)
    tm = x_ref.shape[0]

    def x_rows(rows):
        return x_ref[rows, :] + pos_ref[rows, :] if has_pos else x_ref[rows, :]

    @pl.when(j == 0)
    def _():
        shift = mod_ref[0, mi:mi + 1, :]
        gain = npre_ref[...] * (1.0 + mod_ref[0, mi + 1:mi + 2, :])

        def rows_body(rows):
            h_ref[rows, :] = (_rms(x_rows(rows), gain) + shift).astype(BF16)
            acc_ref[rows, :] = jnp.zeros((ROW_CHUNK, acc_ref.shape[1]), F32)

        _for_row_chunks(tm, rows_body)

    h = h_ref[...]
    g = _dot(h, wg_ref[...])
    u = _dot(h, wu_ref[...])
    a = (_silu(g) * u).astype(BF16)
    acc_ref[...] += _dot(a, wd_ref[...])

    @pl.when(j == pl.num_programs(1) - 1)
    def _():
        gain = npost_ref[...] * (0.5 * mod_ref[0, mi + 2:mi + 3, :])

        def rows_body(rows):
            o_ref[rows, :] = x_rows(rows) + _rms(acc_ref[rows, :], gain)

        _for_row_chunks(tm, rows_body)


def _ffn_call(x, mod, npre, npost, wg, wu, wd, wsel, *, mi, g0, rows_per_req, pos=None, tm=512, tf=512):
    n_rows, d = x.shape
    ff = wg.shape[-1]
    wl, wk = wsel
    tpr = rows_per_req // tm
    const = lambda shape: pl.BlockSpec(shape, lambda i, j: (0, 0))
    in_specs = [pl.BlockSpec((tm, d), lambda i, j: (i, 0))]
    args = [x]
    if pos is not None:
        in_specs.append(pl.BlockSpec((tm, d), lambda i, j: (i % tpr, 0)))
        args.append(pos)
    in_specs += [_mod_spec(d, tm, g0, rows_per_req),
                 const((1, d)), const((1, d)),
                 pl.BlockSpec((None, None, d, tf), lambda i, j: (wl, wk, 0, j)),
                 pl.BlockSpec((None, None, d, tf), lambda i, j: (wl, wk, 0, j)),
                 pl.BlockSpec((None, None, tf, d), lambda i, j: (wl, wk, j, 0))]
    args += [mod, npre.reshape(1, d), npost.reshape(1, d), wg, wu, wd]
    return pl.pallas_call(
        functools.partial(_ffn_body, mi=mi, has_pos=pos is not None),
        grid=(n_rows // tm, ff // tf),
        in_specs=in_specs,
        out_specs=pl.BlockSpec((tm, d), lambda i, j: (i, 0)),
        out_shape=jax.ShapeDtypeStruct((n_rows, d), F32),
        scratch_shapes=[pltpu.VMEM((tm, d), BF16), pltpu.VMEM((tm, d), F32)],
        compiler_params=_cparams(("parallel", "arbitrary")),
        name="ffn",
    )(*args)


def _inproj_body(x_ref, mod_ref, npre_ref, w_ref, wt_ref, o_ref, ot_ref, h_ref, *, mi, n_main):
    j = pl.program_id(1)

    @pl.when(j == 0)
    def _():
        shift = mod_ref[0, mi:mi + 1, :]
        gain = npre_ref[...] * (1.0 + mod_ref[0, mi + 1:mi + 2, :])

        def rows_body(rows):
            h_ref[rows, :] = (_rms(x_ref[rows, :], gain) + shift).astype(BF16)

        _for_row_chunks(x_ref.shape[0], rows_body)

    @pl.when(j < n_main)
    def _():
        o_ref[...] = _dot_nt(h_ref[...], w_ref[...])

    @pl.when(j == n_main)
    def _():
        ot_ref[...] = _dot_nt(h_ref[...], wt_ref[...])


def _inproj_call(x, mod, npre, w_in_t, layer, w_tail_t, *, mi, n_main_cols, g0, rows_per_req, tm=1024, tn=512):
    m, d = x.shape
    n_main = n_main_cols // tn
    nt = w_tail_t.shape[0]
    jm = lambda j: jnp.minimum(j, n_main - 1)
    return pl.pallas_call(
        functools.partial(_inproj_body, mi=mi, n_main=n_main),
        grid=(m // tm, n_main + 1),
        in_specs=[pl.BlockSpec((tm, d), lambda i, j: (i, 0)),
                  _mod_spec(d, tm, g0, rows_per_req),
                  pl.BlockSpec((1, d), lambda i, j: (0, 0)),
                  pl.BlockSpec((None, tn, d), lambda i, j: (layer, jm(j), 0)),
                  pl.BlockSpec((nt, d), lambda i, j: (0, 0))],
        out_specs=[pl.BlockSpec((tm, tn), lambda i, j: (i, jm(j))),
                   pl.BlockSpec((tm, nt), lambda i, j: (i, 0))],
        out_shape=[jax.ShapeDtypeStruct((m, n_main_cols), F32), jax.ShapeDtypeStruct((m, nt), F32)],
        scratch_shapes=[pltpu.VMEM((tm, d), BF16)],
        compiler_params=_cparams(("parallel", "arbitrary")),
        name="in_proj",
    )(x, mod, npre.reshape(1, d), w_in_t, w_tail_t)


def _conv3_rows(x_ref, w, r0, nr, seq_len):
    zeros = jnp.zeros((8, x_ref.shape[1]), x_ref.dtype)
    top = zeros if r0 % seq_len == 0 else x_ref[r0 - 8:r0, :]
    mid = x_ref[r0:r0 + nr, :]
    bot = zeros if (r0 + nr) % seq_len == 0 else x_ref[r0 + nr:r0 + nr + 8, :]
    return _conv3_slab(top, mid, bot, w)


def _conv3_rows_dyn(x_ref, w, r0, nr, seq_len):
    n = x_ref.shape[0]
    top = x_ref[pl.ds(pl.multiple_of(jnp.maximum(r0 - 8, 0), 8), 8), :]
    mid = x_ref[pl.ds(pl.multiple_of(r0, 8), nr), :]
    bot = x_ref[pl.ds(pl.multiple_of(jnp.minimum(r0 + nr, n - 8), 8), 8), :]
    top = jnp.where(r0 % seq_len == 0, 0.0, top)
    bot = jnp.where((r0 + nr) % seq_len == 0, 0.0, bot)
    return _conv3_slab(top, mid, bot, w)


def _conv3_slab(top, mid, bot, w):
    nr = mid.shape[0]
    slab = jnp.concatenate([top, mid, bot], axis=0)
    prev = pltpu.roll(slab, 1, 0)[8:8 + nr]
    nxt = pltpu.roll(slab, nr + 15, 0)[8:8 + nr]
    return prev * w[0:1, :] + mid * w[1:2, :] + nxt * w[2:3, :]


@functools.lru_cache(maxsize=None)
def _dft_tables_np(n_tok, r):
    big = 2 * n_tok
    k = np.arange(n_tok, dtype=np.int64)
    t = np.arange(n_tok, dtype=np.int64)
    ang = ((k[:, None] * t[None, :]) % big).astype(np.float64) * (2.0 * math.pi / big)
    cosm = np.cos(ang)
    sinm = np.sin(ang)
    sinm[0, :] = 1.0 - 2.0 * (t % 2)
    nj = n_tok // r
    fr = np.concatenate([cosm.reshape(nj, r, n_tok), sinm.reshape(nj, r, n_tok)], axis=1)
    g_full = fr.reshape(big, n_tok).T
    gr = np.concatenate([g_full[:, :n_tok].reshape(nj, r, n_tok), g_full[:, n_tok:].reshape(nj, r, n_tok)], axis=1)
    return np.concatenate([fr, gr], axis=0).astype(np.float32)


def _dft_tables(n_tok, r):
    return jnp.asarray(_dft_tables_np(n_tok, r)).astype(BF16)


def _spectra_body(featt_ref, t_ref, w1t_ref, b1_ref, w2t_ref, b2_ref, w3f_ref, w3b_ref, dec_ref, f_ref,
                  ha_ref, hb_ref, hc_ref, e_ref, o_ref, h_ref, *, n_tok, r):
    j = pl.program_id(1)

    @pl.when((pl.program_id(0) == 0) & (j == 0))
    def _():
        h = jnp.sin(HY_SIN_FREQ * (_dot(w1t_ref[...].astype(BF16), featt_ref[...].astype(BF16)) + b1_ref[...]))
        h = jnp.sin(HY_SIN_FREQ * (_dot(w2t_ref[...].astype(BF16), h.astype(BF16)) + b2_ref[...]))
        h_ref[...] = h.astype(BF16)

    @pl.when(j == 0)
    def _():
        hb16 = h_ref[...]
        window = jnp.exp(-t_ref[...] * jnp.abs(dec_ref[...])) + HY_WINDOW_SHIFT
        fwd = _dot_tn(hb16, w3f_ref[...].astype(BF16)) * window
        bwd = _dot_tn(hb16, w3b_ref[...].astype(BF16)) * window
        row = lax.broadcasted_iota(jnp.int32, bwd.shape, 0)
        bwd = jnp.where(row == 0, 0.0, bwd)
        norm = jnp.sum(jnp.abs(fwd), axis=0, keepdims=True) + jnp.sum(jnp.abs(bwd), axis=0, keepdims=True)
        fwd = fwd / norm
        bwd = bwd / norm
        e_ref[...] = (fwd + bwd).astype(BF16)
        o_ref[...] = (fwd - bwd).astype(BF16)

    fb = f_ref[0]
    e = e_ref[...]
    p = _dot(fb[:r], e)
    q = _dot(fb[r:], o_ref[...])
    alt = _dot(fb[r:r + 8], e)[0:1, :]
    k = j * r + lax.broadcasted_iota(jnp.int32, p.shape, 0)
    big = 2.0 * n_tok
    wk = jnp.where(k == 0, 1.0 / big, 2.0 / big)
    ha = wk * p
    ha_ref[...] = ha
    hb_ref[...] = jnp.where(k == 0, 0.0, -wk * q)
    hc_ref[...] = jnp.where(k == 0, alt * (1.0 / big), ha)


def _spectra_call(n_tok, w1, b1, w2, b2, w3, decay, fr, *, r, cn=512):
    f32 = jnp.float32
    order, c = decay.shape
    oc = order * c
    hid = w2.shape[0]
    emb = w1.shape[0]
    embp = 128
    idx = jnp.arange(n_tok, dtype=f32)
    t = idx / (n_tok - 1)
    bands = jnp.arange(1, HY_BANDS + 1, dtype=f32)
    ang = (2.0 * math.pi / n_tok) * idx[:, None] * bands[None, :]
    feats = jnp.concatenate([t[:, None], jnp.cos(ang), jnp.sin(ang)], axis=-1)
    featt = jnp.pad(feats, ((0, 0), (0, embp - emb))).T
    w1t = jnp.pad(w1, ((0, embp - emb), (0, 0))).T
    nj = n_tok // r
    ncb = oc // cn
    out = jax.ShapeDtypeStruct((n_tok, oc), F32)
    const = lambda shape: pl.BlockSpec(shape, lambda cb, j: (0, 0))
    return pl.pallas_call(
        functools.partial(_spectra_body, n_tok=n_tok, r=r),
        grid=(ncb, nj),
        in_specs=[const((embp, n_tok)), const((n_tok, 1)), const((hid, embp)), const((hid, 1)),
                  const((hid, hid)), const((hid, 1)),
                  pl.BlockSpec((hid, cn), lambda cb, j: (0, cb)),
                  pl.BlockSpec((hid, cn), lambda cb, j: (0, ncb + cb)),
                  pl.BlockSpec((1, cn), lambda cb, j: (0, cb)),
                  pl.BlockSpec((1, 2 * r, n_tok), lambda cb, j: (j, 0, 0))],
        out_specs=[pl.BlockSpec((r, cn), lambda cb, j: (j, cb))] * 3,
        out_shape=[out, out, out],
        scratch_shapes=[pltpu.VMEM((n_tok, cn), BF16), pltpu.VMEM((n_tok, cn), BF16),
                        pltpu.VMEM((hid, n_tok), BF16)],
        compiler_params=_cparams(("arbitrary", "arbitrary")),
        name="hyena_spectra",
    )(featt, t[:, None], w1t, b1.reshape(hid, 1), w2.T, b2.reshape(hid, 1), w3, w3, decay.reshape(1, oc), fr)


def _hyena_body(v_ref, x1_ref, x2_ref, wv_ref, wx1_ref, wx2_ref, bias_ref, t_ref, ha_ref, hb_ref, hc_ref,
                o_ref, ub_ref, u2f_ref, z_ref, *, r, nj, seq_len, n_sub):
    t = pl.program_id(2)
    order = t // (2 * nj)
    tt = t % (2 * nj)
    sub = min(128, r)

    @pl.when(t == 0)
    def _():
        w = wv_ref[...]
        for r0 in range(0, n_sub * seq_len, 64):
            ub_ref[0, r0:r0 + 64, :] = _conv3_rows(v_ref, w, r0, 64, seq_len).astype(BF16)

    @pl.when(tt < nj)
    def _():
        tb = t_ref[0]
        ha, hb, hc = ha_ref[...], hb_ref[...], hc_ref[...]
        z0 = pl.multiple_of(tt * (2 * r), 2 * r)
        for s in range(n_sub):
            x = _dot(tb, ub_ref[order, s * seq_len:(s + 1) * seq_len, :])
            p = x[:r]
            q = x[r:]
            z_ref[s, pl.ds(z0, 2 * r), :] = jnp.concatenate([p * ha + q * hb, q * hc - p * hb],
                                                             axis=0).astype(BF16)

    def inverse_rows(s):
        tb = t_ref[0]
        return _dot(tb[:r], z_ref[s, :seq_len, :]) + _dot(tb[r:], z_ref[s, seq_len:, :])

    r_blk = (tt - nj) * r

    @pl.when((tt >= nj) & (order == 0))
    def _():
        wv, wx1 = wv_ref[...], wx1_ref[...]
        for s in range(n_sub):
            y = inverse_rows(s)
            for c0 in range(0, r, sub):
                r0 = s * seq_len + r_blk + c0
                u1 = _conv3_rows_dyn(v_ref, wv, r0, sub, seq_len)
                z1 = _conv3_rows_dyn(x1_ref, wx1, r0, sub, seq_len) * (y[c0:c0 + sub] + u1 * bias_ref[0:1, :])
                u2f_ref[pl.ds(pl.multiple_of(r0, 8), sub), :] = z1
                ub_ref[1, pl.ds(pl.multiple_of(r0, 8), sub), :] = z1.astype(BF16)

    @pl.when((tt >= nj) & (order == 1))
    def _():
        wx2 = wx2_ref[...]
        for s in range(n_sub):
            y = inverse_rows(s)
            for c0 in range(0, r, sub):
                r0 = s * seq_len + r_blk + c0
                u2 = u2f_ref[pl.ds(pl.multiple_of(r0, 8), sub), :]
                o_ref[pl.ds(pl.multiple_of(r0, 8), sub), :] = (
                    _conv3_rows_dyn(x2_ref, wx2, r0, sub, seq_len) * (y[c0:c0 + sub] + u2 * bias_ref[1:2, :]))


def _hyena_call(proj, conv_w, bias, tables, ha, hb, hc, *, n_seq, n_tok, c, r, n_sub, cn=512):
    nj = n_tok // r
    ncb = c // cn
    blk = n_sub * n_tok
    steps = 4 * nj
    single = pl.Buffered(1)
    seq_spec = lambda off: pl.BlockSpec((blk, cn), lambda b, cb, t: (b, off * ncb + cb), pipeline_mode=single)
    w_spec = lambda off: pl.BlockSpec((3, cn), lambda b, cb, t: (0, off * ncb + cb))
    h_spec = pl.BlockSpec((r, cn), lambda b, cb, t: (jnp.minimum(t % (2 * nj), nj - 1), (t // (2 * nj)) * ncb + cb))
    in_specs = [seq_spec(0), seq_spec(1), seq_spec(2), w_spec(0), w_spec(1), w_spec(2),
                pl.BlockSpec((2, cn), lambda b, cb, t: (0, cb)),
                pl.BlockSpec((1, 2 * r, n_tok), lambda b, cb, t: (t % (2 * nj), 0, 0)),
                h_spec, h_spec, h_spec]
    return pl.pallas_call(
        functools.partial(_hyena_body, r=r, nj=nj, seq_len=n_tok, n_sub=n_sub),
        grid=(n_seq // n_sub, ncb, steps),
        in_specs=in_specs,
        out_specs=pl.BlockSpec((blk, cn), lambda b, cb, t: (b, cb)),
        out_shape=jax.ShapeDtypeStruct((n_seq * n_tok, c), F32),
        scratch_shapes=[pltpu.VMEM((2, blk, cn), BF16), pltpu.VMEM((blk, cn), F32),
                        pltpu.VMEM((n_sub, 2 * n_tok, cn), BF16)],
        compiler_params=_cparams(("parallel", "parallel", "arbitrary")),
        name="hyena_conv",
    )(proj, proj, proj, conv_w, conv_w, conv_w, bias, tables, ha, hb, hc)


def _gdn_prep_body(x_ref, w_ref, o_ref, *, heads_per_blk, dk, n_qk_blk, rows, seq_len):
    cb = pl.program_id(1)
    n = x_ref.shape[0]
    w = w_ref[...]
    is_v = cb >= n_qk_blk
    scale = jnp.where(cb < n_qk_blk // 2, dk ** -0.5, 1.0)
    for r0 in range(0, n, rows):
        y = _silu(_conv3_rows(x_ref, w, r0, rows, seq_len))
        for h in range(heads_per_blk):
            yh = y[:, h * dk:(h + 1) * dk]
            inv = lax.rsqrt(jnp.sum(yh * yh, axis=-1, keepdims=True) + RMS_EPS) * scale
            o_ref[r0:r0 + rows, h * dk:(h + 1) * dk] = yh * jnp.where(is_v, 1.0, inv)


def _gdn_prep_call(proj, conv_w, *, n_seq, n_tok, col0, d_gdn, dk, cn=512, blk_rows=2048):
    ncb = 3 * d_gdn // cn
    cb0 = col0 // cn
    rows = n_seq * n_tok
    blk_rows = min(blk_rows, rows)
    return pl.pallas_call(
        functools.partial(_gdn_prep_body, heads_per_blk=cn // dk, dk=dk, n_qk_blk=2 * d_gdn // cn,
                          rows=min(64, n_tok), seq_len=n_tok),
        grid=(rows // blk_rows, ncb),
        in_specs=[pl.BlockSpec((blk_rows, cn), lambda b, cb: (b, cb0 + cb)),
                  pl.BlockSpec((3, cn), lambda b, cb: (0, cb))],
        out_specs=pl.BlockSpec((blk_rows, cn), lambda b, cb: (b, cb)),
        out_shape=jax.ShapeDtypeStruct((rows, 3 * d_gdn), F32),
        compiler_params=_cparams(("parallel", "parallel")),
        name="gdn_prep",
    )(proj, conv_w)


def _split3(x):
    hi = x.astype(BF16)
    r1 = x - hi.astype(F32)
    mid = r1.astype(BF16)
    lo = (r1 - mid.astype(F32)).astype(BF16)
    return hi, mid, lo


def _gdn_body(*refs, heads, dk, chunk, has_s0):
    (qf_ref, kf_ref, vf_ref, abf_ref, abtf_ref, qb_ref, kb_ref, vb_ref, abb_ref, abtb_ref,
     av_ref, dtv_ref, avt_ref, dtvt_ref) = refs[:14]
    s0_ref = refs[14] if has_s0 else None
    of_ref, ob_ref, sfin_ref, s_ref = refs[-4:]
    n = pl.program_id(1)

    @pl.when(n == 0)
    def _():
        s_ref[...] = s0_ref[0] if has_s0 else jnp.zeros_like(s_ref)

    ci = lax.broadcasted_iota(jnp.int32, (chunk, chunk), 0)
    cj = lax.broadcasted_iota(jnp.int32, (chunk, chunk), 1)
    eye = (ci == cj).astype(F32)
    diag_blk = (ci >> 3) == (cj >> 3)
    merge_blks = [((ci >> s) ^ (cj >> s)) == 1 for s in range(3, int(math.log2(chunk)))]
    lower = cj <= ci
    upper = cj >= ci

    units = []
    for dr, (q_ref, k_ref, v_ref, ab_ref, abt_ref, o_ref) in enumerate(
            ((qf_ref, kf_ref, vf_ref, abf_ref, abtf_ref, of_ref),
             (qb_ref, kb_ref, vb_ref, abb_ref, abtb_ref, ob_ref))):
        incl = upper if dr else lower
        strict = (cj > ci) if dr else (cj < ci)
        ones_incl = jnp.where(incl, 1.0, 0.0).astype(BF16)
        ones_incl_t = jnp.where(lower if dr else upper, 1.0, 0.0).astype(BF16)
        ab = ab_ref[...]
        g_col = -av_ref[...] * jax.nn.softplus(ab + dtv_ref[...])
        beta_col = jax.nn.sigmoid(ab)
        abt = abt_ref[...]
        g_row = -avt_ref[...] * jax.nn.softplus(abt + dtvt_ref[...])
        h3 = _split3(g_col)
        gc_col = _dot(ones_incl, h3[0]) + _dot(ones_incl, h3[1]) + _dot(ones_incl, h3[2])
        r3 = _split3(g_row)
        gc_row = _dot(r3[0], ones_incl_t) + _dot(r3[1], ones_incl_t) + _dot(r3[2], ones_incl_t)
        gl_all = jnp.sum(g_col, axis=0, keepdims=True)
        for h in range(heads):
            cg = dr * 2 * heads + h
            cb = cg + heads
            sl = slice(h * dk, (h + 1) * dk)
            units.append(dict(dr=dr, h=h, sl=sl, q_ref=q_ref, k_ref=k_ref, v_ref=v_ref, o_ref=o_ref,
                              incl=incl, strict=strict, gc=gc_col[:, cg:cg + 1], gr=gc_row[cg:cg + 1, :],
                              beta=beta_col[:, cb:cb + 1], gl=gl_all[:, cg:cg + 1]))

    def stage(fn):
        return [fn(u) for u in units]

    bf = lambda t: t.astype(BF16)
    k16 = stage(lambda u: bf(u["k_ref"][:, u["sl"]]))
    decay = stage(lambda u: jnp.where(u["incl"], jnp.exp(jnp.where(u["incl"], u["gc"] - u["gr"], 0.0)), 0.0))
    kbeta = stage(lambda u: u["k_ref"][:, u["sl"]] * u["beta"])
    a_low = [jnp.where(u["strict"], _dot_nt(bf(kb), k) * dc, 0.0)
             for u, kb, k, dc in zip(units, kbeta, k16, decay)]
    attn = [bf(_dot_nt(bf(u["q_ref"][:, u["sl"]]), k) * dc) for u, k, dc in zip(units, k16, decay)]

    d = [jnp.where(diag_blk, a, 0.0) for a in a_low]
    d2 = [_dot(bf(t), bf(t)) for t in d]
    x = [eye - t for t in d]
    x = [t + _dot(bf(t), bf(p)) for t, p in zip(x, d2)]
    d4 = [_dot(bf(p), bf(p)) for p in d2]
    x = [t + _dot(bf(t), bf(p)) for t, p in zip(x, d4)]
    for off_blk in merge_blks:
        x16 = [bf(t) for t in x]
        tmp = [_dot(bf(jnp.where(off_blk, a, 0.0)), t16) for a, t16 in zip(a_low, x16)]
        x = [t - _dot(t16, bf(m)) for t, t16, m in zip(x, x16, tmp)]

    e_gc = stage(lambda u: jnp.exp(u["gc"]))
    rhs = [bf(jnp.concatenate([u["v_ref"][:, u["sl"]] * u["beta"], kb * e], axis=-1))
           for u, kb, e in zip(units, kbeta, e_gc)]
    sol = [_dot(bf(t), r) for t, r in zip(x, rhs)]

    s16 = stage(lambda u: bf(s_ref[u["dr"], u["h"]]))
    v_new = [so[:, :dk] - _dot(bf(so[:, dk:]), s) for so, s in zip(sol, s16)]
    vn16 = [bf(t) for t in v_new]
    o = [_dot(bf(u["q_ref"][:, u["sl"]] * e), s) + _dot(at, vn)
         for u, e, s, at, vn in zip(units, e_gc, s16, attn, vn16)]
    for u, t in zip(units, o):
        u["o_ref"][:, u["sl"]] = t
    s_new = [s_ref[u["dr"], u["h"]] * jnp.exp(u["gl"])
             + _dot_tn(bf(u["k_ref"][:, u["sl"]] * jnp.exp(u["gl"] - u["gc"])), vn)
             for u, vn in zip(units, vn16)]
    for u, t in zip(units, s_new):
        s_ref[u["dr"], u["h"]] = t

    @pl.when(n == pl.num_programs(1) - 1)
    def _():
        sfin_ref[0] = s_ref[...]


def _gdn_call(qkv, ab, abt, av, dtv, avt, dtvt, s0, *, n_seq, n_tok, heads, dk, chunk):
    nc = n_tok // chunk
    d = heads * dk
    fwd = lambda b, n: b * nc + n
    bwd = lambda b, n: b * nc + nc - 1 - n
    qkv_spec = lambda idx, which: pl.BlockSpec((chunk, d), lambda b, n: (idx(b, n), which))
    ab_spec = lambda idx: pl.BlockSpec((chunk, 128), lambda b, n: (idx(b, n), 0))
    abt_spec = lambda idx: pl.BlockSpec((4 * heads, chunk), lambda b, n: (0, idx(b, n)))
    small = lambda shape: pl.BlockSpec(shape, lambda b, n: (0, 0))
    st_spec = pl.BlockSpec((1, 2, heads, dk, dk), lambda b, n: (b, 0, 0, 0, 0))
    o_shape = jax.ShapeDtypeStruct((n_seq * n_tok, d), F32)
    in_specs = [qkv_spec(fwd, 0), qkv_spec(fwd, 1), qkv_spec(fwd, 2), ab_spec(fwd), abt_spec(fwd),
                qkv_spec(bwd, 0), qkv_spec(bwd, 1), qkv_spec(bwd, 2), ab_spec(bwd), abt_spec(bwd),
                small((1, 128)), small((1, 128)), small((4 * heads, 1)), small((4 * heads, 1))]
    args = [qkv, qkv, qkv, ab, abt, qkv, qkv, qkv, ab, abt, av, dtv, avt, dtvt]
    if s0 is not None:
        in_specs.append(st_spec)
        args.append(s0)
    return pl.pallas_call(
        functools.partial(_gdn_body, heads=heads, dk=dk, chunk=chunk, has_s0=s0 is not None),
        grid=(n_seq, nc),
        in_specs=in_specs,
        out_specs=[pl.BlockSpec((chunk, d), lambda b, n: (fwd(b, n), 0)),
                   pl.BlockSpec((chunk, d), lambda b, n: (bwd(b, n), 0)),
                   st_spec],
        out_shape=[o_shape, o_shape, jax.ShapeDtypeStruct((n_seq, 2, heads, dk, dk), F32)],
        scratch_shapes=[pltpu.VMEM((2, heads, dk, dk), F32)],
        compiler_params=_cparams(("parallel", "arbitrary")),
        name="gdn_scan",
    )(*args)


def _outproj_body(x_ref, mod_ref, zhy_ref, of_ref, ob_ref, zg_ref, ghy_ref, go_ref, npost_ref, w_ref,
                  o_ref, *, mi, heads, dk):
    yhy = _rms(zhy_ref[...], ghy_ref[...]).astype(BF16)
    o = of_ref[...] + ob_ref[...]
    zg = zg_ref[...]
    parts = [yhy]
    for h in range(heads):
        sl = slice(h * dk, (h + 1) * dk)
        parts.append((_rms(o[:, sl], go_ref[...]) * _silu(zg[:, sl])).astype(BF16))
    y = _dot(jnp.concatenate(parts, axis=-1), w_ref[...])
    gate = mod_ref[0, mi:mi + 1, :]
    o_ref[...] = x_ref[...] + gate * _rms(y, npost_ref[...])


def _outproj_call(x, mod, zhy, o_f, o_b, proj, ghy, go, npost, w, *, mi, zg_col_blk, g0, rows_per_req, heads, dk,
                  tm=512):
    m, d = x.shape
    c = zhy.shape[1]
    dg = heads * dk
    row = lambda width: pl.BlockSpec((tm, width), lambda i: (i, 0))
    return pl.pallas_call(
        functools.partial(_outproj_body, mi=mi, heads=heads, dk=dk),
        grid=(m // tm,),
        in_specs=[row(d),
                  _mod_spec(d, tm, g0, rows_per_req),
                  row(c), row(dg), row(dg),
                  pl.BlockSpec((tm, dg), lambda i: (i, zg_col_blk)),
                  pl.BlockSpec((1, c), lambda i: (0, 0)),
                  pl.BlockSpec((1, dk), lambda i: (0, 0)),
                  pl.BlockSpec((1, d), lambda i: (0, 0)),
                  pl.BlockSpec((c + dg, d), lambda i: (0, 0))],
        out_specs=row(d),
        out_shape=jax.ShapeDtypeStruct((m, d), F32),
        compiler_params=_cparams(("parallel",)),
        name="out_proj",
    )(x, mod, zhy, o_f, o_b, proj, ghy.reshape(1, c), go.reshape(1, dk), npost.reshape(1, d), w)


@functools.lru_cache(maxsize=None)
def _grid_pos_embed_np(n_tok, dim):
    rows = n_tok // GRID_W
    r = np.repeat(np.arange(rows, dtype=np.float64), GRID_W)
    col = np.tile(np.arange(GRID_W, dtype=np.float64), rows)
    quarter = dim // 4
    omega = 1.0 / (POS_BASE ** (np.arange(quarter, dtype=np.float64) / quarter))
    ar = r[:, None] * omega[None]
    ac = col[:, None] * omega[None]
    return np.concatenate([np.sin(ar), np.cos(ar), np.sin(ac), np.cos(ac)], axis=-1).astype(np.float32)


def _mixer(proj, proj_ab, s0, *, n_seq, n_tok, hy_conv_w, hy_f, hy_decay, hy_bias, gdn_conv_w, gdn_a_log,
           gdn_dt_bias, c_hy, d_gdn):
    heads = GDN_HEADS
    dk = d_gdn // heads
    o0 = 3 * c_hy

    a_exp = jnp.exp(gdn_a_log.astype(F32))
    zeros = jnp.zeros_like(a_exp)
    av32 = jnp.stack([a_exp, zeros], axis=1).reshape(-1)
    dtv32 = jnp.stack([gdn_dt_bias.astype(F32), zeros], axis=1).reshape(-1)
    pad = 128 - 4 * heads
    av = jnp.pad(av32, (0, pad)).reshape(1, 128)
    dtv = jnp.pad(dtv32, (0, pad)).reshape(1, 128)
    avt = av32.reshape(4 * heads, 1)
    dtvt = dtv32.reshape(4 * heads, 1)
    abt = jnp.transpose(proj_ab[:, :4 * heads])

    r = min(512, n_tok)
    tables = _dft_tables(n_tok, r)
    ha, hb, hc = _spectra_call(n_tok, *hy_f, hy_decay, tables, r=r)
    zhy = _hyena_call(proj, hy_conv_w, hy_bias, tables, ha, hb, hc, n_seq=n_seq, n_tok=n_tok, c=c_hy, r=r,
                      n_sub=max(1, min(n_seq, 1024 // n_tok)))
    qkv = _gdn_prep_call(proj, gdn_conv_w, n_seq=n_seq, n_tok=n_tok, col0=o0, d_gdn=d_gdn, dk=dk)
    o_f, o_b, s_fin = _gdn_call(qkv, proj_ab, abt, av, dtv, avt, dtvt, s0, n_seq=n_seq, n_tok=n_tok, heads=heads,
                                dk=dk, chunk=GDN_CHUNK)
    return zhy, o_f, o_b, s_fin


def kernel(x_prompt, x_sample, state_gdn, c, c_ctx, ada_w, ada_b, norm_pre, norm_post, ffn_wg, ffn_wu, ffn_wd,
           w_in, w_out, hy_conv_w, hy_f_w1, hy_f_b1, hy_f_w2, hy_f_b2, hy_f_w3, hy_decay, hy_bias, hy_out_norm,
           gdn_conv_w, gdn_a_log, gdn_dt_bias, gdn_o_norm):
    nb, seq, d = x_prompt.shape
    db, dseq, _ = x_sample.shape
    depth = ada_w.shape[0]
    c_hy = hy_decay.shape[-1]
    d_gdn = gdn_conv_w.shape[-1] // 3
    heads = GDN_HEADS
    dk = d_gdn // heads
    n_ctx_rows = nb * seq
    n_lat_rows = db * dseq
    pos = jnp.asarray(_grid_pos_embed_np(dseq, d)).astype(x_sample.dtype)
    cvec = jnp.concatenate([c_ctx[None], c, jnp.zeros((8 - 1 - db, d), F32)], axis=0)
    wg, wu, wd = ffn_wg.astype(BF16), ffn_wu.astype(BF16), ffn_wd.astype(BF16)
    n_main_cols = 3 * c_hy + 4 * d_gdn
    n_tail = w_in.shape[-1] - n_main_cols
    w_in_t = jnp.swapaxes(w_in, 1, 2).astype(BF16)

    xs = [x_prompt.reshape(n_ctx_rows, d), x_sample.reshape(n_lat_rows, d)]
    groups = [dict(g0=0, rows_per_req=n_ctx_rows), dict(g0=1, rows_per_req=dseq)]
    shapes = [(nb, seq), (db, dseq)]
    ctx_states = []
    for l in range(depth):
        mod = _mod_call(cvec, ada_w[l], ada_b[l]).reshape(8, N_MOD, d)
        w_tail_t = jnp.pad(w_in_t[l, n_main_cols:], ((0, 128 - n_tail), (0, 0)))
        w_out_l = w_out[l].astype(BF16)
        hy_f = (hy_f_w1[l], hy_f_b1[l], hy_f_w2[l], hy_f_b2[l], hy_f_w3[l])
        states = [None, state_gdn[:, l].astype(F32)]
        for gi, (grp, (n_seq, n_tok)) in enumerate(zip(groups, shapes)):
            x = _ffn_call(xs[gi], mod, norm_pre[l, 0], norm_post[l, 0], wg, wu, wd, (l, 0), mi=0,
                          pos=pos if (gi == 1 and l == 0) else None, **grp)
            proj, proj_ab = _inproj_call(x, mod, norm_pre[l, 1], w_in_t, l, w_tail_t, mi=3,
                                         n_main_cols=n_main_cols, **grp)
            zhy, o_f, o_b, s_fin = _mixer(
                proj, proj_ab, states[gi], n_seq=n_seq, n_tok=n_tok, hy_conv_w=hy_conv_w[l], hy_f=hy_f,
                hy_decay=hy_decay[l], hy_bias=hy_bias[l], gdn_conv_w=gdn_conv_w[l], gdn_a_log=gdn_a_log[l],
                gdn_dt_bias=gdn_dt_bias[l], c_hy=c_hy, d_gdn=d_gdn)
            if gi == 0:
                ctx_states.append(s_fin)
            x = _outproj_call(x, mod, zhy, o_f, o_b, proj, hy_out_norm[l], gdn_o_norm[l], norm_post[l, 1], w_out_l,
                              mi=5, zg_col_blk=(3 * c_hy + 3 * d_gdn) // d_gdn, heads=heads, dk=dk, **grp)
            xs[gi] = _ffn_call(x, mod, norm_pre[l, 2], norm_post[l, 2], wg, wu, wd, (l, 1), mi=6, **grp)

    new_state = ctx_states[0][:, None] if depth == 1 else jnp.stack(ctx_states, axis=1)
    return xs[0].reshape(nb, seq, d), xs[1].reshape(db, dseq, d), new_state
```

```python
import functools
import math

import numpy as np
import jax
import jax.numpy as jnp
from jax import lax
from jax.experimental import pallas as pl
from jax.experimental.pallas import tpu as pltpu

F32 = jnp.float32
BF16 = jnp.bfloat16

N_MOD = 9
RMS_EPS = 1e-6
GRID_W = 64
POS_BASE = 10000.0
HY_BANDS = 16
HY_SIN_FREQ = 1.0
HY_WINDOW_SHIFT = 0.05
GDN_HEADS = 8
GDN_CHUNK = 128
VMEM_LIMIT = 56 * 1024 * 1024
ROW_CHUNK = 16


def _cparams(sem):
    return pltpu.CompilerParams(dimension_semantics=sem, vmem_limit_bytes=VMEM_LIMIT)


def _dot(a, b):
    return jnp.dot(a, b, preferred_element_type=F32)


def _dot_nt(a, b):
    return lax.dot_general(a, b, (((1,), (1,)), ((), ())), preferred_element_type=F32)


def _dot_tn(a, b):
    return lax.dot_general(a, b, (((0,), (0,)), ((), ())), preferred_element_type=F32)


def _rms(x, g):
    ms = jnp.mean(x * x, axis=-1, keepdims=True)
    return x * lax.rsqrt(ms + RMS_EPS) * g


def _silu(x):
    return x * jax.nn.sigmoid(x)


def _for_row_chunks(n_rows, fn):
    def body(c, carry):
        fn(pl.ds(pl.multiple_of(c * ROW_CHUNK, ROW_CHUNK), ROW_CHUNK))
        return carry

    lax.fori_loop(0, n_rows // ROW_CHUNK, body, 0, unroll=8)


def _mod_spec(d, tm, g0, rows_per_req):
    tpr = rows_per_req // tm
    return pl.BlockSpec((1, N_MOD, d), lambda i, *_: (g0 + i // tpr, 0, 0))


def _mod_body(c_ref, w_ref, b_ref, o_ref):
    s = _silu(c_ref[...]).astype(BF16)
    o_ref[...] = _dot(s, w_ref[...].astype(BF16)) + b_ref[...]


def _mod_call(cvec, ada_w, ada_b):
    g, d = cvec.shape
    n = ada_w.shape[1]
    tn = 1024
    return pl.pallas_call(
        _mod_body,
        grid=(n // tn,),
        in_specs=[pl.BlockSpec((g, d), lambda j: (0, 0)),
                  pl.BlockSpec((d, tn), lambda j: (0, j)),
                  pl.BlockSpec((1, tn), lambda j: (0, j))],
        out_specs=pl.BlockSpec((g, tn), lambda j: (0, j)),
        out_shape=jax.ShapeDtypeStruct((g, n), F32),
        compiler_params=_cparams(("arbitrary",)),
        name="mod_table",
    )(cvec, ada_w, ada_b.reshape(1, n))


def _ffn_body(*refs, mi, has_pos):
    refs = list(refs)
    x_ref = refs.pop(0)
    pos_ref = refs.pop(0) if has_pos else None
    mod_ref, npre_ref, npost_ref, wg_ref, wu_ref, wd_ref, o_ref, h_ref, acc_ref = refs
    j = pl.program_id(1)
    tm = x_ref.shape[0]

    def x_rows(rows):
        return x_ref[rows, :] + pos_ref[rows, :] if has_pos else x_ref[rows, :]

    @pl.when(j == 0)
    def _():
        shift = mod_ref[0, mi:mi + 1, :]
        gain = npre_ref[...] * (1.0 + mod_ref[0, mi + 1:mi + 2, :])

        def rows_body(rows):
            h_ref[rows, :] = (_rms(x_rows(rows), gain) + shift).astype(BF16)
            acc_ref[rows, :] = jnp.zeros((ROW_CHUNK, acc_ref.shape[1]), F32)

        _for_row_chunks(tm, rows_body)

    h = h_ref[...]
    g = _dot(h, wg_ref[...])
    u = _dot(h, wu_ref[...])
    a = (_silu(g) * u).astype(BF16)
    acc_ref[...] += _dot(a, wd_ref[...])

    @pl.when(j == pl.num_programs(1) - 1)
    def _():
        gain = npost_ref[...] * (0.5 * mod_ref[0, mi + 2:mi + 3, :])

        def rows_body(rows):
            o_ref[rows, :] = x_rows(rows) + _rms(acc_ref[rows, :], gain)

        _for_row_chunks(tm, rows_body)


def _ffn_call(x, mod, npre, npost, wg, wu, wd, wsel, *, mi, g0, rows_per_req, pos=None, tm=512, tf=512):
    n_rows, d = x.shape
    ff = wg.shape[-1]
    wl, wk = wsel
    tpr = rows_per_req // tm
    const = lambda shape: pl.BlockSpec(shape, lambda i, j: (0, 0))
    in_specs = [pl.BlockSpec((tm, d), lambda i, j: (i, 0))]
    args = [x]
    if pos is not None:
        in_specs.append(pl.BlockSpec((tm, d), lambda i, j: (i % tpr, 0)))
        args.append(pos)
    in_specs += [_mod_spec(d, tm, g0, rows_per_req),
                 const((1, d)), const((1, d)),
                 pl.BlockSpec((None, None, d, tf), lambda i, j: (wl, wk, 0, j)),
                 pl.BlockSpec((None, None, d, tf), lambda i, j: (wl, wk, 0, j)),
                 pl.BlockSpec((None, None, tf, d), lambda i, j: (wl, wk, j, 0))]
    args += [mod, npre.reshape(1, d), npost.reshape(1, d), wg, wu, wd]
    return pl.pallas_call(
        functools.partial(_ffn_body, mi=mi, has_pos=pos is not None),
        grid=(n_rows // tm, ff // tf),
        in_specs=in_specs,
        out_specs=pl.BlockSpec((tm, d), lambda i, j: (i, 0)),
        out_shape=jax.ShapeDtypeStruct((n_rows, d), F32),
        scratch_shapes=[pltpu.VMEM((tm, d), BF16), pltpu.VMEM((tm, d), F32)],
        compiler_params=_cparams(("parallel", "arbitrary")),
        name="ffn",
    )(*args)


def _inproj_body(x_ref, mod_ref, npre_ref, w_ref, wt_ref, o_ref, ot_ref, h_ref, *, mi, n_main):
    j = pl.program_id(1)

    @pl.when(j == 0)
    def _():
        shift = mod_ref[0, mi:mi + 1, :]
        gain = npre_ref[...] * (1.0 + mod_ref[0, mi + 1:mi + 2, :])

        def rows_body(rows):
            h_ref[rows, :] = (_rms(x_ref[rows, :], gain) + shift).astype(BF16)

        _for_row_chunks(x_ref.shape[0], rows_body)

    @pl.when(j < n_main)
    def _():
        o_ref[...] = _dot_nt(h_ref[...], w_ref[...])

    @pl.when(j == n_main)
    def _():
        ot_ref[...] = _dot_nt(h_ref[...], wt_ref[...])


def _inproj_call(x, mod, npre, w_in_t, layer, w_tail_t, *, mi, n_main_cols, g0, rows_per_req, tm=1024, tn=1024):
    m, d = x.shape
    n_main = n_main_cols // tn
    nt = w_tail_t.shape[0]
    jm = lambda j: jnp.minimum(j, n_main - 1)
    return pl.pallas_call(
        functools.partial(_inproj_body, mi=mi, n_main=n_main),
        grid=(m // tm, n_main + 1),
        in_specs=[pl.BlockSpec((tm, d), lambda i, j: (i, 0)),
                  _mod_spec(d, tm, g0, rows_per_req),
                  pl.BlockSpec((1, d), lambda i, j: (0, 0)),
                  pl.BlockSpec((None, tn, d), lambda i, j: (layer, jm(j), 0)),
                  pl.BlockSpec((nt, d), lambda i, j: (0, 0))],
        out_specs=[pl.BlockSpec((tm, tn), lambda i, j: (i, jm(j))),
                   pl.BlockSpec((tm, nt), lambda i, j: (i, 0))],
        out_shape=[jax.ShapeDtypeStruct((m, n_main_cols), F32), jax.ShapeDtypeStruct((m, nt), F32)],
        scratch_shapes=[pltpu.VMEM((tm, d), BF16)],
        compiler_params=_cparams(("parallel", "arbitrary")),
        name="in_proj",
    )(x, mod, npre.reshape(1, d), w_in_t, w_tail_t)


def _conv3_rows(x_ref, w, r0, nr, seq_len):
    zeros = jnp.zeros((8, x_ref.shape[1]), x_ref.dtype)
    top = zeros if r0 % seq_len == 0 else x_ref[r0 - 8:r0, :]
    mid = x_ref[r0:r0 + nr, :]
    bot = zeros if (r0 + nr) % seq_len == 0 else x_ref[r0 + nr:r0 + nr + 8, :]
    return _conv3_slab(top, mid, bot, w)


def _conv3_rows_dyn(x_ref, w, r0, nr, seq_len):
    n = x_ref.shape[0]
    top = x_ref[pl.ds(pl.multiple_of(jnp.maximum(r0 - 8, 0), 8), 8), :]
    mid = x_ref[pl.ds(pl.multiple_of(r0, 8), nr), :]
    bot = x_ref[pl.ds(pl.multiple_of(jnp.minimum(r0 + nr, n - 8), 8), 8), :]
    top = jnp.where(r0 % seq_len == 0, 0.0, top)
    bot = jnp.where((r0 + nr) % seq_len == 0, 0.0, bot)
    return _conv3_slab(top, mid, bot, w)


def _conv3_slab(top, mid, bot, w):
    nr = mid.shape[0]
    slab = jnp.concatenate([top, mid, bot], axis=0)
    prev = pltpu.roll(slab, 1, 0)[8:8 + nr]
    nxt = pltpu.roll(slab, nr + 15, 0)[8:8 + nr]
    return prev * w[0:1, :] + mid * w[1:2, :] + nxt * w[2:3, :]


@functools.lru_cache(maxsize=None)
def _dft_tables_np(n_tok, r):
    big = 2 * n_tok
    k = np.arange(n_tok, dtype=np.int64)
    t = np.arange(n_tok, dtype=np.int64)
    ang = ((k[:, None] * t[None, :]) % big).astype(np.float64) * (2.0 * math.pi / big)
    cosm = np.cos(ang)
    sinm = np.sin(ang)
    sinm[0, :] = 1.0 - 2.0 * (t % 2)
    nj = n_tok // r
    fr = np.concatenate([cosm.reshape(nj, r, n_tok), sinm.reshape(nj, r, n_tok)], axis=1)
    g_full = fr.reshape(big, n_tok).T
    gr = np.concatenate([g_full[:, :n_tok].reshape(nj, r, n_tok), g_full[:, n_tok:].reshape(nj, r, n_tok)], axis=1)
    return np.concatenate([fr, gr], axis=0).astype(np.float32)


def _dft_tables(n_tok, r):
    return jnp.asarray(_dft_tables_np(n_tok, r)).astype(BF16)


def _spectra_body(featt_ref, t_ref, w1t_ref, b1_ref, w2t_ref, b2_ref, w3f_ref, w3b_ref, dec_ref, f_ref,
                  ha_ref, hb_ref, hc_ref, e_ref, o_ref, h_ref, *, n_tok, r):
    j = pl.program_id(1)

    @pl.when((pl.program_id(0) == 0) & (j == 0))
    def _():
        h = jnp.sin(HY_SIN_FREQ * (_dot(w1t_ref[...].astype(BF16), featt_ref[...].astype(BF16)) + b1_ref[...]))
        h = jnp.sin(HY_SIN_FREQ * (_dot(w2t_ref[...].astype(BF16), h.astype(BF16)) + b2_ref[...]))
        h_ref[...] = h.astype(BF16)

    @pl.when(j == 0)
    def _():
        hb16 = h_ref[...]
        window = jnp.exp(-t_ref[...] * jnp.abs(dec_ref[...])) + HY_WINDOW_SHIFT
        fwd = _dot_tn(hb16, w3f_ref[...].astype(BF16)) * window
        bwd = _dot_tn(hb16, w3b_ref[...].astype(BF16)) * window
        row = lax.broadcasted_iota(jnp.int32, bwd.shape, 0)
        bwd = jnp.where(row == 0, 0.0, bwd)
        norm = jnp.sum(jnp.abs(fwd), axis=0, keepdims=True) + jnp.sum(jnp.abs(bwd), axis=0, keepdims=True)
        fwd = fwd / norm
        bwd = bwd / norm
        e_ref[...] = (fwd + bwd).astype(BF16)
        o_ref[...] = (fwd - bwd).astype(BF16)

    fb = f_ref[0]
    e = e_ref[...]
    p = _dot(fb[:r], e)
    q = _dot(fb[r:], o_ref[...])
    k = j * r + lax.broadcasted_iota(jnp.int32, p.shape, 0)
    big = 2.0 * n_tok
    wk = jnp.where(k == 0, 1.0 / big, 2.0 / big)
    ha_ref[...] = wk * p
    hb_ref[...] = jnp.where(k == 0, 0.0, -wk * q)

    @pl.when(j == 0)
    def _():
        hc_ref[...] = _dot(fb[r:r + 8], e)[0:1, :] * (1.0 / big)


def _spectra_call(n_tok, w1, b1, w2, b2, w3, decay, fr, *, r, cn=512):
    f32 = jnp.float32
    order, c = decay.shape
    oc = order * c
    hid = w2.shape[0]
    emb = w1.shape[0]
    embp = 128
    idx = jnp.arange(n_tok, dtype=f32)
    t = idx / (n_tok - 1)
    bands = jnp.arange(1, HY_BANDS + 1, dtype=f32)
    ang = (2.0 * math.pi / n_tok) * idx[:, None] * bands[None, :]
    feats = jnp.concatenate([t[:, None], jnp.cos(ang), jnp.sin(ang)], axis=-1)
    featt = jnp.pad(feats, ((0, 0), (0, embp - emb))).T
    w1t = jnp.pad(w1, ((0, embp - emb), (0, 0))).T
    nj = n_tok // r
    ncb = oc // cn
    out = jax.ShapeDtypeStruct((n_tok, oc), F32)
    const = lambda shape: pl.BlockSpec(shape, lambda cb, j: (0, 0))
    return pl.pallas_call(
        functools.partial(_spectra_body, n_tok=n_tok, r=r),
        grid=(ncb, nj),
        in_specs=[const((embp, n_tok)), const((n_tok, 1)), const((hid, embp)), const((hid, 1)),
                  const((hid, hid)), const((hid, 1)),
                  pl.BlockSpec((hid, cn), lambda cb, j: (0, cb)),
                  pl.BlockSpec((hid, cn), lambda cb, j: (0, ncb + cb)),
                  pl.BlockSpec((1, cn), lambda cb, j: (0, cb)),
                  pl.BlockSpec((1, 2 * r, n_tok), lambda cb, j: (j, 0, 0))],
        out_specs=[pl.BlockSpec((r, cn), lambda cb, j: (j, cb))] * 2 + [pl.BlockSpec((1, cn), lambda cb, j: (0, cb))],
        out_shape=[out, out, jax.ShapeDtypeStruct((1, oc), F32)],
        scratch_shapes=[pltpu.VMEM((n_tok, cn), BF16), pltpu.VMEM((n_tok, cn), BF16),
                        pltpu.VMEM((hid, n_tok), BF16)],
        compiler_params=_cparams(("arbitrary", "arbitrary")),
        name="hyena_spectra",
    )(featt, t[:, None], w1t, b1.reshape(hid, 1), w2.T, b2.reshape(hid, 1), w3, w3, decay.reshape(1, oc), fr)


def _hyena_body(v_ref, x1_ref, x2_ref, wv_ref, wx1_ref, wx2_ref, bias_ref, t_ref, ha_ref, hb_ref, hc_ref,
                o_ref, ub_ref, u2f_ref, z_ref, *, r, nj, seq_len, n_sub):
    t = pl.program_id(2)
    order = t // (2 * nj)
    tt = t % (2 * nj)
    sub = min(128, r)

    @pl.when(t == 0)
    def _():
        w = wv_ref[...]
        for r0 in range(0, n_sub * seq_len, 64):
            ub_ref[0, r0:r0 + 64, :] = _conv3_rows(v_ref, w, r0, 64, seq_len).astype(BF16)

    @pl.when(tt < nj)
    def _():
        tb = t_ref[0]
        ha, hb = ha_ref[...], hb_ref[...]
        nyq = (tt * r + lax.broadcasted_iota(jnp.int32, ha.shape, 0)) == 0
        hc = jnp.where(nyq, hc_ref[...], ha)
        z0 = pl.multiple_of(tt * (2 * r), 2 * r)
        for s in range(n_sub):
            x = _dot(tb, ub_ref[order, s * seq_len:(s + 1) * seq_len, :])
            p = x[:r]
            q = x[r:]
            z_ref[s, pl.ds(z0, 2 * r), :] = jnp.concatenate([p * ha + q * hb, q * hc - p * hb],
                                                             axis=0).astype(BF16)

    def inverse_rows(s):
        tb = t_ref[0]
        return _dot(tb[:r], z_ref[s, :seq_len, :]) + _dot(tb[r:], z_ref[s, seq_len:, :])

    r_blk = (tt - nj) * r

    @pl.when((tt >= nj) & (order == 0))
    def _():
        wv, wx1 = wv_ref[...], wx1_ref[...]
        for s in range(n_sub):
            y = inverse_rows(s)
            for c0 in range(0, r, sub):
                r0 = s * seq_len + r_blk + c0
                u1 = _conv3_rows_dyn(v_ref, wv, r0, sub, seq_len)
                z1 = _conv3_rows_dyn(x1_ref, wx1, r0, sub, seq_len) * (y[c0:c0 + sub] + u1 * bias_ref[0:1, :])
                u2f_ref[pl.ds(pl.multiple_of(r0, 8), sub), :] = z1
                ub_ref[1, pl.ds(pl.multiple_of(r0, 8), sub), :] = z1.astype(BF16)

    @pl.when((tt >= nj) & (order == 1))
    def _():
        wx2 = wx2_ref[...]
        for s in range(n_sub):
            y = inverse_rows(s)
            for c0 in range(0, r, sub):
                r0 = s * seq_len + r_blk + c0
                u2 = u2f_ref[pl.ds(pl.multiple_of(r0, 8), sub), :]
                o_ref[pl.ds(pl.multiple_of(r0, 8), sub), :] = (
                    _conv3_rows_dyn(x2_ref, wx2, r0, sub, seq_len) * (y[c0:c0 + sub] + u2 * bias_ref[1:2, :]))


def _hyena_call(proj, conv_w, bias, tables, ha, hb, hc, *, n_seq, n_tok, c, r, n_sub, cn=512):
    nj = n_tok // r
    ncb = c // cn
    blk = n_sub * n_tok
    steps = 4 * nj
    single = pl.Buffered(1)
    seq_spec = lambda off: pl.BlockSpec((blk, cn), lambda b, cb, t: (b, off * ncb + cb), pipeline_mode=single)
    w_spec = lambda off: pl.BlockSpec((3, cn), lambda b, cb, t: (0, off * ncb + cb))
    h_spec = pl.BlockSpec((r, cn), lambda b, cb, t: (jnp.minimum(t % (2 * nj), nj - 1), (t // (2 * nj)) * ncb + cb))
    in_specs = [seq_spec(0), seq_spec(1), seq_spec(2), w_spec(0), w_spec(1), w_spec(2),
                pl.BlockSpec((2, cn), lambda b, cb, t: (0, cb)),
                pl.BlockSpec((1, 2 * r, n_tok), lambda b, cb, t: (t % (2 * nj), 0, 0)),
                h_spec, h_spec,
                pl.BlockSpec((1, cn), lambda b, cb, t: (0, (t // (2 * nj)) * ncb + cb))]
    return pl.pallas_call(
        functools.partial(_hyena_body, r=r, nj=nj, seq_len=n_tok, n_sub=n_sub),
        grid=(n_seq // n_sub, ncb, steps),
        in_specs=in_specs,
        out_specs=pl.BlockSpec((blk, cn), lambda b, cb, t: (b, cb)),
        out_shape=jax.ShapeDtypeStruct((n_seq * n_tok, c), F32),
        scratch_shapes=[pltpu.VMEM((2, blk, cn), BF16), pltpu.VMEM((blk, cn), F32),
                        pltpu.VMEM((n_sub, 2 * n_tok, cn), BF16)],
        compiler_params=_cparams(("parallel", "parallel", "arbitrary")),
        name="hyena_conv",
    )(proj, proj, proj, conv_w, conv_w, conv_w, bias, tables, ha, hb, hc)


def _gdn_prep_body(x_ref, w_ref, o_ref, *, heads_per_blk, dk, n_qk_blk, rows, seq_len):
    cb = pl.program_id(1)
    n = x_ref.shape[0]
    w = w_ref[...]
    is_v = cb >= n_qk_blk
    scale = jnp.where(cb < n_qk_blk // 2, dk ** -0.5, 1.0)
    for r0 in range(0, n, rows):
        y = _silu(_conv3_rows(x_ref, w, r0, rows, seq_len))
        for h in range(heads_per_blk):
            yh = y[:, h * dk:(h + 1) * dk]
            inv = lax.rsqrt(jnp.sum(yh * yh, axis=-1, keepdims=True) + RMS_EPS) * scale
            o_ref[r0:r0 + rows, h * dk:(h + 1) * dk] = yh * jnp.where(is_v, 1.0, inv)


def _gdn_prep_call(proj, conv_w, *, n_seq, n_tok, col0, d_gdn, dk, cn=512, blk_rows=2048):
    ncb = 3 * d_gdn // cn
    cb0 = col0 // cn
    rows = n_seq * n_tok
    blk_rows = min(blk_rows, rows)
    return pl.pallas_call(
        functools.partial(_gdn_prep_body, heads_per_blk=cn // dk, dk=dk, n_qk_blk=2 * d_gdn // cn,
                          rows=min(64, n_tok), seq_len=n_tok),
        grid=(rows // blk_rows, ncb),
        in_specs=[pl.BlockSpec((blk_rows, cn), lambda b, cb: (b, cb0 + cb)),
                  pl.BlockSpec((3, cn), lambda b, cb: (0, cb))],
        out_specs=pl.BlockSpec((blk_rows, cn), lambda b, cb: (b, cb)),
        out_shape=jax.ShapeDtypeStruct((rows, 3 * d_gdn), F32),
        compiler_params=_cparams(("parallel", "parallel")),
        name="gdn_prep",
    )(proj, conv_w)


def _split3(x):
    hi = x.astype(BF16)
    r1 = x - hi.astype(F32)
    mid = r1.astype(BF16)
    lo = (r1 - mid.astype(F32)).astype(BF16)
    return hi, mid, lo


def _gdn_body(*refs, heads, dk, chunk, has_s0):
    (qf_ref, kf_ref, vf_ref, abf_ref, abtf_ref, qb_ref, kb_ref, vb_ref, abb_ref, abtb_ref,
     av_ref, dtv_ref, avt_ref, dtvt_ref) = refs[:14]
    s0_ref = refs[14] if has_s0 else None
    of_ref, ob_ref, sfin_ref, s_ref = refs[-4:]
    n = pl.program_id(1)

    @pl.when(n == 0)
    def _():
        s_ref[...] = s0_ref[0] if has_s0 else jnp.zeros_like(s_ref)

    ci = lax.broadcasted_iota(jnp.int32, (chunk, chunk), 0)
    cj = lax.broadcasted_iota(jnp.int32, (chunk, chunk), 1)
    eye = (ci == cj).astype(F32)
    diag_blk = (ci >> 3) == (cj >> 3)
    merge_blks = [((ci >> s) ^ (cj >> s)) == 1 for s in range(3, int(math.log2(chunk)))]
    lower = cj <= ci
    upper = cj >= ci

    units = []
    for dr, (q_ref, k_ref, v_ref, ab_ref, abt_ref, o_ref) in enumerate(
            ((qf_ref, kf_ref, vf_ref, abf_ref, abtf_ref, of_ref),
             (qb_ref, kb_ref, vb_ref, abb_ref, abtb_ref, ob_ref))):
        incl = upper if dr else lower
        strict = (cj > ci) if dr else (cj < ci)
        ones_incl = jnp.where(incl, 1.0, 0.0).astype(BF16)
        ones_incl_t = jnp.where(lower if dr else upper, 1.0, 0.0).astype(BF16)
        ab = ab_ref[...]
        g_col = -av_ref[...] * jax.nn.softplus(ab + dtv_ref[...])
        beta_col = jax.nn.sigmoid(ab)
        abt = abt_ref[...]
        g_row = -avt_ref[...] * jax.nn.softplus(abt + dtvt_ref[...])
        h3 = _split3(g_col)
        gc_col = _dot(ones_incl, h3[0]) + _dot(ones_incl, h3[1]) + _dot(ones_incl, h3[2])
        r3 = _split3(g_row)
        gc_row = _dot(r3[0], ones_incl_t) + _dot(r3[1], ones_incl_t) + _dot(r3[2], ones_incl_t)
        gl_all = jnp.sum(g_col, axis=0, keepdims=True)
        for h in range(heads):
            cg = dr * 2 * heads + h
            cb = cg + heads
            sl = slice(h * dk, (h + 1) * dk)
            units.append(dict(dr=dr, h=h, sl=sl, q_ref=q_ref, k_ref=k_ref, v_ref=v_ref, o_ref=o_ref,
                              incl=incl, strict=strict, gc=gc_col[:, cg:cg + 1], gr=gc_row[cg:cg + 1, :],
                              beta=beta_col[:, cb:cb + 1], gl=gl_all[:, cg:cg + 1]))

    def stage(fn):
        return [fn(u) for u in units]

    bf = lambda t: t.astype(BF16)
    k16 = stage(lambda u: bf(u["k_ref"][:, u["sl"]]))
    decay = stage(lambda u: jnp.where(u["incl"], jnp.exp(jnp.where(u["incl"], u["gc"] - u["gr"], 0.0)), 0.0))
    kbeta = stage(lambda u: u["k_ref"][:, u["sl"]] * u["beta"])
    kq = [_dot_nt(jnp.concatenate([bf(kb), bf(u["q_ref"][:, u["sl"]])], axis=0), k)
          for u, kb, k in zip(units, kbeta, k16)]
    a_low = [jnp.where(u["strict"], t[:chunk] * dc, 0.0) for u, t, dc in zip(units, kq, decay)]
    attn = [bf(t[chunk:] * dc) for t, dc in zip(kq, decay)]

    d = [jnp.where(diag_blk, a, 0.0) for a in a_low]
    d2 = [_dot(bf(t), bf(t)) for t in d]
    x = [eye - t for t in d]
    x = [t + _dot(bf(t), bf(p)) for t, p in zip(x, d2)]
    d4 = [_dot(bf(p), bf(p)) for p in d2]
    x = [t + _dot(bf(t), bf(p)) for t, p in zip(x, d4)]
    for off_blk in merge_blks:
        x16 = [bf(t) for t in x]
        tmp = [_dot(bf(jnp.where(off_blk, a, 0.0)), t16) for a, t16 in zip(a_low, x16)]
        x = [t - _dot(t16, bf(m)) for t, t16, m in zip(x, x16, tmp)]

    e_gc = stage(lambda u: jnp.exp(u["gc"]))
    rhs = [bf(jnp.concatenate([u["v_ref"][:, u["sl"]] * u["beta"], kb * e], axis=-1))
           for u, kb, e in zip(units, kbeta, e_gc)]
    sol = [_dot(bf(t), r) for t, r in zip(x, rhs)]

    s16 = stage(lambda u: bf(s_ref[u["dr"], u["h"]]))
    ws = [_dot(jnp.concatenate([bf(so[:, dk:]), bf(u["q_ref"][:, u["sl"]] * e)], axis=0), s)
          for u, so, e, s in zip(units, sol, e_gc, s16)]
    vn16 = [bf(so[:, :dk] - t[:chunk]) for so, t in zip(sol, ws)]
    o = [t[chunk:] + _dot(at, vn) for t, at, vn in zip(ws, attn, vn16)]
    for u, t in zip(units, o):
        u["o_ref"][:, u["sl"]] = t
    s_new = [s_ref[u["dr"], u["h"]] * jnp.exp(u["gl"])
             + _dot_tn(bf(u["k_ref"][:, u["sl"]] * jnp.exp(u["gl"] - u["gc"])), vn)
             for u, vn in zip(units, vn16)]
    for u, t in zip(units, s_new):
        s_ref[u["dr"], u["h"]] = t

    @pl.when(n == pl.num_programs(1) - 1)
    def _():
        sfin_ref[0] = s_ref[...]


def _gdn_call(qkv, ab, abt, av, dtv, avt, dtvt, s0, *, n_seq, n_tok, heads, dk, chunk):
    nc = n_tok // chunk
    d = heads * dk
    fwd = lambda b, n: b * nc + n
    bwd = lambda b, n: b * nc + nc - 1 - n
    qkv_spec = lambda idx, which: pl.BlockSpec((chunk, d), lambda b, n: (idx(b, n), which))
    ab_spec = lambda idx: pl.BlockSpec((chunk, 128), lambda b, n: (idx(b, n), 0))
    abt_spec = lambda idx: pl.BlockSpec((4 * heads, chunk), lambda b, n: (0, idx(b, n)))
    small = lambda shape: pl.BlockSpec(shape, lambda b, n: (0, 0))
    st_spec = pl.BlockSpec((1, 2, heads, dk, dk), lambda b, n: (b, 0, 0, 0, 0))
    o_shape = jax.ShapeDtypeStruct((n_seq * n_tok, d), F32)
    in_specs = [qkv_spec(fwd, 0), qkv_spec(fwd, 1), qkv_spec(fwd, 2), ab_spec(fwd), abt_spec(fwd),
                qkv_spec(bwd, 0), qkv_spec(bwd, 1), qkv_spec(bwd, 2), ab_spec(bwd), abt_spec(bwd),
                small((1, 128)), small((1, 128)), small((4 * heads, 1)), small((4 * heads, 1))]
    args = [qkv, qkv, qkv, ab, abt, qkv, qkv, qkv, ab, abt, av, dtv, avt, dtvt]
    if s0 is not None:
        in_specs.append(st_spec)
        args.append(s0)
    return pl.pallas_call(
        functools.partial(_gdn_body, heads=heads, dk=dk, chunk=chunk, has_s0=s0 is not None),
        grid=(n_seq, nc),
        in_specs=in_specs,
        out_specs=[pl.BlockSpec((chunk, d), lambda b, n: (fwd(b, n), 0)),
                   pl.BlockSpec((chunk, d), lambda b, n: (bwd(b, n), 0)),
                   st_spec],
        out_shape=[o_shape, o_shape, jax.ShapeDtypeStruct((n_seq, 2, heads, dk, dk), F32)],
        scratch_shapes=[pltpu.VMEM((2, heads, dk, dk), F32)],
        compiler_params=_cparams(("parallel", "arbitrary")),
        name="gdn_scan",
    )(*args)


def _outproj_body(x_ref, mod_ref, zhy_ref, of_ref, ob_ref, zg_ref, ghy_ref, go_ref, npost_ref, w_ref,
                  o_ref, *, mi, heads, dk):
    yhy = _rms(zhy_ref[...], ghy_ref[...]).astype(BF16)
    o = of_ref[...] + ob_ref[...]
    zg = zg_ref[...]
    parts = [yhy]
    for h in range(heads):
        sl = slice(h * dk, (h + 1) * dk)
        parts.append((_rms(o[:, sl], go_ref[...]) * _silu(zg[:, sl])).astype(BF16))
    y = _dot(jnp.concatenate(parts, axis=-1), w_ref[...])
    gate = mod_ref[0, mi:mi + 1, :]
    o_ref[...] = x_ref[...] + gate * _rms(y, npost_ref[...])


def _outproj_call(x, mod, zhy, o_f, o_b, proj, ghy, go, npost, w, *, mi, zg_col_blk, g0, rows_per_req, heads, dk,
                  tm=512):
    m, d = x.shape
    c = zhy.shape[1]
    dg = heads * dk
    row = lambda width: pl.BlockSpec((tm, width), lambda i: (i, 0))
    return pl.pallas_call(
        functools.partial(_outproj_body, mi=mi, heads=heads, dk=dk),
        grid=(m // tm,),
        in_specs=[row(d),
                  _mod_spec(d, tm, g0, rows_per_req),
                  row(c), row(dg), row(dg),
                  pl.BlockSpec((tm, dg), lambda i: (i, zg_col_blk)),
                  pl.BlockSpec((1, c), lambda i: (0, 0)),
                  pl.BlockSpec((1, dk), lambda i: (0, 0)),
                  pl.BlockSpec((1, d), lambda i: (0, 0)),
                  pl.BlockSpec((c + dg, d), lambda i: (0, 0))],
        out_specs=row(d),
        out_shape=jax.ShapeDtypeStruct((m, d), F32),
        compiler_params=_cparams(("parallel",)),
        name="out_proj",
    )(x, mod, zhy, o_f, o_b, proj, ghy.reshape(1, c), go.reshape(1, dk), npost.reshape(1, d), w)


@functools.lru_cache(maxsize=None)
def _grid_pos_embed_np(n_tok, dim):
    rows = n_tok // GRID_W
    r = np.repeat(np.arange(rows, dtype=np.float64), GRID_W)
    col = np.tile(np.arange(GRID_W, dtype=np.float64), rows)
    quarter = dim // 4
    omega = 1.0 / (POS_BASE ** (np.arange(quarter, dtype=np.float64) / quarter))
    ar = r[:, None] * omega[None]
    ac = col[:, None] * omega[None]
    return np.concatenate([np.sin(ar), np.cos(ar), np.sin(ac), np.cos(ac)], axis=-1).astype(np.float32)


def _mixer(proj, proj_ab, s0, *, n_seq, n_tok, hy_conv_w, hy_f, hy_decay, hy_bias, gdn_conv_w, gdn_a_log,
           gdn_dt_bias, c_hy, d_gdn):
    heads = GDN_HEADS
    dk = d_gdn // heads
    o0 = 3 * c_hy

    a_exp = jnp.exp(gdn_a_log.astype(F32))
    zeros = jnp.zeros_like(a_exp)
    av32 = jnp.stack([a_exp, zeros], axis=1).reshape(-1)
    dtv32 = jnp.stack([gdn_dt_bias.astype(F32), zeros], axis=1).reshape(-1)
    pad = 128 - 4 * heads
    av = jnp.pad(av32, (0, pad)).reshape(1, 128)
    dtv = jnp.pad(dtv32, (0, pad)).reshape(1, 128)
    avt = av32.reshape(4 * heads, 1)
    dtvt = dtv32.reshape(4 * heads, 1)
    abt = jnp.transpose(proj_ab[:, :4 * heads])

    r = min(512, n_tok)
    tables = _dft_tables(n_tok, r)
    ha, hb, hc = _spectra_call(n_tok, *hy_f, hy_decay, tables, r=r)
    zhy = _hyena_call(proj, hy_conv_w, hy_bias, tables, ha, hb, hc, n_seq=n_seq, n_tok=n_tok, c=c_hy, r=r,
                      n_sub=max(1, min(n_seq, 1024 // n_tok)))
    qkv = _gdn_prep_call(proj, gdn_conv_w, n_seq=n_seq, n_tok=n_tok, col0=o0, d_gdn=d_gdn, dk=dk)
    o_f, o_b, s_fin = _gdn_call(qkv, proj_ab, abt, av, dtv, avt, dtvt, s0, n_seq=n_seq, n_tok=n_tok, heads=heads,
                                dk=dk, chunk=GDN_CHUNK)
    return zhy, o_f, o_b, s_fin


def kernel(x_prompt, x_sample, state_gdn, c, c_ctx, ada_w, ada_b, norm_pre, norm_post, ffn_wg, ffn_wu, ffn_wd,
           w_in, w_out, hy_conv_w, hy_f_w1, hy_f_b1, hy_f_w2, hy_f_b2, hy_f_w3, hy_decay, hy_bias, hy_out_norm,
           gdn_conv_w, gdn_a_log, gdn_dt_bias, gdn_o_norm):
    nb, seq, d = x_prompt.shape
    db, dseq, _ = x_sample.shape
    depth = ada_w.shape[0]
    c_hy = hy_decay.shape[-1]
    d_gdn = gdn_conv_w.shape[-1] // 3
    heads = GDN_HEADS
    dk = d_gdn // heads
    n_ctx_rows = nb * seq
    n_lat_rows = db * dseq
    pos = jnp.asarray(_grid_pos_embed_np(dseq, d)).astype(x_sample.dtype)
    cvec = jnp.concatenate([c_ctx[None], c, jnp.zeros((8 - 1 - db, d), F32)], axis=0)
    wg, wu, wd = ffn_wg.astype(BF16), ffn_wu.astype(BF16), ffn_wd.astype(BF16)
    n_main_cols = 3 * c_hy + 4 * d_gdn
    n_tail = w_in.shape[-1] - n_main_cols
    w_in_t = jnp.swapaxes(w_in, 1, 2).astype(BF16)

    xs = [x_prompt.reshape(n_ctx_rows, d), x_sample.reshape(n_lat_rows, d)]
    groups = [dict(g0=0, rows_per_req=n_ctx_rows), dict(g0=1, rows_per_req=dseq)]
    shapes = [(nb, seq), (db, dseq)]
    ctx_states = []
    for l in range(depth):
        mod = _mod_call(cvec, ada_w[l], ada_b[l]).reshape(8, N_MOD, d)
        w_tail_t = jnp.pad(w_in_t[l, n_main_cols:], ((0, 128 - n_tail), (0, 0)))
        w_out_l = w_out[l].astype(BF16)
        hy_f = (hy_f_w1[l], hy_f_b1[l], hy_f_w2[l], hy_f_b2[l], hy_f_w3[l])
        states = [None, state_gdn[:, l].astype(F32)]
        for gi, (grp, (n_seq, n_tok)) in enumerate(zip(groups, shapes)):
            x = _ffn_call(xs[gi], mod, norm_pre[l, 0], norm_post[l, 0], wg, wu, wd, (l, 0), mi=0,
                          pos=pos if (gi == 1 and l == 0) else None, **grp)
            proj, proj_ab = _inproj_call(x, mod, norm_pre[l, 1], w_in_t, l, w_tail_t, mi=3,
                                         n_main_cols=n_main_cols, **grp)
            zhy, o_f, o_b, s_fin = _mixer(
                proj, proj_ab, states[gi], n_seq=n_seq, n_tok=n_tok, hy_conv_w=hy_conv_w[l], hy_f=hy_f,
                hy_decay=hy_decay[l], hy_bias=hy_bias[l], gdn_conv_w=gdn_conv_w[l], gdn_a_log=gdn_a_log[l],
                gdn_dt_bias=gdn_dt_bias[l], c_hy=c_hy, d_gdn=d_gdn)
            if gi == 0:
                ctx_states.append(s_fin)
            x = _outproj_call(x, mod, zhy, o_f, o_b, proj, hy_out_norm[l], gdn_o_norm[l], norm_post[l, 1], w_out_l,
                              mi=5, zg_col_blk=(3 * c_hy + 3 * d_gdn) // d_gdn, heads=heads, dk=dk, **grp)
            xs[gi] = _ffn_call(x, mod, norm_pre[l, 2], norm_post[l, 2], wg, wu, wd, (l, 1), mi=6, **grp)

    new_state = ctx_states[0][:, None] if depth == 1 else jnp.stack(ctx_states, axis=1)
    return xs[0].reshape(nb, seq, d), xs[1].reshape(db, dseq, d), new_state
```

```python
import functools
import math

import numpy as np
import jax
import jax.numpy as jnp
from jax import lax
from jax.experimental import pallas as pl
from jax.experimental.pallas import tpu as pltpu

F32 = jnp.float32
BF16 = jnp.bfloat16

N_MOD = 9
RMS_EPS = 1e-6
GRID_W = 64
POS_BASE = 10000.0
HY_BANDS = 16
HY_SIN_FREQ = 1.0
HY_WINDOW_SHIFT = 0.05
GDN_HEADS = 8
GDN_CHUNK = 128
VMEM_LIMIT = 56 * 1024 * 1024
ROW_CHUNK = 16


def _cparams(sem):
    return pltpu.CompilerParams(dimension_semantics=sem, vmem_limit_bytes=VMEM_LIMIT)


def _dot(a, b):
    return jnp.dot(a, b, preferred_element_type=F32)


def _dot_nt(a, b):
    return lax.dot_general(a, b, (((1,), (1,)), ((), ())), preferred_element_type=F32)


def _dot_tn(a, b):
    return lax.dot_general(a, b, (((0,), (0,)), ((), ())), preferred_element_type=F32)


def _rms(x, g):
    ms = jnp.mean(x * x, axis=-1, keepdims=True)
    return x * lax.rsqrt(ms + RMS_EPS) * g


def _silu(x):
    return x * jax.nn.sigmoid(x)


def _for_row_chunks(n_rows, fn):
    def body(c, carry):
        fn(pl.ds(pl.multiple_of(c * ROW_CHUNK, ROW_CHUNK), ROW_CHUNK))
        return carry

    lax.fori_loop(0, n_rows // ROW_CHUNK, body, 0, unroll=8)


def _mod_spec(d, tm, g0, rows_per_req):
    tpr = rows_per_req // tm
    return pl.BlockSpec((1, N_MOD, d), lambda i, *_: (g0 + i // tpr, 0, 0))


def _mod_body(c_ref, w_ref, b_ref, o_ref):
    s = _silu(c_ref[...]).astype(BF16)
    o_ref[...] = _dot(s, w_ref[...].astype(BF16)) + b_ref[...]


def _mod_call(cvec, ada_w, ada_b):
    g, d = cvec.shape
    n = ada_w.shape[1]
    tn = 1024
    return pl.pallas_call(
        _mod_body,
        grid=(n // tn,),
        in_specs=[pl.BlockSpec((g, d), lambda j: (0, 0)),
                  pl.BlockSpec((d, tn), lambda j: (0, j)),
                  pl.BlockSpec((1, tn), lambda j: (0, j))],
        out_specs=pl.BlockSpec((g, tn), lambda j: (0, j)),
        out_shape=jax.ShapeDtypeStruct((g, n), F32),
        compiler_params=_cparams(("arbitrary",)),
        name="mod_table",
    )(cvec, ada_w, ada_b.reshape(1, n))


def _ffn_body(*refs, mi, has_pos):
    refs = list(refs)
    x_ref = refs.pop(0)
    pos_ref = refs.pop(0) if has_pos else None
    mod_ref, npre_ref, npost_ref, wg_ref, wu_ref, wd_ref, o_ref, h_ref = refs
    j = pl.program_id(1)
    tm = x_ref.shape[0]

    def x_rows(rows):
        return x_ref[rows, :] + pos_ref[rows, :] if has_pos else x_ref[rows, :]

    @pl.when(j == 0)
    def _():
        shift = mod_ref[0, mi:mi + 1, :]
        gain = npre_ref[...] * (1.0 + mod_ref[0, mi + 1:mi + 2, :])

        def rows_body(rows):
            h_ref[rows, :] = (_rms(x_rows(rows), gain) + shift).astype(BF16)
            o_ref[rows, :] = jnp.zeros((ROW_CHUNK, o_ref.shape[1]), F32)

        _for_row_chunks(tm, rows_body)

    h = h_ref[...]
    g = _dot(h, wg_ref[...])
    u = _dot(h, wu_ref[...])
    a = (_silu(g) * u).astype(BF16)
    o_ref[...] += _dot(a, wd_ref[...])

    @pl.when(j == pl.num_programs(1) - 1)
    def _():
        gain = npost_ref[...] * (0.5 * mod_ref[0, mi + 2:mi + 3, :])
        for r0 in range(0, tm, ROW_CHUNK):
            rows = slice(r0, r0 + ROW_CHUNK)
            o_ref[rows, :] = x_rows(rows) + _rms(o_ref[rows, :], gain)


def _ffn_call(x, mod, npre, npost, wg, wu, wd, wsel, *, mi, g0, rows_per_req, pos=None, tf=512):
    n_rows, d = x.shape
    ff = wg.shape[-1]
    wl, wk = wsel
    tm = 512 if pos is not None else 1024
    tpr = rows_per_req // tm
    const = lambda shape: pl.BlockSpec(shape, lambda i, j: (0, 0))
    in_specs = [pl.BlockSpec((tm, d), lambda i, j: (i, 0))]
    args = [x]
    if pos is not None:
        in_specs.append(pl.BlockSpec((tm, d), lambda i, j: (i % tpr, 0)))
        args.append(pos)
    in_specs += [_mod_spec(d, tm, g0, rows_per_req),
                 const((1, d)), const((1, d)),
                 pl.BlockSpec((None, None, d, tf), lambda i, j: (wl, wk, 0, j)),
                 pl.BlockSpec((None, None, d, tf), lambda i, j: (wl, wk, 0, j)),
                 pl.BlockSpec((None, None, tf, d), lambda i, j: (wl, wk, j, 0))]
    args += [mod, npre.reshape(1, d), npost.reshape(1, d), wg, wu, wd]
    return pl.pallas_call(
        functools.partial(_ffn_body, mi=mi, has_pos=pos is not None),
        grid=(n_rows // tm, ff // tf),
        in_specs=in_specs,
        out_specs=pl.BlockSpec((tm, d), lambda i, j: (i, 0)),
        out_shape=jax.ShapeDtypeStruct((n_rows, d), F32),
        scratch_shapes=[pltpu.VMEM((tm, d), BF16)],
        compiler_params=_cparams(("parallel", "arbitrary")),
        name="ffn",
    )(*args)


def _inproj_body(x_ref, mod_ref, npre_ref, w_ref, wt_ref, o_ref, ot_ref, h_ref, *, mi, n_main):
    j = pl.program_id(1)

    @pl.when(j == 0)
    def _():
        shift = mod_ref[0, mi:mi + 1, :]
        gain = npre_ref[...] * (1.0 + mod_ref[0, mi + 1:mi + 2, :])

        def rows_body(rows):
            h_ref[rows, :] = (_rms(x_ref[rows, :], gain) + shift).astype(BF16)

        _for_row_chunks(x_ref.shape[0], rows_body)

    @pl.when(j < n_main)
    def _():
        o_ref[...] = _dot_nt(h_ref[...], w_ref[...])

    @pl.when(j == n_main)
    def _():
        ot_ref[...] = _dot_nt(h_ref[...], wt_ref[...])


def _inproj_call(x, mod, npre, w_in_t, layer, w_tail_t, *, mi, n_main_cols, g0, rows_per_req, tm=1024, tn=1024):
    m, d = x.shape
    n_main = n_main_cols // tn
    nt = w_tail_t.shape[0]
    jm = lambda j: jnp.minimum(j, n_main - 1)
    return pl.pallas_call(
        functools.partial(_inproj_body, mi=mi, n_main=n_main),
        grid=(m // tm, n_main + 1),
        in_specs=[pl.BlockSpec((tm, d), lambda i, j: (i, 0)),
                  _mod_spec(d, tm, g0, rows_per_req),
                  pl.BlockSpec((1, d), lambda i, j: (0, 0)),
                  pl.BlockSpec((None, tn, d), lambda i, j: (layer, jm(j), 0)),
                  pl.BlockSpec((nt, d), lambda i, j: (0, 0))],
        out_specs=[pl.BlockSpec((tm, tn), lambda i, j: (i, jm(j))),
                   pl.BlockSpec((tm, nt), lambda i, j: (i, 0))],
        out_shape=[jax.ShapeDtypeStruct((m, n_main_cols), F32), jax.ShapeDtypeStruct((m, nt), F32)],
        scratch_shapes=[pltpu.VMEM((tm, d), BF16)],
        compiler_params=_cparams(("parallel", "arbitrary")),
        name="in_proj",
    )(x, mod, npre.reshape(1, d), w_in_t, w_tail_t)


def _conv3_rows(x_ref, w, r0, nr, seq_len):
    zeros = jnp.zeros((8, x_ref.shape[1]), x_ref.dtype)
    top = zeros if r0 % seq_len == 0 else x_ref[r0 - 8:r0, :]
    mid = x_ref[r0:r0 + nr, :]
    bot = zeros if (r0 + nr) % seq_len == 0 else x_ref[r0 + nr:r0 + nr + 8, :]
    return _conv3_slab(top, mid, bot, w)


def _conv3_rows_dyn(x_ref, w, r0, nr, seq_len):
    n = x_ref.shape[0]
    top = x_ref[pl.ds(pl.multiple_of(jnp.maximum(r0 - 8, 0), 8), 8), :]
    mid = x_ref[pl.ds(pl.multiple_of(r0, 8), nr), :]
    bot = x_ref[pl.ds(pl.multiple_of(jnp.minimum(r0 + nr, n - 8), 8), 8), :]
    top = jnp.where(r0 % seq_len == 0, 0.0, top)
    bot = jnp.where((r0 + nr) % seq_len == 0, 0.0, bot)
    return _conv3_slab(top, mid, bot, w)


def _conv3_slab(top, mid, bot, w):
    nr = mid.shape[0]
    slab = jnp.concatenate([top, mid, bot], axis=0)
    prev = pltpu.roll(slab, 1, 0)[8:8 + nr]
    nxt = pltpu.roll(slab, nr + 15, 0)[8:8 + nr]
    return prev * w[0:1, :] + mid * w[1:2, :] + nxt * w[2:3, :]


@functools.lru_cache(maxsize=None)
def _dft_tables_np(n_tok, r):
    big = 2 * n_tok
    k = np.arange(n_tok, dtype=np.int64)
    t = np.arange(n_tok, dtype=np.int64)
    ang = ((k[:, None] * t[None, :]) % big).astype(np.float64) * (2.0 * math.pi / big)
    cosm = np.cos(ang)
    sinm = np.sin(ang)
    sinm[0, :] = 1.0 - 2.0 * (t % 2)
    nj = n_tok // r
    fr = np.concatenate([cosm.reshape(nj, r, n_tok), sinm.reshape(nj, r, n_tok)], axis=1)
    g_full = fr.reshape(big, n_tok).T
    gr = np.concatenate([g_full[:, :n_tok].reshape(nj, r, n_tok), g_full[:, n_tok:].reshape(nj, r, n_tok)], axis=1)
    return np.concatenate([fr, gr], axis=0).astype(np.float32)


def _dft_tables(n_tok, r):
    return jnp.asarray(_dft_tables_np(n_tok, r)).astype(BF16)


def _spectra_body(featt_ref, t_ref, w1t_ref, b1_ref, w2t_ref, b2_ref, w3f_ref, w3b_ref, dec_ref, f_ref,
                  ha_ref, hb_ref, hc_ref, e_ref, o_ref, h_ref, *, n_tok, r):
    j = pl.program_id(1)

    @pl.when((pl.program_id(0) == 0) & (j == 0))
    def _():
        h = jnp.sin(HY_SIN_FREQ * (_dot(w1t_ref[...].astype(BF16), featt_ref[...].astype(BF16)) + b1_ref[...]))
        h = jnp.sin(HY_SIN_FREQ * (_dot(w2t_ref[...].astype(BF16), h.astype(BF16)) + b2_ref[...]))
        h_ref[...] = h.astype(BF16)

    @pl.when(j == 0)
    def _():
        hb16 = h_ref[...]
        window = jnp.exp(-t_ref[...] * jnp.abs(dec_ref[...])) + HY_WINDOW_SHIFT
        fwd = _dot_tn(hb16, w3f_ref[...].astype(BF16)) * window
        bwd = _dot_tn(hb16, w3b_ref[...].astype(BF16)) * window
        row = lax.broadcasted_iota(jnp.int32, bwd.shape, 0)
        bwd = jnp.where(row == 0, 0.0, bwd)
        norm = jnp.sum(jnp.abs(fwd), axis=0, keepdims=True) + jnp.sum(jnp.abs(bwd), axis=0, keepdims=True)
        fwd = fwd / norm
        bwd = bwd / norm
        e_ref[...] = (fwd + bwd).astype(BF16)
        o_ref[...] = (fwd - bwd).astype(BF16)

    fb = f_ref[0]
    e = e_ref[...]
    p = _dot(fb[:r], e)
    q = _dot(fb[r:], o_ref[...])
    k = j * r + lax.broadcasted_iota(jnp.int32, p.shape, 0)
    big = 2.0 * n_tok
    wk = jnp.where(k == 0, 1.0 / big, 2.0 / big)
    ha_ref[...] = wk * p
    hb_ref[...] = jnp.where(k == 0, 0.0, -wk * q)

    @pl.when(j == 0)
    def _():
        hc_ref[...] = _dot(fb[r:r + 8], e)[0:1, :] * (1.0 / big)


def _spectra_call(n_tok, w1, b1, w2, b2, w3, decay, fr, *, r, cn=512):
    f32 = jnp.float32
    order, c = decay.shape
    oc = order * c
    hid = w2.shape[0]
    emb = w1.shape[0]
    embp = 128
    idx = jnp.arange(n_tok, dtype=f32)
    t = idx / (n_tok - 1)
    bands = jnp.arange(1, HY_BANDS + 1, dtype=f32)
    ang = (2.0 * math.pi / n_tok) * idx[:, None] * bands[None, :]
    feats = jnp.concatenate([t[:, None], jnp.cos(ang), jnp.sin(ang)], axis=-1)
    featt = jnp.pad(feats, ((0, 0), (0, embp - emb))).T
    w1t = jnp.pad(w1, ((0, embp - emb), (0, 0))).T
    nj = n_tok // r
    ncb = oc // cn
    out = jax.ShapeDtypeStruct((n_tok, oc), F32)
    const = lambda shape: pl.BlockSpec(shape, lambda cb, j: (0, 0))
    return pl.pallas_call(
        functools.partial(_spectra_body, n_tok=n_tok, r=r),
        grid=(ncb, nj),
        in_specs=[const((embp, n_tok)), const((n_tok, 1)), const((hid, embp)), const((hid, 1)),
                  const((hid, hid)), const((hid, 1)),
                  pl.BlockSpec((hid, cn), lambda cb, j: (0, cb)),
                  pl.BlockSpec((hid, cn), lambda cb, j: (0, ncb + cb)),
                  pl.BlockSpec((1, cn), lambda cb, j: (0, cb)),
                  pl.BlockSpec((1, 2 * r, n_tok), lambda cb, j: (j, 0, 0))],
        out_specs=[pl.BlockSpec((r, cn), lambda cb, j: (j, cb))] * 2 + [pl.BlockSpec((1, cn), lambda cb, j: (0, cb))],
        out_shape=[out, out, jax.ShapeDtypeStruct((1, oc), F32)],
        scratch_shapes=[pltpu.VMEM((n_tok, cn), BF16), pltpu.VMEM((n_tok, cn), BF16),
                        pltpu.VMEM((hid, n_tok), BF16)],
        compiler_params=_cparams(("arbitrary", "arbitrary")),
        name="hyena_spectra",
    )(featt, t[:, None], w1t, b1.reshape(hid, 1), w2.T, b2.reshape(hid, 1), w3, w3, decay.reshape(1, oc), fr)


def _hyena_body(v_ref, x1_ref, x2_ref, wv_ref, wx1_ref, wx2_ref, bias_ref, t_ref, ha_ref, hb_ref, hc_ref,
                o_ref, ub_ref, u2f_ref, z_ref, *, r, nj, seq_len, n_sub):
    t = pl.program_id(2)
    order = t // (2 * nj)
    tt = t % (2 * nj)
    sub = min(128, r)

    @pl.when(t == 0)
    def _():
        w = wv_ref[...]
        for r0 in range(0, n_sub * seq_len, 64):
            ub_ref[0, r0:r0 + 64, :] = _conv3_rows(v_ref, w, r0, 64, seq_len).astype(BF16)

    @pl.when(tt < nj)
    def _():
        tb = t_ref[0]
        ha, hb = ha_ref[...], hb_ref[...]
        nyq = (tt * r + lax.broadcasted_iota(jnp.int32, ha.shape, 0)) == 0
        hc = jnp.where(nyq, hc_ref[...], ha)
        z0 = pl.multiple_of(tt * (2 * r), 2 * r)
        for s in range(n_sub):
            x = _dot(tb, ub_ref[order, s * seq_len:(s + 1) * seq_len, :])
            p = x[:r]
            q = x[r:]
            z_ref[s, pl.ds(z0, 2 * r), :] = jnp.concatenate([p * ha + q * hb, q * hc - p * hb],
                                                             axis=0).astype(BF16)

    def inverse_rows(s):
        tb = t_ref[0]
        return _dot(tb[:r], z_ref[s, :seq_len, :]) + _dot(tb[r:], z_ref[s, seq_len:, :])

    r_blk = (tt - nj) * r

    @pl.when((tt >= nj) & (order == 0))
    def _():
        wv, wx1 = wv_ref[...], wx1_ref[...]
        for s in range(n_sub):
            y = inverse_rows(s)
            for c0 in range(0, r, sub):
                r0 = s * seq_len + r_blk + c0
                u1 = _conv3_rows_dyn(v_ref, wv, r0, sub, seq_len)
                z1 = _conv3_rows_dyn(x1_ref, wx1, r0, sub, seq_len) * (y[c0:c0 + sub] + u1 * bias_ref[0:1, :])
                u2f_ref[pl.ds(pl.multiple_of(r0, 8), sub), :] = z1
                ub_ref[1, pl.ds(pl.multiple_of(r0, 8), sub), :] = z1.astype(BF16)

    @pl.when((tt >= nj) & (order == 1))
    def _():
        wx2 = wx2_ref[...]
        for s in range(n_sub):
            y = inverse_rows(s)
            for c0 in range(0, r, sub):
                r0 = s * seq_len + r_blk + c0
                u2 = u2f_ref[pl.ds(pl.multiple_of(r0, 8), sub), :]
                o_ref[pl.ds(pl.multiple_of(r0, 8), sub), :] = (
                    _conv3_rows_dyn(x2_ref, wx2, r0, sub, seq_len) * (y[c0:c0 + sub] + u2 * bias_ref[1:2, :]))


def _hyena_call(proj, conv_w, bias, tables, ha, hb, hc, *, n_seq, n_tok, c, r, n_sub, cn=512):
    nj = n_tok // r
    ncb = c // cn
    blk = n_sub * n_tok
    steps = 4 * nj
    single = pl.Buffered(1)
    seq_spec = lambda off: pl.BlockSpec((blk, cn), lambda b, cb, t: (b, off * ncb + cb), pipeline_mode=single)
    w_spec = lambda off: pl.BlockSpec((3, cn), lambda b, cb, t: (0, off * ncb + cb))
    h_spec = pl.BlockSpec((r, cn), lambda b, cb, t: (jnp.minimum(t % (2 * nj), nj - 1), (t // (2 * nj)) * ncb + cb))
    in_specs = [seq_spec(0), seq_spec(1), seq_spec(2), w_spec(0), w_spec(1), w_spec(2),
                pl.BlockSpec((2, cn), lambda b, cb, t: (0, cb)),
                pl.BlockSpec((1, 2 * r, n_tok), lambda b, cb, t: (t % (2 * nj), 0, 0)),
                h_spec, h_spec,
                pl.BlockSpec((1, cn), lambda b, cb, t: (0, (t // (2 * nj)) * ncb + cb))]
    return pl.pallas_call(
        functools.partial(_hyena_body, r=r, nj=nj, seq_len=n_tok, n_sub=n_sub),
        grid=(n_seq // n_sub, ncb, steps),
        in_specs=in_specs,
        out_specs=pl.BlockSpec((blk, cn), lambda b, cb, t: (b, cb)),
        out_shape=jax.ShapeDtypeStruct((n_seq * n_tok, c), F32),
        scratch_shapes=[pltpu.VMEM((2, blk, cn), BF16), pltpu.VMEM((blk, cn), F32),
                        pltpu.VMEM((n_sub, 2 * n_tok, cn), BF16)],
        compiler_params=_cparams(("parallel", "parallel", "arbitrary")),
        name="hyena_conv",
    )(proj, proj, proj, conv_w, conv_w, conv_w, bias, tables, ha, hb, hc)


def _gdn_prep_body(x_ref, w_ref, o_ref, *, heads_per_blk, dk, n_qk_blk, rows, seq_len):
    cb = pl.program_id(1)
    n = x_ref.shape[0]
    w = w_ref[...]
    is_v = cb >= n_qk_blk
    scale = jnp.where(cb < n_qk_blk // 2, dk ** -0.5, 1.0)
    for r0 in range(0, n, rows):
        y = _silu(_conv3_rows(x_ref, w, r0, rows, seq_len))
        for h in range(heads_per_blk):
            yh = y[:, h * dk:(h + 1) * dk]
            inv = lax.rsqrt(jnp.sum(yh * yh, axis=-1, keepdims=True) + RMS_EPS) * scale
            o_ref[r0:r0 + rows, h * dk:(h + 1) * dk] = yh * jnp.where(is_v, 1.0, inv)


def _gdn_prep_call(proj, conv_w, *, n_seq, n_tok, col0, d_gdn, dk, cn=512, blk_rows=2048):
    ncb = 3 * d_gdn // cn
    cb0 = col0 // cn
    rows = n_seq * n_tok
    blk_rows = min(blk_rows, rows)
    return pl.pallas_call(
        functools.partial(_gdn_prep_body, heads_per_blk=cn // dk, dk=dk, n_qk_blk=2 * d_gdn // cn,
                          rows=min(64, n_tok), seq_len=n_tok),
        grid=(rows // blk_rows, ncb),
        in_specs=[pl.BlockSpec((blk_rows, cn), lambda b, cb: (b, cb0 + cb)),
                  pl.BlockSpec((3, cn), lambda b, cb: (0, cb))],
        out_specs=pl.BlockSpec((blk_rows, cn), lambda b, cb: (b, cb)),
        out_shape=jax.ShapeDtypeStruct((rows, 3 * d_gdn), F32),
        compiler_params=_cparams(("parallel", "parallel")),
        name="gdn_prep",
    )(proj, conv_w)


def _split3(x):
    hi = x.astype(BF16)
    r1 = x - hi.astype(F32)
    mid = r1.astype(BF16)
    lo = (r1 - mid.astype(F32)).astype(BF16)
    return hi, mid, lo


def _gdn_body(*refs, heads, dk, chunk, has_s0):
    (qf_ref, kf_ref, vf_ref, abf_ref, abtf_ref, qb_ref, kb_ref, vb_ref, abb_ref, abtb_ref,
     av_ref, dtv_ref, avt_ref, dtvt_ref) = refs[:14]
    s0_ref = refs[14] if has_s0 else None
    of_ref, ob_ref, sfin_ref, s_ref = refs[-4:]
    n = pl.program_id(1)

    @pl.when(n == 0)
    def _():
        s_ref[...] = s0_ref[0] if has_s0 else jnp.zeros_like(s_ref)

    ci = lax.broadcasted_iota(jnp.int32, (chunk, chunk), 0)
    cj = lax.broadcasted_iota(jnp.int32, (chunk, chunk), 1)
    eye = (ci == cj).astype(F32)
    diag_blk = (ci >> 3) == (cj >> 3)
    merge_blks = [((ci >> s) ^ (cj >> s)) == 1 for s in range(3, int(math.log2(chunk)))]
    lower = cj <= ci
    upper = cj >= ci

    units = []
    for dr, (q_ref, k_ref, v_ref, ab_ref, abt_ref, o_ref) in enumerate(
            ((qf_ref, kf_ref, vf_ref, abf_ref, abtf_ref, of_ref),
             (qb_ref, kb_ref, vb_ref, abb_ref, abtb_ref, ob_ref))):
        incl = upper if dr else lower
        strict = (cj > ci) if dr else (cj < ci)
        ones_incl = jnp.where(incl, 1.0, 0.0).astype(BF16)
        ones_incl_t = jnp.where(lower if dr else upper, 1.0, 0.0).astype(BF16)
        ab = ab_ref[...]
        g_col = -av_ref[...] * jax.nn.softplus(ab + dtv_ref[...])
        beta_col = jax.nn.sigmoid(ab)
        abt = abt_ref[...]
        g_row = -avt_ref[...] * jax.nn.softplus(abt + dtvt_ref[...])
        h3 = _split3(g_col)
        gc_col = _dot(ones_incl, h3[0]) + _dot(ones_incl, h3[1]) + _dot(ones_incl, h3[2])
        r3 = _split3(g_row)
        gc_row = _dot(r3[0], ones_incl_t) + _dot(r3[1], ones_incl_t) + _dot(r3[2], ones_incl_t)
        gl_all = jnp.sum(g_col, axis=0, keepdims=True)
        for h in range(heads):
            cg = dr * 2 * heads + h
            cb = cg + heads
            sl = slice(h * dk, (h + 1) * dk)
            units.append(dict(dr=dr, h=h, sl=sl, q_ref=q_ref, k_ref=k_ref, v_ref=v_ref, o_ref=o_ref,
                              incl=incl, strict=strict, gc=gc_col[:, cg:cg + 1], gr=gc_row[cg:cg + 1, :],
                              beta=beta_col[:, cb:cb + 1], gl=gl_all[:, cg:cg + 1]))

    def stage(fn):
        return [fn(u) for u in units]

    bf = lambda t: t.astype(BF16)
    k16 = stage(lambda u: bf(u["k_ref"][:, u["sl"]]))
    decay = stage(lambda u: jnp.where(u["incl"], jnp.exp(jnp.where(u["incl"], u["gc"] - u["gr"], 0.0)), 0.0))
    kbeta = stage(lambda u: u["k_ref"][:, u["sl"]] * u["beta"])
    kq = [_dot_nt(jnp.concatenate([bf(kb), bf(u["q_ref"][:, u["sl"]])], axis=0), k)
          for u, kb, k in zip(units, kbeta, k16)]
    a_low = [jnp.where(u["strict"], t[:chunk] * dc, 0.0) for u, t, dc in zip(units, kq, decay)]
    attn = [bf(t[chunk:] * dc) for t, dc in zip(kq, decay)]

    d = [jnp.where(diag_blk, a, 0.0) for a in a_low]
    d2 = [_dot(bf(t), bf(t)) for t in d]
    x = [eye - t for t in d]
    x = [t + _dot(bf(t), bf(p)) for t, p in zip(x, d2)]
    d4 = [_dot(bf(p), bf(p)) for p in d2]
    x = [t + _dot(bf(t), bf(p)) for t, p in zip(x, d4)]
    for off_blk in merge_blks:
        x16 = [bf(t) for t in x]
        tmp = [_dot(bf(jnp.where(off_blk, a, 0.0)), t16) for a, t16 in zip(a_low, x16)]
        x = [t - _dot(t16, bf(m)) for t, t16, m in zip(x, x16, tmp)]

    e_gc = stage(lambda u: jnp.exp(u["gc"]))
    rhs = [bf(jnp.concatenate([u["v_ref"][:, u["sl"]] * u["beta"], kb * e], axis=-1))
           for u, kb, e in zip(units, kbeta, e_gc)]
    sol = [_dot(bf(t), r) for t, r in zip(x, rhs)]

    s16 = stage(lambda u: bf(s_ref[u["dr"], u["h"]]))
    ws = [_dot(jnp.concatenate([bf(so[:, dk:]), bf(u["q_ref"][:, u["sl"]] * e)], axis=0), s)
          for u, so, e, s in zip(units, sol, e_gc, s16)]
    vn16 = [bf(so[:, :dk] - t[:chunk]) for so, t in zip(sol, ws)]
    o = [t[chunk:] + _dot(at, vn) for t, at, vn in zip(ws, attn, vn16)]
    for u, t in zip(units, o):
        u["o_ref"][:, u["sl"]] = t
    s_new = [s_ref[u["dr"], u["h"]] * jnp.exp(u["gl"])
             + _dot_tn(bf(u["k_ref"][:, u["sl"]] * jnp.exp(u["gl"] - u["gc"])), vn)
             for u, vn in zip(units, vn16)]
    for u, t in zip(units, s_new):
        s_ref[u["dr"], u["h"]] = t

    @pl.when(n == pl.num_programs(1) - 1)
    def _():
        sfin_ref[0] = s_ref[...]


def _gdn_call(qkv, ab, abt, av, dtv, avt, dtvt, s0, *, n_seq, n_tok, heads, dk, chunk):
    nc = n_tok // chunk
    d = heads * dk
    fwd = lambda b, n: b * nc + n
    bwd = lambda b, n: b * nc + nc - 1 - n
    qkv_spec = lambda idx, which: pl.BlockSpec((chunk, d), lambda b, n: (idx(b, n), which))
    ab_spec = lambda idx: pl.BlockSpec((chunk, 128), lambda b, n: (idx(b, n), 0))
    abt_spec = lambda idx: pl.BlockSpec((4 * heads, chunk), lambda b, n: (0, idx(b, n)))
    small = lambda shape: pl.BlockSpec(shape, lambda b, n: (0, 0))
    st_spec = pl.BlockSpec((1, 2, heads, dk, dk), lambda b, n: (b, 0, 0, 0, 0))
    o_shape = jax.ShapeDtypeStruct((n_seq * n_tok, d), F32)
    in_specs = [qkv_spec(fwd, 0), qkv_spec(fwd, 1), qkv_spec(fwd, 2), ab_spec(fwd), abt_spec(fwd),
                qkv_spec(bwd, 0), qkv_spec(bwd, 1), qkv_spec(bwd, 2), ab_spec(bwd), abt_spec(bwd),
                small((1, 128)), small((1, 128)), small((4 * heads, 1)), small((4 * heads, 1))]
    args = [qkv, qkv, qkv, ab, abt, qkv, qkv, qkv, ab, abt, av, dtv, avt, dtvt]
    if s0 is not None:
        in_specs.append(st_spec)
        args.append(s0)
    return pl.pallas_call(
        functools.partial(_gdn_body, heads=heads, dk=dk, chunk=chunk, has_s0=s0 is not None),
        grid=(n_seq, nc),
        in_specs=in_specs,
        out_specs=[pl.BlockSpec((chunk, d), lambda b, n: (fwd(b, n), 0)),
                   pl.BlockSpec((chunk, d), lambda b, n: (bwd(b, n), 0)),
                   st_spec],
        out_shape=[o_shape, o_shape, jax.ShapeDtypeStruct((n_seq, 2, heads, dk, dk), F32)],
        scratch_shapes=[pltpu.VMEM((2, heads, dk, dk), F32)],
        compiler_params=_cparams(("parallel", "arbitrary")),
        name="gdn_scan",
    )(*args)


def _outproj_body(x_ref, mod_ref, zhy_ref, of_ref, ob_ref, zg_ref, ghy_ref, go_ref, npost_ref, w_ref,
                  o_ref, *, mi, heads, dk):
    yhy = _rms(zhy_ref[...], ghy_ref[...]).astype(BF16)
    o = of_ref[...] + ob_ref[...]
    zg = zg_ref[...]
    parts = [yhy]
    for h in range(heads):
        sl = slice(h * dk, (h + 1) * dk)
        parts.append((_rms(o[:, sl], go_ref[...]) * _silu(zg[:, sl])).astype(BF16))
    y = _dot(jnp.concatenate(parts, axis=-1), w_ref[...])
    gate = mod_ref[0, mi:mi + 1, :]
    o_ref[...] = x_ref[...] + gate * _rms(y, npost_ref[...])


def _outproj_call(x, mod, zhy, o_f, o_b, proj, ghy, go, npost, w, *, mi, zg_col_blk, g0, rows_per_req, heads, dk,
                  tm=512):
    m, d = x.shape
    c = zhy.shape[1]
    dg = heads * dk
    row = lambda width: pl.BlockSpec((tm, width), lambda i: (i, 0))
    return pl.pallas_call(
        functools.partial(_outproj_body, mi=mi, heads=heads, dk=dk),
        grid=(m // tm,),
        in_specs=[row(d),
                  _mod_spec(d, tm, g0, rows_per_req),
                  row(c), row(dg), row(dg),
                  pl.BlockSpec((tm, dg), lambda i: (i, zg_col_blk)),
                  pl.BlockSpec((1, c), lambda i: (0, 0)),
                  pl.BlockSpec((1, dk), lambda i: (0, 0)),
                  pl.BlockSpec((1, d), lambda i: (0, 0)),
                  pl.BlockSpec((c + dg, d), lambda i: (0, 0))],
        out_specs=row(d),
        out_shape=jax.ShapeDtypeStruct((m, d), F32),
        compiler_params=_cparams(("parallel",)),
        name="out_proj",
    )(x, mod, zhy, o_f, o_b, proj, ghy.reshape(1, c), go.reshape(1, dk), npost.reshape(1, d), w)


@functools.lru_cache(maxsize=None)
def _grid_pos_embed_np(n_tok, dim):
    rows = n_tok // GRID_W
    r = np.repeat(np.arange(rows, dtype=np.float64), GRID_W)
    col = np.tile(np.arange(GRID_W, dtype=np.float64), rows)
    quarter = dim // 4
    omega = 1.0 / (POS_BASE ** (np.arange(quarter, dtype=np.float64) / quarter))
    ar = r[:, None] * omega[None]
    ac = col[:, None] * omega[None]
    return np.concatenate([np.sin(ar), np.cos(ar), np.sin(ac), np.cos(ac)], axis=-1).astype(np.float32)


def _mixer(proj, proj_ab, s0, *, n_seq, n_tok, hy_conv_w, hy_f, hy_decay, hy_bias, gdn_conv_w, gdn_a_log,
           gdn_dt_bias, c_hy, d_gdn):
    heads = GDN_HEADS
    dk = d_gdn // heads
    o0 = 3 * c_hy

    a_exp = jnp.exp(gdn_a_log.astype(F32))
    zeros = jnp.zeros_like(a_exp)
    av32 = jnp.stack([a_exp, zeros], axis=1).reshape(-1)
    dtv32 = jnp.stack([gdn_dt_bias.astype(F32), zeros], axis=1).reshape(-1)
    pad = 128 - 4 * heads
    av = jnp.pad(av32, (0, pad)).reshape(1, 128)
    dtv = jnp.pad(dtv32, (0, pad)).reshape(1, 128)
    avt = av32.reshape(4 * heads, 1)
    dtvt = dtv32.reshape(4 * heads, 1)
    abt = jnp.transpose(proj_ab[:, :4 * heads])

    r = min(512, n_tok)
    tables = _dft_tables(n_tok, r)
    ha, hb, hc = _spectra_call(n_tok, *hy_f, hy_decay, tables, r=r)
    zhy = _hyena_call(proj, hy_conv_w, hy_bias, tables, ha, hb, hc, n_seq=n_seq, n_tok=n_tok, c=c_hy, r=r,
                      n_sub=max(1, min(n_seq, 1024 // n_tok)))
    qkv = _gdn_prep_call(proj, gdn_conv_w, n_seq=n_seq, n_tok=n_tok, col0=o0, d_gdn=d_gdn, dk=dk)
    o_f, o_b, s_fin = _gdn_call(qkv, proj_ab, abt, av, dtv, avt, dtvt, s0, n_seq=n_seq, n_tok=n_tok, heads=heads,
                                dk=dk, chunk=GDN_CHUNK)
    return zhy, o_f, o_b, s_fin


def kernel(x_prompt, x_sample, state_gdn, c, c_ctx, ada_w, ada_b, norm_pre, norm_post, ffn_wg, ffn_wu, ffn_wd,
           w_in, w_out, hy_conv_w, hy_f_w1, hy_f_b1, hy_f_w2, hy_f_b2, hy_f_w3, hy_decay, hy_bias, hy_out_norm,
           gdn_conv_w, gdn_a_log, gdn_dt_bias, gdn_o_norm):
    nb, seq, d = x_prompt.shape
    db, dseq, _ = x_sample.shape
    depth = ada_w.shape[0]
    c_hy = hy_decay.shape[-1]
    d_gdn = gdn_conv_w.shape[-1] // 3
    heads = GDN_HEADS
    dk = d_gdn // heads
    n_ctx_rows = nb * seq
    n_lat_rows = db * dseq
    pos = jnp.asarray(_grid_pos_embed_np(dseq, d)).astype(x_sample.dtype)
    cvec = jnp.concatenate([c_ctx[None], c, jnp.zeros((8 - 1 - db, d), F32)], axis=0)
    wg, wu, wd = ffn_wg.astype(BF16), ffn_wu.astype(BF16), ffn_wd.astype(BF16)
    n_main_cols = 3 * c_hy + 4 * d_gdn
    n_tail = w_in.shape[-1] - n_main_cols
    w_in_t = jnp.swapaxes(w_in, 1, 2).astype(BF16)

    xs = [x_prompt.reshape(n_ctx_rows, d), x_sample.reshape(n_lat_rows, d)]
    groups = [dict(g0=0, rows_per_req=n_ctx_rows), dict(g0=1, rows_per_req=dseq)]
    shapes = [(nb, seq), (db, dseq)]
    ctx_states = []
    for l in range(depth):
        mod = _mod_call(cvec, ada_w[l], ada_b[l]).reshape(8, N_MOD, d)
        w_tail_t = jnp.pad(w_in_t[l, n_main_cols:], ((0, 128 - n_tail), (0, 0)))
        w_out_l = w_out[l].astype(BF16)
        hy_f = (hy_f_w1[l], hy_f_b1[l], hy_f_w2[l], hy_f_b2[l], hy_f_w3[l])
        states = [None, state_gdn[:, l].astype(F32)]
        for gi, (grp, (n_seq, n_tok)) in enumerate(zip(groups, shapes)):
            x = _ffn_call(xs[gi], mod, norm_pre[l, 0], norm_post[l, 0], wg, wu, wd, (l, 0), mi=0,
                          pos=pos if (gi == 1 and l == 0) else None, **grp)
            proj, proj_ab = _inproj_call(x, mod, norm_pre[l, 1], w_in_t, l, w_tail_t, mi=3,
                                         n_main_cols=n_main_cols, **grp)
            zhy, o_f, o_b, s_fin = _mixer(
                proj, proj_ab, states[gi], n_seq=n_seq, n_tok=n_tok, hy_conv_w=hy_conv_w[l], hy_f=hy_f,
                hy_decay=hy_decay[l], hy_bias=hy_bias[l], gdn_conv_w=gdn_conv_w[l], gdn_a_log=gdn_a_log[l],
                gdn_dt_bias=gdn_dt_bias[l], c_hy=c_hy, d_gdn=d_gdn)
            if gi == 0:
                ctx_states.append(s_fin)
            x = _outproj_call(x, mod, zhy, o_f, o_b, proj, hy_out_norm[l], gdn_o_norm[l], norm_post[l, 1], w_out_l,
                              mi=5, zg_col_blk=(3 * c_hy + 3 * d_gdn) // d_gdn, heads=heads, dk=dk, **grp)
            xs[gi] = _ffn_call(x, mod, norm_pre[l, 2], norm_post[l, 2], wg, wu, wd, (l, 1), mi=6, **grp)

    new_state = ctx_states[0][:, None] if depth == 1 else jnp.stack(ctx_states, axis=1)
    return xs[0].reshape(nb, seq, d), xs[1].reshape(db, dseq, d), new_state
```

```python
import functools
import math

import numpy as np
import jax
import jax.numpy as jnp
from jax import lax
from jax.experimental import pallas as pl
from jax.experimental.pallas import tpu as pltpu

F32 = jnp.float32
BF16 = jnp.bfloat16

N_MOD = 9
RMS_EPS = 1e-6
GRID_W = 64
POS_BASE = 10000.0
HY_BANDS = 16
HY_SIN_FREQ = 1.0
HY_WINDOW_SHIFT = 0.05
GDN_HEADS = 8
GDN_CHUNK = 128
VMEM_LIMIT = 56 * 1024 * 1024
ROW_CHUNK = 16


def _cparams(sem):
    return pltpu.CompilerParams(dimension_semantics=sem, vmem_limit_bytes=VMEM_LIMIT)


def _dot(a, b):
    return jnp.dot(a, b, preferred_element_type=F32)


def _dot_nt(a, b):
    return lax.dot_general(a, b, (((1,), (1,)), ((), ())), preferred_element_type=F32)


def _dot_tn(a, b):
    return lax.dot_general(a, b, (((0,), (0,)), ((), ())), preferred_element_type=F32)


def _rms(x, g):
    ms = jnp.mean(x * x, axis=-1, keepdims=True)
    return x * lax.rsqrt(ms + RMS_EPS) * g


def _silu(x):
    return x * jax.nn.sigmoid(x)


def _for_row_chunks(n_rows, fn):
    def body(c, carry):
        fn(pl.multiple_of(c * ROW_CHUNK, ROW_CHUNK))
        return carry

    lax.fori_loop(0, n_rows // ROW_CHUNK, body, 0, unroll=8)


def _mod_spec(d, tm, g0, rows_per_req):
    tpr = rows_per_req // tm
    return pl.BlockSpec((1, N_MOD, d), lambda i, *_: (g0 + i // tpr, 0, 0))


def _mod_body(c_ref, w_ref, b_ref, o_ref):
    s = _silu(c_ref[...]).astype(BF16)
    o_ref[...] = _dot(s, w_ref[...].astype(BF16)) + b_ref[...]


def _mod_call(cvec, ada_w, ada_b):
    g, d = cvec.shape
    n = ada_w.shape[1]
    tn = 1024
    return pl.pallas_call(
        _mod_body,
        grid=(n // tn,),
        in_specs=[pl.BlockSpec((g, d), lambda j: (0, 0)),
                  pl.BlockSpec((d, tn), lambda j: (0, j)),
                  pl.BlockSpec((1, tn), lambda j: (0, j))],
        out_specs=pl.BlockSpec((g, tn), lambda j: (0, j)),
        out_shape=jax.ShapeDtypeStruct((g, n), F32),
        compiler_params=_cparams(("arbitrary",)),
        name="mod_table",
    )(cvec, ada_w, ada_b.reshape(1, n))


def _ffn_body(*refs, mi, has_pos, tok_per_req):
    refs = list(refs)
    x_ref = refs.pop(0)
    prow_ref, pcol_ref = (refs.pop(0), refs.pop(0)) if has_pos else (None, None)
    mod_ref, npre_ref, npost_ref, wg_ref, wu_ref, wd_ref, o_ref, h_ref = refs
    j = pl.program_id(1)
    tm = x_ref.shape[0]

    def x_rows(r0):
        x = x_ref[pl.ds(r0, ROW_CHUNK), :]
        if not has_pos:
            return x
        t0 = (pl.program_id(0) * tm + r0) % tok_per_req
        prow = jnp.broadcast_to(prow_ref[t0 // GRID_W], (ROW_CHUNK, prow_ref.shape[-1]))
        return x + jnp.concatenate([prow, pcol_ref[(t0 % GRID_W) // ROW_CHUNK]], axis=1)

    @pl.when(j == 0)
    def _():
        shift = mod_ref[0, mi:mi + 1, :]
        gain = npre_ref[...] * (1.0 + mod_ref[0, mi + 1:mi + 2, :])

        def rows_body(r0):
            rows = pl.ds(r0, ROW_CHUNK)
            h_ref[rows, :] = (_rms(x_rows(r0), gain) + shift).astype(BF16)
            o_ref[rows, :] = jnp.zeros((ROW_CHUNK, o_ref.shape[1]), F32)

        _for_row_chunks(tm, rows_body)

    h = h_ref[...]
    g = _dot(h, wg_ref[...])
    u = _dot(h, wu_ref[...])
    a = (_silu(g) * u).astype(BF16)
    o_ref[...] += _dot(a, wd_ref[...])

    @pl.when(j == pl.num_programs(1) - 1)
    def _():
        gain = npost_ref[...] * (0.5 * mod_ref[0, mi + 2:mi + 3, :])
        for r0 in range(0, tm, ROW_CHUNK):
            rows = slice(r0, r0 + ROW_CHUNK)
            o_ref[rows, :] = x_rows(r0) + _rms(o_ref[rows, :], gain)


def _ffn_call(x, mod, npre, npost, wg, wu, wd, wsel, *, mi, g0, rows_per_req, pos=None, tm=1024, tf=512):
    n_rows, d = x.shape
    ff = wg.shape[-1]
    wl, wk = wsel
    const = lambda shape: pl.BlockSpec(shape, lambda i, j: (0, 0))
    in_specs = [pl.BlockSpec((tm, d), lambda i, j: (i, 0))]
    args = [x]
    if pos is not None:
        in_specs += [pl.BlockSpec(t.shape, lambda i, j: (0, 0, 0)) for t in pos]
        args += list(pos)
    in_specs += [_mod_spec(d, tm, g0, rows_per_req),
                 const((1, d)), const((1, d)),
                 pl.BlockSpec((None, None, d, tf), lambda i, j: (wl, wk, 0, j)),
                 pl.BlockSpec((None, None, d, tf), lambda i, j: (wl, wk, 0, j)),
                 pl.BlockSpec((None, None, tf, d), lambda i, j: (wl, wk, j, 0))]
    args += [mod, npre.reshape(1, d), npost.reshape(1, d), wg, wu, wd]
    return pl.pallas_call(
        functools.partial(_ffn_body, mi=mi, has_pos=pos is not None, tok_per_req=rows_per_req),
        grid=(n_rows // tm, ff // tf),
        in_specs=in_specs,
        out_specs=pl.BlockSpec((tm, d), lambda i, j: (i, 0)),
        out_shape=jax.ShapeDtypeStruct((n_rows, d), F32),
        scratch_shapes=[pltpu.VMEM((tm, d), BF16)],
        compiler_params=_cparams(("parallel", "arbitrary")),
        name="ffn",
    )(*args)


def _inproj_body(x_ref, mod_ref, npre_ref, w_ref, wt_ref, o_ref, ot_ref, h_ref, *, mi, n_main):
    j = pl.program_id(1)

    @pl.when(j == 0)
    def _():
        shift = mod_ref[0, mi:mi + 1, :]
        gain = npre_ref[...] * (1.0 + mod_ref[0, mi + 1:mi + 2, :])

        def rows_body(r0):
            rows = pl.ds(r0, ROW_CHUNK)
            h_ref[rows, :] = (_rms(x_ref[rows, :], gain) + shift).astype(BF16)

        _for_row_chunks(x_ref.shape[0], rows_body)

    @pl.when(j < n_main)
    def _():
        o_ref[...] = _dot_nt(h_ref[...], w_ref[...])

    @pl.when(j == n_main)
    def _():
        ot_ref[...] = _dot_nt(h_ref[...], wt_ref[...])


def _inproj_call(x, mod, npre, w_in_t, layer, w_tail_t, *, mi, n_main_cols, g0, rows_per_req, tm=1024, tn=1024):
    m, d = x.shape
    n_main = n_main_cols // tn
    nt = w_tail_t.shape[0]
    jm = lambda j: jnp.minimum(j, n_main - 1)
    return pl.pallas_call(
        functools.partial(_inproj_body, mi=mi, n_main=n_main),
        grid=(m // tm, n_main + 1),
        in_specs=[pl.BlockSpec((tm, d), lambda i, j: (i, 0)),
                  _mod_spec(d, tm, g0, rows_per_req),
                  pl.BlockSpec((1, d), lambda i, j: (0, 0)),
                  pl.BlockSpec((None, tn, d), lambda i, j: (layer, jm(j), 0)),
                  pl.BlockSpec((nt, d), lambda i, j: (0, 0))],
        out_specs=[pl.BlockSpec((tm, tn), lambda i, j: (i, jm(j))),
                   pl.BlockSpec((tm, nt), lambda i, j: (i, 0))],
        out_shape=[jax.ShapeDtypeStruct((m, n_main_cols), F32), jax.ShapeDtypeStruct((m, nt), F32)],
        scratch_shapes=[pltpu.VMEM((tm, d), BF16)],
        compiler_params=_cparams(("parallel", "arbitrary")),
        name="in_proj",
    )(x, mod, npre.reshape(1, d), w_in_t, w_tail_t)


def _conv3_rows(x_ref, w, r0, nr, seq_len):
    zeros = jnp.zeros((8, x_ref.shape[1]), x_ref.dtype)
    top = zeros if r0 % seq_len == 0 else x_ref[r0 - 8:r0, :]
    mid = x_ref[r0:r0 + nr, :]
    bot = zeros if (r0 + nr) % seq_len == 0 else x_ref[r0 + nr:r0 + nr + 8, :]
    return _conv3_slab(top, mid, bot, w)


def _conv3_rows_dyn(x_ref, w, r0, nr, seq_len):
    n = x_ref.shape[0]
    top = x_ref[pl.ds(pl.multiple_of(jnp.maximum(r0 - 8, 0), 8), 8), :]
    mid = x_ref[pl.ds(pl.multiple_of(r0, 8), nr), :]
    bot = x_ref[pl.ds(pl.multiple_of(jnp.minimum(r0 + nr, n - 8), 8), 8), :]
    top = jnp.where(r0 % seq_len == 0, 0.0, top)
    bot = jnp.where((r0 + nr) % seq_len == 0, 0.0, bot)
    return _conv3_slab(top, mid, bot, w)


def _conv3_slab(top, mid, bot, w):
    nr = mid.shape[0]
    slab = jnp.concatenate([top, mid, bot], axis=0)
    prev = pltpu.roll(slab, 1, 0)[8:8 + nr]
    nxt = pltpu.roll(slab, nr + 15, 0)[8:8 + nr]
    return prev * w[0:1, :] + mid * w[1:2, :] + nxt * w[2:3, :]


@functools.lru_cache(maxsize=None)
def _dft_tables_np(n_tok, r):
    big = 2 * n_tok
    k = np.arange(n_tok, dtype=np.int64)
    t = np.arange(n_tok, dtype=np.int64)
    ang = ((k[:, None] * t[None, :]) % big).astype(np.float64) * (2.0 * math.pi / big)
    cosm = np.cos(ang)
    sinm = np.sin(ang)
    sinm[0, :] = 1.0 - 2.0 * (t % 2)
    nj = n_tok // r
    fr = np.concatenate([cosm.reshape(nj, r, n_tok), sinm.reshape(nj, r, n_tok)], axis=1)
    g_full = fr.reshape(big, n_tok).T
    gr = np.concatenate([g_full[:, :n_tok].reshape(nj, r, n_tok), g_full[:, n_tok:].reshape(nj, r, n_tok)], axis=1)
    return np.concatenate([fr, gr], axis=0).astype(np.float32)


def _dft_tables(n_tok, r):
    return jnp.asarray(_dft_tables_np(n_tok, r)).astype(BF16)


def _spectra_body(featt_ref, t_ref, w1t_ref, b1_ref, w2t_ref, b2_ref, w3f_ref, w3b_ref, dec_ref, f_ref,
                  ha_ref, hb_ref, hc_ref, e_ref, o_ref, h_ref, *, n_tok, r):
    j = pl.program_id(1)

    @pl.when((pl.program_id(0) == 0) & (j == 0))
    def _():
        h = jnp.sin(HY_SIN_FREQ * (_dot(w1t_ref[...].astype(BF16), featt_ref[...].astype(BF16)) + b1_ref[...]))
        h = jnp.sin(HY_SIN_FREQ * (_dot(w2t_ref[...].astype(BF16), h.astype(BF16)) + b2_ref[...]))
        h_ref[...] = h.astype(BF16)

    @pl.when(j == 0)
    def _():
        hb16 = h_ref[...]
        window = jnp.exp(-t_ref[...] * jnp.abs(dec_ref[...])) + HY_WINDOW_SHIFT
        fwd = _dot_tn(hb16, w3f_ref[...].astype(BF16)) * window
        bwd = _dot_tn(hb16, w3b_ref[...].astype(BF16)) * window
        row = lax.broadcasted_iota(jnp.int32, bwd.shape, 0)
        bwd = jnp.where(row == 0, 0.0, bwd)
        norm = jnp.sum(jnp.abs(fwd), axis=0, keepdims=True) + jnp.sum(jnp.abs(bwd), axis=0, keepdims=True)
        fwd = fwd / norm
        bwd = bwd / norm
        e_ref[...] = (fwd + bwd).astype(BF16)
        o_ref[...] = (fwd - bwd).astype(BF16)

    fb = f_ref[0]
    e = e_ref[...]
    p = _dot(fb[:r], e)
    q = _dot(fb[r:], o_ref[...])
    k = j * r + lax.broadcasted_iota(jnp.int32, p.shape, 0)
    big = 2.0 * n_tok
    wk = jnp.where(k == 0, 1.0 / big, 2.0 / big)
    ha_ref[...] = wk * p
    hb_ref[...] = jnp.where(k == 0, 0.0, -wk * q)

    @pl.when(j == 0)
    def _():
        hc_ref[...] = _dot(fb[r:r + 8], e)[0:1, :] * (1.0 / big)


def _spectra_call(n_tok, w1, b1, w2, b2, w3, decay, fr, *, r, cn=512):
    f32 = jnp.float32
    order, c = decay.shape
    oc = order * c
    hid = w2.shape[0]
    emb = w1.shape[0]
    embp = 128
    idx = jnp.arange(n_tok, dtype=f32)
    t = idx / (n_tok - 1)
    bands = jnp.arange(1, HY_BANDS + 1, dtype=f32)
    ang = (2.0 * math.pi / n_tok) * idx[:, None] * bands[None, :]
    feats = jnp.concatenate([t[:, None], jnp.cos(ang), jnp.sin(ang)], axis=-1)
    featt = jnp.pad(feats, ((0, 0), (0, embp - emb))).T
    w1t = jnp.pad(w1, ((0, embp - emb), (0, 0))).T
    nj = n_tok // r
    ncb = oc // cn
    out = jax.ShapeDtypeStruct((n_tok, oc), F32)
    const = lambda shape: pl.BlockSpec(shape, lambda cb, j: (0, 0))
    return pl.pallas_call(
        functools.partial(_spectra_body, n_tok=n_tok, r=r),
        grid=(ncb, nj),
        in_specs=[const((embp, n_tok)), const((n_tok, 1)), const((hid, embp)), const((hid, 1)),
                  const((hid, hid)), const((hid, 1)),
                  pl.BlockSpec((hid, cn), lambda cb, j: (0, cb)),
                  pl.BlockSpec((hid, cn), lambda cb, j: (0, ncb + cb)),
                  pl.BlockSpec((1, cn), lambda cb, j: (0, cb)),
                  pl.BlockSpec((1, 2 * r, n_tok), lambda cb, j: (j, 0, 0))],
        out_specs=[pl.BlockSpec((r, cn), lambda cb, j: (j, cb))] * 2 + [pl.BlockSpec((1, cn), lambda cb, j: (0, cb))],
        out_shape=[out, out, jax.ShapeDtypeStruct((1, oc), F32)],
        scratch_shapes=[pltpu.VMEM((n_tok, cn), BF16), pltpu.VMEM((n_tok, cn), BF16),
                        pltpu.VMEM((hid, n_tok), BF16)],
        compiler_params=_cparams(("arbitrary", "arbitrary")),
        name="hyena_spectra",
    )(featt, t[:, None], w1t, b1.reshape(hid, 1), w2.T, b2.reshape(hid, 1), w3, w3, decay.reshape(1, oc), fr)


def _hyena_body(v_ref, x1_ref, x2_ref, wv_ref, wx1_ref, wx2_ref, bias_ref, t_ref, ha_ref, hb_ref, hc_ref,
                o_ref, ub_ref, u2f_ref, z_ref, *, r, nj, seq_len, n_sub):
    t = pl.program_id(2)
    order = t // (2 * nj)
    tt = t % (2 * nj)
    sub = min(128, r)

    @pl.when(t == 0)
    def _():
        w = wv_ref[...]
        for r0 in range(0, n_sub * seq_len, 64):
            ub_ref[0, r0:r0 + 64, :] = _conv3_rows(v_ref, w, r0, 64, seq_len).astype(BF16)

    @pl.when(tt < nj)
    def _():
        tb = t_ref[0]
        ha, hb = ha_ref[...], hb_ref[...]
        nyq = (tt * r + lax.broadcasted_iota(jnp.int32, ha.shape, 0)) == 0
        hc = jnp.where(nyq, hc_ref[...], ha)
        z0 = pl.multiple_of(tt * (2 * r), 2 * r)
        for s in range(n_sub):
            x = _dot(tb, ub_ref[order, s * seq_len:(s + 1) * seq_len, :])
            p = x[:r]
            q = x[r:]
            z_ref[s, pl.ds(z0, 2 * r), :] = jnp.concatenate([p * ha + q * hb, q * hc - p * hb],
                                                             axis=0).astype(BF16)

    def inverse_rows(s):
        tb = t_ref[0]
        return _dot(tb[:r], z_ref[s, :seq_len, :]) + _dot(tb[r:], z_ref[s, seq_len:, :])

    r_blk = (tt - nj) * r

    @pl.when((tt >= nj) & (order == 0))
    def _():
        wv, wx1 = wv_ref[...], wx1_ref[...]
        for s in range(n_sub):
            y = inverse_rows(s)
            for c0 in range(0, r, sub):
                r0 = s * seq_len + r_blk + c0
                u1 = _conv3_rows_dyn(v_ref, wv, r0, sub, seq_len)
                z1 = _conv3_rows_dyn(x1_ref, wx1, r0, sub, seq_len) * (y[c0:c0 + sub] + u1 * bias_ref[0:1, :])
                u2f_ref[pl.ds(pl.multiple_of(r0, 8), sub), :] = z1
                ub_ref[1, pl.ds(pl.multiple_of(r0, 8), sub), :] = z1.astype(BF16)

    @pl.when((tt >= nj) & (order == 1))
    def _():
        wx2 = wx2_ref[...]
        for s in range(n_sub):
            y = inverse_rows(s)
            for c0 in range(0, r, sub):
                r0 = s * seq_len + r_blk + c0
                u2 = u2f_ref[pl.ds(pl.multiple_of(r0, 8), sub), :]
                o_ref[pl.ds(pl.multiple_of(r0, 8), sub), :] = (
                    _conv3_rows_dyn(x2_ref, wx2, r0, sub, seq_len) * (y[c0:c0 + sub] + u2 * bias_ref[1:2, :]))


def _hyena_call(proj, conv_w, bias, tables, ha, hb, hc, *, n_seq, n_tok, c, r, n_sub, cn=512):
    nj = n_tok // r
    ncb = c // cn
    blk = n_sub * n_tok
    steps = 4 * nj
    single = pl.Buffered(1)
    seq_spec = lambda off: pl.BlockSpec((blk, cn), lambda b, cb, t: (b, off * ncb + cb), pipeline_mode=single)
    w_spec = lambda off: pl.BlockSpec((3, cn), lambda b, cb, t: (0, off * ncb + cb))
    h_spec = pl.BlockSpec((r, cn), lambda b, cb, t: (jnp.minimum(t % (2 * nj), nj - 1), (t // (2 * nj)) * ncb + cb))
    in_specs = [seq_spec(0), seq_spec(1), seq_spec(2), w_spec(0), w_spec(1), w_spec(2),
                pl.BlockSpec((2, cn), lambda b, cb, t: (0, cb)),
                pl.BlockSpec((1, 2 * r, n_tok), lambda b, cb, t: (t % (2 * nj), 0, 0)),
                h_spec, h_spec,
                pl.BlockSpec((1, cn), lambda b, cb, t: (0, (t // (2 * nj)) * ncb + cb))]
    return pl.pallas_call(
        functools.partial(_hyena_body, r=r, nj=nj, seq_len=n_tok, n_sub=n_sub),
        grid=(n_seq // n_sub, ncb, steps),
        in_specs=in_specs,
        out_specs=pl.BlockSpec((blk, cn), lambda b, cb, t: (b, cb)),
        out_shape=jax.ShapeDtypeStruct((n_seq * n_tok, c), F32),
        scratch_shapes=[pltpu.VMEM((2, blk, cn), BF16), pltpu.VMEM((blk, cn), F32),
                        pltpu.VMEM((n_sub, 2 * n_tok, cn), BF16)],
        compiler_params=_cparams(("parallel", "parallel", "arbitrary")),
        name="hyena_conv",
    )(proj, proj, proj, conv_w, conv_w, conv_w, bias, tables, ha, hb, hc)


def _gdn_prep_body(x_ref, w_ref, o_ref, *, heads_per_blk, dk, n_qk_blk, rows, seq_len):
    cb = pl.program_id(1)
    n = x_ref.shape[0]
    w = w_ref[...]
    is_v = cb >= n_qk_blk
    scale = jnp.where(cb < n_qk_blk // 2, dk ** -0.5, 1.0)
    for r0 in range(0, n, rows):
        y = _silu(_conv3_rows(x_ref, w, r0, rows, seq_len))
        for h in range(heads_per_blk):
            yh = y[:, h * dk:(h + 1) * dk]
            inv = lax.rsqrt(jnp.sum(yh * yh, axis=-1, keepdims=True) + RMS_EPS) * scale
            o_ref[r0:r0 + rows, h * dk:(h + 1) * dk] = yh * jnp.where(is_v, 1.0, inv)


def _gdn_prep_call(proj, conv_w, *, n_seq, n_tok, col0, d_gdn, dk, cn=512, blk_rows=2048):
    ncb = 3 * d_gdn // cn
    cb0 = col0 // cn
    rows = n_seq * n_tok
    blk_rows = min(blk_rows, rows)
    return pl.pallas_call(
        functools.partial(_gdn_prep_body, heads_per_blk=cn // dk, dk=dk, n_qk_blk=2 * d_gdn // cn,
                          rows=min(64, n_tok), seq_len=n_tok),
        grid=(rows // blk_rows, ncb),
        in_specs=[pl.BlockSpec((blk_rows, cn), lambda b, cb: (b, cb0 + cb)),
                  pl.BlockSpec((3, cn), lambda b, cb: (0, cb))],
        out_specs=pl.BlockSpec((blk_rows, cn), lambda b, cb: (b, cb)),
        out_shape=jax.ShapeDtypeStruct((rows, 3 * d_gdn), F32),
        compiler_params=_cparams(("parallel", "parallel")),
        name="gdn_prep",
    )(proj, conv_w)


def _split3(x):
    hi = x.astype(BF16)
    r1 = x - hi.astype(F32)
    mid = r1.astype(BF16)
    lo = (r1 - mid.astype(F32)).astype(BF16)
    return hi, mid, lo


def _gdn_body(*refs, heads, dk, chunk, has_s0):
    (qf_ref, kf_ref, vf_ref, abf_ref, abtf_ref, qb_ref, kb_ref, vb_ref, abb_ref, abtb_ref,
     av_ref, dtv_ref, avt_ref, dtvt_ref) = refs[:14]
    s0_ref = refs[14] if has_s0 else None
    of_ref, ob_ref, sfin_ref, s_ref = refs[-4:]
    n = pl.program_id(1)

    @pl.when(n == 0)
    def _():
        s_ref[...] = s0_ref[0] if has_s0 else jnp.zeros_like(s_ref)

    ci = lax.broadcasted_iota(jnp.int32, (chunk, chunk), 0)
    cj = lax.broadcasted_iota(jnp.int32, (chunk, chunk), 1)
    eye = (ci == cj).astype(F32)
    diag_blk = (ci >> 3) == (cj >> 3)
    merge_blks = [((ci >> s) ^ (cj >> s)) == 1 for s in range(3, int(math.log2(chunk)))]
    lower = cj <= ci
    upper = cj >= ci

    units = []
    for dr, (q_ref, k_ref, v_ref, ab_ref, abt_ref, o_ref) in enumerate(
            ((qf_ref, kf_ref, vf_ref, abf_ref, abtf_ref, of_ref),
             (qb_ref, kb_ref, vb_ref, abb_ref, abtb_ref, ob_ref))):
        incl = upper if dr else lower
        strict = (cj > ci) if dr else (cj < ci)
        ones_incl = jnp.where(incl, 1.0, 0.0).astype(BF16)
        ones_incl_t = jnp.where(lower if dr else upper, 1.0, 0.0).astype(BF16)
        for p in range(2):
            c0 = (1 - p if dr else p) * chunk
            rows = slice(c0, c0 + chunk)
            ab = ab_ref[rows, :]
            g_col = -av_ref[...] * jax.nn.softplus(ab + dtv_ref[...])
            beta_col = jax.nn.sigmoid(ab)
            abt = abt_ref[:, rows]
            g_row = -avt_ref[...] * jax.nn.softplus(abt + dtvt_ref[...])
            h3 = _split3(g_col)
            gc_col = _dot(ones_incl, h3[0]) + _dot(ones_incl, h3[1]) + _dot(ones_incl, h3[2])
            r3 = _split3(g_row)
            gc_row = _dot(r3[0], ones_incl_t) + _dot(r3[1], ones_incl_t) + _dot(r3[2], ones_incl_t)
            gl_all = jnp.sum(g_col, axis=0, keepdims=True)
            for h in range(heads):
                cg = dr * 2 * heads + h
                cb = cg + heads
                sl = slice(h * dk, (h + 1) * dk)
                units.append(dict(dr=dr, p=p, h=h, rows=rows, sl=sl, q_ref=q_ref, k_ref=k_ref, v_ref=v_ref,
                                  o_ref=o_ref, incl=incl, strict=strict, gc=gc_col[:, cg:cg + 1],
                                  gr=gc_row[cg:cg + 1, :], beta=beta_col[:, cb:cb + 1], gl=gl_all[:, cg:cg + 1]))

    def stage(fn):
        return [fn(u) for u in units]

    bf = lambda t: t.astype(BF16)
    ld = lambda u, name: u[name][u["rows"], u["sl"]]
    k16 = stage(lambda u: bf(ld(u, "k_ref")))
    decay = stage(lambda u: jnp.where(u["incl"], jnp.exp(jnp.where(u["incl"], u["gc"] - u["gr"], 0.0)), 0.0))
    kbeta = stage(lambda u: ld(u, "k_ref") * u["beta"])
    kq = [_dot_nt(jnp.concatenate([bf(kb), bf(ld(u, "q_ref"))], axis=0), k)
          for u, kb, k in zip(units, kbeta, k16)]
    a_low = [jnp.where(u["strict"], t[:chunk] * dc, 0.0) for u, t, dc in zip(units, kq, decay)]
    attn = [bf(t[chunk:] * dc) for t, dc in zip(kq, decay)]

    d = [jnp.where(diag_blk, a, 0.0) for a in a_low]
    d2 = [_dot(bf(t), bf(t)) for t in d]
    x = [eye - t for t in d]
    x = [t + _dot(bf(t), bf(p)) for t, p in zip(x, d2)]
    d4 = [_dot(bf(p), bf(p)) for p in d2]
    x = [t + _dot(bf(t), bf(p)) for t, p in zip(x, d4)]
    for off_blk in merge_blks:
        x16 = [bf(t) for t in x]
        tmp = [_dot(bf(jnp.where(off_blk, a, 0.0)), t16) for a, t16 in zip(a_low, x16)]
        x = [t - _dot(t16, bf(m)) for t, t16, m in zip(x, x16, tmp)]

    e_gc = stage(lambda u: jnp.exp(u["gc"]))
    rhs = [bf(jnp.concatenate([ld(u, "v_ref") * u["beta"], kb * e], axis=-1))
           for u, kb, e in zip(units, kbeta, e_gc)]
    sol = [_dot(bf(t), r) for t, r in zip(x, rhs)]

    state = {(dr, h): s_ref[dr, h] for dr in range(2) for h in range(heads)}
    for p in range(2):
        ids = [i for i, u in enumerate(units) if u["p"] == p]
        s16 = [bf(state[units[i]["dr"], units[i]["h"]]) for i in ids]
        ws = [_dot(jnp.concatenate([bf(sol[i][:, dk:]), bf(ld(units[i], "q_ref") * e_gc[i])], axis=0), s)
              for i, s in zip(ids, s16)]
        vn16 = [bf(sol[i][:, :dk] - t[:chunk]) for i, t in zip(ids, ws)]
        for i, t, vn in zip(ids, ws, vn16):
            u = units[i]
            u["o_ref"][u["rows"], u["sl"]] = t[chunk:] + _dot(attn[i], vn)
        for i, vn in zip(ids, vn16):
            u = units[i]
            state[u["dr"], u["h"]] = (state[u["dr"], u["h"]] * jnp.exp(u["gl"])
                                      + _dot_tn(bf(ld(u, "k_ref") * jnp.exp(u["gl"] - u["gc"])), vn))
    for (dr, h), t in state.items():
        s_ref[dr, h] = t

    @pl.when(n == pl.num_programs(1) - 1)
    def _():
        sfin_ref[0] = s_ref[...]


def _gdn_call(qkv, ab, abt, av, dtv, avt, dtvt, s0, *, n_seq, n_tok, heads, dk, chunk):
    blk = 2 * chunk
    nb = n_tok // blk
    d = heads * dk
    fwd = lambda b, n: b * nb + n
    bwd = lambda b, n: b * nb + nb - 1 - n
    qkv_spec = lambda idx, which: pl.BlockSpec((blk, d), lambda b, n: (idx(b, n), which))
    ab_spec = lambda idx: pl.BlockSpec((blk, 128), lambda b, n: (idx(b, n), 0))
    abt_spec = lambda idx: pl.BlockSpec((4 * heads, blk), lambda b, n: (0, idx(b, n)))
    small = lambda shape: pl.BlockSpec(shape, lambda b, n: (0, 0))
    st_spec = pl.BlockSpec((1, 2, heads, dk, dk), lambda b, n: (b, 0, 0, 0, 0))
    o_shape = jax.ShapeDtypeStruct((n_seq * n_tok, d), F32)
    in_specs = [qkv_spec(fwd, 0), qkv_spec(fwd, 1), qkv_spec(fwd, 2), ab_spec(fwd), abt_spec(fwd),
                qkv_spec(bwd, 0), qkv_spec(bwd, 1), qkv_spec(bwd, 2), ab_spec(bwd), abt_spec(bwd),
                small((1, 128)), small((1, 128)), small((4 * heads, 1)), small((4 * heads, 1))]
    args = [qkv, qkv, qkv, ab, abt, qkv, qkv, qkv, ab, abt, av, dtv, avt, dtvt]
    if s0 is not None:
        in_specs.append(st_spec)
        args.append(s0)
    return pl.pallas_call(
        functools.partial(_gdn_body, heads=heads, dk=dk, chunk=chunk, has_s0=s0 is not None),
        grid=(n_seq, nb),
        in_specs=in_specs,
        out_specs=[pl.BlockSpec((blk, d), lambda b, n: (fwd(b, n), 0)),
                   pl.BlockSpec((blk, d), lambda b, n: (bwd(b, n), 0)),
                   st_spec],
        out_shape=[o_shape, o_shape, jax.ShapeDtypeStruct((n_seq, 2, heads, dk, dk), F32)],
        scratch_shapes=[pltpu.VMEM((2, heads, dk, dk), F32)],
        compiler_params=_cparams(("parallel", "arbitrary")),
        name="gdn_scan",
    )(*args)


def _outproj_body(x_ref, mod_ref, zhy_ref, of_ref, ob_ref, zg_ref, ghy_ref, go_ref, npost_ref, w_ref,
                  o_ref, *, mi, heads, dk):
    yhy = _rms(zhy_ref[...], ghy_ref[...]).astype(BF16)
    o = of_ref[...] + ob_ref[...]
    zg = zg_ref[...]
    parts = [yhy]
    for h in range(heads):
        sl = slice(h * dk, (h + 1) * dk)
        parts.append((_rms(o[:, sl], go_ref[...]) * _silu(zg[:, sl])).astype(BF16))
    y = _dot(jnp.concatenate(parts, axis=-1), w_ref[...])
    gate = mod_ref[0, mi:mi + 1, :]
    o_ref[...] = x_ref[...] + gate * _rms(y, npost_ref[...])


def _outproj_call(x, mod, zhy, o_f, o_b, proj, ghy, go, npost, w, *, mi, zg_col_blk, g0, rows_per_req, heads, dk,
                  tm=512):
    m, d = x.shape
    c = zhy.shape[1]
    dg = heads * dk
    row = lambda width: pl.BlockSpec((tm, width), lambda i: (i, 0))
    return pl.pallas_call(
        functools.partial(_outproj_body, mi=mi, heads=heads, dk=dk),
        grid=(m // tm,),
        in_specs=[row(d),
                  _mod_spec(d, tm, g0, rows_per_req),
                  row(c), row(dg), row(dg),
                  pl.BlockSpec((tm, dg), lambda i: (i, zg_col_blk)),
                  pl.BlockSpec((1, c), lambda i: (0, 0)),
                  pl.BlockSpec((1, dk), lambda i: (0, 0)),
                  pl.BlockSpec((1, d), lambda i: (0, 0)),
                  pl.BlockSpec((c + dg, d), lambda i: (0, 0))],
        out_specs=row(d),
        out_shape=jax.ShapeDtypeStruct((m, d), F32),
        compiler_params=_cparams(("parallel",)),
        name="out_proj",
    )(x, mod, zhy, o_f, o_b, proj, ghy.reshape(1, c), go.reshape(1, dk), npost.reshape(1, d), w)


@functools.lru_cache(maxsize=None)
def _grid_pos_tables_np(n_tok, dim):
    rows = n_tok // GRID_W
    quarter = dim // 4
    omega = 1.0 / (POS_BASE ** (np.arange(quarter, dtype=np.float64) / quarter))
    ar = np.arange(rows, dtype=np.float64)[:, None] * omega[None]
    ac = np.arange(GRID_W, dtype=np.float64)[:, None] * omega[None]
    row_t = np.concatenate([np.sin(ar), np.cos(ar)], axis=-1).astype(np.float32)
    col_t = np.concatenate([np.sin(ac), np.cos(ac)], axis=-1).astype(np.float32)
    return row_t.reshape(rows, 1, dim // 2), col_t.reshape(GRID_W // ROW_CHUNK, ROW_CHUNK, dim // 2)


def _mixer(proj, proj_ab, s0, *, n_seq, n_tok, hy_conv_w, hy_f, hy_decay, hy_bias, gdn_conv_w, gdn_a_log,
           gdn_dt_bias, c_hy, d_gdn):
    heads = GDN_HEADS
    dk = d_gdn // heads
    o0 = 3 * c_hy

    a_exp = jnp.exp(gdn_a_log.astype(F32))
    zeros = jnp.zeros_like(a_exp)
    av32 = jnp.stack([a_exp, zeros], axis=1).reshape(-1)
    dtv32 = jnp.stack([gdn_dt_bias.astype(F32), zeros], axis=1).reshape(-1)
    pad = 128 - 4 * heads
    av = jnp.pad(av32, (0, pad)).reshape(1, 128)
    dtv = jnp.pad(dtv32, (0, pad)).reshape(1, 128)
    avt = av32.reshape(4 * heads, 1)
    dtvt = dtv32.reshape(4 * heads, 1)
    abt = jnp.transpose(proj_ab[:, :4 * heads])

    r = min(512, n_tok)
    tables = _dft_tables(n_tok, r)
    ha, hb, hc = _spectra_call(n_tok, *hy_f, hy_decay, tables, r=r)
    zhy = _hyena_call(proj, hy_conv_w, hy_bias, tables, ha, hb, hc, n_seq=n_seq, n_tok=n_tok, c=c_hy, r=r,
                      n_sub=max(1, min(n_seq, 1024 // n_tok)))
    qkv = _gdn_prep_call(proj, gdn_conv_w, n_seq=n_seq, n_tok=n_tok, col0=o0, d_gdn=d_gdn, dk=dk)
    o_f, o_b, s_fin = _gdn_call(qkv, proj_ab, abt, av, dtv, avt, dtvt, s0, n_seq=n_seq, n_tok=n_tok, heads=heads,
                                dk=dk, chunk=GDN_CHUNK)
    return zhy, o_f, o_b, s_fin


def kernel(x_prompt, x_sample, state_gdn, c, c_ctx, ada_w, ada_b, norm_pre, norm_post, ffn_wg, ffn_wu, ffn_wd,
           w_in, w_out, hy_conv_w, hy_f_w1, hy_f_b1, hy_f_w2, hy_f_b2, hy_f_w3, hy_decay, hy_bias, hy_out_norm,
           gdn_conv_w, gdn_a_log, gdn_dt_bias, gdn_o_norm):
    nb, seq, d = x_prompt.shape
    db, dseq, _ = x_sample.shape
    depth = ada_w.shape[0]
    c_hy = hy_decay.shape[-1]
    d_gdn = gdn_conv_w.shape[-1] // 3
    heads = GDN_HEADS
    dk = d_gdn // heads
    n_ctx_rows = nb * seq
    n_lat_rows = db * dseq
    pos = tuple(jnp.asarray(t).astype(x_sample.dtype) for t in _grid_pos_tables_np(dseq, d))
    cvec = jnp.concatenate([c_ctx[None], c, jnp.zeros((8 - 1 - db, d), F32)], axis=0)
    wg, wu, wd = ffn_wg.astype(BF16), ffn_wu.astype(BF16), ffn_wd.astype(BF16)
    n_main_cols = 3 * c_hy + 4 * d_gdn
    n_tail = w_in.shape[-1] - n_main_cols
    w_in_t = jnp.swapaxes(w_in, 1, 2).astype(BF16)

    xs = [x_prompt.reshape(n_ctx_rows, d), x_sample.reshape(n_lat_rows, d)]
    groups = [dict(g0=0, rows_per_req=n_ctx_rows), dict(g0=1, rows_per_req=dseq)]
    shapes = [(nb, seq), (db, dseq)]
    ctx_states = []
    for l in range(depth):
        mod = _mod_call(cvec, ada_w[l], ada_b[l]).reshape(8, N_MOD, d)
        w_tail_t = jnp.pad(w_in_t[l, n_main_cols:], ((0, 128 - n_tail), (0, 0)))
        w_out_l = w_out[l].astype(BF16)
        hy_f = (hy_f_w1[l], hy_f_b1[l], hy_f_w2[l], hy_f_b2[l], hy_f_w3[l])
        states = [None, state_gdn[:, l].astype(F32)]
        for gi, (grp, (n_seq, n_tok)) in enumerate(zip(groups, shapes)):
            x = _ffn_call(xs[gi], mod, norm_pre[l, 0], norm_post[l, 0], wg, wu, wd, (l, 0), mi=0,
                          pos=pos if (gi == 1 and l == 0) else None, **grp)
            proj, proj_ab = _inproj_call(x, mod, norm_pre[l, 1], w_in_t, l, w_tail_t, mi=3,
                                         n_main_cols=n_main_cols, **grp)
            zhy, o_f, o_b, s_fin = _mixer(
                proj, proj_ab, states[gi], n_seq=n_seq, n_tok=n_tok, hy_conv_w=hy_conv_w[l], hy_f=hy_f,
                hy_decay=hy_decay[l], hy_bias=hy_bias[l], gdn_conv_w=gdn_conv_w[l], gdn_a_log=gdn_a_log[l],
                gdn_dt_bias=gdn_dt_bias[l], c_hy=c_hy, d_gdn=d_gdn)
            if gi == 0:
                ctx_states.append(s_fin)
            x = _outproj_call(x, mod, zhy, o_f, o_b, proj, hy_out_norm[l], gdn_o_norm[l], norm_post[l, 1], w_out_l,
                              mi=5, zg_col_blk=(3 * c_hy + 3 * d_gdn) // d_gdn, heads=heads, dk=dk, **grp)
            xs[gi] = _ffn_call(x, mod, norm_pre[l, 2], norm_post[l, 2], wg, wu, wd, (l, 1), mi=6, **grp)

    new_state = ctx_states[0][:, None] if depth == 1 else jnp.stack(ctx_states, axis=1)
    return xs[0].reshape(nb, seq, d), xs[1].reshape(db, dseq, d), new_state
```

```python
import functools
import math

import numpy as np
import jax
import jax.numpy as jnp
from jax import lax
from jax.experimental import pallas as pl
from jax.experimental.pallas import tpu as pltpu

F32 = jnp.float32
BF16 = jnp.bfloat16

N_MOD = 9
RMS_EPS = 1e-6
GRID_W = 64
POS_BASE = 10000.0
HY_BANDS = 16
HY_SIN_FREQ = 1.0
HY_WINDOW_SHIFT = 0.05
GDN_HEADS = 8
GDN_CHUNK = 128
VMEM_LIMIT = 56 * 1024 * 1024
ROW_CHUNK = 16


def _cparams(sem):
    return pltpu.CompilerParams(dimension_semantics=sem, vmem_limit_bytes=VMEM_LIMIT)


def _dot(a, b):
    return jnp.dot(a, b, preferred_element_type=F32)


def _dot_nt(a, b):
    return lax.dot_general(a, b, (((1,), (1,)), ((), ())), preferred_element_type=F32)


def _dot_tn(a, b):
    return lax.dot_general(a, b, (((0,), (0,)), ((), ())), preferred_element_type=F32)


def _rms(x, g):
    ms = jnp.mean(x * x, axis=-1, keepdims=True)
    return x * lax.rsqrt(ms + RMS_EPS) * g


def _silu(x):
    return x * jax.nn.sigmoid(x)


def _for_row_chunks(n_rows, fn):
    def body(c, carry):
        fn(pl.multiple_of(c * ROW_CHUNK, ROW_CHUNK))
        return carry

    lax.fori_loop(0, n_rows // ROW_CHUNK, body, 0, unroll=8)


def _mod_spec(d, tm, g0, rows_per_req):
    tpr = rows_per_req // tm
    return pl.BlockSpec((1, N_MOD, d), lambda i, *_: (g0 + i // tpr, 0, 0))


def _mod_body(c_ref, w_ref, b_ref, o_ref):
    s = _silu(c_ref[...]).astype(BF16)
    o_ref[...] = _dot(s, w_ref[...].astype(BF16)) + b_ref[...]


def _mod_call(cvec, ada_w, ada_b):
    g, d = cvec.shape
    n = ada_w.shape[1]
    tn = 1024
    return pl.pallas_call(
        _mod_body,
        grid=(n // tn,),
        in_specs=[pl.BlockSpec((g, d), lambda j: (0, 0)),
                  pl.BlockSpec((d, tn), lambda j: (0, j)),
                  pl.BlockSpec((1, tn), lambda j: (0, j))],
        out_specs=pl.BlockSpec((g, tn), lambda j: (0, j)),
        out_shape=jax.ShapeDtypeStruct((g, n), F32),
        compiler_params=_cparams(("arbitrary",)),
        name="mod_table",
    )(cvec, ada_w, ada_b.reshape(1, n))


def _ffn_body(*refs, mi, has_pos, tok_per_req):
    refs = list(refs)
    x_ref = refs.pop(0)
    prow_ref, pcol_ref = (refs.pop(0), refs.pop(0)) if has_pos else (None, None)
    mod_ref, npre_ref, npost_ref, wg_ref, wu_ref, wd_ref, o_ref, h_ref = refs
    j = pl.program_id(1)
    tm = x_ref.shape[0]

    def x_rows(r0):
        x = x_ref[pl.ds(r0, ROW_CHUNK), :]
        if not has_pos:
            return x
        t0 = (pl.program_id(0) * tm + r0) % tok_per_req
        prow = jnp.broadcast_to(prow_ref[t0 // GRID_W], (ROW_CHUNK, prow_ref.shape[-1]))
        return x + jnp.concatenate([prow, pcol_ref[(t0 % GRID_W) // ROW_CHUNK]], axis=1)

    @pl.when(j == 0)
    def _():
        shift = mod_ref[0, mi:mi + 1, :]
        gain = npre_ref[...] * (1.0 + mod_ref[0, mi + 1:mi + 2, :])

        def rows_body(r0):
            rows = pl.ds(r0, ROW_CHUNK)
            h_ref[rows, :] = (_rms(x_rows(r0), gain) + shift).astype(BF16)
            o_ref[rows, :] = jnp.zeros((ROW_CHUNK, o_ref.shape[1]), F32)

        _for_row_chunks(tm, rows_body)

    h = h_ref[...]
    g = _dot(h, wg_ref[...])
    u = _dot(h, wu_ref[...])
    a = (_silu(g) * u).astype(BF16)
    o_ref[...] += _dot(a, wd_ref[...])

    @pl.when(j == pl.num_programs(1) - 1)
    def _():
        gain = npost_ref[...] * (0.5 * mod_ref[0, mi + 2:mi + 3, :])
        for r0 in range(0, tm, ROW_CHUNK):
            rows = slice(r0, r0 + ROW_CHUNK)
            o_ref[rows, :] = x_rows(r0) + _rms(o_ref[rows, :], gain)


def _ffn_call(x, mod, npre, npost, wg, wu, wd, wsel, *, mi, g0, rows_per_req, pos=None, tm=1024, tf=512):
    n_rows, d = x.shape
    ff = wg.shape[-1]
    wl, wk = wsel
    const = lambda shape: pl.BlockSpec(shape, lambda i, j: (0, 0))
    in_specs = [pl.BlockSpec((tm, d), lambda i, j: (i, 0))]
    args = [x]
    if pos is not None:
        in_specs += [pl.BlockSpec(t.shape, lambda i, j: (0, 0, 0)) for t in pos]
        args += list(pos)
    in_specs += [_mod_spec(d, tm, g0, rows_per_req),
                 const((1, d)), const((1, d)),
                 pl.BlockSpec((None, None, d, tf), lambda i, j: (wl, wk, 0, j)),
                 pl.BlockSpec((None, None, d, tf), lambda i, j: (wl, wk, 0, j)),
                 pl.BlockSpec((None, None, tf, d), lambda i, j: (wl, wk, j, 0))]
    args += [mod, npre.reshape(1, d), npost.reshape(1, d), wg, wu, wd]
    return pl.pallas_call(
        functools.partial(_ffn_body, mi=mi, has_pos=pos is not None, tok_per_req=rows_per_req),
        grid=(n_rows // tm, ff // tf),
        in_specs=in_specs,
        out_specs=pl.BlockSpec((tm, d), lambda i, j: (i, 0)),
        out_shape=jax.ShapeDtypeStruct((n_rows, d), F32),
        scratch_shapes=[pltpu.VMEM((tm, d), BF16)],
        compiler_params=_cparams(("parallel", "arbitrary")),
        name="ffn",
    )(*args)


def _inproj_body(x_ref, mod_ref, npre_ref, w_ref, wt_ref, o_ref, ot_ref, h_ref, *, mi, n_main):
    j = pl.program_id(1)

    @pl.when(j == 0)
    def _():
        shift = mod_ref[0, mi:mi + 1, :]
        gain = npre_ref[...] * (1.0 + mod_ref[0, mi + 1:mi + 2, :])

        def rows_body(r0):
            rows = pl.ds(r0, ROW_CHUNK)
            h_ref[rows, :] = (_rms(x_ref[rows, :], gain) + shift).astype(BF16)

        _for_row_chunks(x_ref.shape[0], rows_body)

    @pl.when(j < n_main)
    def _():
        o_ref[...] = _dot_nt(h_ref[...], w_ref[...])

    @pl.when(j == n_main)
    def _():
        ot_ref[...] = _dot_nt(h_ref[...], wt_ref[...])


def _inproj_call(x, mod, npre, w_in_t, layer, w_tail_t, *, mi, n_main_cols, g0, rows_per_req, tm=1024, tn=1024):
    m, d = x.shape
    n_main = n_main_cols // tn
    nt = w_tail_t.shape[0]
    jm = lambda j: jnp.minimum(j, n_main - 1)
    return pl.pallas_call(
        functools.partial(_inproj_body, mi=mi, n_main=n_main),
        grid=(m // tm, n_main + 1),
        in_specs=[pl.BlockSpec((tm, d), lambda i, j: (i, 0)),
                  _mod_spec(d, tm, g0, rows_per_req),
                  pl.BlockSpec((1, d), lambda i, j: (0, 0)),
                  pl.BlockSpec((None, tn, d), lambda i, j: (layer, jm(j), 0)),
                  pl.BlockSpec((nt, d), lambda i, j: (0, 0))],
        out_specs=[pl.BlockSpec((tm, tn), lambda i, j: (i, jm(j))),
                   pl.BlockSpec((tm, nt), lambda i, j: (i, 0))],
        out_shape=[jax.ShapeDtypeStruct((m, n_main_cols), F32), jax.ShapeDtypeStruct((m, nt), F32)],
        scratch_shapes=[pltpu.VMEM((tm, d), BF16)],
        compiler_params=_cparams(("parallel", "arbitrary")),
        name="in_proj",
    )(x, mod, npre.reshape(1, d), w_in_t, w_tail_t)


def _conv3_rows(x_ref, w, r0, nr, seq_len):
    zeros = jnp.zeros((8, x_ref.shape[1]), x_ref.dtype)
    top = zeros if r0 % seq_len == 0 else x_ref[r0 - 8:r0, :]
    mid = x_ref[r0:r0 + nr, :]
    bot = zeros if (r0 + nr) % seq_len == 0 else x_ref[r0 + nr:r0 + nr + 8, :]
    return _conv3_slab(top, mid, bot, w)


def _conv3_rows_dyn(x_ref, w, r0, nr, seq_len):
    n = x_ref.shape[0]
    top = x_ref[pl.ds(pl.multiple_of(jnp.maximum(r0 - 8, 0), 8), 8), :]
    mid = x_ref[pl.ds(pl.multiple_of(r0, 8), nr), :]
    bot = x_ref[pl.ds(pl.multiple_of(jnp.minimum(r0 + nr, n - 8), 8), 8), :]
    top = jnp.where(r0 % seq_len == 0, 0.0, top)
    bot = jnp.where((r0 + nr) % seq_len == 0, 0.0, bot)
    return _conv3_slab(top, mid, bot, w)


def _conv3_slab(top, mid, bot, w):
    nr = mid.shape[0]
    slab = jnp.concatenate([top, mid, bot], axis=0)
    prev = pltpu.roll(slab, 1, 0)[8:8 + nr]
    nxt = pltpu.roll(slab, nr + 15, 0)[8:8 + nr]
    return prev * w[0:1, :] + mid * w[1:2, :] + nxt * w[2:3, :]


@functools.lru_cache(maxsize=None)
def _dft_tables_np(n_tok, r):
    big = 2 * n_tok
    k = np.arange(n_tok, dtype=np.int64)
    t = np.arange(n_tok, dtype=np.int64)
    ang = ((k[:, None] * t[None, :]) % big).astype(np.float64) * (2.0 * math.pi / big)
    cosm = np.cos(ang)
    sinm = np.sin(ang)
    sinm[0, :] = 1.0 - 2.0 * (t % 2)
    nj = n_tok // r
    fr = np.concatenate([cosm.reshape(nj, r, n_tok), sinm.reshape(nj, r, n_tok)], axis=1)
    g_full = fr.reshape(big, n_tok).T
    gr = np.concatenate([g_full[:, :n_tok].reshape(nj, r, n_tok), g_full[:, n_tok:].reshape(nj, r, n_tok)], axis=1)
    return np.concatenate([fr, gr], axis=0).astype(np.float32)


def _dft_tables(n_tok, r):
    return jnp.asarray(_dft_tables_np(n_tok, r)).astype(BF16)


def _spectra_body(featt_ref, t_ref, w1t_ref, b1_ref, w2t_ref, b2_ref, w3f_ref, w3b_ref, dec_ref, f_ref,
                  ha_ref, hb_ref, hc_ref, e_ref, o_ref, h_ref, *, n_tok, r):
    j = pl.program_id(1)

    @pl.when((pl.program_id(0) == 0) & (j == 0))
    def _():
        h = jnp.sin(HY_SIN_FREQ * (_dot(w1t_ref[...].astype(BF16), featt_ref[...].astype(BF16)) + b1_ref[...]))
        h = jnp.sin(HY_SIN_FREQ * (_dot(w2t_ref[...].astype(BF16), h.astype(BF16)) + b2_ref[...]))
        h_ref[...] = h.astype(BF16)

    @pl.when(j == 0)
    def _():
        hb16 = h_ref[...]
        window = jnp.exp(-t_ref[...] * jnp.abs(dec_ref[...])) + HY_WINDOW_SHIFT
        fwd = _dot_tn(hb16, w3f_ref[...].astype(BF16)) * window
        bwd = _dot_tn(hb16, w3b_ref[...].astype(BF16)) * window
        row = lax.broadcasted_iota(jnp.int32, bwd.shape, 0)
        bwd = jnp.where(row == 0, 0.0, bwd)
        norm = jnp.sum(jnp.abs(fwd), axis=0, keepdims=True) + jnp.sum(jnp.abs(bwd), axis=0, keepdims=True)
        fwd = fwd / norm
        bwd = bwd / norm
        e_ref[...] = (fwd + bwd).astype(BF16)
        o_ref[...] = (fwd - bwd).astype(BF16)

    fb = f_ref[0]
    e = e_ref[...]
    p = _dot(fb[:r], e)
    q = _dot(fb[r:], o_ref[...])
    k = j * r + lax.broadcasted_iota(jnp.int32, p.shape, 0)
    big = 2.0 * n_tok
    wk = jnp.where(k == 0, 1.0 / big, 2.0 / big)
    ha_ref[...] = wk * p
    hb_ref[...] = jnp.where(k == 0, 0.0, -wk * q)

    @pl.when(j == 0)
    def _():
        hc_ref[...] = _dot(fb[r:r + 8], e)[0:1, :] * (1.0 / big)


def _spectra_call(n_tok, w1, b1, w2, b2, w3, decay, fr, *, r, cn=512):
    f32 = jnp.float32
    order, c = decay.shape
    oc = order * c
    hid = w2.shape[0]
    emb = w1.shape[0]
    embp = 128
    idx = jnp.arange(n_tok, dtype=f32)
    t = idx / (n_tok - 1)
    bands = jnp.arange(1, HY_BANDS + 1, dtype=f32)
    ang = (2.0 * math.pi / n_tok) * idx[:, None] * bands[None, :]
    feats = jnp.concatenate([t[:, None], jnp.cos(ang), jnp.sin(ang)], axis=-1)
    featt = jnp.pad(feats, ((0, 0), (0, embp - emb))).T
    w1t = jnp.pad(w1, ((0, embp - emb), (0, 0))).T
    nj = n_tok // r
    ncb = oc // cn
    out = jax.ShapeDtypeStruct((n_tok, oc), F32)
    const = lambda shape: pl.BlockSpec(shape, lambda cb, j: (0, 0))
    return pl.pallas_call(
        functools.partial(_spectra_body, n_tok=n_tok, r=r),
        grid=(ncb, nj),
        in_specs=[const((embp, n_tok)), const((n_tok, 1)), const((hid, embp)), const((hid, 1)),
                  const((hid, hid)), const((hid, 1)),
                  pl.BlockSpec((hid, cn), lambda cb, j: (0, cb)),
                  pl.BlockSpec((hid, cn), lambda cb, j: (0, ncb + cb)),
                  pl.BlockSpec((1, cn), lambda cb, j: (0, cb)),
                  pl.BlockSpec((1, 2 * r, n_tok), lambda cb, j: (j, 0, 0))],
        out_specs=[pl.BlockSpec((r, cn), lambda cb, j: (j, cb))] * 2 + [pl.BlockSpec((1, cn), lambda cb, j: (0, cb))],
        out_shape=[out, out, jax.ShapeDtypeStruct((1, oc), F32)],
        scratch_shapes=[pltpu.VMEM((n_tok, cn), BF16), pltpu.VMEM((n_tok, cn), BF16),
                        pltpu.VMEM((hid, n_tok), BF16)],
        compiler_params=_cparams(("arbitrary", "arbitrary")),
        name="hyena_spectra",
    )(featt, t[:, None], w1t, b1.reshape(hid, 1), w2.T, b2.reshape(hid, 1), w3, w3, decay.reshape(1, oc), fr)


def _hyena_body(v_ref, x1_ref, x2_ref, wv_ref, wx1_ref, wx2_ref, bias_ref, t_ref, ha_ref, hb_ref, hc_ref,
                o_ref, ub_ref, u2f_ref, z_ref, *, r, nj, seq_len, n_sub):
    t = pl.program_id(2)
    order = t // (2 * nj)
    tt = t % (2 * nj)
    sub = min(128, r)

    @pl.when(t == 0)
    def _():
        w = wv_ref[...]
        for r0 in range(0, n_sub * seq_len, 64):
            ub_ref[0, r0:r0 + 64, :] = _conv3_rows(v_ref, w, r0, 64, seq_len).astype(BF16)

    @pl.when(tt < nj)
    def _():
        tb = t_ref[0]
        ha, hb = ha_ref[...], hb_ref[...]
        nyq = (tt * r + lax.broadcasted_iota(jnp.int32, ha.shape, 0)) == 0
        hc = jnp.where(nyq, hc_ref[...], ha)
        z0 = pl.multiple_of(tt * (2 * r), 2 * r)
        for s in range(n_sub):
            x = _dot(tb, ub_ref[order, s * seq_len:(s + 1) * seq_len, :])
            p = x[:r]
            q = x[r:]
            z_ref[s, pl.ds(z0, 2 * r), :] = jnp.concatenate([p * ha + q * hb, q * hc - p * hb],
                                                             axis=0).astype(BF16)

    def inverse_rows(s):
        tb = t_ref[0]
        return _dot(tb[:r], z_ref[s, :seq_len, :]) + _dot(tb[r:], z_ref[s, seq_len:, :])

    r_blk = (tt - nj) * r

    @pl.when((tt >= nj) & (order == 0))
    def _():
        wv, wx1 = wv_ref[...], wx1_ref[...]
        for s in range(n_sub):
            y = inverse_rows(s)
            for c0 in range(0, r, sub):
                r0 = s * seq_len + r_blk + c0
                u1 = _conv3_rows_dyn(v_ref, wv, r0, sub, seq_len)
                z1 = _conv3_rows_dyn(x1_ref, wx1, r0, sub, seq_len) * (y[c0:c0 + sub] + u1 * bias_ref[0:1, :])
                u2f_ref[pl.ds(pl.multiple_of(r0, 8), sub), :] = z1
                ub_ref[1, pl.ds(pl.multiple_of(r0, 8), sub), :] = z1.astype(BF16)

    @pl.when((tt >= nj) & (order == 1))
    def _():
        wx2 = wx2_ref[...]
        for s in range(n_sub):
            y = inverse_rows(s)
            for c0 in range(0, r, sub):
                r0 = s * seq_len + r_blk + c0
                u2 = u2f_ref[pl.ds(pl.multiple_of(r0, 8), sub), :]
                o_ref[pl.ds(pl.multiple_of(r0, 8), sub), :] = (
                    _conv3_rows_dyn(x2_ref, wx2, r0, sub, seq_len) * (y[c0:c0 + sub] + u2 * bias_ref[1:2, :]))


def _hyena_call(proj, conv_w, bias, tables, ha, hb, hc, *, n_seq, n_tok, c, r, n_sub, cn=512):
    nj = n_tok // r
    ncb = c // cn
    blk = n_sub * n_tok
    steps = 4 * nj
    single = pl.Buffered(1)
    seq_spec = lambda off: pl.BlockSpec((blk, cn), lambda b, cb, t: (b, off * ncb + cb), pipeline_mode=single)
    w_spec = lambda off: pl.BlockSpec((3, cn), lambda b, cb, t: (0, off * ncb + cb))
    h_spec = pl.BlockSpec((r, cn), lambda b, cb, t: (jnp.minimum(t % (2 * nj), nj - 1), (t // (2 * nj)) * ncb + cb))
    in_specs = [seq_spec(0), seq_spec(1), seq_spec(2), w_spec(0), w_spec(1), w_spec(2),
                pl.BlockSpec((2, cn), lambda b, cb, t: (0, cb)),
                pl.BlockSpec((1, 2 * r, n_tok), lambda b, cb, t: (t % (2 * nj), 0, 0)),
                h_spec, h_spec,
                pl.BlockSpec((1, cn), lambda b, cb, t: (0, (t // (2 * nj)) * ncb + cb))]
    return pl.pallas_call(
        functools.partial(_hyena_body, r=r, nj=nj, seq_len=n_tok, n_sub=n_sub),
        grid=(n_seq // n_sub, ncb, steps),
        in_specs=in_specs,
        out_specs=pl.BlockSpec((blk, cn), lambda b, cb, t: (b, cb)),
        out_shape=jax.ShapeDtypeStruct((n_seq * n_tok, c), F32),
        scratch_shapes=[pltpu.VMEM((2, blk, cn), BF16), pltpu.VMEM((blk, cn), F32),
                        pltpu.VMEM((n_sub, 2 * n_tok, cn), BF16)],
        compiler_params=_cparams(("parallel", "parallel", "arbitrary")),
        name="hyena_conv",
    )(proj, proj, proj, conv_w, conv_w, conv_w, bias, tables, ha, hb, hc)


def _gdn_prep_body(x_ref, w_ref, o_ref, *, heads_per_blk, dk, n_qk_blk, rows, seq_len):
    cb = pl.program_id(1)
    n = x_ref.shape[0]
    w = w_ref[...]
    is_v = cb >= n_qk_blk
    scale = jnp.where(cb < n_qk_blk // 2, dk ** -0.5, 1.0)
    for r0 in range(0, n, rows):
        y = _silu(_conv3_rows(x_ref, w, r0, rows, seq_len))
        for h in range(heads_per_blk):
            yh = y[:, h * dk:(h + 1) * dk]
            inv = lax.rsqrt(jnp.sum(yh * yh, axis=-1, keepdims=True) + RMS_EPS) * scale
            o_ref[r0:r0 + rows, h * dk:(h + 1) * dk] = yh * jnp.where(is_v, 1.0, inv)


def _gdn_prep_call(proj, conv_w, *, n_seq, n_tok, col0, d_gdn, dk, cn=512, blk_rows=2048):
    ncb = 3 * d_gdn // cn
    cb0 = col0 // cn
    rows = n_seq * n_tok
    blk_rows = min(blk_rows, rows)
    return pl.pallas_call(
        functools.partial(_gdn_prep_body, heads_per_blk=cn // dk, dk=dk, n_qk_blk=2 * d_gdn // cn,
                          rows=min(64, n_tok), seq_len=n_tok),
        grid=(rows // blk_rows, ncb),
        in_specs=[pl.BlockSpec((blk_rows, cn), lambda b, cb: (b, cb0 + cb)),
                  pl.BlockSpec((3, cn), lambda b, cb: (0, cb))],
        out_specs=pl.BlockSpec((blk_rows, cn), lambda b, cb: (b, cb)),
        out_shape=jax.ShapeDtypeStruct((rows, 3 * d_gdn), F32),
        compiler_params=_cparams(("parallel", "parallel")),
        name="gdn_prep",
    )(proj, conv_w)


def _split3(x):
    hi = x.astype(BF16)
    r1 = x - hi.astype(F32)
    mid = r1.astype(BF16)
    lo = (r1 - mid.astype(F32)).astype(BF16)
    return hi, mid, lo


def _gdn_body(*refs, heads, dk, chunk, has_s0):
    (qf_ref, kf_ref, vf_ref, abf_ref, abtf_ref, qb_ref, kb_ref, vb_ref, abb_ref, abtb_ref,
     av_ref, dtv_ref, avt_ref, dtvt_ref) = refs[:14]
    s0_ref = refs[14] if has_s0 else None
    of_ref, ob_ref, sfin_ref, s_ref = refs[-4:]
    n = pl.program_id(1)

    @pl.when(n == 0)
    def _():
        s_ref[...] = s0_ref[0] if has_s0 else jnp.zeros_like(s_ref)

    ci = lax.broadcasted_iota(jnp.int32, (chunk, chunk), 0)
    cj = lax.broadcasted_iota(jnp.int32, (chunk, chunk), 1)
    eye = (ci == cj).astype(F32)
    diag_blk = (ci >> 3) == (cj >> 3)
    merge_blks = [((ci >> s) ^ (cj >> s)) == 1 for s in range(3, int(math.log2(chunk)))]
    lower = cj <= ci
    upper = cj >= ci

    units = []
    for dr, (q_ref, k_ref, v_ref, ab_ref, abt_ref, o_ref) in enumerate(
            ((qf_ref, kf_ref, vf_ref, abf_ref, abtf_ref, of_ref),
             (qb_ref, kb_ref, vb_ref, abb_ref, abtb_ref, ob_ref))):
        incl = upper if dr else lower
        strict = (cj > ci) if dr else (cj < ci)
        ones_incl = jnp.where(incl, 1.0, 0.0).astype(BF16)
        ones_incl_t = jnp.where(lower if dr else upper, 1.0, 0.0).astype(BF16)
        ab = ab_ref[...]
        g_col = -av_ref[...] * jax.nn.softplus(ab + dtv_ref[...])
        beta_col = jax.nn.sigmoid(ab)
        abt = abt_ref[...]
        g_row = -avt_ref[...] * jax.nn.softplus(abt + dtvt_ref[...])
        h3 = _split3(g_col)
        gc_col = _dot(ones_incl, h3[0]) + _dot(ones_incl, h3[1]) + _dot(ones_incl, h3[2])
        r3 = _split3(g_row)
        gc_row = _dot(r3[0], ones_incl_t) + _dot(r3[1], ones_incl_t) + _dot(r3[2], ones_incl_t)
        gl_all = jnp.sum(g_col, axis=0, keepdims=True)
        for h in range(heads):
            cg = dr * 2 * heads + h
            cb = cg + heads
            sl = slice(h * dk, (h + 1) * dk)
            units.append(dict(dr=dr, h=h, sl=sl, q_ref=q_ref, k_ref=k_ref, v_ref=v_ref, o_ref=o_ref,
                              incl=incl, strict=strict, gc=gc_col[:, cg:cg + 1], gr=gc_row[cg:cg + 1, :],
                              beta=beta_col[:, cb:cb + 1], gl=gl_all[:, cg:cg + 1]))

    def stage(fn):
        return [fn(u) for u in units]

    bf = lambda t: t.astype(BF16)
    k16 = stage(lambda u: bf(u["k_ref"][:, u["sl"]]))
    decay = stage(lambda u: jnp.where(u["incl"], jnp.exp(jnp.where(u["incl"], u["gc"] - u["gr"], 0.0)), 0.0))
    kbeta = stage(lambda u: u["k_ref"][:, u["sl"]] * u["beta"])
    kq = [_dot_nt(jnp.concatenate([bf(kb), bf(u["q_ref"][:, u["sl"]])], axis=0), k)
          for u, kb, k in zip(units, kbeta, k16)]
    a_low = [jnp.where(u["strict"], t[:chunk] * dc, 0.0) for u, t, dc in zip(units, kq, decay)]
    attn = [bf(t[chunk:] * dc) for t, dc in zip(kq, decay)]

    d = [jnp.where(diag_blk, a, 0.0) for a in a_low]
    d2 = [_dot(bf(t), bf(t)) for t in d]
    x = [eye - t for t in d]
    x = [t + _dot(bf(t), bf(p)) for t, p in zip(x, d2)]
    d4 = [_dot(bf(p), bf(p)) for p in d2]
    x = [t + _dot(bf(t), bf(p)) for t, p in zip(x, d4)]
    for off_blk in merge_blks:
        x16 = [bf(t) for t in x]
        tmp = [_dot(bf(jnp.where(off_blk, a, 0.0)), t16) for a, t16 in zip(a_low, x16)]
        x = [t - _dot(t16, bf(m)) for t, t16, m in zip(x, x16, tmp)]

    e_gc = stage(lambda u: jnp.exp(u["gc"]))
    rhs = [bf(jnp.concatenate([u["v_ref"][:, u["sl"]] * u["beta"], kb * e], axis=-1))
           for u, kb, e in zip(units, kbeta, e_gc)]
    sol = [_dot(bf(t), r) for t, r in zip(x, rhs)]

    s16 = stage(lambda u: bf(s_ref[u["dr"], u["h"]]))
    ws = [_dot(jnp.concatenate([bf(so[:, dk:]), bf(u["q_ref"][:, u["sl"]] * e)], axis=0), s)
          for u, so, e, s in zip(units, sol, e_gc, s16)]
    vn16 = [bf(so[:, :dk] - t[:chunk]) for so, t in zip(sol, ws)]
    o = [t[chunk:] + _dot(at, vn) for t, at, vn in zip(ws, attn, vn16)]
    for u, t in zip(units, o):
        u["o_ref"][:, u["sl"]] = t
    s_new = [s_ref[u["dr"], u["h"]] * jnp.exp(u["gl"])
             + _dot_tn(bf(u["k_ref"][:, u["sl"]] * jnp.exp(u["gl"] - u["gc"])), vn)
             for u, vn in zip(units, vn16)]
    for u, t in zip(units, s_new):
        s_ref[u["dr"], u["h"]] = t

    @pl.when(n == pl.num_programs(1) - 1)
    def _():
        sfin_ref[0] = s_ref[...]


def _gdn_call(qkv, ab, abt, av, dtv, avt, dtvt, s0, *, n_seq, n_tok, heads, dk, chunk):
    nc = n_tok // chunk
    d = heads * dk
    fwd = lambda b, n: b * nc + n
    bwd = lambda b, n: b * nc + nc - 1 - n
    qkv_spec = lambda idx, which: pl.BlockSpec((chunk, d), lambda b, n: (idx(b, n), which))
    ab_spec = lambda idx: pl.BlockSpec((chunk, 128), lambda b, n: (idx(b, n), 0))
    abt_spec = lambda idx: pl.BlockSpec((4 * heads, chunk), lambda b, n: (0, idx(b, n)))
    small = lambda shape: pl.BlockSpec(shape, lambda b, n: (0, 0))
    st_spec = pl.BlockSpec((1, 2, heads, dk, dk), lambda b, n: (b, 0, 0, 0, 0))
    o_shape = jax.ShapeDtypeStruct((n_seq * n_tok, d), F32)
    in_specs = [qkv_spec(fwd, 0), qkv_spec(fwd, 1), qkv_spec(fwd, 2), ab_spec(fwd), abt_spec(fwd),
                qkv_spec(bwd, 0), qkv_spec(bwd, 1), qkv_spec(bwd, 2), ab_spec(bwd), abt_spec(bwd),
                small((1, 128)), small((1, 128)), small((4 * heads, 1)), small((4 * heads, 1))]
    args = [qkv, qkv, qkv, ab, abt, qkv, qkv, qkv, ab, abt, av, dtv, avt, dtvt]
    if s0 is not None:
        in_specs.append(st_spec)
        args.append(s0)
    return pl.pallas_call(
        functools.partial(_gdn_body, heads=heads, dk=dk, chunk=chunk, has_s0=s0 is not None),
        grid=(n_seq, nc),
        in_specs=in_specs,
        out_specs=[pl.BlockSpec((chunk, d), lambda b, n: (fwd(b, n), 0)),
                   pl.BlockSpec((chunk, d), lambda b, n: (bwd(b, n), 0)),
                   st_spec],
        out_shape=[o_shape, o_shape, jax.ShapeDtypeStruct((n_seq, 2, heads, dk, dk), F32)],
        scratch_shapes=[pltpu.VMEM((2, heads, dk, dk), F32)],
        compiler_params=_cparams(("parallel", "arbitrary")),
        name="gdn_scan",
    )(*args)


def _outproj_body(x_ref, mod_ref, zhy_ref, of_ref, ob_ref, zg_ref, ghy_ref, go_ref, npost_ref, w_ref,
                  o_ref, *, mi, heads, dk):
    yhy = _rms(zhy_ref[...], ghy_ref[...]).astype(BF16)
    o = of_ref[...] + ob_ref[...]
    zg = zg_ref[...]
    parts = [yhy]
    for h in range(heads):
        sl = slice(h * dk, (h + 1) * dk)
        parts.append((_rms(o[:, sl], go_ref[...]) * _silu(zg[:, sl])).astype(BF16))
    y = _dot(jnp.concatenate(parts, axis=-1), w_ref[...])
    gate = mod_ref[0, mi:mi + 1, :]
    o_ref[...] = x_ref[...] + gate * _rms(y, npost_ref[...])


def _outproj_call(x, mod, zhy, o_f, o_b, proj, ghy, go, npost, w, *, mi, zg_col_blk, g0, rows_per_req, heads, dk,
                  tm=512):
    m, d = x.shape
    c = zhy.shape[1]
    dg = heads * dk
    row = lambda width: pl.BlockSpec((tm, width), lambda i: (i, 0))
    return pl.pallas_call(
        functools.partial(_outproj_body, mi=mi, heads=heads, dk=dk),
        grid=(m // tm,),
        in_specs=[row(d),
                  _mod_spec(d, tm, g0, rows_per_req),
                  row(c), row(dg), row(dg),
                  pl.BlockSpec((tm, dg), lambda i: (i, zg_col_blk)),
                  pl.BlockSpec((1, c), lambda i: (0, 0)),
                  pl.BlockSpec((1, dk), lambda i: (0, 0)),
                  pl.BlockSpec((1, d), lambda i: (0, 0)),
                  pl.BlockSpec((c + dg, d), lambda i: (0, 0))],
        out_specs=row(d),
        out_shape=jax.ShapeDtypeStruct((m, d), F32),
        compiler_params=_cparams(("parallel",)),
        name="out_proj",
    )(x, mod, zhy, o_f, o_b, proj, ghy.reshape(1, c), go.reshape(1, dk), npost.reshape(1, d), w)


@functools.lru_cache(maxsize=None)
def _grid_pos_tables_np(n_tok, dim):
    rows = n_tok // GRID_W
    quarter = dim // 4
    omega = 1.0 / (POS_BASE ** (np.arange(quarter, dtype=np.float64) / quarter))
    ar = np.arange(rows, dtype=np.float64)[:, None] * omega[None]
    ac = np.arange(GRID_W, dtype=np.float64)[:, None] * omega[None]
    row_t = np.concatenate([np.sin(ar), np.cos(ar)], axis=-1).astype(np.float32)
    col_t = np.concatenate([np.sin(ac), np.cos(ac)], axis=-1).astype(np.float32)
    return row_t.reshape(rows, 1, dim // 2), col_t.reshape(GRID_W // ROW_CHUNK, ROW_CHUNK, dim // 2)


def _mixer(proj, proj_ab, s0, *, n_seq, n_tok, hy_conv_w, hy_f, hy_decay, hy_bias, gdn_conv_w, gdn_a_log,
           gdn_dt_bias, c_hy, d_gdn):
    heads = GDN_HEADS
    dk = d_gdn // heads
    o0 = 3 * c_hy

    a_exp = jnp.exp(gdn_a_log.astype(F32))
    zeros = jnp.zeros_like(a_exp)
    av32 = jnp.stack([a_exp, zeros], axis=1).reshape(-1)
    dtv32 = jnp.stack([gdn_dt_bias.astype(F32), zeros], axis=1).reshape(-1)
    pad = 128 - 4 * heads
    av = jnp.pad(av32, (0, pad)).reshape(1, 128)
    dtv = jnp.pad(dtv32, (0, pad)).reshape(1, 128)
    avt = av32.reshape(4 * heads, 1)
    dtvt = dtv32.reshape(4 * heads, 1)
    abt = jnp.transpose(proj_ab[:, :4 * heads])

    r = min(512, n_tok)
    tables = _dft_tables(n_tok, r)
    ha, hb, hc = _spectra_call(n_tok, *hy_f, hy_decay, tables, r=r)
    zhy = _hyena_call(proj, hy_conv_w, hy_bias, tables, ha, hb, hc, n_seq=n_seq, n_tok=n_tok, c=c_hy, r=r,
                      n_sub=max(1, min(n_seq, 2048 // n_tok)))
    qkv = _gdn_prep_call(proj, gdn_conv_w, n_seq=n_seq, n_tok=n_tok, col0=o0, d_gdn=d_gdn, dk=dk)
    o_f, o_b, s_fin = _gdn_call(qkv, proj_ab, abt, av, dtv, avt, dtvt, s0, n_seq=n_seq, n_tok=n_tok, heads=heads,
                                dk=dk, chunk=GDN_CHUNK)
    return zhy, o_f, o_b, s_fin


def kernel(x_prompt, x_sample, state_gdn, c, c_ctx, ada_w, ada_b, norm_pre, norm_post, ffn_wg, ffn_wu, ffn_wd,
           w_in, w_out, hy_conv_w, hy_f_w1, hy_f_b1, hy_f_w2, hy_f_b2, hy_f_w3, hy_decay, hy_bias, hy_out_norm,
           gdn_conv_w, gdn_a_log, gdn_dt_bias, gdn_o_norm):
    nb, seq, d = x_prompt.shape
    db, dseq, _ = x_sample.shape
    depth = ada_w.shape[0]
    c_hy = hy_decay.shape[-1]
    d_gdn = gdn_conv_w.shape[-1] // 3
    heads = GDN_HEADS
    dk = d_gdn // heads
    n_ctx_rows = nb * seq
    n_lat_rows = db * dseq
    pos = tuple(jnp.asarray(t).astype(x_sample.dtype) for t in _grid_pos_tables_np(dseq, d))
    cvec = jnp.concatenate([c_ctx[None], c, jnp.zeros((8 - 1 - db, d), F32)], axis=0)
    wg, wu, wd = ffn_wg.astype(BF16), ffn_wu.astype(BF16), ffn_wd.astype(BF16)
    n_main_cols = 3 * c_hy + 4 * d_gdn
    n_tail = w_in.shape[-1] - n_main_cols
    w_in_t = jnp.swapaxes(w_in, 1, 2).astype(BF16)

    xs = [x_prompt.reshape(n_ctx_rows, d), x_sample.reshape(n_lat_rows, d)]
    groups = [dict(g0=0, rows_per_req=n_ctx_rows), dict(g0=1, rows_per_req=dseq)]
    shapes = [(nb, seq), (db, dseq)]
    ctx_states = []
    for l in range(depth):
        mod = _mod_call(cvec, ada_w[l], ada_b[l]).reshape(8, N_MOD, d)
        w_tail_t = jnp.pad(w_in_t[l, n_main_cols:], ((0, 128 - n_tail), (0, 0)))
        w_out_l = w_out[l].astype(BF16)
        hy_f = (hy_f_w1[l], hy_f_b1[l], hy_f_w2[l], hy_f_b2[l], hy_f_w3[l])
        states = [None, state_gdn[:, l].astype(F32)]
        for gi, (grp, (n_seq, n_tok)) in enumerate(zip(groups, shapes)):
            x = _ffn_call(xs[gi], mod, norm_pre[l, 0], norm_post[l, 0], wg, wu, wd, (l, 0), mi=0,
                          pos=pos if (gi == 1 and l == 0) else None, **grp)
            proj, proj_ab = _inproj_call(x, mod, norm_pre[l, 1], w_in_t, l, w_tail_t, mi=3,
                                         n_main_cols=n_main_cols, **grp)
            zhy, o_f, o_b, s_fin = _mixer(
                proj, proj_ab, states[gi], n_seq=n_seq, n_tok=n_tok, hy_conv_w=hy_conv_w[l], hy_f=hy_f,
                hy_decay=hy_decay[l], hy_bias=hy_bias[l], gdn_conv_w=gdn_conv_w[l], gdn_a_log=gdn_a_log[l],
                gdn_dt_bias=gdn_dt_bias[l], c_hy=c_hy, d_gdn=d_gdn)
            if gi == 0:
                ctx_states.append(s_fin)
            x = _outproj_call(x, mod, zhy, o_f, o_b, proj, hy_out_norm[l], gdn_o_norm[l], norm_post[l, 1], w_out_l,
                              mi=5, zg_col_blk=(3 * c_hy + 3 * d_gdn) // d_gdn, heads=heads, dk=dk, **grp)
            xs[gi] = _ffn_call(x, mod, norm_pre[l, 2], norm_post[l, 2], wg, wu, wd, (l, 1), mi=6, **grp)

    new_state = ctx_states[0][:, None] if depth == 1 else jnp.stack(ctx_states, axis=1)
    return xs[0].reshape(nb, seq, d), xs[1].reshape(db, dseq, d), new_state
```

```python
import functools
import math

import numpy as np
import jax
import jax.numpy as jnp
from jax import lax
from jax.experimental import pallas as pl
from jax.experimental.pallas import tpu as pltpu

F32 = jnp.float32
BF16 = jnp.bfloat16

N_MOD = 9
RMS_EPS = 1e-6
GRID_W = 64
POS_BASE = 10000.0
HY_BANDS = 16
HY_SIN_FREQ = 1.0
HY_WINDOW_SHIFT = 0.05
GDN_HEADS = 8
GDN_CHUNK = 128
VMEM_LIMIT = 56 * 1024 * 1024
ROW_CHUNK = 16


def _cparams(sem):
    return pltpu.CompilerParams(dimension_semantics=sem, vmem_limit_bytes=VMEM_LIMIT)


def _dot(a, b):
    return jnp.dot(a, b, preferred_element_type=F32)


def _dot_nt(a, b):
    return lax.dot_general(a, b, (((1,), (1,)), ((), ())), preferred_element_type=F32)


def _dot_tn(a, b):
    return lax.dot_general(a, b, (((0,), (0,)), ((), ())), preferred_element_type=F32)


def _rms(x, g):
    ms = jnp.mean(x * x, axis=-1, keepdims=True)
    return x * lax.rsqrt(ms + RMS_EPS) * g


def _silu(x):
    return x * jax.nn.sigmoid(x)


def _for_row_chunks(n_rows, fn):
    def body(c, carry):
        fn(pl.multiple_of(c * ROW_CHUNK, ROW_CHUNK))
        return carry

    lax.fori_loop(0, n_rows // ROW_CHUNK, body, 0, unroll=8)


def _mod_spec(d, tm, g0, rows_per_req):
    tpr = rows_per_req // tm
    return pl.BlockSpec((1, N_MOD, d), lambda i, *_: (g0 + i // tpr, 0, 0))


def _mod_body(c_ref, w_ref, b_ref, o_ref):
    s = _silu(c_ref[...]).astype(BF16)
    o_ref[...] = _dot(s, w_ref[...].astype(BF16)) + b_ref[...]


def _mod_call(cvec, ada_w, ada_b):
    g, d = cvec.shape
    n = ada_w.shape[1]
    tn = 1024
    return pl.pallas_call(
        _mod_body,
        grid=(n // tn,),
        in_specs=[pl.BlockSpec((g, d), lambda j: (0, 0)),
                  pl.BlockSpec((d, tn), lambda j: (0, j)),
                  pl.BlockSpec((1, tn), lambda j: (0, j))],
        out_specs=pl.BlockSpec((g, tn), lambda j: (0, j)),
        out_shape=jax.ShapeDtypeStruct((g, n), F32),
        compiler_params=_cparams(("arbitrary",)),
        name="mod_table",
    )(cvec, ada_w, ada_b.reshape(1, n))


def _ffn_body(*refs, mi, has_pos, tok_per_req):
    refs = list(refs)
    x_ref = refs.pop(0)
    prow_ref, pcol_ref = (refs.pop(0), refs.pop(0)) if has_pos else (None, None)
    mod_ref, npre_ref, npost_ref, wg_ref, wu_ref, wd_ref, o_ref, h_ref = refs
    j = pl.program_id(1)
    tm = x_ref.shape[0]

    def x_rows(r0):
        x = x_ref[pl.ds(r0, ROW_CHUNK), :]
        if not has_pos:
            return x
        t0 = (pl.program_id(0) * tm + r0) % tok_per_req
        prow = jnp.broadcast_to(prow_ref[t0 // GRID_W], (ROW_CHUNK, prow_ref.shape[-1]))
        return x + jnp.concatenate([prow, pcol_ref[(t0 % GRID_W) // ROW_CHUNK]], axis=1)

    @pl.when(j == 0)
    def _():
        shift = mod_ref[0, mi:mi + 1, :]
        gain = npre_ref[...] * (1.0 + mod_ref[0, mi + 1:mi + 2, :])

        def rows_body(r0):
            rows = pl.ds(r0, ROW_CHUNK)
            h_ref[rows, :] = (_rms(x_rows(r0), gain) + shift).astype(BF16)
            o_ref[rows, :] = jnp.zeros((ROW_CHUNK, o_ref.shape[1]), F32)

        _for_row_chunks(tm, rows_body)

    h = h_ref[...]
    g = _dot(h, wg_ref[...])
    u = _dot(h, wu_ref[...])
    a = (_silu(g) * u).astype(BF16)
    o_ref[...] += _dot(a, wd_ref[...])

    @pl.when(j == pl.num_programs(1) - 1)
    def _():
        gain = npost_ref[...] * (0.5 * mod_ref[0, mi + 2:mi + 3, :])
        for r0 in range(0, tm, ROW_CHUNK):
            rows = slice(r0, r0 + ROW_CHUNK)
            o_ref[rows, :] = x_rows(r0) + _rms(o_ref[rows, :], gain)


def _ffn_call(x, mod, npre, npost, wg, wu, wd, wsel, *, mi, g0, rows_per_req, pos=None, tm=1024, tf=512):
    n_rows, d = x.shape
    ff = wg.shape[-1]
    wl, wk = wsel
    const = lambda shape: pl.BlockSpec(shape, lambda i, j: (0, 0))
    in_specs = [pl.BlockSpec((tm, d), lambda i, j: (i, 0))]
    args = [x]
    if pos is not None:
        in_specs += [pl.BlockSpec(t.shape, lambda i, j: (0, 0, 0)) for t in pos]
        args += list(pos)
    in_specs += [_mod_spec(d, tm, g0, rows_per_req),
                 const((1, d)), const((1, d)),
                 pl.BlockSpec((None, None, d, tf), lambda i, j: (wl, wk, 0, j)),
                 pl.BlockSpec((None, None, d, tf), lambda i, j: (wl, wk, 0, j)),
                 pl.BlockSpec((None, None, tf, d), lambda i, j: (wl, wk, j, 0))]
    args += [mod, npre.reshape(1, d), npost.reshape(1, d), wg, wu, wd]
    return pl.pallas_call(
        functools.partial(_ffn_body, mi=mi, has_pos=pos is not None, tok_per_req=rows_per_req),
        grid=(n_rows // tm, ff // tf),
        in_specs=in_specs,
        out_specs=pl.BlockSpec((tm, d), lambda i, j: (i, 0)),
        out_shape=jax.ShapeDtypeStruct((n_rows, d), F32),
        scratch_shapes=[pltpu.VMEM((tm, d), BF16)],
        compiler_params=_cparams(("parallel", "arbitrary")),
        name="ffn",
    )(*args)


def _inproj_body(x_ref, mod_ref, npre_ref, w_ref, wt_ref, o_ref, ot_ref, h_ref, *, mi, n_main):
    j = pl.program_id(1)

    @pl.when(j == 0)
    def _():
        shift = mod_ref[0, mi:mi + 1, :]
        gain = npre_ref[...] * (1.0 + mod_ref[0, mi + 1:mi + 2, :])

        def rows_body(r0):
            rows = pl.ds(r0, ROW_CHUNK)
            h_ref[rows, :] = (_rms(x_ref[rows, :], gain) + shift).astype(BF16)

        _for_row_chunks(x_ref.shape[0], rows_body)

    @pl.when(j < n_main)
    def _():
        o_ref[...] = _dot_nt(h_ref[...], w_ref[...])

    @pl.when(j == n_main)
    def _():
        ot_ref[...] = _dot_nt(h_ref[...], wt_ref[...])


def _inproj_call(x, mod, npre, w_in_t, layer, w_tail_t, *, mi, n_main_cols, g0, rows_per_req, tm=1024, tn=1792):
    m, d = x.shape
    n_main = n_main_cols // tn
    nt = w_tail_t.shape[0]
    jm = lambda j: jnp.minimum(j, n_main - 1)
    return pl.pallas_call(
        functools.partial(_inproj_body, mi=mi, n_main=n_main),
        grid=(m // tm, n_main + 1),
        in_specs=[pl.BlockSpec((tm, d), lambda i, j: (i, 0)),
                  _mod_spec(d, tm, g0, rows_per_req),
                  pl.BlockSpec((1, d), lambda i, j: (0, 0)),
                  pl.BlockSpec((None, tn, d), lambda i, j: (layer, jm(j), 0)),
                  pl.BlockSpec((nt, d), lambda i, j: (0, 0))],
        out_specs=[pl.BlockSpec((tm, tn), lambda i, j: (i, jm(j))),
                   pl.BlockSpec((tm, nt), lambda i, j: (i, 0))],
        out_shape=[jax.ShapeDtypeStruct((m, n_main_cols), F32), jax.ShapeDtypeStruct((m, nt), F32)],
        scratch_shapes=[pltpu.VMEM((tm, d), BF16)],
        compiler_params=_cparams(("parallel", "arbitrary")),
        name="in_proj",
    )(x, mod, npre.reshape(1, d), w_in_t, w_tail_t)


def _conv3_rows(x_ref, w, r0, nr, seq_len):
    zeros = jnp.zeros((8, x_ref.shape[1]), x_ref.dtype)
    top = zeros if r0 % seq_len == 0 else x_ref[r0 - 8:r0, :]
    mid = x_ref[r0:r0 + nr, :]
    bot = zeros if (r0 + nr) % seq_len == 0 else x_ref[r0 + nr:r0 + nr + 8, :]
    return _conv3_slab(top, mid, bot, w)


def _conv3_rows_dyn(x_ref, w, r0, nr, seq_len):
    n = x_ref.shape[0]
    top = x_ref[pl.ds(pl.multiple_of(jnp.maximum(r0 - 8, 0), 8), 8), :]
    mid = x_ref[pl.ds(pl.multiple_of(r0, 8), nr), :]
    bot = x_ref[pl.ds(pl.multiple_of(jnp.minimum(r0 + nr, n - 8), 8), 8), :]
    top = jnp.where(r0 % seq_len == 0, 0.0, top)
    bot = jnp.where((r0 + nr) % seq_len == 0, 0.0, bot)
    return _conv3_slab(top, mid, bot, w)


def _conv3_slab(top, mid, bot, w):
    nr = mid.shape[0]
    slab = jnp.concatenate([top, mid, bot], axis=0)
    prev = pltpu.roll(slab, 1, 0)[8:8 + nr]
    nxt = pltpu.roll(slab, nr + 15, 0)[8:8 + nr]
    return prev * w[0:1, :] + mid * w[1:2, :] + nxt * w[2:3, :]


@functools.lru_cache(maxsize=None)
def _dft_tables_np(n_tok, r):
    big = 2 * n_tok
    k = np.arange(n_tok, dtype=np.int64)
    t = np.arange(n_tok, dtype=np.int64)
    ang = ((k[:, None] * t[None, :]) % big).astype(np.float64) * (2.0 * math.pi / big)
    cosm = np.cos(ang)
    sinm = np.sin(ang)
    sinm[0, :] = 1.0 - 2.0 * (t % 2)
    nj = n_tok // r
    fr = np.concatenate([cosm.reshape(nj, r, n_tok), sinm.reshape(nj, r, n_tok)], axis=1)
    g_full = fr.reshape(big, n_tok).T
    gr = np.concatenate([g_full[:, :n_tok].reshape(nj, r, n_tok), g_full[:, n_tok:].reshape(nj, r, n_tok)], axis=1)
    return np.concatenate([fr, gr], axis=0).astype(np.float32)


def _dft_tables(n_tok, r):
    return jnp.asarray(_dft_tables_np(n_tok, r)).astype(BF16)


def _spectra_body(featt_ref, t_ref, w1t_ref, b1_ref, w2t_ref, b2_ref, w3f_ref, w3b_ref, dec_ref, f_ref,
                  ha_ref, hb_ref, hc_ref, e_ref, o_ref, h_ref, *, n_tok, r):
    j = pl.program_id(1)

    @pl.when((pl.program_id(0) == 0) & (j == 0))
    def _():
        h = jnp.sin(HY_SIN_FREQ * (_dot(w1t_ref[...].astype(BF16), featt_ref[...].astype(BF16)) + b1_ref[...]))
        h = jnp.sin(HY_SIN_FREQ * (_dot(w2t_ref[...].astype(BF16), h.astype(BF16)) + b2_ref[...]))
        h_ref[...] = h.astype(BF16)

    @pl.when(j == 0)
    def _():
        hb16 = h_ref[...]
        window = jnp.exp(-t_ref[...] * jnp.abs(dec_ref[...])) + HY_WINDOW_SHIFT
        fwd = _dot_tn(hb16, w3f_ref[...].astype(BF16)) * window
        bwd = _dot_tn(hb16, w3b_ref[...].astype(BF16)) * window
        row = lax.broadcasted_iota(jnp.int32, bwd.shape, 0)
        bwd = jnp.where(row == 0, 0.0, bwd)
        norm = jnp.sum(jnp.abs(fwd), axis=0, keepdims=True) + jnp.sum(jnp.abs(bwd), axis=0, keepdims=True)
        fwd = fwd / norm
        bwd = bwd / norm
        e_ref[...] = (fwd + bwd).astype(BF16)
        o_ref[...] = (fwd - bwd).astype(BF16)

    fb = f_ref[0]
    e = e_ref[...]
    p = _dot(fb[:r], e)
    q = _dot(fb[r:], o_ref[...])
    k = j * r + lax.broadcasted_iota(jnp.int32, p.shape, 0)
    big = 2.0 * n_tok
    wk = jnp.where(k == 0, 1.0 / big, 2.0 / big)
    ha_ref[...] = wk * p
    hb_ref[...] = jnp.where(k == 0, 0.0, -wk * q)

    @pl.when(j == 0)
    def _():
        hc_ref[...] = _dot(fb[r:r + 8], e)[0:1, :] * (1.0 / big)


def _spectra_call(n_tok, w1, b1, w2, b2, w3, decay, fr, *, r, cn=512):
    f32 = jnp.float32
    order, c = decay.shape
    oc = order * c
    hid = w2.shape[0]
    emb = w1.shape[0]
    embp = 128
    idx = jnp.arange(n_tok, dtype=f32)
    t = idx / (n_tok - 1)
    bands = jnp.arange(1, HY_BANDS + 1, dtype=f32)
    ang = (2.0 * math.pi / n_tok) * idx[:, None] * bands[None, :]
    feats = jnp.concatenate([t[:, None], jnp.cos(ang), jnp.sin(ang)], axis=-1)
    featt = jnp.pad(feats, ((0, 0), (0, embp - emb))).T
    w1t = jnp.pad(w1, ((0, embp - emb), (0, 0))).T
    nj = n_tok // r
    ncb = oc // cn
    out = jax.ShapeDtypeStruct((n_tok, oc), F32)
    const = lambda shape: pl.BlockSpec(shape, lambda cb, j: (0, 0))
    return pl.pallas_call(
        functools.partial(_spectra_body, n_tok=n_tok, r=r),
        grid=(ncb, nj),
        in_specs=[const((embp, n_tok)), const((n_tok, 1)), const((hid, embp)), const((hid, 1)),
                  const((hid, hid)), const((hid, 1)),
                  pl.BlockSpec((hid, cn), lambda cb, j: (0, cb)),
                  pl.BlockSpec((hid, cn), lambda cb, j: (0, ncb + cb)),
                  pl.BlockSpec((1, cn), lambda cb, j: (0, cb)),
                  pl.BlockSpec((1, 2 * r, n_tok), lambda cb, j: (j, 0, 0))],
        out_specs=[pl.BlockSpec((r, cn), lambda cb, j: (j, cb))] * 2 + [pl.BlockSpec((1, cn), lambda cb, j: (0, cb))],
        out_shape=[out, out, jax.ShapeDtypeStruct((1, oc), F32)],
        scratch_shapes=[pltpu.VMEM((n_tok, cn), BF16), pltpu.VMEM((n_tok, cn), BF16),
                        pltpu.VMEM((hid, n_tok), BF16)],
        compiler_params=_cparams(("arbitrary", "arbitrary")),
        name="hyena_spectra",
    )(featt, t[:, None], w1t, b1.reshape(hid, 1), w2.T, b2.reshape(hid, 1), w3, w3, decay.reshape(1, oc), fr)


def _hyena_body(v_ref, x1_ref, x2_ref, wv_ref, wx1_ref, wx2_ref, bias_ref, t_ref, ha_ref, hb_ref, hc_ref,
                o_ref, ub_ref, u2f_ref, z_ref, *, r, nj, seq_len, n_sub):
    t = pl.program_id(2)
    order = t // (2 * nj)
    tt = t % (2 * nj)
    sub = min(128, r)

    @pl.when(t == 0)
    def _():
        w = wv_ref[...]
        for r0 in range(0, n_sub * seq_len, 64):
            ub_ref[0, r0:r0 + 64, :] = _conv3_rows(v_ref, w, r0, 64, seq_len).astype(BF16)

    @pl.when(tt < nj)
    def _():
        tb = t_ref[0]
        ha, hb = ha_ref[...], hb_ref[...]
        nyq = (tt * r + lax.broadcasted_iota(jnp.int32, ha.shape, 0)) == 0
        hc = jnp.where(nyq, hc_ref[...], ha)
        z0 = pl.multiple_of(tt * (2 * r), 2 * r)
        for s in range(n_sub):
            x = _dot(tb, ub_ref[order, s * seq_len:(s + 1) * seq_len, :])
            p = x[:r]
            q = x[r:]
            z_ref[s, pl.ds(z0, 2 * r), :] = jnp.concatenate([p * ha + q * hb, q * hc - p * hb],
                                                             axis=0).astype(BF16)

    def inverse_rows(s):
        tb = t_ref[0]
        return _dot(tb[:r], z_ref[s, :seq_len, :]) + _dot(tb[r:], z_ref[s, seq_len:, :])

    r_blk = (tt - nj) * r

    @pl.when((tt >= nj) & (order == 0))
    def _():
        wv, wx1 = wv_ref[...], wx1_ref[...]
        for s in range(n_sub):
            y = inverse_rows(s)
            for c0 in range(0, r, sub):
                r0 = s * seq_len + r_blk + c0
                u1 = _conv3_rows_dyn(v_ref, wv, r0, sub, seq_len)
                z1 = _conv3_rows_dyn(x1_ref, wx1, r0, sub, seq_len) * (y[c0:c0 + sub] + u1 * bias_ref[0:1, :])
                u2f_ref[pl.ds(pl.multiple_of(r0, 8), sub), :] = z1
                ub_ref[1, pl.ds(pl.multiple_of(r0, 8), sub), :] = z1.astype(BF16)

    @pl.when((tt >= nj) & (order == 1))
    def _():
        wx2 = wx2_ref[...]
        for s in range(n_sub):
            y = inverse_rows(s)
            for c0 in range(0, r, sub):
                r0 = s * seq_len + r_blk + c0
                u2 = u2f_ref[pl.ds(pl.multiple_of(r0, 8), sub), :]
                o_ref[pl.ds(pl.multiple_of(r0, 8), sub), :] = (
                    _conv3_rows_dyn(x2_ref, wx2, r0, sub, seq_len) * (y[c0:c0 + sub] + u2 * bias_ref[1:2, :]))


def _hyena_call(proj, conv_w, bias, tables, ha, hb, hc, *, n_seq, n_tok, c, r, n_sub, cn=512):
    nj = n_tok // r
    ncb = c // cn
    blk = n_sub * n_tok
    steps = 4 * nj
    seq_spec = lambda off: pl.BlockSpec((blk, cn), lambda b, cb, t: (b, off * ncb + cb),
                                        **({"pipeline_mode": pl.Buffered(1)} if off == 0 else {}))
    w_spec = lambda off: pl.BlockSpec((3, cn), lambda b, cb, t: (0, off * ncb + cb))
    h_spec = pl.BlockSpec((r, cn), lambda b, cb, t: (jnp.minimum(t % (2 * nj), nj - 1), (t // (2 * nj)) * ncb + cb))
    in_specs = [seq_spec(0), seq_spec(1), seq_spec(2), w_spec(0), w_spec(1), w_spec(2),
                pl.BlockSpec((2, cn), lambda b, cb, t: (0, cb)),
                pl.BlockSpec((1, 2 * r, n_tok), lambda b, cb, t: (t % (2 * nj), 0, 0)),
                h_spec, h_spec,
                pl.BlockSpec((1, cn), lambda b, cb, t: (0, (t // (2 * nj)) * ncb + cb))]
    return pl.pallas_call(
        functools.partial(_hyena_body, r=r, nj=nj, seq_len=n_tok, n_sub=n_sub),
        grid=(n_seq // n_sub, ncb, steps),
        in_specs=in_specs,
        out_specs=pl.BlockSpec((blk, cn), lambda b, cb, t: (b, cb)),
        out_shape=jax.ShapeDtypeStruct((n_seq * n_tok, c), F32),
        scratch_shapes=[pltpu.VMEM((2, blk, cn), BF16), pltpu.VMEM((blk, cn), F32),
                        pltpu.VMEM((n_sub, 2 * n_tok, cn), BF16)],
        compiler_params=_cparams(("parallel", "parallel", "arbitrary")),
        name="hyena_conv",
    )(proj, proj, proj, conv_w, conv_w, conv_w, bias, tables, ha, hb, hc)


def _gdn_prep_body(x_ref, w_ref, o_ref, *, heads_per_blk, dk, n_qk_blk, rows, seq_len):
    cb = pl.program_id(1)
    n = x_ref.shape[0]
    w = w_ref[...]
    is_v = cb >= n_qk_blk
    scale = jnp.where(cb < n_qk_blk // 2, dk ** -0.5, 1.0)
    for r0 in range(0, n, rows):
        y = _silu(_conv3_rows(x_ref, w, r0, rows, seq_len))
        for h in range(heads_per_blk):
            yh = y[:, h * dk:(h + 1) * dk]
            inv = lax.rsqrt(jnp.sum(yh * yh, axis=-1, keepdims=True) + RMS_EPS) * scale
            o_ref[r0:r0 + rows, h * dk:(h + 1) * dk] = yh * jnp.where(is_v, 1.0, inv)


def _gdn_prep_call(proj, conv_w, *, n_seq, n_tok, col0, d_gdn, dk, cn=512, blk_rows=2048):
    ncb = 3 * d_gdn // cn
    cb0 = col0 // cn
    rows = n_seq * n_tok
    blk_rows = min(blk_rows, rows)
    return pl.pallas_call(
        functools.partial(_gdn_prep_body, heads_per_blk=cn // dk, dk=dk, n_qk_blk=2 * d_gdn // cn,
                          rows=min(64, n_tok), seq_len=n_tok),
        grid=(rows // blk_rows, ncb),
        in_specs=[pl.BlockSpec((blk_rows, cn), lambda b, cb: (b, cb0 + cb)),
                  pl.BlockSpec((3, cn), lambda b, cb: (0, cb))],
        out_specs=pl.BlockSpec((blk_rows, cn), lambda b, cb: (b, cb)),
        out_shape=jax.ShapeDtypeStruct((rows, 3 * d_gdn), F32),
        compiler_params=_cparams(("parallel", "parallel")),
        name="gdn_prep",
    )(proj, conv_w)


def _split3(x):
    hi = x.astype(BF16)
    r1 = x - hi.astype(F32)
    mid = r1.astype(BF16)
    lo = (r1 - mid.astype(F32)).astype(BF16)
    return hi, mid, lo


def _gdn_body(*refs, heads, dk, chunk, has_s0):
    (qf_ref, kf_ref, vf_ref, abf_ref, abtf_ref, qb_ref, kb_ref, vb_ref, abb_ref, abtb_ref,
     av_ref, dtv_ref, avt_ref, dtvt_ref) = refs[:14]
    s0_ref = refs[14] if has_s0 else None
    of_ref, ob_ref, sfin_ref, s_ref = refs[-4:]
    n = pl.program_id(1)

    @pl.when(n == 0)
    def _():
        s_ref[...] = s0_ref[0] if has_s0 else jnp.zeros_like(s_ref)

    ci = lax.broadcasted_iota(jnp.int32, (chunk, chunk), 0)
    cj = lax.broadcasted_iota(jnp.int32, (chunk, chunk), 1)
    eye = (ci == cj).astype(F32)
    diag_blk = (ci >> 3) == (cj >> 3)
    merge_blks = [((ci >> s) ^ (cj >> s)) == 1 for s in range(3, int(math.log2(chunk)))]
    lower = cj <= ci
    upper = cj >= ci

    units = []
    for dr, (q_ref, k_ref, v_ref, ab_ref, abt_ref, o_ref) in enumerate(
            ((qf_ref, kf_ref, vf_ref, abf_ref, abtf_ref, of_ref),
             (qb_ref, kb_ref, vb_ref, abb_ref, abtb_ref, ob_ref))):
        incl = upper if dr else lower
        strict = (cj > ci) if dr else (cj < ci)
        ones_incl = jnp.where(incl, 1.0, 0.0).astype(BF16)
        ones_incl_t = jnp.where(lower if dr else upper, 1.0, 0.0).astype(BF16)
        ab = ab_ref[...]
        g_col = -av_ref[...] * jax.nn.softplus(ab + dtv_ref[...])
        beta_col = jax.nn.sigmoid(ab)
        abt = abt_ref[...]
        g_row = -avt_ref[...] * jax.nn.softplus(abt + dtvt_ref[...])
        h3 = _split3(g_col)
        gc_col = _dot(ones_incl, h3[0]) + _dot(ones_incl, h3[1]) + _dot(ones_incl, h3[2])
        r3 = _split3(g_row)
        gc_row = _dot(r3[0], ones_incl_t) + _dot(r3[1], ones_incl_t) + _dot(r3[2], ones_incl_t)
        gl_all = jnp.sum(g_col, axis=0, keepdims=True)
        for h in range(heads):
            cg = dr * 2 * heads + h
            cb = cg + heads
            sl = slice(h * dk, (h + 1) * dk)
            units.append(dict(dr=dr, h=h, sl=sl, q_ref=q_ref, k_ref=k_ref, v_ref=v_ref, o_ref=o_ref,
                              incl=incl, strict=strict, gc=gc_col[:, cg:cg + 1], gr=gc_row[cg:cg + 1, :],
                              beta=beta_col[:, cb:cb + 1], gl=gl_all[:, cg:cg + 1]))

    def stage(fn):
        return [fn(u) for u in units]

    bf = lambda t: t.astype(BF16)
    k16 = stage(lambda u: bf(u["k_ref"][:, u["sl"]]))
    decay = stage(lambda u: jnp.where(u["incl"], jnp.exp(jnp.where(u["incl"], u["gc"] - u["gr"], 0.0)), 0.0))
    kbeta = stage(lambda u: u["k_ref"][:, u["sl"]] * u["beta"])
    kq = [_dot_nt(jnp.concatenate([bf(kb), bf(u["q_ref"][:, u["sl"]])], axis=0), k)
          for u, kb, k in zip(units, kbeta, k16)]
    a_low = [jnp.where(u["strict"], t[:chunk] * dc, 0.0) for u, t, dc in zip(units, kq, decay)]
    attn = [bf(t[chunk:] * dc) for t, dc in zip(kq, decay)]

    d = [jnp.where(diag_blk, a, 0.0) for a in a_low]
    d2 = [_dot(bf(t), bf(t)) for t in d]
    x = [eye - t for t in d]
    x = [t + _dot(bf(t), bf(p)) for t, p in zip(x, d2)]
    d4 = [_dot(bf(p), bf(p)) for p in d2]
    x = [t + _dot(bf(t), bf(p)) for t, p in zip(x, d4)]
    for off_blk in merge_blks:
        x16 = [bf(t) for t in x]
        tmp = [_dot(bf(jnp.where(off_blk, a, 0.0)), t16) for a, t16 in zip(a_low, x16)]
        x = [t - _dot(t16, bf(m)) for t, t16, m in zip(x, x16, tmp)]

    e_gc = stage(lambda u: jnp.exp(u["gc"]))
    rhs = [bf(jnp.concatenate([u["v_ref"][:, u["sl"]] * u["beta"], kb * e], axis=-1))
           for u, kb, e in zip(units, kbeta, e_gc)]
    sol = [_dot(bf(t), r) for t, r in zip(x, rhs)]

    s16 = stage(lambda u: bf(s_ref[u["dr"], u["h"]]))
    ws = [_dot(jnp.concatenate([bf(so[:, dk:]), bf(u["q_ref"][:, u["sl"]] * e)], axis=0), s)
          for u, so, e, s in zip(units, sol, e_gc, s16)]
    vn16 = [bf(so[:, :dk] - t[:chunk]) for so, t in zip(sol, ws)]
    o = [t[chunk:] + _dot(at, vn) for t, at, vn in zip(ws, attn, vn16)]
    for u, t in zip(units, o):
        u["o_ref"][:, u["sl"]] = t
    s_new = [s_ref[u["dr"], u["h"]] * jnp.exp(u["gl"])
             + _dot_tn(bf(u["k_ref"][:, u["sl"]] * jnp.exp(u["gl"] - u["gc"])), vn)
             for u, vn in zip(units, vn16)]
    for u, t in zip(units, s_new):
        s_ref[u["dr"], u["h"]] = t

    @pl.when(n == pl.num_programs(1) - 1)
    def _():
        sfin_ref[0] = s_ref[...]


def _gdn_call(qkv, ab, abt, av, dtv, avt, dtvt, s0, *, n_seq, n_tok, heads, dk, chunk):
    nc = n_tok // chunk
    d = heads * dk
    fwd = lambda b, n: b * nc + n
    bwd = lambda b, n: b * nc + nc - 1 - n
    qkv_spec = lambda idx, which: pl.BlockSpec((chunk, d), lambda b, n: (idx(b, n), which))
    ab_spec = lambda idx: pl.BlockSpec((chunk, 128), lambda b, n: (idx(b, n), 0))
    abt_spec = lambda idx: pl.BlockSpec((4 * heads, chunk), lambda b, n: (0, idx(b, n)))
    small = lambda shape: pl.BlockSpec(shape, lambda b, n: (0, 0))
    st_spec = pl.BlockSpec((1, 2, heads, dk, dk), lambda b, n: (b, 0, 0, 0, 0))
    o_shape = jax.ShapeDtypeStruct((n_seq * n_tok, d), F32)
    in_specs = [qkv_spec(fwd, 0), qkv_spec(fwd, 1), qkv_spec(fwd, 2), ab_spec(fwd), abt_spec(fwd),
                qkv_spec(bwd, 0), qkv_spec(bwd, 1), qkv_spec(bwd, 2), ab_spec(bwd), abt_spec(bwd),
                small((1, 128)), small((1, 128)), small((4 * heads, 1)), small((4 * heads, 1))]
    args = [qkv, qkv, qkv, ab, abt, qkv, qkv, qkv, ab, abt, av, dtv, avt, dtvt]
    if s0 is not None:
        in_specs.append(st_spec)
        args.append(s0)
    return pl.pallas_call(
        functools.partial(_gdn_body, heads=heads, dk=dk, chunk=chunk, has_s0=s0 is not None),
        grid=(n_seq, nc),
        in_specs=in_specs,
        out_specs=[pl.BlockSpec((chunk, d), lambda b, n: (fwd(b, n), 0)),
                   pl.BlockSpec((chunk, d), lambda b, n: (bwd(b, n), 0)),
                   st_spec],
        out_shape=[o_shape, o_shape, jax.ShapeDtypeStruct((n_seq, 2, heads, dk, dk), F32)],
        scratch_shapes=[pltpu.VMEM((2, heads, dk, dk), F32)],
        compiler_params=_cparams(("parallel", "arbitrary")),
        name="gdn_scan",
    )(*args)


def _outproj_body(x_ref, mod_ref, zhy_ref, of_ref, ob_ref, zg_ref, ghy_ref, go_ref, npost_ref, w_ref,
                  o_ref, *, mi, heads, dk):
    yhy = _rms(zhy_ref[...], ghy_ref[...]).astype(BF16)
    o = of_ref[...] + ob_ref[...]
    zg = zg_ref[...]
    parts = [yhy]
    for h in range(heads):
        sl = slice(h * dk, (h + 1) * dk)
        parts.append((_rms(o[:, sl], go_ref[...]) * _silu(zg[:, sl])).astype(BF16))
    y = _dot(jnp.concatenate(parts, axis=-1), w_ref[...])
    gate = mod_ref[0, mi:mi + 1, :]
    o_ref[...] = x_ref[...] + gate * _rms(y, npost_ref[...])


def _outproj_call(x, mod, zhy, o_f, o_b, proj, ghy, go, npost, w, *, mi, zg_col_blk, g0, rows_per_req, heads, dk,
                  tm=512):
    m, d = x.shape
    c = zhy.shape[1]
    dg = heads * dk
    row = lambda width: pl.BlockSpec((tm, width), lambda i: (i, 0))
    return pl.pallas_call(
        functools.partial(_outproj_body, mi=mi, heads=heads, dk=dk),
        grid=(m // tm,),
        in_specs=[row(d),
                  _mod_spec(d, tm, g0, rows_per_req),
                  row(c), row(dg), row(dg),
                  pl.BlockSpec((tm, dg), lambda i: (i, zg_col_blk)),
                  pl.BlockSpec((1, c), lambda i: (0, 0)),
                  pl.BlockSpec((1, dk), lambda i: (0, 0)),
                  pl.BlockSpec((1, d), lambda i: (0, 0)),
                  pl.BlockSpec((c + dg, d), lambda i: (0, 0))],
        out_specs=row(d),
        out_shape=jax.ShapeDtypeStruct((m, d), F32),
        compiler_params=_cparams(("parallel",)),
        name="out_proj",
    )(x, mod, zhy, o_f, o_b, proj, ghy.reshape(1, c), go.reshape(1, dk), npost.reshape(1, d), w)


@functools.lru_cache(maxsize=None)
def _grid_pos_tables_np(n_tok, dim):
    rows = n_tok // GRID_W
    quarter = dim // 4
    omega = 1.0 / (POS_BASE ** (np.arange(quarter, dtype=np.float64) / quarter))
    ar = np.arange(rows, dtype=np.float64)[:, None] * omega[None]
    ac = np.arange(GRID_W, dtype=np.float64)[:, None] * omega[None]
    row_t = np.concatenate([np.sin(ar), np.cos(ar)], axis=-1).astype(np.float32)
    col_t = np.concatenate([np.sin(ac), np.cos(ac)], axis=-1).astype(np.float32)
    return row_t.reshape(rows, 1, dim // 2), col_t.reshape(GRID_W // ROW_CHUNK, ROW_CHUNK, dim // 2)


def _mixer(proj, proj_ab, s0, *, n_seq, n_tok, hy_conv_w, hy_f, hy_decay, hy_bias, gdn_conv_w, gdn_a_log,
           gdn_dt_bias, c_hy, d_gdn):
    heads = GDN_HEADS
    dk = d_gdn // heads
    o0 = 3 * c_hy

    a_exp = jnp.exp(gdn_a_log.astype(F32))
    zeros = jnp.zeros_like(a_exp)
    av32 = jnp.stack([a_exp, zeros], axis=1).reshape(-1)
    dtv32 = jnp.stack([gdn_dt_bias.astype(F32), zeros], axis=1).reshape(-1)
    pad = 128 - 4 * heads
    av = jnp.pad(av32, (0, pad)).reshape(1, 128)
    dtv = jnp.pad(dtv32, (0, pad)).reshape(1, 128)
    avt = av32.reshape(4 * heads, 1)
    dtvt = dtv32.reshape(4 * heads, 1)
    abt = jnp.transpose(proj_ab[:, :4 * heads])

    r = min(512, n_tok)
    tables = _dft_tables(n_tok, r)
    ha, hb, hc = _spectra_call(n_tok, *hy_f, hy_decay, tables, r=r)
    zhy = _hyena_call(proj, hy_conv_w, hy_bias, tables, ha, hb, hc, n_seq=n_seq, n_tok=n_tok, c=c_hy, r=r,
                      n_sub=max(1, min(n_seq, 2048 // n_tok)))
    qkv = _gdn_prep_call(proj, gdn_conv_w, n_seq=n_seq, n_tok=n_tok, col0=o0, d_gdn=d_gdn, dk=dk)
    o_f, o_b, s_fin = _gdn_call(qkv, proj_ab, abt, av, dtv, avt, dtvt, s0, n_seq=n_seq, n_tok=n_tok, heads=heads,
                                dk=dk, chunk=GDN_CHUNK)
    return zhy, o_f, o_b, s_fin


def kernel(x_prompt, x_sample, state_gdn, c, c_ctx, ada_w, ada_b, norm_pre, norm_post, ffn_wg, ffn_wu, ffn_wd,
           w_in, w_out, hy_conv_w, hy_f_w1, hy_f_b1, hy_f_w2, hy_f_b2, hy_f_w3, hy_decay, hy_bias, hy_out_norm,
           gdn_conv_w, gdn_a_log, gdn_dt_bias, gdn_o_norm):
    nb, seq, d = x_prompt.shape
    db, dseq, _ = x_sample.shape
    depth = ada_w.shape[0]
    c_hy = hy_decay.shape[-1]
    d_gdn = gdn_conv_w.shape[-1] // 3
    heads = GDN_HEADS
    dk = d_gdn // heads
    n_ctx_rows = nb * seq
    n_lat_rows = db * dseq
    pos = tuple(jnp.asarray(t).astype(x_sample.dtype) for t in _grid_pos_tables_np(dseq, d))
    cvec = jnp.concatenate([c_ctx[None], c, jnp.zeros((8 - 1 - db, d), F32)], axis=0)
    wg, wu, wd = ffn_wg.astype(BF16), ffn_wu.astype(BF16), ffn_wd.astype(BF16)
    n_main_cols = 3 * c_hy + 4 * d_gdn
    n_tail = w_in.shape[-1] - n_main_cols
    w_in_t = jnp.swapaxes(w_in, 1, 2).astype(BF16)

    xs = [x_prompt.reshape(n_ctx_rows, d), x_sample.reshape(n_lat_rows, d)]
    groups = [dict(g0=0, rows_per_req=n_ctx_rows), dict(g0=1, rows_per_req=dseq)]
    shapes = [(nb, seq), (db, dseq)]
    ctx_states = []
    for l in range(depth):
        mod = _mod_call(cvec, ada_w[l], ada_b[l]).reshape(8, N_MOD, d)
        w_tail_t = jnp.pad(w_in_t[l, n_main_cols:], ((0, 128 - n_tail), (0, 0)))
        w_out_l = w_out[l].astype(BF16)
        hy_f = (hy_f_w1[l], hy_f_b1[l], hy_f_w2[l], hy_f_b2[l], hy_f_w3[l])
        states = [None, state_gdn[:, l].astype(F32)]
        for gi, (grp, (n_seq, n_tok)) in enumerate(zip(groups, shapes)):
            x = _ffn_call(xs[gi], mod, norm_pre[l, 0], norm_post[l, 0], wg, wu, wd, (l, 0), mi=0,
                          pos=pos if (gi == 1 and l == 0) else None, **grp)
            proj, proj_ab = _inproj_call(x, mod, norm_pre[l, 1], w_in_t, l, w_tail_t, mi=3,
                                         n_main_cols=n_main_cols, **grp)
            zhy, o_f, o_b, s_fin = _mixer(
                proj, proj_ab, states[gi], n_seq=n_seq, n_tok=n_tok, hy_conv_w=hy_conv_w[l], hy_f=hy_f,
                hy_decay=hy_decay[l], hy_bias=hy_bias[l], gdn_conv_w=gdn_conv_w[l], gdn_a_log=gdn_a_log[l],
                gdn_dt_bias=gdn_dt_bias[l], c_hy=c_hy, d_gdn=d_gdn)
            if gi == 0:
                ctx_states.append(s_fin)
            x = _outproj_call(x, mod, zhy, o_f, o_b, proj, hy_out_norm[l], gdn_o_norm[l], norm_post[l, 1], w_out_l,
                              mi=5, zg_col_blk=(3 * c_hy + 3 * d_gdn) // d_gdn, heads=heads, dk=dk, **grp)
            xs[gi] = _ffn_call(x, mod, norm_pre[l, 2], norm_post[l, 2], wg, wu, wd, (l, 1), mi=6, **grp)

    new_state = ctx_states[0][:, None] if depth == 1 else jnp.stack(ctx_states, axis=1)
    return xs[0].reshape(nb, seq, d), xs[1].reshape(db, dseq, d), new_state
```

```python
import functools
import math

import numpy as np
import jax
import jax.numpy as jnp
from jax import lax
from jax.experimental import pallas as pl
from jax.experimental.pallas import tpu as pltpu

F32 = jnp.float32
BF16 = jnp.bfloat16

N_MOD = 9
RMS_EPS = 1e-6
GRID_W = 64
POS_BASE = 10000.0
HY_BANDS = 16
HY_SIN_FREQ = 1.0
HY_WINDOW_SHIFT = 0.05
GDN_HEADS = 8
GDN_CHUNK = 128
VMEM_LIMIT = 56 * 1024 * 1024
ROW_CHUNK = 16


def _cparams(sem):
    return pltpu.CompilerParams(dimension_semantics=sem, vmem_limit_bytes=VMEM_LIMIT)


def _dot(a, b):
    return jnp.dot(a, b, preferred_element_type=F32)


def _dot_nt(a, b):
    return lax.dot_general(a, b, (((1,), (1,)), ((), ())), preferred_element_type=F32)


def _dot_tn(a, b):
    return lax.dot_general(a, b, (((0,), (0,)), ((), ())), preferred_element_type=F32)


def _rms(x, g):
    ms = jnp.mean(x * x, axis=-1, keepdims=True)
    return x * lax.rsqrt(ms + RMS_EPS) * g


def _silu(x):
    return x * jax.nn.sigmoid(x)


def _for_row_chunks(n_rows, fn):
    def body(c, carry):
        fn(pl.multiple_of(c * ROW_CHUNK, ROW_CHUNK))
        return carry

    lax.fori_loop(0, n_rows // ROW_CHUNK, body, 0, unroll=8)


def _mod_spec(d, tm, g0, rows_per_req):
    tpr = rows_per_req // tm
    return pl.BlockSpec((1, N_MOD, d), lambda i, *_: (g0 + i // tpr, 0, 0))


def _mod_body(c_ref, w_ref, b_ref, o_ref):
    s = _silu(c_ref[...]).astype(BF16)
    o_ref[...] = _dot(s, w_ref[...].astype(BF16)) + b_ref[...]


def _mod_call(cvec, ada_w, ada_b):
    g, d = cvec.shape
    n = ada_w.shape[1]
    tn = 1024
    return pl.pallas_call(
        _mod_body,
        grid=(n // tn,),
        in_specs=[pl.BlockSpec((g, d), lambda j: (0, 0)),
                  pl.BlockSpec((d, tn), lambda j: (0, j)),
                  pl.BlockSpec((1, tn), lambda j: (0, j))],
        out_specs=pl.BlockSpec((g, tn), lambda j: (0, j)),
        out_shape=jax.ShapeDtypeStruct((g, n), F32),
        compiler_params=_cparams(("arbitrary",)),
        name="mod_table",
    )(cvec, ada_w, ada_b.reshape(1, n))


def _ffn_body(*refs, mi, has_pos, tok_per_req):
    refs = list(refs)
    x_ref = refs.pop(0)
    prow_ref, pcol_ref = (refs.pop(0), refs.pop(0)) if has_pos else (None, None)
    mod_ref, npre_ref, npost_ref, wg_ref, wu_ref, wd_ref, o_ref, h_ref = refs
    j = pl.program_id(1)
    tm = x_ref.shape[0]

    def x_rows(r0):
        x = x_ref[pl.ds(r0, ROW_CHUNK), :]
        if not has_pos:
            return x
        t0 = (pl.program_id(0) * tm + r0) % tok_per_req
        prow = jnp.broadcast_to(prow_ref[t0 // GRID_W], (ROW_CHUNK, prow_ref.shape[-1]))
        return x + jnp.concatenate([prow, pcol_ref[(t0 % GRID_W) // ROW_CHUNK]], axis=1)

    @pl.when(j == 0)
    def _():
        shift = mod_ref[0, mi:mi + 1, :]
        gain = npre_ref[...] * (1.0 + mod_ref[0, mi + 1:mi + 2, :])

        def rows_body(r0):
            rows = pl.ds(r0, ROW_CHUNK)
            h_ref[rows, :] = (_rms(x_rows(r0), gain) + shift).astype(BF16)
            o_ref[rows, :] = jnp.zeros((ROW_CHUNK, o_ref.shape[1]), F32)

        _for_row_chunks(tm, rows_body)

    h = h_ref[...]
    g = _dot(h, wg_ref[...])
    u = _dot(h, wu_ref[...])
    a = (_silu(g) * u).astype(BF16)
    o_ref[...] += _dot(a, wd_ref[...])

    @pl.when(j == pl.num_programs(1) - 1)
    def _():
        gain = npost_ref[...] * (0.5 * mod_ref[0, mi + 2:mi + 3, :])
        for r0 in range(0, tm, ROW_CHUNK):
            rows = slice(r0, r0 + ROW_CHUNK)
            o_ref[rows, :] = x_rows(r0) + _rms(o_ref[rows, :], gain)


def _ffn_call(x, mod, npre, npost, wg, wu, wd, wsel, *, mi, g0, rows_per_req, pos=None, tm=1024, tf=512):
    n_rows, d = x.shape
    ff = wg.shape[-1]
    wl, wk = wsel
    const = lambda shape: pl.BlockSpec(shape, lambda i, j: (0, 0))
    in_specs = [pl.BlockSpec((tm, d), lambda i, j: (i, 0))]
    args = [x]
    if pos is not None:
        in_specs += [pl.BlockSpec(t.shape, lambda i, j: (0, 0, 0)) for t in pos]
        args += list(pos)
    in_specs += [_mod_spec(d, tm, g0, rows_per_req),
                 const((1, d)), const((1, d)),
                 pl.BlockSpec((None, None, d, tf), lambda i, j: (wl, wk, 0, j)),
                 pl.BlockSpec((None, None, d, tf), lambda i, j: (wl, wk, 0, j)),
                 pl.BlockSpec((None, None, tf, d), lambda i, j: (wl, wk, j, 0))]
    args += [mod, npre.reshape(1, d), npost.reshape(1, d), wg, wu, wd]
    return pl.pallas_call(
        functools.partial(_ffn_body, mi=mi, has_pos=pos is not None, tok_per_req=rows_per_req),
        grid=(n_rows // tm, ff // tf),
        in_specs=in_specs,
        out_specs=pl.BlockSpec((tm, d), lambda i, j: (i, 0)),
        out_shape=jax.ShapeDtypeStruct((n_rows, d), F32),
        scratch_shapes=[pltpu.VMEM((tm, d), BF16)],
        compiler_params=_cparams(("parallel", "arbitrary")),
        name="ffn",
    )(*args)


def _inproj_body(x_ref, mod_ref, npre_ref, w_ref, wt_ref, o_ref, ot_ref, h_ref, *, mi, n_main):
    j = pl.program_id(1)

    @pl.when(j == 0)
    def _():
        shift = mod_ref[0, mi:mi + 1, :]
        gain = npre_ref[...] * (1.0 + mod_ref[0, mi + 1:mi + 2, :])

        def rows_body(r0):
            rows = pl.ds(r0, ROW_CHUNK)
            h_ref[rows, :] = (_rms(x_ref[rows, :], gain) + shift).astype(BF16)

        _for_row_chunks(x_ref.shape[0], rows_body)

    @pl.when(j < n_main)
    def _():
        o_ref[...] = _dot_nt(h_ref[...], w_ref[...])

    @pl.when(j == n_main)
    def _():
        ot_ref[...] = _dot_nt(h_ref[...], wt_ref[...])


def _inproj_call(x, mod, npre, w_in_t, layer, w_tail_t, *, mi, n_main_cols, g0, rows_per_req, tm=1024, tn=1024):
    m, d = x.shape
    n_main = n_main_cols // tn
    nt = w_tail_t.shape[0]
    jm = lambda j: jnp.minimum(j, n_main - 1)
    return pl.pallas_call(
        functools.partial(_inproj_body, mi=mi, n_main=n_main),
        grid=(m // tm, n_main + 1),
        in_specs=[pl.BlockSpec((tm, d), lambda i, j: (i, 0)),
                  _mod_spec(d, tm, g0, rows_per_req),
                  pl.BlockSpec((1, d), lambda i, j: (0, 0)),
                  pl.BlockSpec((None, tn, d), lambda i, j: (layer, jm(j), 0)),
                  pl.BlockSpec((nt, d), lambda i, j: (0, 0))],
        out_specs=[pl.BlockSpec((tm, tn), lambda i, j: (i, jm(j))),
                   pl.BlockSpec((tm, nt), lambda i, j: (i, 0))],
        out_shape=[jax.ShapeDtypeStruct((m, n_main_cols), F32), jax.ShapeDtypeStruct((m, nt), F32)],
        scratch_shapes=[pltpu.VMEM((tm, d), BF16)],
        compiler_params=_cparams(("parallel", "arbitrary")),
        name="in_proj",
    )(x, mod, npre.reshape(1, d), w_in_t, w_tail_t)


def _conv3_rows(x_ref, w, r0, nr, seq_len):
    zeros = jnp.zeros((8, x_ref.shape[1]), x_ref.dtype)
    top = zeros if r0 % seq_len == 0 else x_ref[r0 - 8:r0, :]
    mid = x_ref[r0:r0 + nr, :]
    bot = zeros if (r0 + nr) % seq_len == 0 else x_ref[r0 + nr:r0 + nr + 8, :]
    return _conv3_slab(top, mid, bot, w)


def _conv3_rows_dyn(x_ref, w, r0, nr, seq_len):
    n = x_ref.shape[0]
    top = x_ref[pl.ds(pl.multiple_of(jnp.maximum(r0 - 8, 0), 8), 8), :]
    mid = x_ref[pl.ds(pl.multiple_of(r0, 8), nr), :]
    bot = x_ref[pl.ds(pl.multiple_of(jnp.minimum(r0 + nr, n - 8), 8), 8), :]
    top = jnp.where(r0 % seq_len == 0, 0.0, top)
    bot = jnp.where((r0 + nr) % seq_len == 0, 0.0, bot)
    return _conv3_slab(top, mid, bot, w)


def _conv3_slab(top, mid, bot, w):
    nr = mid.shape[0]
    slab = jnp.concatenate([top, mid, bot], axis=0)
    prev = pltpu.roll(slab, 1, 0)[8:8 + nr]
    nxt = pltpu.roll(slab, nr + 15, 0)[8:8 + nr]
    return prev * w[0:1, :] + mid * w[1:2, :] + nxt * w[2:3, :]


@functools.lru_cache(maxsize=None)
def _dft_tables_np(n_tok, r):
    big = 2 * n_tok
    k = np.arange(n_tok, dtype=np.int64)
    t = np.arange(n_tok, dtype=np.int64)
    ang = ((k[:, None] * t[None, :]) % big).astype(np.float64) * (2.0 * math.pi / big)
    cosm = np.cos(ang)
    sinm = np.sin(ang)
    sinm[0, :] = 1.0 - 2.0 * (t % 2)
    nj = n_tok // r
    fr = np.concatenate([cosm.reshape(nj, r, n_tok), sinm.reshape(nj, r, n_tok)], axis=1)
    g_full = fr.reshape(big, n_tok).T
    gr = np.concatenate([g_full[:, :n_tok].reshape(nj, r, n_tok), g_full[:, n_tok:].reshape(nj, r, n_tok)], axis=1)
    return np.concatenate([fr, gr], axis=0).astype(np.float32)


def _dft_tables(n_tok, r):
    return jnp.asarray(_dft_tables_np(n_tok, r)).astype(BF16)


def _spectra_body(featt_ref, t_ref, w1t_ref, b1_ref, w2t_ref, b2_ref, w3f_ref, w3b_ref, dec_ref, f_ref,
                  ha_ref, hb_ref, hc_ref, e_ref, o_ref, h_ref, *, n_tok, r):
    j = pl.program_id(1)

    @pl.when((pl.program_id(0) == 0) & (j == 0))
    def _():
        h = jnp.sin(HY_SIN_FREQ * (_dot(w1t_ref[...].astype(BF16), featt_ref[...].astype(BF16)) + b1_ref[...]))
        h = jnp.sin(HY_SIN_FREQ * (_dot(w2t_ref[...].astype(BF16), h.astype(BF16)) + b2_ref[...]))
        h_ref[...] = h.astype(BF16)

    @pl.when(j == 0)
    def _():
        hb16 = h_ref[...]
        window = jnp.exp(-t_ref[...] * jnp.abs(dec_ref[...])) + HY_WINDOW_SHIFT
        fwd = _dot_tn(hb16, w3f_ref[...].astype(BF16)) * window
        bwd = _dot_tn(hb16, w3b_ref[...].astype(BF16)) * window
        row = lax.broadcasted_iota(jnp.int32, bwd.shape, 0)
        bwd = jnp.where(row == 0, 0.0, bwd)
        norm = jnp.sum(jnp.abs(fwd), axis=0, keepdims=True) + jnp.sum(jnp.abs(bwd), axis=0, keepdims=True)
        fwd = fwd / norm
        bwd = bwd / norm
        e_ref[...] = (fwd + bwd).astype(BF16)
        o_ref[...] = (fwd - bwd).astype(BF16)

    fb = f_ref[0]
    e = e_ref[...]
    p = _dot(fb[:r], e)
    q = _dot(fb[r:], o_ref[...])
    k = j * r + lax.broadcasted_iota(jnp.int32, p.shape, 0)
    big = 2.0 * n_tok
    wk = jnp.where(k == 0, 1.0 / big, 2.0 / big)
    ha_ref[...] = wk * p
    hb_ref[...] = jnp.where(k == 0, 0.0, -wk * q)

    @pl.when(j == 0)
    def _():
        hc_ref[...] = _dot(fb[r:r + 8], e)[0:1, :] * (1.0 / big)


def _spectra_call(n_tok, w1, b1, w2, b2, w3, decay, fr, *, r, cn=512):
    f32 = jnp.float32
    order, c = decay.shape
    oc = order * c
    hid = w2.shape[0]
    emb = w1.shape[0]
    embp = 128
    idx = jnp.arange(n_tok, dtype=f32)
    t = idx / (n_tok - 1)
    bands = jnp.arange(1, HY_BANDS + 1, dtype=f32)
    ang = (2.0 * math.pi / n_tok) * idx[:, None] * bands[None, :]
    feats = jnp.concatenate([t[:, None], jnp.cos(ang), jnp.sin(ang)], axis=-1)
    featt = jnp.pad(feats, ((0, 0), (0, embp - emb))).T
    w1t = jnp.pad(w1, ((0, embp - emb), (0, 0))).T
    nj = n_tok // r
    ncb = oc // cn
    out = jax.ShapeDtypeStruct((n_tok, oc), F32)
    const = lambda shape: pl.BlockSpec(shape, lambda cb, j: (0, 0))
    return pl.pallas_call(
        functools.partial(_spectra_body, n_tok=n_tok, r=r),
        grid=(ncb, nj),
        in_specs=[const((embp, n_tok)), const((n_tok, 1)), const((hid, embp)), const((hid, 1)),
                  const((hid, hid)), const((hid, 1)),
                  pl.BlockSpec((hid, cn), lambda cb, j: (0, cb)),
                  pl.BlockSpec((hid, cn), lambda cb, j: (0, ncb + cb)),
                  pl.BlockSpec((1, cn), lambda cb, j: (0, cb)),
                  pl.BlockSpec((1, 2 * r, n_tok), lambda cb, j: (j, 0, 0))],
        out_specs=[pl.BlockSpec((r, cn), lambda cb, j: (j, cb))] * 2 + [pl.BlockSpec((1, cn), lambda cb, j: (0, cb))],
        out_shape=[out, out, jax.ShapeDtypeStruct((1, oc), F32)],
        scratch_shapes=[pltpu.VMEM((n_tok, cn), BF16), pltpu.VMEM((n_tok, cn), BF16),
                        pltpu.VMEM((hid, n_tok), BF16)],
        compiler_params=_cparams(("arbitrary", "arbitrary")),
        name="hyena_spectra",
    )(featt, t[:, None], w1t, b1.reshape(hid, 1), w2.T, b2.reshape(hid, 1), w3, w3, decay.reshape(1, oc), fr)


def _hyena_body(v_ref, x1_ref, x2_ref, wv_ref, wx1_ref, wx2_ref, bias_ref, t_ref, ha_ref, hb_ref, hc_ref,
                o_ref, ub_ref, u2f_ref, z_ref, *, r, nj, seq_len, n_sub):
    t = pl.program_id(2)
    order = t // (2 * nj)
    tt = t % (2 * nj)
    sub = min(128, r)

    @pl.when(t == 0)
    def _():
        w = wv_ref[...]
        for r0 in range(0, n_sub * seq_len, 64):
            ub_ref[0, r0:r0 + 64, :] = _conv3_rows(v_ref, w, r0, 64, seq_len).astype(BF16)

    @pl.when(tt < nj)
    def _():
        tb = t_ref[0]
        ha, hb = ha_ref[...], hb_ref[...]
        nyq = (tt * r + lax.broadcasted_iota(jnp.int32, ha.shape, 0)) == 0
        hc = jnp.where(nyq, hc_ref[...], ha)
        z0 = pl.multiple_of(tt * (2 * r), 2 * r)
        for s in range(n_sub):
            x = _dot(tb, ub_ref[order, s * seq_len:(s + 1) * seq_len, :])
            p = x[:r]
            q = x[r:]
            z_ref[s, pl.ds(z0, 2 * r), :] = jnp.concatenate([p * ha + q * hb, q * hc - p * hb],
                                                             axis=0).astype(BF16)

    def inverse_rows(s):
        tb = t_ref[0]
        return _dot(tb[:r], z_ref[s, :seq_len, :]) + _dot(tb[r:], z_ref[s, seq_len:, :])

    r_blk = (tt - nj) * r

    @pl.when((tt >= nj) & (order == 0))
    def _():
        wv, wx1 = wv_ref[...], wx1_ref[...]
        for s in range(n_sub):
            y = inverse_rows(s)
            for c0 in range(0, r, sub):
                r0 = s * seq_len + r_blk + c0
                u1 = _conv3_rows_dyn(v_ref, wv, r0, sub, seq_len)
                z1 = _conv3_rows_dyn(x1_ref, wx1, r0, sub, seq_len) * (y[c0:c0 + sub] + u1 * bias_ref[0:1, :])
                u2f_ref[pl.ds(pl.multiple_of(r0, 8), sub), :] = z1
                ub_ref[1, pl.ds(pl.multiple_of(r0, 8), sub), :] = z1.astype(BF16)

    @pl.when((tt >= nj) & (order == 1))
    def _():
        wx2 = wx2_ref[...]
        for s in range(n_sub):
            y = inverse_rows(s)
            for c0 in range(0, r, sub):
                r0 = s * seq_len + r_blk + c0
                u2 = u2f_ref[pl.ds(pl.multiple_of(r0, 8), sub), :]
                o_ref[pl.ds(pl.multiple_of(r0, 8), sub), :] = (
                    _conv3_rows_dyn(x2_ref, wx2, r0, sub, seq_len) * (y[c0:c0 + sub] + u2 * bias_ref[1:2, :]))


def _hyena_call(proj, conv_w, bias, tables, ha, hb, hc, *, n_seq, n_tok, c, r, n_sub, cn=512):
    nj = n_tok // r
    ncb = c // cn
    blk = n_sub * n_tok
    steps = 4 * nj
    seq_spec = lambda off: pl.BlockSpec((blk, cn), lambda b, cb, t: (b, off * ncb + cb),
                                        **({"pipeline_mode": pl.Buffered(1)} if off == 0 else {}))
    w_spec = lambda off: pl.BlockSpec((3, cn), lambda b, cb, t: (0, off * ncb + cb))
    h_spec = pl.BlockSpec((r, cn), lambda b, cb, t: (jnp.minimum(t % (2 * nj), nj - 1), (t // (2 * nj)) * ncb + cb))
    in_specs = [seq_spec(0), seq_spec(1), seq_spec(2), w_spec(0), w_spec(1), w_spec(2),
                pl.BlockSpec((2, cn), lambda b, cb, t: (0, cb)),
                pl.BlockSpec((1, 2 * r, n_tok), lambda b, cb, t: (t % (2 * nj), 0, 0)),
                h_spec, h_spec,
                pl.BlockSpec((1, cn), lambda b, cb, t: (0, (t // (2 * nj)) * ncb + cb))]
    return pl.pallas_call(
        functools.partial(_hyena_body, r=r, nj=nj, seq_len=n_tok, n_sub=n_sub),
        grid=(n_seq // n_sub, ncb, steps),
        in_specs=in_specs,
        out_specs=pl.BlockSpec((blk, cn), lambda b, cb, t: (b, cb)),
        out_shape=jax.ShapeDtypeStruct((n_seq * n_tok, c), F32),
        scratch_shapes=[pltpu.VMEM((2, blk, cn), BF16), pltpu.VMEM((blk, cn), F32),
                        pltpu.VMEM((n_sub, 2 * n_tok, cn), BF16)],
        compiler_params=_cparams(("parallel", "parallel", "arbitrary")),
        name="hyena_conv",
    )(proj, proj, proj, conv_w, conv_w, conv_w, bias, tables, ha, hb, hc)


def _gdn_prep_body(x_ref, w_ref, o_ref, *, heads_per_blk, dk, n_qk_blk, rows, seq_len):
    cb = pl.program_id(1)
    n = x_ref.shape[0]
    w = w_ref[...]
    is_v = cb >= n_qk_blk
    scale = jnp.where(cb < n_qk_blk // 2, dk ** -0.5, 1.0)
    for r0 in range(0, n, rows):
        y = _silu(_conv3_rows(x_ref, w, r0, rows, seq_len))
        for h in range(heads_per_blk):
            yh = y[:, h * dk:(h + 1) * dk]
            inv = lax.rsqrt(jnp.sum(yh * yh, axis=-1, keepdims=True) + RMS_EPS) * scale
            o_ref[r0:r0 + rows, h * dk:(h + 1) * dk] = yh * jnp.where(is_v, 1.0, inv)


def _gdn_prep_call(proj, conv_w, *, n_seq, n_tok, col0, d_gdn, dk, cn=512, blk_rows=2048):
    ncb = 3 * d_gdn // cn
    cb0 = col0 // cn
    rows = n_seq * n_tok
    blk_rows = min(blk_rows, rows)
    return pl.pallas_call(
        functools.partial(_gdn_prep_body, heads_per_blk=cn // dk, dk=dk, n_qk_blk=2 * d_gdn // cn,
                          rows=min(64, n_tok), seq_len=n_tok),
        grid=(rows // blk_rows, ncb),
        in_specs=[pl.BlockSpec((blk_rows, cn), lambda b, cb: (b, cb0 + cb)),
                  pl.BlockSpec((3, cn), lambda b, cb: (0, cb))],
        out_specs=pl.BlockSpec((blk_rows, cn), lambda b, cb: (b, cb)),
        out_shape=jax.ShapeDtypeStruct((rows, 3 * d_gdn), F32),
        compiler_params=_cparams(("parallel", "parallel")),
        name="gdn_prep",
    )(proj, conv_w)


def _split3(x):
    hi = x.astype(BF16)
    r1 = x - hi.astype(F32)
    mid = r1.astype(BF16)
    lo = (r1 - mid.astype(F32)).astype(BF16)
    return hi, mid, lo


def _gdn_body(*refs, heads, dk, chunk, has_s0):
    (qf_ref, kf_ref, vf_ref, abf_ref, abtf_ref, qb_ref, kb_ref, vb_ref, abb_ref, abtb_ref,
     av_ref, dtv_ref, avt_ref, dtvt_ref) = refs[:14]
    s0_ref = refs[14] if has_s0 else None
    of_ref, ob_ref, sfin_ref, s_ref = refs[-4:]
    n = pl.program_id(1)

    @pl.when(n == 0)
    def _():
        s_ref[...] = s0_ref[0] if has_s0 else jnp.zeros_like(s_ref)

    ci = lax.broadcasted_iota(jnp.int32, (chunk, chunk), 0)
    cj = lax.broadcasted_iota(jnp.int32, (chunk, chunk), 1)
    eye = (ci == cj).astype(F32)
    diag_blk = (ci >> 3) == (cj >> 3)
    merge_blks = [((ci >> s) ^ (cj >> s)) == 1 for s in range(3, int(math.log2(chunk)))]
    lower = cj <= ci
    upper = cj >= ci

    units = []
    for dr, (q_ref, k_ref, v_ref, ab_ref, abt_ref, o_ref) in enumerate(
            ((qf_ref, kf_ref, vf_ref, abf_ref, abtf_ref, of_ref),
             (qb_ref, kb_ref, vb_ref, abb_ref, abtb_ref, ob_ref))):
        incl = upper if dr else lower
        strict = (cj > ci) if dr else (cj < ci)
        ones_incl = jnp.where(incl, 1.0, 0.0).astype(BF16)
        ones_incl_t = jnp.where(lower if dr else upper, 1.0, 0.0).astype(BF16)
        ab = ab_ref[...]
        g_col = -av_ref[...] * jax.nn.softplus(ab + dtv_ref[...])
        beta_col = jax.nn.sigmoid(ab)
        abt = abt_ref[...]
        g_row = -avt_ref[...] * jax.nn.softplus(abt + dtvt_ref[...])
        h3 = _split3(g_col)
        gc_col = _dot(ones_incl, h3[0]) + _dot(ones_incl, h3[1]) + _dot(ones_incl, h3[2])
        r3 = _split3(g_row)
        gc_row = _dot(r3[0], ones_incl_t) + _dot(r3[1], ones_incl_t) + _dot(r3[2], ones_incl_t)
        gl_all = jnp.sum(g_col, axis=0, keepdims=True)
        for h in range(heads):
            cg = dr * 2 * heads + h
            cb = cg + heads
            sl = slice(h * dk, (h + 1) * dk)
            units.append(dict(dr=dr, h=h, sl=sl, q_ref=q_ref, k_ref=k_ref, v_ref=v_ref, o_ref=o_ref,
                              incl=incl, strict=strict, gc=gc_col[:, cg:cg + 1], gr=gc_row[cg:cg + 1, :],
                              beta=beta_col[:, cb:cb + 1], gl=gl_all[:, cg:cg + 1]))

    def stage(fn):
        return [fn(u) for u in units]

    bf = lambda t: t.astype(BF16)
    k16 = stage(lambda u: bf(u["k_ref"][:, u["sl"]]))
    decay = stage(lambda u: jnp.where(u["incl"], jnp.exp(jnp.where(u["incl"], u["gc"] - u["gr"], 0.0)), 0.0))
    kbeta = stage(lambda u: u["k_ref"][:, u["sl"]] * u["beta"])
    kq = [_dot_nt(jnp.concatenate([bf(kb), bf(u["q_ref"][:, u["sl"]])], axis=0), k)
          for u, kb, k in zip(units, kbeta, k16)]
    a_low = [jnp.where(u["strict"], t[:chunk] * dc, 0.0) for u, t, dc in zip(units, kq, decay)]
    attn = [bf(t[chunk:] * dc) for t, dc in zip(kq, decay)]

    d = [jnp.where(diag_blk, a, 0.0) for a in a_low]
    d2 = [_dot(bf(t), bf(t)) for t in d]
    x = [eye - t for t in d]
    x = [t + _dot(bf(t), bf(p)) for t, p in zip(x, d2)]
    d4 = [_dot(bf(p), bf(p)) for p in d2]
    x = [t + _dot(bf(t), bf(p)) for t, p in zip(x, d4)]
    for off_blk in merge_blks:
        x16 = [bf(t) for t in x]
        tmp = [_dot(bf(jnp.where(off_blk, a, 0.0)), t16) for a, t16 in zip(a_low, x16)]
        x = [t - _dot(t16, bf(m)) for t, t16, m in zip(x, x16, tmp)]

    e_gc = stage(lambda u: jnp.exp(u["gc"]))
    rhs = [bf(jnp.concatenate([u["v_ref"][:, u["sl"]] * u["beta"], kb * e], axis=-1))
           for u, kb, e in zip(units, kbeta, e_gc)]
    sol = [_dot(bf(t), r) for t, r in zip(x, rhs)]

    s16 = stage(lambda u: bf(s_ref[u["dr"], u["h"]]))
    ws = [_dot(jnp.concatenate([bf(so[:, dk:]), bf(u["q_ref"][:, u["sl"]] * e)], axis=0), s)
          for u, so, e, s in zip(units, sol, e_gc, s16)]
    vn16 = [bf(so[:, :dk] - t[:chunk]) for so, t in zip(sol, ws)]
    o = [t[chunk:] + _dot(at, vn) for t, at, vn in zip(ws, attn, vn16)]
    for u, t in zip(units, o):
        u["o_ref"][:, u["sl"]] = t
    s_new = [s_ref[u["dr"], u["h"]] * jnp.exp(u["gl"])
             + _dot_tn(bf(u["k_ref"][:, u["sl"]] * jnp.exp(u["gl"] - u["gc"])), vn)
             for u, vn in zip(units, vn16)]
    for u, t in zip(units, s_new):
        s_ref[u["dr"], u["h"]] = t

    @pl.when(n == pl.num_programs(1) - 1)
    def _():
        sfin_ref[0] = s_ref[...]


def _gdn_call(qkv, ab, abt, av, dtv, avt, dtvt, s0, *, n_seq, n_tok, heads, dk, chunk):
    nc = n_tok // chunk
    d = heads * dk
    fwd = lambda b, n: b * nc + n
    bwd = lambda b, n: b * nc + nc - 1 - n
    qkv_spec = lambda idx, which: pl.BlockSpec((chunk, d), lambda b, n: (idx(b, n), which))
    ab_spec = lambda idx: pl.BlockSpec((chunk, 128), lambda b, n: (idx(b, n), 0))
    abt_spec = lambda idx: pl.BlockSpec((4 * heads, chunk), lambda b, n: (0, idx(b, n)))
    small = lambda shape: pl.BlockSpec(shape, lambda b, n: (0, 0))
    st_spec = pl.BlockSpec((1, 2, heads, dk, dk), lambda b, n: (b, 0, 0, 0, 0))
    o_shape = jax.ShapeDtypeStruct((n_seq * n_tok, d), F32)
    in_specs = [qkv_spec(fwd, 0), qkv_spec(fwd, 1), qkv_spec(fwd, 2), ab_spec(fwd), abt_spec(fwd),
                qkv_spec(bwd, 0), qkv_spec(bwd, 1), qkv_spec(bwd, 2), ab_spec(bwd), abt_spec(bwd),
                small((1, 128)), small((1, 128)), small((4 * heads, 1)), small((4 * heads, 1))]
    args = [qkv, qkv, qkv, ab, abt, qkv, qkv, qkv, ab, abt, av, dtv, avt, dtvt]
    if s0 is not None:
        in_specs.append(st_spec)
        args.append(s0)
    return pl.pallas_call(
        functools.partial(_gdn_body, heads=heads, dk=dk, chunk=chunk, has_s0=s0 is not None),
        grid=(n_seq, nc),
        in_specs=in_specs,
        out_specs=[pl.BlockSpec((chunk, d), lambda b, n: (fwd(b, n), 0)),
                   pl.BlockSpec((chunk, d), lambda b, n: (bwd(b, n), 0)),
                   st_spec],
        out_shape=[o_shape, o_shape, jax.ShapeDtypeStruct((n_seq, 2, heads, dk, dk), F32)],
        scratch_shapes=[pltpu.VMEM((2, heads, dk, dk), F32)],
        compiler_params=_cparams(("parallel", "arbitrary")),
        name="gdn_scan",
    )(*args)


def _outproj_body(x_ref, mod_ref, zhy_ref, of_ref, ob_ref, zg_ref, ghy_ref, go_ref, npost_ref, w_ref,
                  o_ref, *, mi, heads, dk):
    yhy = _rms(zhy_ref[...], ghy_ref[...]).astype(BF16)
    o = of_ref[...] + ob_ref[...]
    zg = zg_ref[...]
    parts = [yhy]
    for h in range(heads):
        sl = slice(h * dk, (h + 1) * dk)
        parts.append((_rms(o[:, sl], go_ref[...]) * _silu(zg[:, sl])).astype(BF16))
    y = _dot(jnp.concatenate(parts, axis=-1), w_ref[...])
    gate = mod_ref[0, mi:mi + 1, :]
    o_ref[...] = x_ref[...] + gate * _rms(y, npost_ref[...])


def _outproj_call(x, mod, zhy, o_f, o_b, proj, ghy, go, npost, w, *, mi, zg_col_blk, g0, rows_per_req, heads, dk,
                  tm=512):
    m, d = x.shape
    c = zhy.shape[1]
    dg = heads * dk
    row = lambda width: pl.BlockSpec((tm, width), lambda i: (i, 0))
    return pl.pallas_call(
        functools.partial(_outproj_body, mi=mi, heads=heads, dk=dk),
        grid=(m // tm,),
        in_specs=[row(d),
                  _mod_spec(d, tm, g0, rows_per_req),
                  row(c), row(dg), row(dg),
                  pl.BlockSpec((tm, dg), lambda i: (i, zg_col_blk)),
                  pl.BlockSpec((1, c), lambda i: (0, 0)),
                  pl.BlockSpec((1, dk), lambda i: (0, 0)),
                  pl.BlockSpec((1, d), lambda i: (0, 0)),
                  pl.BlockSpec((c + dg, d), lambda i: (0, 0))],
        out_specs=row(d),
        out_shape=jax.ShapeDtypeStruct((m, d), F32),
        compiler_params=_cparams(("parallel",)),
        name="out_proj",
    )(x, mod, zhy, o_f, o_b, proj, ghy.reshape(1, c), go.reshape(1, dk), npost.reshape(1, d), w)


@functools.lru_cache(maxsize=None)
def _grid_pos_tables_np(n_tok, dim):
    rows = n_tok // GRID_W
    quarter = dim // 4
    omega = 1.0 / (POS_BASE ** (np.arange(quarter, dtype=np.float64) / quarter))
    ar = np.arange(rows, dtype=np.float64)[:, None] * omega[None]
    ac = np.arange(GRID_W, dtype=np.float64)[:, None] * omega[None]
    row_t = np.concatenate([np.sin(ar), np.cos(ar)], axis=-1).astype(np.float32)
    col_t = np.concatenate([np.sin(ac), np.cos(ac)], axis=-1).astype(np.float32)
    return row_t.reshape(rows, 1, dim // 2), col_t.reshape(GRID_W // ROW_CHUNK, ROW_CHUNK, dim // 2)


def _mixer(proj, proj_ab, s0, *, n_seq, n_tok, hy_conv_w, hy_f, hy_decay, hy_bias, gdn_conv_w, gdn_a_log,
           gdn_dt_bias, c_hy, d_gdn):
    heads = GDN_HEADS
    dk = d_gdn // heads
    o0 = 3 * c_hy

    a_exp = jnp.exp(gdn_a_log.astype(F32))
    zeros = jnp.zeros_like(a_exp)
    av32 = jnp.stack([a_exp, zeros], axis=1).reshape(-1)
    dtv32 = jnp.stack([gdn_dt_bias.astype(F32), zeros], axis=1).reshape(-1)
    pad = 128 - 4 * heads
    av = jnp.pad(av32, (0, pad)).reshape(1, 128)
    dtv = jnp.pad(dtv32, (0, pad)).reshape(1, 128)
    avt = av32.reshape(4 * heads, 1)
    dtvt = dtv32.reshape(4 * heads, 1)
    abt = jnp.transpose(proj_ab[:, :4 * heads])

    r = min(512, n_tok)
    tables = _dft_tables(n_tok, r)
    ha, hb, hc = _spectra_call(n_tok, *hy_f, hy_decay, tables, r=r)
    zhy = _hyena_call(proj, hy_conv_w, hy_bias, tables, ha, hb, hc, n_seq=n_seq, n_tok=n_tok, c=c_hy, r=r,
                      n_sub=max(1, min(n_seq, 2048 // n_tok)))
    qkv = _gdn_prep_call(proj, gdn_conv_w, n_seq=n_seq, n_tok=n_tok, col0=o0, d_gdn=d_gdn, dk=dk)
    o_f, o_b, s_fin = _gdn_call(qkv, proj_ab, abt, av, dtv, avt, dtvt, s0, n_seq=n_seq, n_tok=n_tok, heads=heads,
                                dk=dk, chunk=GDN_CHUNK)
    return zhy, o_f, o_b, s_fin


def kernel(x_prompt, x_sample, state_gdn, c, c_ctx, ada_w, ada_b, norm_pre, norm_post, ffn_wg, ffn_wu, ffn_wd,
           w_in, w_out, hy_conv_w, hy_f_w1, hy_f_b1, hy_f_w2, hy_f_b2, hy_f_w3, hy_decay, hy_bias, hy_out_norm,
           gdn_conv_w, gdn_a_log, gdn_dt_bias, gdn_o_norm):
    nb, seq, d = x_prompt.shape
    db, dseq, _ = x_sample.shape
    depth = ada_w.shape[0]
    c_hy = hy_decay.shape[-1]
    d_gdn = gdn_conv_w.shape[-1] // 3
    heads = GDN_HEADS
    dk = d_gdn // heads
    n_ctx_rows = nb * seq
    n_lat_rows = db * dseq
    pos = tuple(jnp.asarray(t).astype(x_sample.dtype) for t in _grid_pos_tables_np(dseq, d))
    cvec = jnp.concatenate([c_ctx[None], c, jnp.zeros((8 - 1 - db, d), F32)], axis=0)
    wg, wu, wd = ffn_wg.astype(BF16), ffn_wu.astype(BF16), ffn_wd.astype(BF16)
    n_main_cols = 3 * c_hy + 4 * d_gdn
    n_tail = w_in.shape[-1] - n_main_cols
    w_in_t = jnp.swapaxes(w_in, 1, 2).astype(BF16)

    xs = [x_prompt.reshape(n_ctx_rows, d), x_sample.reshape(n_lat_rows, d)]
    groups = [dict(g0=0, rows_per_req=n_ctx_rows), dict(g0=1, rows_per_req=dseq)]
    shapes = [(nb, seq), (db, dseq)]
    ctx_states = []
    for l in range(depth):
        mod = _mod_call(cvec, ada_w[l], ada_b[l]).reshape(8, N_MOD, d)
        w_tail_t = jnp.pad(w_in_t[l, n_main_cols:], ((0, 128 - n_tail), (0, 0)))
        w_out_l = w_out[l].astype(BF16)
        hy_f = (hy_f_w1[l], hy_f_b1[l], hy_f_w2[l], hy_f_b2[l], hy_f_w3[l])
        states = [None, state_gdn[:, l].astype(F32)]
        for gi, (grp, (n_seq, n_tok)) in enumerate(zip(groups, shapes)):
            x = _ffn_call(xs[gi], mod, norm_pre[l, 0], norm_post[l, 0], wg, wu, wd, (l, 0), mi=0,
                          pos=pos if (gi == 1 and l == 0) else None, **grp)
            proj, proj_ab = _inproj_call(x, mod, norm_pre[l, 1], w_in_t, l, w_tail_t, mi=3,
                                         n_main_cols=n_main_cols, **grp)
            zhy, o_f, o_b, s_fin = _mixer(
                proj, proj_ab, states[gi], n_seq=n_seq, n_tok=n_tok, hy_conv_w=hy_conv_w[l], hy_f=hy_f,
                hy_decay=hy_decay[l], hy_bias=hy_bias[l], gdn_conv_w=gdn_conv_w[l], gdn_a_log=gdn_a_log[l],
                gdn_dt_bias=gdn_dt_bias[l], c_hy=c_hy, d_gdn=d_gdn)
            if gi == 0:
                ctx_states.append(s_fin)
            x = _outproj_call(x, mod, zhy, o_f, o_b, proj, hy_out_norm[l], gdn_o_norm[l], norm_post[l, 1], w_out_l,
                              mi=5, zg_col_blk=(3 * c_hy + 3 * d_gdn) // d_gdn, heads=heads, dk=dk, **grp)
            xs[gi] = _ffn_call(x, mod, norm_pre[l, 2], norm_post[l, 2], wg, wu, wd, (l, 1), mi=6, **grp)

    new_state = ctx_states[0][:, None] if depth == 1 else jnp.stack(ctx_states, axis=1)
    return xs[0].reshape(nb, seq, d), xs[1].reshape(db, dseq, d), new_state
```

```python
import functools
import math

import numpy as np
import jax
import jax.numpy as jnp
from jax import lax
from jax.experimental import pallas as pl
from jax.experimental.pallas import tpu as pltpu

F32 = jnp.float32
BF16 = jnp.bfloat16

N_MOD = 9
RMS_EPS = 1e-6
GRID_W = 64
POS_BASE = 10000.0
HY_BANDS = 16
HY_SIN_FREQ = 1.0
HY_WINDOW_SHIFT = 0.05
GDN_HEADS = 8
GDN_CHUNK = 128
LANES = 128
SUBLANES = 8
VMEM_LIMIT = 56 * 1024 * 1024
ROW_CHUNK = 16
CONV_ROWS = 64
GATE_ROWS = 128


def _cparams(sem):
    return pltpu.CompilerParams(dimension_semantics=sem, vmem_limit_bytes=VMEM_LIMIT)


def _dot(a, b):
    return jnp.dot(a, b, preferred_element_type=F32)


def _dot_nt(a, b):
    return lax.dot_general(a, b, (((1,), (1,)), ((), ())), preferred_element_type=F32)


def _dot_tn(a, b):
    return lax.dot_general(a, b, (((0,), (0,)), ((), ())), preferred_element_type=F32)


def _rms(x, g):
    ms = jnp.mean(x * x, axis=-1, keepdims=True)
    return x * lax.rsqrt(ms + RMS_EPS) * g


def _silu(x):
    return x * jax.nn.sigmoid(x)


def _for_row_chunks(n_rows, fn):
    def body(c, carry):
        fn(pl.multiple_of(c * ROW_CHUNK, ROW_CHUNK))
        return carry

    lax.fori_loop(0, n_rows // ROW_CHUNK, body, 0, unroll=8)


def _mod_spec(d, tm, g0, rows_per_req):
    tpr = rows_per_req // tm
    return pl.BlockSpec((1, N_MOD, d), lambda i, *_: (g0 + i // tpr, 0, 0))


def _mod_body(c_ref, w_ref, b_ref, o_ref):
    s = _silu(c_ref[...]).astype(BF16)
    o_ref[...] = _dot(s, w_ref[...].astype(BF16)) + b_ref[...]


def _mod_call(cvec, ada_w, ada_b):
    g, d = cvec.shape
    n = ada_w.shape[1]
    tn = 1024
    return pl.pallas_call(
        _mod_body,
        grid=(n // tn,),
        in_specs=[pl.BlockSpec((g, d), lambda j: (0, 0)),
                  pl.BlockSpec((d, tn), lambda j: (0, j)),
                  pl.BlockSpec((1, tn), lambda j: (0, j))],
        out_specs=pl.BlockSpec((g, tn), lambda j: (0, j)),
        out_shape=jax.ShapeDtypeStruct((g, n), F32),
        compiler_params=_cparams(("arbitrary",)),
        name="mod_table",
    )(cvec, ada_w, ada_b.reshape(1, n))


def _ffn_body(*refs, mi, has_pos, tok_per_req):
    refs = list(refs)
    x_ref = refs.pop(0)
    prow_ref, pcol_ref = (refs.pop(0), refs.pop(0)) if has_pos else (None, None)
    mod_ref, npre_ref, npost_ref, wg_ref, wu_ref, wd_ref, o_ref, h_ref = refs
    j = pl.program_id(1)
    tm = x_ref.shape[0]

    def x_rows(r0):
        x = x_ref[pl.ds(r0, ROW_CHUNK), :]
        if not has_pos:
            return x
        t0 = (pl.program_id(0) * tm + r0) % tok_per_req
        prow = jnp.broadcast_to(prow_ref[t0 // GRID_W], (ROW_CHUNK, prow_ref.shape[-1]))
        return x + jnp.concatenate([prow, pcol_ref[(t0 % GRID_W) // ROW_CHUNK]], axis=1)

    @pl.when(j == 0)
    def _():
        shift = mod_ref[0, mi:mi + 1, :]
        gain = npre_ref[...] * (1.0 + mod_ref[0, mi + 1:mi + 2, :])

        def rows_body(r0):
            rows = pl.ds(r0, ROW_CHUNK)
            h_ref[rows, :] = (_rms(x_rows(r0), gain) + shift).astype(BF16)
            o_ref[rows, :] = jnp.zeros((ROW_CHUNK, o_ref.shape[1]), F32)

        _for_row_chunks(tm, rows_body)

    h = h_ref[...]
    g = _dot(h, wg_ref[...])
    u = _dot(h, wu_ref[...])
    a = (_silu(g) * u).astype(BF16)
    o_ref[...] += _dot(a, wd_ref[...])

    @pl.when(j == pl.num_programs(1) - 1)
    def _():
        gain = npost_ref[...] * (0.5 * mod_ref[0, mi + 2:mi + 3, :])
        for r0 in range(0, tm, ROW_CHUNK):
            rows = slice(r0, r0 + ROW_CHUNK)
            o_ref[rows, :] = x_rows(r0) + _rms(o_ref[rows, :], gain)


def _ffn_call(x, mod, npre, npost, wg, wu, wd, wsel, *, mi, g0, rows_per_req, pos=None, tm=1024, tf=512):
    n_rows, d = x.shape
    ff = wg.shape[-1]
    wl, wk = wsel
    assert n_rows % tm == 0 and rows_per_req % tm == 0 and ff % tf == 0 and tm % ROW_CHUNK == 0
    assert pos is None or (GRID_W % ROW_CHUNK == 0 and rows_per_req % GRID_W == 0)
    const = lambda shape: pl.BlockSpec(shape, lambda i, j: (0, 0))
    in_specs = [pl.BlockSpec((tm, d), lambda i, j: (i, 0))]
    args = [x]
    if pos is not None:
        in_specs += [pl.BlockSpec(t.shape, lambda i, j: (0, 0, 0)) for t in pos]
        args += list(pos)
    in_specs += [_mod_spec(d, tm, g0, rows_per_req),
                 const((1, d)), const((1, d)),
                 pl.BlockSpec((None, None, d, tf), lambda i, j: (wl, wk, 0, j)),
                 pl.BlockSpec((None, None, d, tf), lambda i, j: (wl, wk, 0, j)),
                 pl.BlockSpec((None, None, tf, d), lambda i, j: (wl, wk, j, 0))]
    args += [mod, npre.reshape(1, d), npost.reshape(1, d), wg, wu, wd]
    return pl.pallas_call(
        functools.partial(_ffn_body, mi=mi, has_pos=pos is not None, tok_per_req=rows_per_req),
        grid=(n_rows // tm, ff // tf),
        in_specs=in_specs,
        out_specs=pl.BlockSpec((tm, d), lambda i, j: (i, 0)),
        out_shape=jax.ShapeDtypeStruct((n_rows, d), F32),
        scratch_shapes=[pltpu.VMEM((tm, d), BF16)],
        compiler_params=_cparams(("parallel", "arbitrary")),
        name="ffn",
    )(*args)


def _inproj_body(x_ref, mod_ref, npre_ref, w_ref, wt_ref, o_ref, ot_ref, h_ref, *, mi, n_main):
    j = pl.program_id(1)

    @pl.when(j == 0)
    def _():
        shift = mod_ref[0, mi:mi + 1, :]
        gain = npre_ref[...] * (1.0 + mod_ref[0, mi + 1:mi + 2, :])

        def rows_body(r0):
            rows = pl.ds(r0, ROW_CHUNK)
            h_ref[rows, :] = (_rms(x_ref[rows, :], gain) + shift).astype(BF16)

        _for_row_chunks(x_ref.shape[0], rows_body)

    @pl.when(j < n_main)
    def _():
        o_ref[...] = _dot_nt(h_ref[...], w_ref[...])

    @pl.when(j == n_main)
    def _():
        ot_ref[...] = _dot_nt(h_ref[...], wt_ref[...])


def _inproj_call(x, mod, npre, w_in_t, layer, w_tail_t, *, mi, n_main_cols, g0, rows_per_req, tm=1024, tn=1024):
    m, d = x.shape
    n_main = n_main_cols // tn
    nt = w_tail_t.shape[0]
    assert m % tm == 0 and rows_per_req % tm == 0 and n_main_cols % tn == 0 and nt % LANES == 0
    jm = lambda j: jnp.minimum(j, n_main - 1)
    return pl.pallas_call(
        functools.partial(_inproj_body, mi=mi, n_main=n_main),
        grid=(m // tm, n_main + 1),
        in_specs=[pl.BlockSpec((tm, d), lambda i, j: (i, 0)),
                  _mod_spec(d, tm, g0, rows_per_req),
                  pl.BlockSpec((1, d), lambda i, j: (0, 0)),
                  pl.BlockSpec((None, tn, d), lambda i, j: (layer, jm(j), 0)),
                  pl.BlockSpec((nt, d), lambda i, j: (0, 0))],
        out_specs=[pl.BlockSpec((tm, tn), lambda i, j: (i, jm(j))),
                   pl.BlockSpec((tm, nt), lambda i, j: (i, 0))],
        out_shape=[jax.ShapeDtypeStruct((m, n_main_cols), F32), jax.ShapeDtypeStruct((m, nt), F32)],
        scratch_shapes=[pltpu.VMEM((tm, d), BF16)],
        compiler_params=_cparams(("parallel", "arbitrary")),
        name="in_proj",
    )(x, mod, npre.reshape(1, d), w_in_t, w_tail_t)


def _conv3_rows(x_ref, w, r0, nr, seq_len):
    zeros = jnp.zeros((SUBLANES, x_ref.shape[1]), x_ref.dtype)
    top = zeros if r0 % seq_len == 0 else x_ref[r0 - SUBLANES:r0, :]
    mid = x_ref[r0:r0 + nr, :]
    bot = zeros if (r0 + nr) % seq_len == 0 else x_ref[r0 + nr:r0 + nr + SUBLANES, :]
    return _conv3_slab(top, mid, bot, w)


def _conv3_rows_dyn(x_ref, w, r0, nr, seq_len):
    n = x_ref.shape[0]
    top = x_ref[pl.ds(pl.multiple_of(jnp.maximum(r0 - SUBLANES, 0), SUBLANES), SUBLANES), :]
    mid = x_ref[pl.ds(pl.multiple_of(r0, SUBLANES), nr), :]
    bot = x_ref[pl.ds(pl.multiple_of(jnp.minimum(r0 + nr, n - SUBLANES), SUBLANES), SUBLANES), :]
    top = jnp.where(r0 % seq_len == 0, 0.0, top)
    bot = jnp.where((r0 + nr) % seq_len == 0, 0.0, bot)
    return _conv3_slab(top, mid, bot, w)


def _conv3_slab(top, mid, bot, w):
    nr = mid.shape[0]
    slab = jnp.concatenate([top, mid, bot], axis=0)
    prev = pltpu.roll(slab, 1, 0)[SUBLANES:SUBLANES + nr]
    nxt = pltpu.roll(slab, nr + 2 * SUBLANES - 1, 0)[SUBLANES:SUBLANES + nr]
    return prev * w[0:1, :] + mid * w[1:2, :] + nxt * w[2:3, :]


@functools.lru_cache(maxsize=None)
def _dft_tables_np(n_tok, r):
    big = 2 * n_tok
    k = np.arange(n_tok, dtype=np.int64)
    t = np.arange(n_tok, dtype=np.int64)
    ang = ((k[:, None] * t[None, :]) % big).astype(np.float64) * (2.0 * math.pi / big)
    cosm = np.cos(ang)
    sinm = np.sin(ang)
    sinm[0, :] = 1.0 - 2.0 * (t % 2)
    nj = n_tok // r
    fr = np.concatenate([cosm.reshape(nj, r, n_tok), sinm.reshape(nj, r, n_tok)], axis=1)
    g_full = fr.reshape(big, n_tok).T
    gr = np.concatenate([g_full[:, :n_tok].reshape(nj, r, n_tok), g_full[:, n_tok:].reshape(nj, r, n_tok)], axis=1)
    return np.concatenate([fr, gr], axis=0).astype(np.float32)


def _dft_tables(n_tok, r):
    return jnp.asarray(_dft_tables_np(n_tok, r)).astype(BF16)


def _spectra_body(featt_ref, t_ref, w1t_ref, b1_ref, w2t_ref, b2_ref, w3f_ref, w3b_ref, dec_ref, f_ref,
                  ha_ref, hb_ref, hc_ref, e_ref, o_ref, h_ref, *, n_tok, r):
    j = pl.program_id(1)

    @pl.when((pl.program_id(0) == 0) & (j == 0))
    def _():
        h = jnp.sin(HY_SIN_FREQ * (_dot(w1t_ref[...].astype(BF16), featt_ref[...].astype(BF16)) + b1_ref[...]))
        h = jnp.sin(HY_SIN_FREQ * (_dot(w2t_ref[...].astype(BF16), h.astype(BF16)) + b2_ref[...]))
        h_ref[...] = h.astype(BF16)

    @pl.when(j == 0)
    def _():
        hb16 = h_ref[...]
        window = jnp.exp(-t_ref[...] * jnp.abs(dec_ref[...])) + HY_WINDOW_SHIFT
        fwd = _dot_tn(hb16, w3f_ref[...].astype(BF16)) * window
        bwd = _dot_tn(hb16, w3b_ref[...].astype(BF16)) * window
        row = lax.broadcasted_iota(jnp.int32, bwd.shape, 0)
        bwd = jnp.where(row == 0, 0.0, bwd)
        norm = jnp.sum(jnp.abs(fwd), axis=0, keepdims=True) + jnp.sum(jnp.abs(bwd), axis=0, keepdims=True)
        fwd = fwd / norm
        bwd = bwd / norm
        e_ref[...] = (fwd + bwd).astype(BF16)
        o_ref[...] = (fwd - bwd).astype(BF16)

    fb = f_ref[0]
    e = e_ref[...]
    p = _dot(fb[:r], e)
    q = _dot(fb[r:], o_ref[...])
    k = j * r + lax.broadcasted_iota(jnp.int32, p.shape, 0)
    big = 2.0 * n_tok
    wk = jnp.where(k == 0, 1.0 / big, 2.0 / big)
    ha_ref[...] = wk * p
    hb_ref[...] = jnp.where(k == 0, 0.0, -wk * q)

    @pl.when(j == 0)
    def _():
        hc_ref[...] = _dot(fb[r:r + SUBLANES], e)[0:1, :] * (1.0 / big)


def _spectra_call(n_tok, w1, b1, w2, b2, w3, decay, fr, *, r, cn=512):
    f32 = jnp.float32
    order, c = decay.shape
    oc = order * c
    hid = w2.shape[0]
    emb = w1.shape[0]
    embp = LANES
    idx = jnp.arange(n_tok, dtype=f32)
    t = idx / (n_tok - 1)
    bands = jnp.arange(1, HY_BANDS + 1, dtype=f32)
    ang = (2.0 * math.pi / n_tok) * idx[:, None] * bands[None, :]
    feats = jnp.concatenate([t[:, None], jnp.cos(ang), jnp.sin(ang)], axis=-1)
    featt = jnp.pad(feats, ((0, 0), (0, embp - emb))).T
    w1t = jnp.pad(w1, ((0, embp - emb), (0, 0))).T
    nj = n_tok // r
    ncb = oc // cn
    out = jax.ShapeDtypeStruct((n_tok, oc), F32)
    const = lambda shape: pl.BlockSpec(shape, lambda cb, j: (0, 0))
    return pl.pallas_call(
        functools.partial(_spectra_body, n_tok=n_tok, r=r),
        grid=(ncb, nj),
        in_specs=[const((embp, n_tok)), const((n_tok, 1)), const((hid, embp)), const((hid, 1)),
                  const((hid, hid)), const((hid, 1)),
                  pl.BlockSpec((hid, cn), lambda cb, j: (0, cb)),
                  pl.BlockSpec((hid, cn), lambda cb, j: (0, ncb + cb)),
                  pl.BlockSpec((1, cn), lambda cb, j: (0, cb)),
                  pl.BlockSpec((1, 2 * r, n_tok), lambda cb, j: (j, 0, 0))],
        out_specs=[pl.BlockSpec((r, cn), lambda cb, j: (j, cb))] * 2 + [pl.BlockSpec((1, cn), lambda cb, j: (0, cb))],
        out_shape=[out, out, jax.ShapeDtypeStruct((1, oc), F32)],
        scratch_shapes=[pltpu.VMEM((n_tok, cn), BF16), pltpu.VMEM((n_tok, cn), BF16),
                        pltpu.VMEM((hid, n_tok), BF16)],
        compiler_params=_cparams(("arbitrary", "arbitrary")),
        name="hyena_spectra",
    )(featt, t[:, None], w1t, b1.reshape(hid, 1), w2.T, b2.reshape(hid, 1), w3, w3, decay.reshape(1, oc), fr)


def _hyena_body(v_ref, x1_ref, x2_ref, wv_ref, wx1_ref, wx2_ref, bias_ref, t_ref, ha_ref, hb_ref, hc_ref,
                o_ref, ub_ref, u2f_ref, z_ref, *, r, nj, seq_len, n_sub):
    t = pl.program_id(2)
    order = t // (2 * nj)
    tt = t % (2 * nj)
    sub = min(GATE_ROWS, r)

    @pl.when(t == 0)
    def _():
        w = wv_ref[...]
        for r0 in range(0, n_sub * seq_len, CONV_ROWS):
            ub_ref[0, r0:r0 + CONV_ROWS, :] = _conv3_rows(v_ref, w, r0, CONV_ROWS, seq_len).astype(BF16)

    @pl.when(tt < nj)
    def _():
        tb = t_ref[0]
        ha, hb = ha_ref[...], hb_ref[...]
        nyq = (tt * r + lax.broadcasted_iota(jnp.int32, ha.shape, 0)) == 0
        hc = jnp.where(nyq, hc_ref[...], ha)
        z0 = pl.multiple_of(tt * (2 * r), 2 * r)
        for s in range(n_sub):
            x = _dot(tb, ub_ref[order, s * seq_len:(s + 1) * seq_len, :])
            p = x[:r]
            q = x[r:]
            z_ref[s, pl.ds(z0, 2 * r), :] = jnp.concatenate([p * ha + q * hb, q * hc - p * hb],
                                                             axis=0).astype(BF16)

    def inverse_rows(s):
        tb = t_ref[0]
        return _dot(tb[:r], z_ref[s, :seq_len, :]) + _dot(tb[r:], z_ref[s, seq_len:, :])

    r_blk = (tt - nj) * r

    @pl.when((tt >= nj) & (order == 0))
    def _():
        wv, wx1 = wv_ref[...], wx1_ref[...]
        for s in range(n_sub):
            y = inverse_rows(s)
            for c0 in range(0, r, sub):
                r0 = s * seq_len + r_blk + c0
                u1 = _conv3_rows_dyn(v_ref, wv, r0, sub, seq_len)
                z1 = _conv3_rows_dyn(x1_ref, wx1, r0, sub, seq_len) * (y[c0:c0 + sub] + u1 * bias_ref[0:1, :])
                u2f_ref[pl.ds(pl.multiple_of(r0, SUBLANES), sub), :] = z1
                ub_ref[1, pl.ds(pl.multiple_of(r0, SUBLANES), sub), :] = z1.astype(BF16)

    @pl.when((tt >= nj) & (order == 1))
    def _():
        wx2 = wx2_ref[...]
        for s in range(n_sub):
            y = inverse_rows(s)
            for c0 in range(0, r, sub):
                r0 = s * seq_len + r_blk + c0
                u2 = u2f_ref[pl.ds(pl.multiple_of(r0, SUBLANES), sub), :]
                o_ref[pl.ds(pl.multiple_of(r0, SUBLANES), sub), :] = (
                    _conv3_rows_dyn(x2_ref, wx2, r0, sub, seq_len) * (y[c0:c0 + sub] + u2 * bias_ref[1:2, :]))


def _hyena_call(proj, conv_w, bias, tables, ha, hb, hc, *, n_seq, n_tok, c, r, n_sub, cn=512):
    nj = n_tok // r
    ncb = c // cn
    blk = n_sub * n_tok
    steps = 4 * nj
    assert n_tok % r == 0 and c % cn == 0 and n_seq % n_sub == 0
    assert r % min(GATE_ROWS, r) == 0 and n_tok % CONV_ROWS == 0
    seq_spec = lambda off: pl.BlockSpec((blk, cn), lambda b, cb, t: (b, off * ncb + cb),
                                        **({"pipeline_mode": pl.Buffered(1)} if off == 0 else {}))
    w_spec = lambda off: pl.BlockSpec((3, cn), lambda b, cb, t: (0, off * ncb + cb))
    h_spec = pl.BlockSpec((r, cn), lambda b, cb, t: (jnp.minimum(t % (2 * nj), nj - 1), (t // (2 * nj)) * ncb + cb))
    in_specs = [seq_spec(0), seq_spec(1), seq_spec(2), w_spec(0), w_spec(1), w_spec(2),
                pl.BlockSpec((2, cn), lambda b, cb, t: (0, cb)),
                pl.BlockSpec((1, 2 * r, n_tok), lambda b, cb, t: (t % (2 * nj), 0, 0)),
                h_spec, h_spec,
                pl.BlockSpec((1, cn), lambda b, cb, t: (0, (t // (2 * nj)) * ncb + cb))]
    return pl.pallas_call(
        functools.partial(_hyena_body, r=r, nj=nj, seq_len=n_tok, n_sub=n_sub),
        grid=(n_seq // n_sub, ncb, steps),
        in_specs=in_specs,
        out_specs=pl.BlockSpec((blk, cn), lambda b, cb, t: (b, cb)),
        out_shape=jax.ShapeDtypeStruct((n_seq * n_tok, c), F32),
        scratch_shapes=[pltpu.VMEM((2, blk, cn), BF16), pltpu.VMEM((blk, cn), F32),
                        pltpu.VMEM((n_sub, 2 * n_tok, cn), BF16)],
        compiler_params=_cparams(("parallel", "parallel", "arbitrary")),
        name="hyena_conv",
    )(proj, proj, proj, conv_w, conv_w, conv_w, bias, tables, ha, hb, hc)


def _gdn_prep_body(x_ref, w_ref, o_ref, *, heads_per_blk, dk, n_qk_blk, rows, seq_len):
    cb = pl.program_id(1)
    n = x_ref.shape[0]
    w = w_ref[...]
    is_v = cb >= n_qk_blk
    scale = jnp.where(cb < n_qk_blk // 2, dk ** -0.5, 1.0)
    for r0 in range(0, n, rows):
        y = _silu(_conv3_rows(x_ref, w, r0, rows, seq_len))
        for h in range(heads_per_blk):
            yh = y[:, h * dk:(h + 1) * dk]
            inv = lax.rsqrt(jnp.sum(yh * yh, axis=-1, keepdims=True) + RMS_EPS) * scale
            o_ref[r0:r0 + rows, h * dk:(h + 1) * dk] = yh * jnp.where(is_v, 1.0, inv)


def _gdn_prep_call(proj, conv_w, *, n_seq, n_tok, col0, d_gdn, dk, cn=512, blk_rows=2048):
    ncb = 3 * d_gdn // cn
    cb0 = col0 // cn
    rows = n_seq * n_tok
    blk_rows = min(blk_rows, rows)
    return pl.pallas_call(
        functools.partial(_gdn_prep_body, heads_per_blk=cn // dk, dk=dk, n_qk_blk=2 * d_gdn // cn,
                          rows=min(CONV_ROWS, n_tok), seq_len=n_tok),
        grid=(rows // blk_rows, ncb),
        in_specs=[pl.BlockSpec((blk_rows, cn), lambda b, cb: (b, cb0 + cb)),
                  pl.BlockSpec((3, cn), lambda b, cb: (0, cb))],
        out_specs=pl.BlockSpec((blk_rows, cn), lambda b, cb: (b, cb)),
        out_shape=jax.ShapeDtypeStruct((rows, 3 * d_gdn), F32),
        compiler_params=_cparams(("parallel", "parallel")),
        name="gdn_prep",
    )(proj, conv_w)


def _split3(x):
    hi = x.astype(BF16)
    r1 = x - hi.astype(F32)
    mid = r1.astype(BF16)
    lo = (r1 - mid.astype(F32)).astype(BF16)
    return hi, mid, lo


def _gdn_body(*refs, heads, dk, chunk, has_s0):
    (qf_ref, kf_ref, vf_ref, abf_ref, abtf_ref, qb_ref, kb_ref, vb_ref, abb_ref, abtb_ref,
     av_ref, dtv_ref, avt_ref, dtvt_ref) = refs[:14]
    s0_ref = refs[14] if has_s0 else None
    of_ref, ob_ref, sfin_ref, s_ref = refs[-4:]
    n = pl.program_id(1)

    @pl.when(n == 0)
    def _():
        s_ref[...] = s0_ref[0] if has_s0 else jnp.zeros_like(s_ref)

    ci = lax.broadcasted_iota(jnp.int32, (chunk, chunk), 0)
    cj = lax.broadcasted_iota(jnp.int32, (chunk, chunk), 1)
    eye = (ci == cj).astype(F32)
    diag_blk = (ci >> 3) == (cj >> 3)
    merge_blks = [((ci >> s) ^ (cj >> s)) == 1 for s in range(3, int(math.log2(chunk)))]
    lower = cj <= ci
    upper = cj >= ci

    units = []
    for dr, (q_ref, k_ref, v_ref, ab_ref, abt_ref, o_ref) in enumerate(
            ((qf_ref, kf_ref, vf_ref, abf_ref, abtf_ref, of_ref),
             (qb_ref, kb_ref, vb_ref, abb_ref, abtb_ref, ob_ref))):
        incl = upper if dr else lower
        strict = (cj > ci) if dr else (cj < ci)
        ones_incl = jnp.where(incl, 1.0, 0.0).astype(BF16)
        ones_incl_t = jnp.where(lower if dr else upper, 1.0, 0.0).astype(BF16)
        ab = ab_ref[...]
        g_col = -av_ref[...] * jax.nn.softplus(ab + dtv_ref[...])
        beta_col = jax.nn.sigmoid(ab)
        abt = abt_ref[...]
        g_row = -avt_ref[...] * jax.nn.softplus(abt + dtvt_ref[...])
        h3 = _split3(g_col)
        gc_col = _dot(ones_incl, h3[0]) + _dot(ones_incl, h3[1]) + _dot(ones_incl, h3[2])
        r3 = _split3(g_row)
        gc_row = _dot(r3[0], ones_incl_t) + _dot(r3[1], ones_incl_t) + _dot(r3[2], ones_incl_t)
        gl_all = jnp.sum(g_col, axis=0, keepdims=True)
        for h in range(heads):
            cg = dr * 2 * heads + h
            cb = cg + heads
            sl = slice(h * dk, (h + 1) * dk)
            units.append(dict(dr=dr, h=h, sl=sl, q_ref=q_ref, k_ref=k_ref, v_ref=v_ref, o_ref=o_ref,
                              incl=incl, strict=strict, gc=gc_col[:, cg:cg + 1], gr=gc_row[cg:cg + 1, :],
                              beta=beta_col[:, cb:cb + 1], gl=gl_all[:, cg:cg + 1]))

    def stage(fn):
        return [fn(u) for u in units]

    bf = lambda t: t.astype(BF16)
    k16 = stage(lambda u: bf(u["k_ref"][:, u["sl"]]))
    decay = stage(lambda u: jnp.where(u["incl"], jnp.exp(jnp.where(u["incl"], u["gc"] - u["gr"], 0.0)), 0.0))
    kbeta = stage(lambda u: u["k_ref"][:, u["sl"]] * u["beta"])
    kq = [_dot_nt(jnp.concatenate([bf(kb), bf(u["q_ref"][:, u["sl"]])], axis=0), k)
          for u, kb, k in zip(units, kbeta, k16)]
    a_low = [jnp.where(u["strict"], t[:chunk] * dc, 0.0) for u, t, dc in zip(units, kq, decay)]
    attn = [bf(t[chunk:] * dc) for t, dc in zip(kq, decay)]

    d = [jnp.where(diag_blk, a, 0.0) for a in a_low]
    d2 = [_dot(bf(t), bf(t)) for t in d]
    x = [eye - t for t in d]
    x = [t + _dot(bf(t), bf(p)) for t, p in zip(x, d2)]
    d4 = [_dot(bf(p), bf(p)) for p in d2]
    x = [t + _dot(bf(t), bf(p)) for t, p in zip(x, d4)]
    for off_blk in merge_blks:
        x16 = [bf(t) for t in x]
        tmp = [_dot(bf(jnp.where(off_blk, a, 0.0)), t16) for a, t16 in zip(a_low, x16)]
        x = [t - _dot(t16, bf(m)) for t, t16, m in zip(x, x16, tmp)]

    e_gc = stage(lambda u: jnp.exp(u["gc"]))
    rhs = [bf(jnp.concatenate([u["v_ref"][:, u["sl"]] * u["beta"], kb * e], axis=-1))
           for u, kb, e in zip(units, kbeta, e_gc)]
    sol = [_dot(bf(t), r) for t, r in zip(x, rhs)]

    s16 = stage(lambda u: bf(s_ref[u["dr"], u["h"]]))
    ws = [_dot(jnp.concatenate([bf(so[:, dk:]), bf(u["q_ref"][:, u["sl"]] * e)], axis=0), s)
          for u, so, e, s in zip(units, sol, e_gc, s16)]
    vn16 = [bf(so[:, :dk] - t[:chunk]) for so, t in zip(sol, ws)]
    o = [t[chunk:] + _dot(at, vn) for t, at, vn in zip(ws, attn, vn16)]
    for u, t in zip(units, o):
        u["o_ref"][:, u["sl"]] = t
    s_new = [s_ref[u["dr"], u["h"]] * jnp.exp(u["gl"])
             + _dot_tn(bf(u["k_ref"][:, u["sl"]] * jnp.exp(u["gl"] - u["gc"])), vn)
             for u, vn in zip(units, vn16)]
    for u, t in zip(units, s_new):
        s_ref[u["dr"], u["h"]] = t

    @pl.when(n == pl.num_programs(1) - 1)
    def _():
        sfin_ref[0] = s_ref[...]


def _gdn_call(qkv, ab, abt, av, dtv, avt, dtvt, s0, *, n_seq, n_tok, heads, dk, chunk):
    nc = n_tok // chunk
    d = heads * dk
    fwd = lambda b, n: b * nc + n
    bwd = lambda b, n: b * nc + nc - 1 - n
    qkv_spec = lambda idx, which: pl.BlockSpec((chunk, d), lambda b, n: (idx(b, n), which))
    assert n_tok % chunk == 0 and ab.shape[1] == LANES and 4 * heads <= LANES
    ab_spec = lambda idx: pl.BlockSpec((chunk, LANES), lambda b, n: (idx(b, n), 0))
    abt_spec = lambda idx: pl.BlockSpec((4 * heads, chunk), lambda b, n: (0, idx(b, n)))
    small = lambda shape: pl.BlockSpec(shape, lambda b, n: (0, 0))
    st_spec = pl.BlockSpec((1, 2, heads, dk, dk), lambda b, n: (b, 0, 0, 0, 0))
    o_shape = jax.ShapeDtypeStruct((n_seq * n_tok, d), F32)
    in_specs = [qkv_spec(fwd, 0), qkv_spec(fwd, 1), qkv_spec(fwd, 2), ab_spec(fwd), abt_spec(fwd),
                qkv_spec(bwd, 0), qkv_spec(bwd, 1), qkv_spec(bwd, 2), ab_spec(bwd), abt_spec(bwd),
                small((1, LANES)), small((1, LANES)), small((4 * heads, 1)), small((4 * heads, 1))]
    args = [qkv, qkv, qkv, ab, abt, qkv, qkv, qkv, ab, abt, av, dtv, avt, dtvt]
    if s0 is not None:
        in_specs.append(st_spec)
        args.append(s0)
    return pl.pallas_call(
        functools.partial(_gdn_body, heads=heads, dk=dk, chunk=chunk, has_s0=s0 is not None),
        grid=(n_seq, nc),
        in_specs=in_specs,
        out_specs=[pl.BlockSpec((chunk, d), lambda b, n: (fwd(b, n), 0)),
                   pl.BlockSpec((chunk, d), lambda b, n: (bwd(b, n), 0)),
                   st_spec],
        out_shape=[o_shape, o_shape, jax.ShapeDtypeStruct((n_seq, 2, heads, dk, dk), F32)],
        scratch_shapes=[pltpu.VMEM((2, heads, dk, dk), F32)],
        compiler_params=_cparams(("parallel", "arbitrary")),
        name="gdn_scan",
    )(*args)


def _outproj_body(x_ref, mod_ref, zhy_ref, of_ref, ob_ref, zg_ref, ghy_ref, go_ref, npost_ref, w_ref,
                  o_ref, *, mi, heads, dk):
    yhy = _rms(zhy_ref[...], ghy_ref[...]).astype(BF16)
    o = of_ref[...] + ob_ref[...]
    zg = zg_ref[...]
    parts = [yhy]
    for h in range(heads):
        sl = slice(h * dk, (h + 1) * dk)
        parts.append((_rms(o[:, sl], go_ref[...]) * _silu(zg[:, sl])).astype(BF16))
    y = _dot(jnp.concatenate(parts, axis=-1), w_ref[...])
    gate = mod_ref[0, mi:mi + 1, :]
    o_ref[...] = x_ref[...] + gate * _rms(y, npost_ref[...])


def _outproj_call(x, mod, zhy, o_f, o_b, proj, ghy, go, npost, w, *, mi, zg_col_blk, g0, rows_per_req, heads, dk,
                  tm=512):
    m, d = x.shape
    c = zhy.shape[1]
    dg = heads * dk
    row = lambda width: pl.BlockSpec((tm, width), lambda i: (i, 0))
    return pl.pallas_call(
        functools.partial(_outproj_body, mi=mi, heads=heads, dk=dk),
        grid=(m // tm,),
        in_specs=[row(d),
                  _mod_spec(d, tm, g0, rows_per_req),
                  row(c), row(dg), row(dg),
                  pl.BlockSpec((tm, dg), lambda i: (i, zg_col_blk)),
                  pl.BlockSpec((1, c), lambda i: (0, 0)),
                  pl.BlockSpec((1, dk), lambda i: (0, 0)),
                  pl.BlockSpec((1, d), lambda i: (0, 0)),
                  pl.BlockSpec((c + dg, d), lambda i: (0, 0))],
        out_specs=row(d),
        out_shape=jax.ShapeDtypeStruct((m, d), F32),
        compiler_params=_cparams(("parallel",)),
        name="out_proj",
    )(x, mod, zhy, o_f, o_b, proj, ghy.reshape(1, c), go.reshape(1, dk), npost.reshape(1, d), w)


@functools.lru_cache(maxsize=None)
def _grid_pos_tables_np(n_tok, dim):
    rows = n_tok // GRID_W
    quarter = dim // 4
    omega = 1.0 / (POS_BASE ** (np.arange(quarter, dtype=np.float64) / quarter))
    ar = np.arange(rows, dtype=np.float64)[:, None] * omega[None]
    ac = np.arange(GRID_W, dtype=np.float64)[:, None] * omega[None]
    row_t = np.concatenate([np.sin(ar), np.cos(ar)], axis=-1).astype(np.float32)
    col_t = np.concatenate([np.sin(ac), np.cos(ac)], axis=-1).astype(np.float32)
    return row_t.reshape(rows, 1, dim // 2), col_t.reshape(GRID_W // ROW_CHUNK, ROW_CHUNK, dim // 2)


def _mixer(proj, proj_ab, s0, *, n_seq, n_tok, hy_conv_w, hy_f, hy_decay, hy_bias, gdn_conv_w, gdn_a_log,
           gdn_dt_bias, c_hy, d_gdn):
    heads = GDN_HEADS
    dk = d_gdn // heads
    o0 = 3 * c_hy

    a_exp = jnp.exp(gdn_a_log.astype(F32))
    zeros = jnp.zeros_like(a_exp)
    av32 = jnp.stack([a_exp, zeros], axis=1).reshape(-1)
    dtv32 = jnp.stack([gdn_dt_bias.astype(F32), zeros], axis=1).reshape(-1)
    pad = LANES - 4 * heads
    av = jnp.pad(av32, (0, pad)).reshape(1, LANES)
    dtv = jnp.pad(dtv32, (0, pad)).reshape(1, LANES)
    avt = av32.reshape(4 * heads, 1)
    dtvt = dtv32.reshape(4 * heads, 1)
    abt = jnp.transpose(proj_ab[:, :4 * heads])

    r = min(512, n_tok)
    tables = _dft_tables(n_tok, r)
    ha, hb, hc = _spectra_call(n_tok, *hy_f, hy_decay, tables, r=r)
    zhy = _hyena_call(proj, hy_conv_w, hy_bias, tables, ha, hb, hc, n_seq=n_seq, n_tok=n_tok, c=c_hy, r=r,
                      n_sub=max(1, min(n_seq, 2048 // n_tok)))
    qkv = _gdn_prep_call(proj, gdn_conv_w, n_seq=n_seq, n_tok=n_tok, col0=o0, d_gdn=d_gdn, dk=dk)
    o_f, o_b, s_fin = _gdn_call(qkv, proj_ab, abt, av, dtv, avt, dtvt, s0, n_seq=n_seq, n_tok=n_tok, heads=heads,
                                dk=dk, chunk=GDN_CHUNK)
    return zhy, o_f, o_b, s_fin


def kernel(x_prompt, x_sample, state_gdn, c, c_ctx, ada_w, ada_b, norm_pre, norm_post, ffn_wg, ffn_wu, ffn_wd,
           w_in, w_out, hy_conv_w, hy_f_w1, hy_f_b1, hy_f_w2, hy_f_b2, hy_f_w3, hy_decay, hy_bias, hy_out_norm,
           gdn_conv_w, gdn_a_log, gdn_dt_bias, gdn_o_norm):
    nb, seq, d = x_prompt.shape
    db, dseq, _ = x_sample.shape
    depth = ada_w.shape[0]
    c_hy = hy_decay.shape[-1]
    d_gdn = gdn_conv_w.shape[-1] // 3
    heads = GDN_HEADS
    dk = d_gdn // heads
    n_ctx_rows = nb * seq
    n_lat_rows = db * dseq
    pos = tuple(jnp.asarray(t).astype(x_sample.dtype) for t in _grid_pos_tables_np(dseq, d))
    n_mod_rows = -(-(1 + db) // SUBLANES) * SUBLANES
    cvec = jnp.concatenate([c_ctx[None], c, jnp.zeros((n_mod_rows - 1 - db, d), F32)], axis=0)
    wg, wu, wd = ffn_wg.astype(BF16), ffn_wu.astype(BF16), ffn_wd.astype(BF16)
    n_main_cols = 3 * c_hy + 4 * d_gdn
    n_tail = w_in.shape[-1] - n_main_cols
    w_in_t = jnp.swapaxes(w_in, 1, 2).astype(BF16)

    xs = [x_prompt.reshape(n_ctx_rows, d), x_sample.reshape(n_lat_rows, d)]
    groups = [dict(g0=0, rows_per_req=n_ctx_rows), dict(g0=1, rows_per_req=dseq)]
    shapes = [(nb, seq), (db, dseq)]
    ctx_states = []
    for l in range(depth):
        mod = _mod_call(cvec, ada_w[l], ada_b[l]).reshape(n_mod_rows, N_MOD, d)
        w_tail_t = jnp.pad(w_in_t[l, n_main_cols:], ((0, LANES - n_tail), (0, 0)))
        w_out_l = w_out[l].astype(BF16)
        hy_f = (hy_f_w1[l], hy_f_b1[l], hy_f_w2[l], hy_f_b2[l], hy_f_w3[l])
        states = [None, state_gdn[:, l].astype(F32)]
        for gi, (grp, (n_seq, n_tok)) in enumerate(zip(groups, shapes)):
            x = _ffn_call(xs[gi], mod, norm_pre[l, 0], norm_post[l, 0], wg, wu, wd, (l, 0), mi=0,
                          pos=pos if (gi == 1 and l == 0) else None, **grp)
            proj, proj_ab = _inproj_call(x, mod, norm_pre[l, 1], w_in_t, l, w_tail_t, mi=3,
                                         n_main_cols=n_main_cols, **grp)
            zhy, o_f, o_b, s_fin = _mixer(
                proj, proj_ab, states[gi], n_seq=n_seq, n_tok=n_tok, hy_conv_w=hy_conv_w[l], hy_f=hy_f,
                hy_decay=hy_decay[l], hy_bias=hy_bias[l], gdn_conv_w=gdn_conv_w[l], gdn_a_log=gdn_a_log[l],
                gdn_dt_bias=gdn_dt_bias[l], c_hy=c_hy, d_gdn=d_gdn)
            if gi == 0:
                ctx_states.append(s_fin)
            x = _outproj_call(x, mod, zhy, o_f, o_b, proj, hy_out_norm[l], gdn_o_norm[l], norm_post[l, 1], w_out_l,
                              mi=5, zg_col_blk=(3 * c_hy + 3 * d_gdn) // d_gdn, heads=heads, dk=dk, **grp)
            xs[gi] = _ffn_call(x, mod, norm_pre[l, 2], norm_post[l, 2], wg, wu, wd, (l, 1), mi=6, **grp)

    new_state = ctx_states[0][:, None] if depth == 1 else jnp.stack(ctx_states, axis=1)
    return xs[0].reshape(nb, seq, d), xs[1].reshape(db, dseq, d), new_state
```

```python
import functools
import math

import numpy as np
import jax
import jax.numpy as jnp
from jax import lax
from jax.experimental import pallas as pl
from jax.experimental.pallas import tpu as pltpu

F32 = jnp.float32
BF16 = jnp.bfloat16

N_MOD = 9
RMS_EPS = 1e-6
GRID_W = 64
POS_BASE = 10000.0
HY_BANDS = 16
HY_SIN_FREQ = 1.0
HY_WINDOW_SHIFT = 0.05
GDN_HEADS = 8
GDN_CHUNK = 128
LANES = 128
SUBLANES = 8
VMEM_LIMIT = 56 * 1024 * 1024
ROW_CHUNK = 16
CONV_ROWS = 64
GATE_ROWS = 128


def _cparams(sem):
    return pltpu.CompilerParams(dimension_semantics=sem, vmem_limit_bytes=VMEM_LIMIT)


def _dot(a, b):
    return jnp.dot(a, b, preferred_element_type=F32)


def _dot_nt(a, b):
    return lax.dot_general(a, b, (((1,), (1,)), ((), ())), preferred_element_type=F32)


def _dot_tn(a, b):
    return lax.dot_general(a, b, (((0,), (0,)), ((), ())), preferred_element_type=F32)


def _rms(x, g):
    ms = jnp.mean(x * x, axis=-1, keepdims=True)
    return x * lax.rsqrt(ms + RMS_EPS) * g


def _silu(x):
    return x * jax.nn.sigmoid(x)


def _for_row_chunks(n_rows, fn):
    def body(c, carry):
        fn(pl.multiple_of(c * ROW_CHUNK, ROW_CHUNK))
        return carry

    lax.fori_loop(0, n_rows // ROW_CHUNK, body, 0, unroll=8)


def _mod_spec(d, tm, g0, rows_per_req):
    tpr = rows_per_req // tm
    return pl.BlockSpec((1, N_MOD, d), lambda i, *_: (g0 + i // tpr, 0, 0))


def _mod_body(c_ref, w_ref, b_ref, o_ref):
    s = _silu(c_ref[...]).astype(BF16)
    o_ref[...] = _dot(s, w_ref[...].astype(BF16)) + b_ref[...]


def _mod_call(cvec, ada_w, ada_b):
    g, d = cvec.shape
    n = ada_w.shape[1]
    tn = 1024
    return pl.pallas_call(
        _mod_body,
        grid=(n // tn,),
        in_specs=[pl.BlockSpec((g, d), lambda j: (0, 0)),
                  pl.BlockSpec((d, tn), lambda j: (0, j)),
                  pl.BlockSpec((1, tn), lambda j: (0, j))],
        out_specs=pl.BlockSpec((g, tn), lambda j: (0, j)),
        out_shape=jax.ShapeDtypeStruct((g, n), F32),
        compiler_params=_cparams(("arbitrary",)),
        name="mod_table",
    )(cvec, ada_w, ada_b.reshape(1, n))


def _ffn_body(*refs, mi, has_pos, tok_per_req):
    refs = list(refs)
    x_ref = refs.pop(0)
    prow_ref, pcol_ref = (refs.pop(0), refs.pop(0)) if has_pos else (None, None)
    mod_ref, npre_ref, npost_ref, wg_ref, wu_ref, wd_ref, o_ref, h_ref = refs
    j = pl.program_id(1)
    tm = x_ref.shape[0]

    def x_rows(r0):
        x = x_ref[pl.ds(r0, ROW_CHUNK), :]
        if not has_pos:
            return x
        t0 = (pl.program_id(0) * tm + r0) % tok_per_req
        prow = jnp.broadcast_to(prow_ref[t0 // GRID_W], (ROW_CHUNK, prow_ref.shape[-1]))
        return x + jnp.concatenate([prow, pcol_ref[(t0 % GRID_W) // ROW_CHUNK]], axis=1)

    @pl.when(j == 0)
    def _():
        shift = mod_ref[0, mi:mi + 1, :]
        gain = npre_ref[...] * (1.0 + mod_ref[0, mi + 1:mi + 2, :])

        def rows_body(r0):
            rows = pl.ds(r0, ROW_CHUNK)
            h_ref[rows, :] = (_rms(x_rows(r0), gain) + shift).astype(BF16)
            o_ref[rows, :] = jnp.zeros((ROW_CHUNK, o_ref.shape[1]), F32)

        _for_row_chunks(tm, rows_body)

    h = h_ref[...]
    g = _dot(h, wg_ref[...])
    u = _dot(h, wu_ref[...])
    a = (_silu(g) * u).astype(BF16)
    o_ref[...] += _dot(a, wd_ref[...])

    @pl.when(j == pl.num_programs(1) - 1)
    def _():
        gain = npost_ref[...] * (0.5 * mod_ref[0, mi + 2:mi + 3, :])
        for r0 in range(0, tm, ROW_CHUNK):
            rows = slice(r0, r0 + ROW_CHUNK)
            o_ref[rows, :] = x_rows(r0) + _rms(o_ref[rows, :], gain)


def _ffn_call(x, mod, npre, npost, wg, wu, wd, wsel, *, mi, g0, rows_per_req, pos=None, tm=1024, tf=512):
    n_rows, d = x.shape
    ff = wg.shape[-1]
    wl, wk = wsel
    assert n_rows % tm == 0 and rows_per_req % tm == 0 and ff % tf == 0 and tm % ROW_CHUNK == 0
    assert pos is None or (GRID_W % ROW_CHUNK == 0 and rows_per_req % GRID_W == 0)
    const = lambda shape: pl.BlockSpec(shape, lambda i, j: (0, 0))
    in_specs = [pl.BlockSpec((tm, d), lambda i, j: (i, 0))]
    args = [x]
    if pos is not None:
        in_specs += [pl.BlockSpec(t.shape, lambda i, j: (0, 0, 0)) for t in pos]
        args += list(pos)
    in_specs += [_mod_spec(d, tm, g0, rows_per_req),
                 const((1, d)), const((1, d)),
                 pl.BlockSpec((None, None, d, tf), lambda i, j: (wl, wk, 0, j)),
                 pl.BlockSpec((None, None, d, tf), lambda i, j: (wl, wk, 0, j)),
                 pl.BlockSpec((None, None, tf, d), lambda i, j: (wl, wk, j, 0))]
    args += [mod, npre.reshape(1, d), npost.reshape(1, d), wg, wu, wd]
    return pl.pallas_call(
        functools.partial(_ffn_body, mi=mi, has_pos=pos is not None, tok_per_req=rows_per_req),
        grid=(n_rows // tm, ff // tf),
        in_specs=in_specs,
        out_specs=pl.BlockSpec((tm, d), lambda i, j: (i, 0)),
        out_shape=jax.ShapeDtypeStruct((n_rows, d), F32),
        scratch_shapes=[pltpu.VMEM((tm, d), BF16)],
        compiler_params=_cparams(("parallel", "arbitrary")),
        name="ffn",
    )(*args)


def _inproj_body(x_ref, mod_ref, npre_ref, w_ref, wt_ref, o_ref, ot_ref, h_ref, *, mi):
    @pl.when(pl.program_id(1) == 0)
    def _():
        shift = mod_ref[0, mi:mi + 1, :]
        gain = npre_ref[...] * (1.0 + mod_ref[0, mi + 1:mi + 2, :])

        def rows_body(r0):
            rows = pl.ds(r0, ROW_CHUNK)
            h_ref[rows, :] = (_rms(x_ref[rows, :], gain) + shift).astype(BF16)

        _for_row_chunks(x_ref.shape[0], rows_body)
        ot_ref[...] = _dot_nt(h_ref[...], wt_ref[...])

    o_ref[...] = _dot_nt(h_ref[...], w_ref[...])


def _inproj_call(x, mod, npre, w_in_t, layer, w_tail_t, *, mi, n_main_cols, g0, rows_per_req, tm=1024, tn=1024):
    m, d = x.shape
    n_main = n_main_cols // tn
    nt = w_tail_t.shape[0]
    assert m % tm == 0 and rows_per_req % tm == 0 and n_main_cols % tn == 0 and nt % LANES == 0
    return pl.pallas_call(
        functools.partial(_inproj_body, mi=mi),
        grid=(m // tm, n_main),
        in_specs=[pl.BlockSpec((tm, d), lambda i, j: (i, 0)),
                  _mod_spec(d, tm, g0, rows_per_req),
                  pl.BlockSpec((1, d), lambda i, j: (0, 0)),
                  pl.BlockSpec((None, tn, d), lambda i, j: (layer, j, 0)),
                  pl.BlockSpec((nt, d), lambda i, j: (0, 0))],
        out_specs=[pl.BlockSpec((tm, tn), lambda i, j: (i, j)),
                   pl.BlockSpec((tm, nt), lambda i, j: (i, 0))],
        out_shape=[jax.ShapeDtypeStruct((m, n_main_cols), F32), jax.ShapeDtypeStruct((m, nt), F32)],
        scratch_shapes=[pltpu.VMEM((tm, d), BF16)],
        compiler_params=_cparams(("parallel", "arbitrary")),
        name="in_proj",
    )(x, mod, npre.reshape(1, d), w_in_t, w_tail_t)


def _conv3_rows(x_ref, w, r0, nr, seq_len):
    zeros = jnp.zeros((SUBLANES, x_ref.shape[1]), x_ref.dtype)
    top = zeros if r0 % seq_len == 0 else x_ref[r0 - SUBLANES:r0, :]
    mid = x_ref[r0:r0 + nr, :]
    bot = zeros if (r0 + nr) % seq_len == 0 else x_ref[r0 + nr:r0 + nr + SUBLANES, :]
    return _conv3_slab(top, mid, bot, w)


def _conv3_rows_dyn(x_ref, w, r0, nr, seq_len):
    n = x_ref.shape[0]
    top = x_ref[pl.ds(pl.multiple_of(jnp.maximum(r0 - SUBLANES, 0), SUBLANES), SUBLANES), :]
    mid = x_ref[pl.ds(pl.multiple_of(r0, SUBLANES), nr), :]
    bot = x_ref[pl.ds(pl.multiple_of(jnp.minimum(r0 + nr, n - SUBLANES), SUBLANES), SUBLANES), :]
    top = jnp.where(r0 % seq_len == 0, 0.0, top)
    bot = jnp.where((r0 + nr) % seq_len == 0, 0.0, bot)
    return _conv3_slab(top, mid, bot, w)


def _conv3_slab(top, mid, bot, w):
    nr = mid.shape[0]
    slab = jnp.concatenate([top, mid, bot], axis=0)
    prev = pltpu.roll(slab, 1, 0)[SUBLANES:SUBLANES + nr]
    nxt = pltpu.roll(slab, nr + 2 * SUBLANES - 1, 0)[SUBLANES:SUBLANES + nr]
    return prev * w[0:1, :] + mid * w[1:2, :] + nxt * w[2:3, :]


@functools.lru_cache(maxsize=None)
def _dft_tables_np(n_tok, r):
    big = 2 * n_tok
    k = np.arange(n_tok, dtype=np.int64)
    t = np.arange(n_tok, dtype=np.int64)
    ang = ((k[:, None] * t[None, :]) % big).astype(np.float64) * (2.0 * math.pi / big)
    cosm = np.cos(ang)
    sinm = np.sin(ang)
    sinm[0, :] = 1.0 - 2.0 * (t % 2)
    nj = n_tok // r
    fr = np.concatenate([cosm.reshape(nj, r, n_tok), sinm.reshape(nj, r, n_tok)], axis=1)
    g_full = fr.reshape(big, n_tok).T
    gr = np.concatenate([g_full[:, :n_tok].reshape(nj, r, n_tok), g_full[:, n_tok:].reshape(nj, r, n_tok)], axis=1)
    return np.concatenate([fr, gr], axis=0).astype(np.float32)


def _dft_tables(n_tok, r):
    return jnp.asarray(_dft_tables_np(n_tok, r)).astype(BF16)


def _spectra_body(featt_ref, t_ref, w1t_ref, b1_ref, w2t_ref, b2_ref, w3f_ref, w3b_ref, dec_ref, f_ref,
                  ha_ref, hb_ref, hc_ref, e_ref, o_ref, h_ref, *, n_tok, r):
    j = pl.program_id(1)

    @pl.when((pl.program_id(0) == 0) & (j == 0))
    def _():
        h = jnp.sin(HY_SIN_FREQ * (_dot(w1t_ref[...].astype(BF16), featt_ref[...].astype(BF16)) + b1_ref[...]))
        h = jnp.sin(HY_SIN_FREQ * (_dot(w2t_ref[...].astype(BF16), h.astype(BF16)) + b2_ref[...]))
        h_ref[...] = h.astype(BF16)

    @pl.when(j == 0)
    def _():
        hb16 = h_ref[...]
        window = jnp.exp(-t_ref[...] * jnp.abs(dec_ref[...])) + HY_WINDOW_SHIFT
        fwd = _dot_tn(hb16, w3f_ref[...].astype(BF16)) * window
        bwd = _dot_tn(hb16, w3b_ref[...].astype(BF16)) * window
        row = lax.broadcasted_iota(jnp.int32, bwd.shape, 0)
        bwd = jnp.where(row == 0, 0.0, bwd)
        norm = jnp.sum(jnp.abs(fwd), axis=0, keepdims=True) + jnp.sum(jnp.abs(bwd), axis=0, keepdims=True)
        fwd = fwd / norm
        bwd = bwd / norm
        e_ref[...] = (fwd + bwd).astype(BF16)
        o_ref[...] = (fwd - bwd).astype(BF16)

    fb = f_ref[0]
    e = e_ref[...]
    p = _dot(fb[:r], e)
    q = _dot(fb[r:], o_ref[...])
    k = j * r + lax.broadcasted_iota(jnp.int32, p.shape, 0)
    big = 2.0 * n_tok
    wk = jnp.where(k == 0, 1.0 / big, 2.0 / big)
    ha_ref[...] = wk * p
    hb_ref[...] = jnp.where(k == 0, 0.0, -wk * q)

    @pl.when(j == 0)
    def _():
        hc_ref[...] = _dot(fb[r:r + SUBLANES], e)[0:1, :] * (1.0 / big)


def _spectra_call(n_tok, w1, b1, w2, b2, w3, decay, fr, *, r, cn=512):
    f32 = jnp.float32
    order, c = decay.shape
    oc = order * c
    hid = w2.shape[0]
    emb = w1.shape[0]
    embp = LANES
    idx = jnp.arange(n_tok, dtype=f32)
    t = idx / (n_tok - 1)
    bands = jnp.arange(1, HY_BANDS + 1, dtype=f32)
    ang = (2.0 * math.pi / n_tok) * idx[:, None] * bands[None, :]
    feats = jnp.concatenate([t[:, None], jnp.cos(ang), jnp.sin(ang)], axis=-1)
    featt = jnp.pad(feats, ((0, 0), (0, embp - emb))).T
    w1t = jnp.pad(w1, ((0, embp - emb), (0, 0))).T
    nj = n_tok // r
    ncb = oc // cn
    out = jax.ShapeDtypeStruct((n_tok, oc), F32)
    const = lambda shape: pl.BlockSpec(shape, lambda cb, j: (0, 0))
    return pl.pallas_call(
        functools.partial(_spectra_body, n_tok=n_tok, r=r),
        grid=(ncb, nj),
        in_specs=[const((embp, n_tok)), const((n_tok, 1)), const((hid, embp)), const((hid, 1)),
                  const((hid, hid)), const((hid, 1)),
                  pl.BlockSpec((hid, cn), lambda cb, j: (0, cb)),
                  pl.BlockSpec((hid, cn), lambda cb, j: (0, ncb + cb)),
                  pl.BlockSpec((1, cn), lambda cb, j: (0, cb)),
                  pl.BlockSpec((1, 2 * r, n_tok), lambda cb, j: (j, 0, 0))],
        out_specs=[pl.BlockSpec((r, cn), lambda cb, j: (j, cb))] * 2 + [pl.BlockSpec((1, cn), lambda cb, j: (0, cb))],
        out_shape=[out, out, jax.ShapeDtypeStruct((1, oc), F32)],
        scratch_shapes=[pltpu.VMEM((n_tok, cn), BF16), pltpu.VMEM((n_tok, cn), BF16),
                        pltpu.VMEM((hid, n_tok), BF16)],
        compiler_params=_cparams(("arbitrary", "arbitrary")),
        name="hyena_spectra",
    )(featt, t[:, None], w1t, b1.reshape(hid, 1), w2.T, b2.reshape(hid, 1), w3, w3, decay.reshape(1, oc), fr)


def _hyena_body(v_ref, x1_ref, x2_ref, wv_ref, wx1_ref, wx2_ref, bias_ref, t_ref, ha_ref, hb_ref, hc_ref,
                o_ref, ub_ref, u2f_ref, z_ref, *, r, nj, seq_len, n_sub):
    t = pl.program_id(2)
    order = t // (2 * nj)
    tt = t % (2 * nj)
    sub = min(GATE_ROWS, r)

    @pl.when(t == 0)
    def _():
        w = wv_ref[...]
        for r0 in range(0, n_sub * seq_len, CONV_ROWS):
            ub_ref[0, r0:r0 + CONV_ROWS, :] = _conv3_rows(v_ref, w, r0, CONV_ROWS, seq_len).astype(BF16)

    @pl.when(tt < nj)
    def _():
        tb = t_ref[0]
        ha, hb = ha_ref[...], hb_ref[...]
        nyq = (tt * r + lax.broadcasted_iota(jnp.int32, ha.shape, 0)) == 0
        hc = jnp.where(nyq, hc_ref[...], ha)
        z0 = pl.multiple_of(tt * (2 * r), 2 * r)
        for s in range(n_sub):
            x = _dot(tb, ub_ref[order, s * seq_len:(s + 1) * seq_len, :])
            p = x[:r]
            q = x[r:]
            z_ref[s, pl.ds(z0, 2 * r), :] = jnp.concatenate([p * ha + q * hb, q * hc - p * hb],
                                                             axis=0).astype(BF16)

    def inverse_rows(s):
        tb = t_ref[0]
        return _dot(tb[:r], z_ref[s, :seq_len, :]) + _dot(tb[r:], z_ref[s, seq_len:, :])

    r_blk = (tt - nj) * r

    @pl.when((tt >= nj) & (order == 0))
    def _():
        wv, wx1 = wv_ref[...], wx1_ref[...]
        for s in range(n_sub):
            y = inverse_rows(s)
            for c0 in range(0, r, sub):
                r0 = s * seq_len + r_blk + c0
                u1 = _conv3_rows_dyn(v_ref, wv, r0, sub, seq_len)
                z1 = _conv3_rows_dyn(x1_ref, wx1, r0, sub, seq_len) * (y[c0:c0 + sub] + u1 * bias_ref[0:1, :])
                u2f_ref[pl.ds(pl.multiple_of(r0, SUBLANES), sub), :] = z1
                ub_ref[1, pl.ds(pl.multiple_of(r0, SUBLANES), sub), :] = z1.astype(BF16)

    @pl.when((tt >= nj) & (order == 1))
    def _():
        wx2 = wx2_ref[...]
        for s in range(n_sub):
            y = inverse_rows(s)
            for c0 in range(0, r, sub):
                r0 = s * seq_len + r_blk + c0
                u2 = u2f_ref[pl.ds(pl.multiple_of(r0, SUBLANES), sub), :]
                o_ref[pl.ds(pl.multiple_of(r0, SUBLANES), sub), :] = (
                    _conv3_rows_dyn(x2_ref, wx2, r0, sub, seq_len) * (y[c0:c0 + sub] + u2 * bias_ref[1:2, :]))


def _hyena_call(proj, conv_w, bias, tables, ha, hb, hc, *, n_seq, n_tok, c, r, n_sub, cn=512):
    nj = n_tok // r
    ncb = c // cn
    blk = n_sub * n_tok
    steps = 4 * nj
    assert n_tok % r == 0 and c % cn == 0 and n_seq % n_sub == 0
    assert r % min(GATE_ROWS, r) == 0 and n_tok % CONV_ROWS == 0
    seq_spec = lambda off: pl.BlockSpec((blk, cn), lambda b, cb, t: (b, off * ncb + cb),
                                        **({"pipeline_mode": pl.Buffered(1)} if off == 0 else {}))
    w_spec = lambda off: pl.BlockSpec((3, cn), lambda b, cb, t: (0, off * ncb + cb))
    h_spec = pl.BlockSpec((r, cn), lambda b, cb, t: (jnp.minimum(t % (2 * nj), nj - 1), (t // (2 * nj)) * ncb + cb))
    in_specs = [seq_spec(0), seq_spec(1), seq_spec(2), w_spec(0), w_spec(1), w_spec(2),
                pl.BlockSpec((2, cn), lambda b, cb, t: (0, cb)),
                pl.BlockSpec((1, 2 * r, n_tok), lambda b, cb, t: (t % (2 * nj), 0, 0)),
                h_spec, h_spec,
                pl.BlockSpec((1, cn), lambda b, cb, t: (0, (t // (2 * nj)) * ncb + cb))]
    return pl.pallas_call(
        functools.partial(_hyena_body, r=r, nj=nj, seq_len=n_tok, n_sub=n_sub),
        grid=(n_seq // n_sub, ncb, steps),
        in_specs=in_specs,
        out_specs=pl.BlockSpec((blk, cn), lambda b, cb, t: (b, cb)),
        out_shape=jax.ShapeDtypeStruct((n_seq * n_tok, c), F32),
        scratch_shapes=[pltpu.VMEM((2, blk, cn), BF16), pltpu.VMEM((blk, cn), F32),
                        pltpu.VMEM((n_sub, 2 * n_tok, cn), BF16)],
        compiler_params=_cparams(("parallel", "parallel", "arbitrary")),
        name="hyena_conv",
    )(proj, proj, proj, conv_w, conv_w, conv_w, bias, tables, ha, hb, hc)


def _gdn_prep_body(x_ref, w_ref, o_ref, *, heads_per_blk, dk, n_qk_blk, rows, seq_len):
    cb = pl.program_id(1)
    n = x_ref.shape[0]
    w = w_ref[...]
    is_v = cb >= n_qk_blk
    scale = jnp.where(cb < n_qk_blk // 2, dk ** -0.5, 1.0)
    for r0 in range(0, n, rows):
        y = _silu(_conv3_rows(x_ref, w, r0, rows, seq_len))
        for h in range(heads_per_blk):
            yh = y[:, h * dk:(h + 1) * dk]
            inv = lax.rsqrt(jnp.sum(yh * yh, axis=-1, keepdims=True) + RMS_EPS) * scale
            o_ref[r0:r0 + rows, h * dk:(h + 1) * dk] = yh * jnp.where(is_v, 1.0, inv)


def _gdn_prep_call(proj, conv_w, *, n_seq, n_tok, col0, d_gdn, dk, cn=512, blk_rows=2048):
    ncb = 3 * d_gdn // cn
    cb0 = col0 // cn
    rows = n_seq * n_tok
    blk_rows = min(blk_rows, rows)
    return pl.pallas_call(
        functools.partial(_gdn_prep_body, heads_per_blk=cn // dk, dk=dk, n_qk_blk=2 * d_gdn // cn,
                          rows=min(CONV_ROWS, n_tok), seq_len=n_tok),
        grid=(rows // blk_rows, ncb),
        in_specs=[pl.BlockSpec((blk_rows, cn), lambda b, cb: (b, cb0 + cb)),
                  pl.BlockSpec((3, cn), lambda b, cb: (0, cb))],
        out_specs=pl.BlockSpec((blk_rows, cn), lambda b, cb: (b, cb)),
        out_shape=jax.ShapeDtypeStruct((rows, 3 * d_gdn), F32),
        compiler_params=_cparams(("parallel", "parallel")),
        name="gdn_prep",
    )(proj, conv_w)


def _split3(x):
    hi = x.astype(BF16)
    r1 = x - hi.astype(F32)
    mid = r1.astype(BF16)
    lo = (r1 - mid.astype(F32)).astype(BF16)
    return hi, mid, lo


def _gdn_body(*refs, heads, dk, chunk, has_s0):
    (qf_ref, kf_ref, vf_ref, abf_ref, abtf_ref, qb_ref, kb_ref, vb_ref, abb_ref, abtb_ref,
     av_ref, dtv_ref, avt_ref, dtvt_ref) = refs[:14]
    s0_ref = refs[14] if has_s0 else None
    of_ref, ob_ref, sfin_ref, s_ref = refs[-4:]
    n = pl.program_id(1)

    @pl.when(n == 0)
    def _():
        s_ref[...] = s0_ref[0] if has_s0 else jnp.zeros_like(s_ref)

    ci = lax.broadcasted_iota(jnp.int32, (chunk, chunk), 0)
    cj = lax.broadcasted_iota(jnp.int32, (chunk, chunk), 1)
    eye = (ci == cj).astype(F32)
    diag_blk = (ci >> 3) == (cj >> 3)
    merge_blks = [((ci >> s) ^ (cj >> s)) == 1 for s in range(3, int(math.log2(chunk)))]
    lower = cj <= ci
    upper = cj >= ci

    units = []
    for dr, (q_ref, k_ref, v_ref, ab_ref, abt_ref, o_ref) in enumerate(
            ((qf_ref, kf_ref, vf_ref, abf_ref, abtf_ref, of_ref),
             (qb_ref, kb_ref, vb_ref, abb_ref, abtb_ref, ob_ref))):
        incl = upper if dr else lower
        strict = (cj > ci) if dr else (cj < ci)
        ones_incl = jnp.where(incl, 1.0, 0.0).astype(BF16)
        ones_incl_t = jnp.where(lower if dr else upper, 1.0, 0.0).astype(BF16)
        ab = ab_ref[...]
        g_col = -av_ref[...] * jax.nn.softplus(ab + dtv_ref[...])
        beta_col = jax.nn.sigmoid(ab)
        abt = abt_ref[...]
        g_row = -avt_ref[...] * jax.nn.softplus(abt + dtvt_ref[...])
        h3 = _split3(g_col)
        gc_col = _dot(ones_incl, h3[0]) + _dot(ones_incl, h3[1]) + _dot(ones_incl, h3[2])
        r3 = _split3(g_row)
        gc_row = _dot(r3[0], ones_incl_t) + _dot(r3[1], ones_incl_t) + _dot(r3[2], ones_incl_t)
        gl_all = jnp.sum(g_col, axis=0, keepdims=True)
        for h in range(heads):
            cg = dr * 2 * heads + h
            cb = cg + heads
            sl = slice(h * dk, (h + 1) * dk)
            units.append(dict(dr=dr, h=h, sl=sl, q_ref=q_ref, k_ref=k_ref, v_ref=v_ref, o_ref=o_ref,
                              incl=incl, strict=strict, gc=gc_col[:, cg:cg + 1], gr=gc_row[cg:cg + 1, :],
                              beta=beta_col[:, cb:cb + 1], gl=gl_all[:, cg:cg + 1]))

    def stage(fn):
        return [fn(u) for u in units]

    bf = lambda t: t.astype(BF16)
    k16 = stage(lambda u: bf(u["k_ref"][:, u["sl"]]))
    decay = stage(lambda u: jnp.where(u["incl"], jnp.exp(jnp.where(u["incl"], u["gc"] - u["gr"], 0.0)), 0.0))
    kbeta = stage(lambda u: u["k_ref"][:, u["sl"]] * u["beta"])
    kq = [_dot_nt(jnp.concatenate([bf(kb), bf(u["q_ref"][:, u["sl"]])], axis=0), k)
          for u, kb, k in zip(units, kbeta, k16)]
    a_low = [jnp.where(u["strict"], t[:chunk] * dc, 0.0) for u, t, dc in zip(units, kq, decay)]
    attn = [bf(t[chunk:] * dc) for t, dc in zip(kq, decay)]

    d = [jnp.where(diag_blk, a, 0.0) for a in a_low]
    d2 = [_dot(bf(t), bf(t)) for t in d]
    x = [eye - t for t in d]
    x = [t + _dot(bf(t), bf(p)) for t, p in zip(x, d2)]
    d4 = [_dot(bf(p), bf(p)) for p in d2]
    x = [t + _dot(bf(t), bf(p)) for t, p in zip(x, d4)]
    for off_blk in merge_blks:
        x16 = [bf(t) for t in x]
        tmp = [_dot(bf(jnp.where(off_blk, a, 0.0)), t16) for a, t16 in zip(a_low, x16)]
        x = [t - _dot(t16, bf(m)) for t, t16, m in zip(x, x16, tmp)]

    e_gc = stage(lambda u: jnp.exp(u["gc"]))
    rhs = [bf(jnp.concatenate([u["v_ref"][:, u["sl"]] * u["beta"], kb * e], axis=-1))
           for u, kb, e in zip(units, kbeta, e_gc)]
    sol = [_dot(bf(t), r) for t, r in zip(x, rhs)]

    s16 = stage(lambda u: bf(s_ref[u["dr"], u["h"]]))
    ws = [_dot(jnp.concatenate([bf(so[:, dk:]), bf(u["q_ref"][:, u["sl"]] * e)], axis=0), s)
          for u, so, e, s in zip(units, sol, e_gc, s16)]
    vn16 = [bf(so[:, :dk] - t[:chunk]) for so, t in zip(sol, ws)]
    o = [t[chunk:] + _dot(at, vn) for t, at, vn in zip(ws, attn, vn16)]
    for u, t in zip(units, o):
        u["o_ref"][:, u["sl"]] = t
    s_new = [s_ref[u["dr"], u["h"]] * jnp.exp(u["gl"])
             + _dot_tn(bf(u["k_ref"][:, u["sl"]] * jnp.exp(u["gl"] - u["gc"])), vn)
             for u, vn in zip(units, vn16)]
    for u, t in zip(units, s_new):
        s_ref[u["dr"], u["h"]] = t

    @pl.when(n == pl.num_programs(1) - 1)
    def _():
        sfin_ref[0] = s_ref[...]


def _gdn_call(qkv, ab, abt, av, dtv, avt, dtvt, s0, *, n_seq, n_tok, heads, dk, chunk):
    nc = n_tok // chunk
    d = heads * dk
    fwd = lambda b, n: b * nc + n
    bwd = lambda b, n: b * nc + nc - 1 - n
    qkv_spec = lambda idx, which: pl.BlockSpec((chunk, d), lambda b, n: (idx(b, n), which))
    assert n_tok % chunk == 0 and ab.shape[1] == LANES and 4 * heads <= LANES
    ab_spec = lambda idx: pl.BlockSpec((chunk, LANES), lambda b, n: (idx(b, n), 0))
    abt_spec = lambda idx: pl.BlockSpec((4 * heads, chunk), lambda b, n: (0, idx(b, n)))
    small = lambda shape: pl.BlockSpec(shape, lambda b, n: (0, 0))
    st_spec = pl.BlockSpec((1, 2, heads, dk, dk), lambda b, n: (b, 0, 0, 0, 0))
    o_shape = jax.ShapeDtypeStruct((n_seq * n_tok, d), F32)
    in_specs = [qkv_spec(fwd, 0), qkv_spec(fwd, 1), qkv_spec(fwd, 2), ab_spec(fwd), abt_spec(fwd),
                qkv_spec(bwd, 0), qkv_spec(bwd, 1), qkv_spec(bwd, 2), ab_spec(bwd), abt_spec(bwd),
                small((1, LANES)), small((1, LANES)), small((4 * heads, 1)), small((4 * heads, 1))]
    args = [qkv, qkv, qkv, ab, abt, qkv, qkv, qkv, ab, abt, av, dtv, avt, dtvt]
    if s0 is not None:
        in_specs.append(st_spec)
        args.append(s0)
    return pl.pallas_call(
        functools.partial(_gdn_body, heads=heads, dk=dk, chunk=chunk, has_s0=s0 is not None),
        grid=(n_seq, nc),
        in_specs=in_specs,
        out_specs=[pl.BlockSpec((chunk, d), lambda b, n: (fwd(b, n), 0)),
                   pl.BlockSpec((chunk, d), lambda b, n: (bwd(b, n), 0)),
                   st_spec],
        out_shape=[o_shape, o_shape, jax.ShapeDtypeStruct((n_seq, 2, heads, dk, dk), F32)],
        scratch_shapes=[pltpu.VMEM((2, heads, dk, dk), F32)],
        compiler_params=_cparams(("parallel", "arbitrary")),
        name="gdn_scan",
    )(*args)


def _outproj_body(x_ref, mod_ref, zhy_ref, of_ref, ob_ref, zg_ref, ghy_ref, go_ref, npost_ref, w_ref,
                  o_ref, *, mi, heads, dk):
    yhy = _rms(zhy_ref[...], ghy_ref[...]).astype(BF16)
    o = of_ref[...] + ob_ref[...]
    zg = zg_ref[...]
    parts = [yhy]
    for h in range(heads):
        sl = slice(h * dk, (h + 1) * dk)
        parts.append((_rms(o[:, sl], go_ref[...]) * _silu(zg[:, sl])).astype(BF16))
    y = _dot(jnp.concatenate(parts, axis=-1), w_ref[...])
    gate = mod_ref[0, mi:mi + 1, :]
    o_ref[...] = x_ref[...] + gate * _rms(y, npost_ref[...])


def _outproj_call(x, mod, zhy, o_f, o_b, proj, ghy, go, npost, w, *, mi, zg_col_blk, g0, rows_per_req, heads, dk,
                  tm=512):
    m, d = x.shape
    c = zhy.shape[1]
    dg = heads * dk
    row = lambda width: pl.BlockSpec((tm, width), lambda i: (i, 0))
    return pl.pallas_call(
        functools.partial(_outproj_body, mi=mi, heads=heads, dk=dk),
        grid=(m // tm,),
        in_specs=[row(d),
                  _mod_spec(d, tm, g0, rows_per_req),
                  row(c), row(dg), row(dg),
                  pl.BlockSpec((tm, dg), lambda i: (i, zg_col_blk)),
                  pl.BlockSpec((1, c), lambda i: (0, 0)),
                  pl.BlockSpec((1, dk), lambda i: (0, 0)),
                  pl.BlockSpec((1, d), lambda i: (0, 0)),
                  pl.BlockSpec((c + dg, d), lambda i: (0, 0))],
        out_specs=row(d),
        out_shape=jax.ShapeDtypeStruct((m, d), F32),
        compiler_params=_cparams(("parallel",)),
        name="out_proj",
    )(x, mod, zhy, o_f, o_b, proj, ghy.reshape(1, c), go.reshape(1, dk), npost.reshape(1, d), w)


@functools.lru_cache(maxsize=None)
def _grid_pos_tables_np(n_tok, dim):
    rows = n_tok // GRID_W
    quarter = dim // 4
    omega = 1.0 / (POS_BASE ** (np.arange(quarter, dtype=np.float64) / quarter))
    ar = np.arange(rows, dtype=np.float64)[:, None] * omega[None]
    ac = np.arange(GRID_W, dtype=np.float64)[:, None] * omega[None]
    row_t = np.concatenate([np.sin(ar), np.cos(ar)], axis=-1).astype(np.float32)
    col_t = np.concatenate([np.sin(ac), np.cos(ac)], axis=-1).astype(np.float32)
    return row_t.reshape(rows, 1, dim // 2), col_t.reshape(GRID_W // ROW_CHUNK, ROW_CHUNK, dim // 2)


def _mixer(proj, proj_ab, s0, *, n_seq, n_tok, hy_conv_w, hy_f, hy_decay, hy_bias, gdn_conv_w, gdn_a_log,
           gdn_dt_bias, c_hy, d_gdn):
    heads = GDN_HEADS
    dk = d_gdn // heads
    o0 = 3 * c_hy

    a_exp = jnp.exp(gdn_a_log.astype(F32))
    zeros = jnp.zeros_like(a_exp)
    av32 = jnp.stack([a_exp, zeros], axis=1).reshape(-1)
    dtv32 = jnp.stack([gdn_dt_bias.astype(F32), zeros], axis=1).reshape(-1)
    pad = LANES - 4 * heads
    av = jnp.pad(av32, (0, pad)).reshape(1, LANES)
    dtv = jnp.pad(dtv32, (0, pad)).reshape(1, LANES)
    avt = av32.reshape(4 * heads, 1)
    dtvt = dtv32.reshape(4 * heads, 1)
    abt = jnp.transpose(proj_ab[:, :4 * heads])

    r = min(512, n_tok)
    tables = _dft_tables(n_tok, r)
    ha, hb, hc = _spectra_call(n_tok, *hy_f, hy_decay, tables, r=r)
    zhy = _hyena_call(proj, hy_conv_w, hy_bias, tables, ha, hb, hc, n_seq=n_seq, n_tok=n_tok, c=c_hy, r=r,
                      n_sub=max(1, min(n_seq, 2048 // n_tok)))
    qkv = _gdn_prep_call(proj, gdn_conv_w, n_seq=n_seq, n_tok=n_tok, col0=o0, d_gdn=d_gdn, dk=dk)
    o_f, o_b, s_fin = _gdn_call(qkv, proj_ab, abt, av, dtv, avt, dtvt, s0, n_seq=n_seq, n_tok=n_tok, heads=heads,
                                dk=dk, chunk=GDN_CHUNK)
    return zhy, o_f, o_b, s_fin


def kernel(x_prompt, x_sample, state_gdn, c, c_ctx, ada_w, ada_b, norm_pre, norm_post, ffn_wg, ffn_wu, ffn_wd,
           w_in, w_out, hy_conv_w, hy_f_w1, hy_f_b1, hy_f_w2, hy_f_b2, hy_f_w3, hy_decay, hy_bias, hy_out_norm,
           gdn_conv_w, gdn_a_log, gdn_dt_bias, gdn_o_norm):
    nb, seq, d = x_prompt.shape
    db, dseq, _ = x_sample.shape
    depth = ada_w.shape[0]
    c_hy = hy_decay.shape[-1]
    d_gdn = gdn_conv_w.shape[-1] // 3
    heads = GDN_HEADS
    dk = d_gdn // heads
    n_ctx_rows = nb * seq
    n_lat_rows = db * dseq
    pos = tuple(jnp.asarray(t).astype(x_sample.dtype) for t in _grid_pos_tables_np(dseq, d))
    n_mod_rows = -(-(1 + db) // SUBLANES) * SUBLANES
    cvec = jnp.concatenate([c_ctx[None], c, jnp.zeros((n_mod_rows - 1 - db, d), F32)], axis=0)
    wg, wu, wd = ffn_wg.astype(BF16), ffn_wu.astype(BF16), ffn_wd.astype(BF16)
    n_main_cols = 3 * c_hy + 4 * d_gdn
    n_tail = w_in.shape[-1] - n_main_cols
    w_in_t = jnp.swapaxes(w_in, 1, 2).astype(BF16)

    xs = [x_prompt.reshape(n_ctx_rows, d), x_sample.reshape(n_lat_rows, d)]
    groups = [dict(g0=0, rows_per_req=n_ctx_rows), dict(g0=1, rows_per_req=dseq)]
    shapes = [(nb, seq), (db, dseq)]
    ctx_states = []
    for l in range(depth):
        mod = _mod_call(cvec, ada_w[l], ada_b[l]).reshape(n_mod_rows, N_MOD, d)
        w_tail_t = jnp.pad(w_in_t[l, n_main_cols:], ((0, LANES - n_tail), (0, 0)))
        w_out_l = w_out[l].astype(BF16)
        hy_f = (hy_f_w1[l], hy_f_b1[l], hy_f_w2[l], hy_f_b2[l], hy_f_w3[l])
        states = [None, state_gdn[:, l].astype(F32)]
        for gi, (grp, (n_seq, n_tok)) in enumerate(zip(groups, shapes)):
            x = _ffn_call(xs[gi], mod, norm_pre[l, 0], norm_post[l, 0], wg, wu, wd, (l, 0), mi=0,
                          pos=pos if (gi == 1 and l == 0) else None, **grp)
            proj, proj_ab = _inproj_call(x, mod, norm_pre[l, 1], w_in_t, l, w_tail_t, mi=3,
                                         n_main_cols=n_main_cols, **grp)
            zhy, o_f, o_b, s_fin = _mixer(
                proj, proj_ab, states[gi], n_seq=n_seq, n_tok=n_tok, hy_conv_w=hy_conv_w[l], hy_f=hy_f,
                hy_decay=hy_decay[l], hy_bias=hy_bias[l], gdn_conv_w=gdn_conv_w[l], gdn_a_log=gdn_a_log[l],
                gdn_dt_bias=gdn_dt_bias[l], c_hy=c_hy, d_gdn=d_gdn)
            if gi == 0:
                ctx_states.append(s_fin)
            x = _outproj_call(x, mod, zhy, o_f, o_b, proj, hy_out_norm[l], gdn_o_norm[l], norm_post[l, 1], w_out_l,
                              mi=5, zg_col_blk=(3 * c_hy + 3 * d_gdn) // d_gdn, heads=heads, dk=dk, **grp)
            xs[gi] = _ffn_call(x, mod, norm_pre[l, 2], norm_post[l, 2], wg, wu, wd, (l, 1), mi=6, **grp)

    new_state = ctx_states[0][:, None] if depth == 1 else jnp.stack(ctx_states, axis=1)
    return xs[0].reshape(nb, seq, d), xs[1].reshape(db, dseq, d), new_state
```

```python
import functools
import math

import numpy as np
import jax
import jax.numpy as jnp
from jax import lax
from jax.experimental import pallas as pl
from jax.experimental.pallas import tpu as pltpu

F32 = jnp.float32
BF16 = jnp.bfloat16

N_MOD = 9
RMS_EPS = 1e-6
GRID_W = 64
POS_BASE = 10000.0
HY_BANDS = 16
HY_SIN_FREQ = 1.0
HY_WINDOW_SHIFT = 0.05
GDN_HEADS = 8
GDN_CHUNK = 128
LANES = 128
SUBLANES = 8
VMEM_LIMIT = 56 * 1024 * 1024
ROW_CHUNK = 16
CONV_ROWS = 64
GATE_ROWS = 128


def _cparams(sem):
    return pltpu.CompilerParams(dimension_semantics=sem, vmem_limit_bytes=VMEM_LIMIT)


def _dot(a, b):
    return jnp.dot(a, b, preferred_element_type=F32)


def _dot_nt(a, b):
    return lax.dot_general(a, b, (((1,), (1,)), ((), ())), preferred_element_type=F32)


def _dot_tn(a, b):
    return lax.dot_general(a, b, (((0,), (0,)), ((), ())), preferred_element_type=F32)


def _rms(x, g):
    ms = jnp.mean(x * x, axis=-1, keepdims=True)
    return x * lax.rsqrt(ms + RMS_EPS) * g


def _silu(x):
    return x * jax.nn.sigmoid(x)


def _for_row_chunks(n_rows, fn):
    def body(c, carry):
        fn(pl.multiple_of(c * ROW_CHUNK, ROW_CHUNK))
        return carry

    lax.fori_loop(0, n_rows // ROW_CHUNK, body, 0, unroll=8)


def _mod_spec(d, tm, g0, rows_per_req):
    tpr = rows_per_req // tm
    return pl.BlockSpec((1, N_MOD, d), lambda i, *_: (g0 + i // tpr, 0, 0))


def _mod_body(c_ref, w_ref, b_ref, o_ref):
    s = _silu(c_ref[...]).astype(BF16)
    o_ref[...] = _dot(s, w_ref[...].astype(BF16)) + b_ref[...]


def _mod_call(cvec, ada_w, ada_b):
    g, d = cvec.shape
    n = ada_w.shape[1]
    tn = 2048
    return pl.pallas_call(
        _mod_body,
        grid=(n // tn,),
        in_specs=[pl.BlockSpec((g, d), lambda j: (0, 0)),
                  pl.BlockSpec((d, tn), lambda j: (0, j)),
                  pl.BlockSpec((1, tn), lambda j: (0, j))],
        out_specs=pl.BlockSpec((g, tn), lambda j: (0, j)),
        out_shape=jax.ShapeDtypeStruct((g, n), F32),
        compiler_params=_cparams(("arbitrary",)),
        name="mod_table",
    )(cvec, ada_w, ada_b.reshape(1, n))


def _ffn_body(*refs, mi, has_pos, tok_per_req):
    refs = list(refs)
    x_ref = refs.pop(0)
    prow_ref, pcol_ref = (refs.pop(0), refs.pop(0)) if has_pos else (None, None)
    mod_ref, npre_ref, npost_ref, wg_ref, wu_ref, wd_ref, o_ref, h_ref = refs
    j = pl.program_id(1)
    tm = x_ref.shape[0]

    def x_rows(r0):
        x = x_ref[pl.ds(r0, ROW_CHUNK), :]
        if not has_pos:
            return x
        t0 = (pl.program_id(0) * tm + r0) % tok_per_req
        prow = jnp.broadcast_to(prow_ref[t0 // GRID_W], (ROW_CHUNK, prow_ref.shape[-1]))
        return x + jnp.concatenate([prow, pcol_ref[(t0 % GRID_W) // ROW_CHUNK]], axis=1)

    @pl.when(j == 0)
    def _():
        shift = mod_ref[0, mi:mi + 1, :]
        gain = npre_ref[...] * (1.0 + mod_ref[0, mi + 1:mi + 2, :])

        def rows_body(r0):
            rows = pl.ds(r0, ROW_CHUNK)
            h_ref[rows, :] = (_rms(x_rows(r0), gain) + shift).astype(BF16)
            o_ref[rows, :] = jnp.zeros((ROW_CHUNK, o_ref.shape[1]), F32)

        _for_row_chunks(tm, rows_body)

    h = h_ref[...]
    g = _dot(h, wg_ref[...])
    u = _dot(h, wu_ref[...])
    a = (_silu(g) * u).astype(BF16)
    o_ref[...] += _dot(a, wd_ref[...])

    @pl.when(j == pl.num_programs(1) - 1)
    def _():
        gain = npost_ref[...] * (0.5 * mod_ref[0, mi + 2:mi + 3, :])
        for r0 in range(0, tm, ROW_CHUNK):
            rows = slice(r0, r0 + ROW_CHUNK)
            o_ref[rows, :] = x_rows(r0) + _rms(o_ref[rows, :], gain)


def _ffn_call(x, mod, npre, npost, wg, wu, wd, wsel, *, mi, g0, rows_per_req, pos=None, tm=1024, tf=512):
    n_rows, d = x.shape
    ff = wg.shape[-1]
    wl, wk = wsel
    assert n_rows % tm == 0 and rows_per_req % tm == 0 and ff % tf == 0 and tm % ROW_CHUNK == 0
    assert pos is None or (GRID_W % ROW_CHUNK == 0 and rows_per_req % GRID_W == 0)
    const = lambda shape: pl.BlockSpec(shape, lambda i, j: (0, 0))
    in_specs = [pl.BlockSpec((tm, d), lambda i, j: (i, 0))]
    args = [x]
    if pos is not None:
        in_specs += [pl.BlockSpec(t.shape, lambda i, j: (0, 0, 0)) for t in pos]
        args += list(pos)
    in_specs += [_mod_spec(d, tm, g0, rows_per_req),
                 const((1, d)), const((1, d)),
                 pl.BlockSpec((None, None, d, tf), lambda i, j: (wl, wk, 0, j)),
                 pl.BlockSpec((None, None, d, tf), lambda i, j: (wl, wk, 0, j)),
                 pl.BlockSpec((None, None, tf, d), lambda i, j: (wl, wk, j, 0))]
    args += [mod, npre.reshape(1, d), npost.reshape(1, d), wg, wu, wd]
    return pl.pallas_call(
        functools.partial(_ffn_body, mi=mi, has_pos=pos is not None, tok_per_req=rows_per_req),
        grid=(n_rows // tm, ff // tf),
        in_specs=in_specs,
        out_specs=pl.BlockSpec((tm, d), lambda i, j: (i, 0)),
        out_shape=jax.ShapeDtypeStruct((n_rows, d), F32),
        scratch_shapes=[pltpu.VMEM((tm, d), BF16)],
        compiler_params=_cparams(("parallel", "arbitrary")),
        name="ffn",
    )(*args)


def _inproj_body(x_ref, mod_ref, npre_ref, w_ref, wt_ref, o_ref, ot_ref, h_ref, *, mi):
    @pl.when(pl.program_id(1) == 0)
    def _():
        shift = mod_ref[0, mi:mi + 1, :]
        gain = npre_ref[...] * (1.0 + mod_ref[0, mi + 1:mi + 2, :])

        def rows_body(r0):
            rows = pl.ds(r0, ROW_CHUNK)
            h_ref[rows, :] = (_rms(x_ref[rows, :], gain) + shift).astype(BF16)

        _for_row_chunks(x_ref.shape[0], rows_body)
        ot_ref[...] = _dot_nt(h_ref[...], wt_ref[...])

    o_ref[...] = _dot_nt(h_ref[...], w_ref[...])


def _inproj_call(x, mod, npre, w_in_t, layer, w_tail_t, *, mi, n_main_cols, g0, rows_per_req, tm=1024, tn=1024):
    m, d = x.shape
    n_main = n_main_cols // tn
    nt = w_tail_t.shape[0]
    assert m % tm == 0 and rows_per_req % tm == 0 and n_main_cols % tn == 0 and nt % LANES == 0
    return pl.pallas_call(
        functools.partial(_inproj_body, mi=mi),
        grid=(m // tm, n_main),
        in_specs=[pl.BlockSpec((tm, d), lambda i, j: (i, 0)),
                  _mod_spec(d, tm, g0, rows_per_req),
                  pl.BlockSpec((1, d), lambda i, j: (0, 0)),
                  pl.BlockSpec((None, tn, d), lambda i, j: (layer, j, 0)),
                  pl.BlockSpec((nt, d), lambda i, j: (0, 0))],
        out_specs=[pl.BlockSpec((tm, tn), lambda i, j: (i, j)),
                   pl.BlockSpec((tm, nt), lambda i, j: (i, 0))],
        out_shape=[jax.ShapeDtypeStruct((m, n_main_cols), F32), jax.ShapeDtypeStruct((m, nt), F32)],
        scratch_shapes=[pltpu.VMEM((tm, d), BF16)],
        compiler_params=_cparams(("parallel", "arbitrary")),
        name="in_proj",
    )(x, mod, npre.reshape(1, d), w_in_t, w_tail_t)


def _conv3_rows(x_ref, w, r0, nr, seq_len):
    zeros = jnp.zeros((SUBLANES, x_ref.shape[1]), x_ref.dtype)
    top = zeros if r0 % seq_len == 0 else x_ref[r0 - SUBLANES:r0, :]
    mid = x_ref[r0:r0 + nr, :]
    bot = zeros if (r0 + nr) % seq_len == 0 else x_ref[r0 + nr:r0 + nr + SUBLANES, :]
    return _conv3_slab(top, mid, bot, w)


def _conv3_rows_dyn(x_ref, w, r0, nr, seq_len):
    n = x_ref.shape[0]
    top = x_ref[pl.ds(pl.multiple_of(jnp.maximum(r0 - SUBLANES, 0), SUBLANES), SUBLANES), :]
    mid = x_ref[pl.ds(pl.multiple_of(r0, SUBLANES), nr), :]
    bot = x_ref[pl.ds(pl.multiple_of(jnp.minimum(r0 + nr, n - SUBLANES), SUBLANES), SUBLANES), :]
    top = jnp.where(r0 % seq_len == 0, 0.0, top)
    bot = jnp.where((r0 + nr) % seq_len == 0, 0.0, bot)
    return _conv3_slab(top, mid, bot, w)


def _conv3_slab(top, mid, bot, w):
    nr = mid.shape[0]
    slab = jnp.concatenate([top, mid, bot], axis=0)
    prev = pltpu.roll(slab, 1, 0)[SUBLANES:SUBLANES + nr]
    nxt = pltpu.roll(slab, nr + 2 * SUBLANES - 1, 0)[SUBLANES:SUBLANES + nr]
    return prev * w[0:1, :] + mid * w[1:2, :] + nxt * w[2:3, :]


@functools.lru_cache(maxsize=None)
def _dft_tables_np(n_tok, r):
    big = 2 * n_tok
    k = np.arange(n_tok, dtype=np.int64)
    t = np.arange(n_tok, dtype=np.int64)
    ang = ((k[:, None] * t[None, :]) % big).astype(np.float64) * (2.0 * math.pi / big)
    cosm = np.cos(ang)
    sinm = np.sin(ang)
    sinm[0, :] = 1.0 - 2.0 * (t % 2)
    nj = n_tok // r
    fr = np.concatenate([cosm.reshape(nj, r, n_tok), sinm.reshape(nj, r, n_tok)], axis=1)
    g_full = fr.reshape(big, n_tok).T
    gr = np.concatenate([g_full[:, :n_tok].reshape(nj, r, n_tok), g_full[:, n_tok:].reshape(nj, r, n_tok)], axis=1)
    return np.concatenate([fr, gr], axis=0).astype(np.float32)


def _dft_tables(n_tok, r):
    return jnp.asarray(_dft_tables_np(n_tok, r)).astype(BF16)


def _spectra_body(featt_ref, t_ref, w1t_ref, b1_ref, w2t_ref, b2_ref, w3f_ref, w3b_ref, dec_ref, f_ref,
                  ha_ref, hb_ref, hc_ref, e_ref, o_ref, h_ref, *, n_tok, r):
    j = pl.program_id(1)

    @pl.when((pl.program_id(0) == 0) & (j == 0))
    def _():
        h = jnp.sin(HY_SIN_FREQ * (_dot(w1t_ref[...].astype(BF16), featt_ref[...].astype(BF16)) + b1_ref[...]))
        h = jnp.sin(HY_SIN_FREQ * (_dot(w2t_ref[...].astype(BF16), h.astype(BF16)) + b2_ref[...]))
        h_ref[...] = h.astype(BF16)

    @pl.when(j == 0)
    def _():
        hb16 = h_ref[...]
        window = jnp.exp(-t_ref[...] * jnp.abs(dec_ref[...])) + HY_WINDOW_SHIFT
        fwd = _dot_tn(hb16, w3f_ref[...].astype(BF16)) * window
        bwd = _dot_tn(hb16, w3b_ref[...].astype(BF16)) * window
        row = lax.broadcasted_iota(jnp.int32, bwd.shape, 0)
        bwd = jnp.where(row == 0, 0.0, bwd)
        norm = jnp.sum(jnp.abs(fwd), axis=0, keepdims=True) + jnp.sum(jnp.abs(bwd), axis=0, keepdims=True)
        fwd = fwd / norm
        bwd = bwd / norm
        e_ref[...] = (fwd + bwd).astype(BF16)
        o_ref[...] = (fwd - bwd).astype(BF16)

    fb = f_ref[0]
    e = e_ref[...]
    p = _dot(fb[:r], e)
    q = _dot(fb[r:], o_ref[...])
    k = j * r + lax.broadcasted_iota(jnp.int32, p.shape, 0)
    big = 2.0 * n_tok
    wk = jnp.where(k == 0, 1.0 / big, 2.0 / big)
    ha_ref[...] = wk * p
    hb_ref[...] = jnp.where(k == 0, 0.0, -wk * q)

    @pl.when(j == 0)
    def _():
        hc_ref[...] = _dot(fb[r:r + SUBLANES], e)[0:1, :] * (1.0 / big)


def _spectra_call(n_tok, w1, b1, w2, b2, w3, decay, fr, *, r, cn=512):
    f32 = jnp.float32
    order, c = decay.shape
    oc = order * c
    hid = w2.shape[0]
    emb = w1.shape[0]
    embp = LANES
    idx = jnp.arange(n_tok, dtype=f32)
    t = idx / (n_tok - 1)
    bands = jnp.arange(1, HY_BANDS + 1, dtype=f32)
    ang = (2.0 * math.pi / n_tok) * idx[:, None] * bands[None, :]
    feats = jnp.concatenate([t[:, None], jnp.cos(ang), jnp.sin(ang)], axis=-1)
    featt = jnp.pad(feats, ((0, 0), (0, embp - emb))).T
    w1t = jnp.pad(w1, ((0, embp - emb), (0, 0))).T
    nj = n_tok // r
    ncb = oc // cn
    out = jax.ShapeDtypeStruct((n_tok, oc), F32)
    const = lambda shape: pl.BlockSpec(shape, lambda cb, j: (0, 0))
    return pl.pallas_call(
        functools.partial(_spectra_body, n_tok=n_tok, r=r),
        grid=(ncb, nj),
        in_specs=[const((embp, n_tok)), const((n_tok, 1)), const((hid, embp)), const((hid, 1)),
                  const((hid, hid)), const((hid, 1)),
                  pl.BlockSpec((hid, cn), lambda cb, j: (0, cb)),
                  pl.BlockSpec((hid, cn), lambda cb, j: (0, ncb + cb)),
                  pl.BlockSpec((1, cn), lambda cb, j: (0, cb)),
                  pl.BlockSpec((1, 2 * r, n_tok), lambda cb, j: (j, 0, 0))],
        out_specs=[pl.BlockSpec((r, cn), lambda cb, j: (j, cb))] * 2 + [pl.BlockSpec((1, cn), lambda cb, j: (0, cb))],
        out_shape=[out, out, jax.ShapeDtypeStruct((1, oc), F32)],
        scratch_shapes=[pltpu.VMEM((n_tok, cn), BF16), pltpu.VMEM((n_tok, cn), BF16),
                        pltpu.VMEM((hid, n_tok), BF16)],
        compiler_params=_cparams(("arbitrary", "arbitrary")),
        name="hyena_spectra",
    )(featt, t[:, None], w1t, b1.reshape(hid, 1), w2.T, b2.reshape(hid, 1), w3, w3, decay.reshape(1, oc), fr)


def _hyena_body(v_ref, x1_ref, x2_ref, wv_ref, wx1_ref, wx2_ref, bias_ref, t_ref, ha_ref, hb_ref, hc_ref,
                o_ref, ub_ref, u2f_ref, z_ref, *, r, nj, seq_len, n_sub):
    t = pl.program_id(2)
    order = t // (2 * nj)
    tt = t % (2 * nj)
    sub = min(GATE_ROWS, r)

    @pl.when(t == 0)
    def _():
        w = wv_ref[...]
        for r0 in range(0, n_sub * seq_len, CONV_ROWS):
            ub_ref[0, r0:r0 + CONV_ROWS, :] = _conv3_rows(v_ref, w, r0, CONV_ROWS, seq_len).astype(BF16)

    @pl.when(tt < nj)
    def _():
        tb = t_ref[0]
        ha, hb = ha_ref[...], hb_ref[...]
        nyq = (tt * r + lax.broadcasted_iota(jnp.int32, ha.shape, 0)) == 0
        hc = jnp.where(nyq, hc_ref[...], ha)
        z0 = pl.multiple_of(tt * (2 * r), 2 * r)
        for s in range(n_sub):
            x = _dot(tb, ub_ref[order, s * seq_len:(s + 1) * seq_len, :])
            p = x[:r]
            q = x[r:]
            z_ref[s, pl.ds(z0, 2 * r), :] = jnp.concatenate([p * ha + q * hb, q * hc - p * hb],
                                                             axis=0).astype(BF16)

    def inverse_rows(s):
        tb = t_ref[0]
        return _dot(tb[:r], z_ref[s, :seq_len, :]) + _dot(tb[r:], z_ref[s, seq_len:, :])

    r_blk = (tt - nj) * r

    @pl.when((tt >= nj) & (order == 0))
    def _():
        wv, wx1 = wv_ref[...], wx1_ref[...]
        for s in range(n_sub):
            y = inverse_rows(s)
            for c0 in range(0, r, sub):
                r0 = s * seq_len + r_blk + c0
                u1 = _conv3_rows_dyn(v_ref, wv, r0, sub, seq_len)
                z1 = _conv3_rows_dyn(x1_ref, wx1, r0, sub, seq_len) * (y[c0:c0 + sub] + u1 * bias_ref[0:1, :])
                u2f_ref[pl.ds(pl.multiple_of(r0, SUBLANES), sub), :] = z1
                ub_ref[1, pl.ds(pl.multiple_of(r0, SUBLANES), sub), :] = z1.astype(BF16)

    @pl.when((tt >= nj) & (order == 1))
    def _():
        wx2 = wx2_ref[...]
        for s in range(n_sub):
            y = inverse_rows(s)
            for c0 in range(0, r, sub):
                r0 = s * seq_len + r_blk + c0
                u2 = u2f_ref[pl.ds(pl.multiple_of(r0, SUBLANES), sub), :]
                o_ref[pl.ds(pl.multiple_of(r0, SUBLANES), sub), :] = (
                    _conv3_rows_dyn(x2_ref, wx2, r0, sub, seq_len) * (y[c0:c0 + sub] + u2 * bias_ref[1:2, :]))


def _hyena_call(proj, conv_w, bias, tables, ha, hb, hc, *, n_seq, n_tok, c, r, n_sub, cn=512):
    nj = n_tok // r
    ncb = c // cn
    blk = n_sub * n_tok
    steps = 4 * nj
    assert n_tok % r == 0 and c % cn == 0 and n_seq % n_sub == 0
    assert r % min(GATE_ROWS, r) == 0 and n_tok % CONV_ROWS == 0
    seq_spec = lambda off: pl.BlockSpec((blk, cn), lambda b, cb, t: (b, off * ncb + cb),
                                        **({"pipeline_mode": pl.Buffered(1)} if off == 0 else {}))
    w_spec = lambda off: pl.BlockSpec((3, cn), lambda b, cb, t: (0, off * ncb + cb))
    h_spec = pl.BlockSpec((r, cn), lambda b, cb, t: (jnp.minimum(t % (2 * nj), nj - 1), (t // (2 * nj)) * ncb + cb))
    in_specs = [seq_spec(0), seq_spec(1), seq_spec(2), w_spec(0), w_spec(1), w_spec(2),
                pl.BlockSpec((2, cn), lambda b, cb, t: (0, cb)),
                pl.BlockSpec((1, 2 * r, n_tok), lambda b, cb, t: (t % (2 * nj), 0, 0)),
                h_spec, h_spec,
                pl.BlockSpec((1, cn), lambda b, cb, t: (0, (t // (2 * nj)) * ncb + cb))]
    return pl.pallas_call(
        functools.partial(_hyena_body, r=r, nj=nj, seq_len=n_tok, n_sub=n_sub),
        grid=(n_seq // n_sub, ncb, steps),
        in_specs=in_specs,
        out_specs=pl.BlockSpec((blk, cn), lambda b, cb, t: (b, cb)),
        out_shape=jax.ShapeDtypeStruct((n_seq * n_tok, c), F32),
        scratch_shapes=[pltpu.VMEM((2, blk, cn), BF16), pltpu.VMEM((blk, cn), F32),
                        pltpu.VMEM((n_sub, 2 * n_tok, cn), BF16)],
        compiler_params=_cparams(("parallel", "parallel", "arbitrary")),
        name="hyena_conv",
    )(proj, proj, proj, conv_w, conv_w, conv_w, bias, tables, ha, hb, hc)


def _gdn_prep_body(x_ref, w_ref, o_ref, *, heads_per_blk, dk, n_qk_blk, rows, seq_len):
    cb = pl.program_id(1)
    n = x_ref.shape[0]
    w = w_ref[...]
    is_v = cb >= n_qk_blk
    scale = jnp.where(cb < n_qk_blk // 2, dk ** -0.5, 1.0)
    for r0 in range(0, n, rows):
        y = _silu(_conv3_rows(x_ref, w, r0, rows, seq_len))
        for h in range(heads_per_blk):
            yh = y[:, h * dk:(h + 1) * dk]
            inv = lax.rsqrt(jnp.sum(yh * yh, axis=-1, keepdims=True) + RMS_EPS) * scale
            o_ref[r0:r0 + rows, h * dk:(h + 1) * dk] = yh * jnp.where(is_v, 1.0, inv)


def _gdn_prep_call(proj, conv_w, *, n_seq, n_tok, col0, d_gdn, dk, cn=1024, blk_rows=2048):
    ncb = 3 * d_gdn // cn
    cb0 = col0 // cn
    rows = n_seq * n_tok
    blk_rows = min(blk_rows, rows)
    return pl.pallas_call(
        functools.partial(_gdn_prep_body, heads_per_blk=cn // dk, dk=dk, n_qk_blk=2 * d_gdn // cn,
                          rows=min(CONV_ROWS, n_tok), seq_len=n_tok),
        grid=(rows // blk_rows, ncb),
        in_specs=[pl.BlockSpec((blk_rows, cn), lambda b, cb: (b, cb0 + cb)),
                  pl.BlockSpec((3, cn), lambda b, cb: (0, cb))],
        out_specs=pl.BlockSpec((blk_rows, cn), lambda b, cb: (b, cb)),
        out_shape=jax.ShapeDtypeStruct((rows, 3 * d_gdn), F32),
        compiler_params=_cparams(("parallel", "parallel")),
        name="gdn_prep",
    )(proj, conv_w)


def _split3(x):
    hi = x.astype(BF16)
    r1 = x - hi.astype(F32)
    mid = r1.astype(BF16)
    lo = (r1 - mid.astype(F32)).astype(BF16)
    return hi, mid, lo


def _gdn_body(*refs, heads, dk, chunk, has_s0):
    (qf_ref, kf_ref, vf_ref, abf_ref, abtf_ref, qb_ref, kb_ref, vb_ref, abb_ref, abtb_ref,
     av_ref, dtv_ref, avt_ref, dtvt_ref) = refs[:14]
    s0_ref = refs[14] if has_s0 else None
    of_ref, ob_ref, sfin_ref, s_ref = refs[-4:]
    n = pl.program_id(1)

    @pl.when(n == 0)
    def _():
        s_ref[...] = s0_ref[0] if has_s0 else jnp.zeros_like(s_ref)

    ci = lax.broadcasted_iota(jnp.int32, (chunk, chunk), 0)
    cj = lax.broadcasted_iota(jnp.int32, (chunk, chunk), 1)
    eye = (ci == cj).astype(F32)
    diag_blk = (ci >> 3) == (cj >> 3)
    merge_blks = [((ci >> s) ^ (cj >> s)) == 1 for s in range(3, int(math.log2(chunk)))]
    lower = cj <= ci
    upper = cj >= ci

    units = []
    for dr, (q_ref, k_ref, v_ref, ab_ref, abt_ref, o_ref) in enumerate(
            ((qf_ref, kf_ref, vf_ref, abf_ref, abtf_ref, of_ref),
             (qb_ref, kb_ref, vb_ref, abb_ref, abtb_ref, ob_ref))):
        incl = upper if dr else lower
        strict = (cj > ci) if dr else (cj < ci)
        ones_incl = jnp.where(incl, 1.0, 0.0).astype(BF16)
        ones_incl_t = jnp.where(lower if dr else upper, 1.0, 0.0).astype(BF16)
        ab = ab_ref[...]
        g_col = -av_ref[...] * jax.nn.softplus(ab + dtv_ref[...])
        beta_col = jax.nn.sigmoid(ab)
        abt = abt_ref[...]
        g_row = -avt_ref[...] * jax.nn.softplus(abt + dtvt_ref[...])
        h3 = _split3(g_col)
        gc_col = _dot(ones_incl, h3[0]) + _dot(ones_incl, h3[1]) + _dot(ones_incl, h3[2])
        r3 = _split3(g_row)
        gc_row = _dot(r3[0], ones_incl_t) + _dot(r3[1], ones_incl_t) + _dot(r3[2], ones_incl_t)
        gl_all = jnp.sum(g_col, axis=0, keepdims=True)
        for h in range(heads):
            cg = dr * 2 * heads + h
            cb = cg + heads
            sl = slice(h * dk, (h + 1) * dk)
            units.append(dict(dr=dr, h=h, sl=sl, q_ref=q_ref, k_ref=k_ref, v_ref=v_ref, o_ref=o_ref,
                              incl=incl, strict=strict, gc=gc_col[:, cg:cg + 1], gr=gc_row[cg:cg + 1, :],
                              beta=beta_col[:, cb:cb + 1], gl=gl_all[:, cg:cg + 1]))

    def stage(fn):
        return [fn(u) for u in units]

    bf = lambda t: t.astype(BF16)
    k16 = stage(lambda u: bf(u["k_ref"][:, u["sl"]]))
    decay = stage(lambda u: jnp.where(u["incl"], jnp.exp(jnp.where(u["incl"], u["gc"] - u["gr"], 0.0)), 0.0))
    kbeta = stage(lambda u: u["k_ref"][:, u["sl"]] * u["beta"])
    kq = [_dot_nt(jnp.concatenate([bf(kb), bf(u["q_ref"][:, u["sl"]])], axis=0), k)
          for u, kb, k in zip(units, kbeta, k16)]
    a_low = [jnp.where(u["strict"], t[:chunk] * dc, 0.0) for u, t, dc in zip(units, kq, decay)]
    attn = [bf(t[chunk:] * dc) for t, dc in zip(kq, decay)]

    d = [jnp.where(diag_blk, a, 0.0) for a in a_low]
    d2 = [_dot(bf(t), bf(t)) for t in d]
    x = [eye - t for t in d]
    x = [t + _dot(bf(t), bf(p)) for t, p in zip(x, d2)]
    d4 = [_dot(bf(p), bf(p)) for p in d2]
    x = [t + _dot(bf(t), bf(p)) for t, p in zip(x, d4)]
    for off_blk in merge_blks:
        x16 = [bf(t) for t in x]
        tmp = [_dot(bf(jnp.where(off_blk, a, 0.0)), t16) for a, t16 in zip(a_low, x16)]
        x = [t - _dot(t16, bf(m)) for t, t16, m in zip(x, x16, tmp)]

    e_gc = stage(lambda u: jnp.exp(u["gc"]))
    rhs = [bf(jnp.concatenate([u["v_ref"][:, u["sl"]] * u["beta"], kb * e], axis=-1))
           for u, kb, e in zip(units, kbeta, e_gc)]
    sol = [_dot(bf(t), r) for t, r in zip(x, rhs)]

    s16 = stage(lambda u: bf(s_ref[u["dr"], u["h"]]))
    ws = [_dot(jnp.concatenate([bf(so[:, dk:]), bf(u["q_ref"][:, u["sl"]] * e)], axis=0), s)
          for u, so, e, s in zip(units, sol, e_gc, s16)]
    vn16 = [bf(so[:, :dk] - t[:chunk]) for so, t in zip(sol, ws)]
    o = [t[chunk:] + _dot(at, vn) for t, at, vn in zip(ws, attn, vn16)]
    for u, t in zip(units, o):
        u["o_ref"][:, u["sl"]] = t
    s_new = [s_ref[u["dr"], u["h"]] * jnp.exp(u["gl"])
             + _dot_tn(bf(u["k_ref"][:, u["sl"]] * jnp.exp(u["gl"] - u["gc"])), vn)
             for u, vn in zip(units, vn16)]
    for u, t in zip(units, s_new):
        s_ref[u["dr"], u["h"]] = t

    @pl.when(n == pl.num_programs(1) - 1)
    def _():
        sfin_ref[0] = s_ref[...]


def _gdn_call(qkv, ab, abt, av, dtv, avt, dtvt, s0, *, n_seq, n_tok, heads, dk, chunk):
    nc = n_tok // chunk
    d = heads * dk
    fwd = lambda b, n: b * nc + n
    bwd = lambda b, n: b * nc + nc - 1 - n
    qkv_spec = lambda idx, which: pl.BlockSpec((chunk, d), lambda b, n: (idx(b, n), which))
    assert n_tok % chunk == 0 and ab.shape[1] == LANES and 4 * heads <= LANES
    ab_spec = lambda idx: pl.BlockSpec((chunk, LANES), lambda b, n: (idx(b, n), 0))
    abt_spec = lambda idx: pl.BlockSpec((4 * heads, chunk), lambda b, n: (0, idx(b, n)))
    small = lambda shape: pl.BlockSpec(shape, lambda b, n: (0, 0))
    st_spec = pl.BlockSpec((1, 2, heads, dk, dk), lambda b, n: (b, 0, 0, 0, 0))
    o_shape = jax.ShapeDtypeStruct((n_seq * n_tok, d), F32)
    in_specs = [qkv_spec(fwd, 0), qkv_spec(fwd, 1), qkv_spec(fwd, 2), ab_spec(fwd), abt_spec(fwd),
                qkv_spec(bwd, 0), qkv_spec(bwd, 1), qkv_spec(bwd, 2), ab_spec(bwd), abt_spec(bwd),
                small((1, LANES)), small((1, LANES)), small((4 * heads, 1)), small((4 * heads, 1))]
    args = [qkv, qkv, qkv, ab, abt, qkv, qkv, qkv, ab, abt, av, dtv, avt, dtvt]
    if s0 is not None:
        in_specs.append(st_spec)
        args.append(s0)
    return pl.pallas_call(
        functools.partial(_gdn_body, heads=heads, dk=dk, chunk=chunk, has_s0=s0 is not None),
        grid=(n_seq, nc),
        in_specs=in_specs,
        out_specs=[pl.BlockSpec((chunk, d), lambda b, n: (fwd(b, n), 0)),
                   pl.BlockSpec((chunk, d), lambda b, n: (bwd(b, n), 0)),
                   st_spec],
        out_shape=[o_shape, o_shape, jax.ShapeDtypeStruct((n_seq, 2, heads, dk, dk), F32)],
        scratch_shapes=[pltpu.VMEM((2, heads, dk, dk), F32)],
        compiler_params=_cparams(("parallel", "arbitrary")),
        name="gdn_scan",
    )(*args)


def _outproj_body(x_ref, mod_ref, zhy_ref, of_ref, ob_ref, zg_ref, ghy_ref, go_ref, npost_ref, w_ref,
                  o_ref, *, mi, heads, dk):
    yhy = _rms(zhy_ref[...], ghy_ref[...]).astype(BF16)
    o = of_ref[...] + ob_ref[...]
    zg = zg_ref[...]
    parts = [yhy]
    for h in range(heads):
        sl = slice(h * dk, (h + 1) * dk)
        parts.append((_rms(o[:, sl], go_ref[...]) * _silu(zg[:, sl])).astype(BF16))
    y = _dot(jnp.concatenate(parts, axis=-1), w_ref[...])
    gate = mod_ref[0, mi:mi + 1, :]
    o_ref[...] = x_ref[...] + gate * _rms(y, npost_ref[...])


def _outproj_call(x, mod, zhy, o_f, o_b, proj, ghy, go, npost, w, *, mi, zg_col_blk, g0, rows_per_req, heads, dk,
                  tm=512):
    m, d = x.shape
    c = zhy.shape[1]
    dg = heads * dk
    row = lambda width: pl.BlockSpec((tm, width), lambda i: (i, 0))
    return pl.pallas_call(
        functools.partial(_outproj_body, mi=mi, heads=heads, dk=dk),
        grid=(m // tm,),
        in_specs=[row(d),
                  _mod_spec(d, tm, g0, rows_per_req),
                  row(c), row(dg), row(dg),
                  pl.BlockSpec((tm, dg), lambda i: (i, zg_col_blk)),
                  pl.BlockSpec((1, c), lambda i: (0, 0)),
                  pl.BlockSpec((1, dk), lambda i: (0, 0)),
                  pl.BlockSpec((1, d), lambda i: (0, 0)),
                  pl.BlockSpec((c + dg, d), lambda i: (0, 0))],
        out_specs=row(d),
        out_shape=jax.ShapeDtypeStruct((m, d), F32),
        compiler_params=_cparams(("parallel",)),
        name="out_proj",
    )(x, mod, zhy, o_f, o_b, proj, ghy.reshape(1, c), go.reshape(1, dk), npost.reshape(1, d), w)


@functools.lru_cache(maxsize=None)
def _grid_pos_tables_np(n_tok, dim):
    rows = n_tok // GRID_W
    quarter = dim // 4
    omega = 1.0 / (POS_BASE ** (np.arange(quarter, dtype=np.float64) / quarter))
    ar = np.arange(rows, dtype=np.float64)[:, None] * omega[None]
    ac = np.arange(GRID_W, dtype=np.float64)[:, None] * omega[None]
    row_t = np.concatenate([np.sin(ar), np.cos(ar)], axis=-1).astype(np.float32)
    col_t = np.concatenate([np.sin(ac), np.cos(ac)], axis=-1).astype(np.float32)
    return row_t.reshape(rows, 1, dim // 2), col_t.reshape(GRID_W // ROW_CHUNK, ROW_CHUNK, dim // 2)


def _mixer(proj, proj_ab, s0, *, n_seq, n_tok, hy_conv_w, hy_f, hy_decay, hy_bias, gdn_conv_w, gdn_a_log,
           gdn_dt_bias, c_hy, d_gdn):
    heads = GDN_HEADS
    dk = d_gdn // heads
    o0 = 3 * c_hy

    a_exp = jnp.exp(gdn_a_log.astype(F32))
    zeros = jnp.zeros_like(a_exp)
    av32 = jnp.stack([a_exp, zeros], axis=1).reshape(-1)
    dtv32 = jnp.stack([gdn_dt_bias.astype(F32), zeros], axis=1).reshape(-1)
    pad = LANES - 4 * heads
    av = jnp.pad(av32, (0, pad)).reshape(1, LANES)
    dtv = jnp.pad(dtv32, (0, pad)).reshape(1, LANES)
    avt = av32.reshape(4 * heads, 1)
    dtvt = dtv32.reshape(4 * heads, 1)
    abt = jnp.transpose(proj_ab[:, :4 * heads])

    r = min(512, n_tok)
    tables = _dft_tables(n_tok, r)
    ha, hb, hc = _spectra_call(n_tok, *hy_f, hy_decay, tables, r=r)
    zhy = _hyena_call(proj, hy_conv_w, hy_bias, tables, ha, hb, hc, n_seq=n_seq, n_tok=n_tok, c=c_hy, r=r,
                      n_sub=max(1, min(n_seq, 2048 // n_tok)))
    qkv = _gdn_prep_call(proj, gdn_conv_w, n_seq=n_seq, n_tok=n_tok, col0=o0, d_gdn=d_gdn, dk=dk)
    o_f, o_b, s_fin = _gdn_call(qkv, proj_ab, abt, av, dtv, avt, dtvt, s0, n_seq=n_seq, n_tok=n_tok, heads=heads,
                                dk=dk, chunk=GDN_CHUNK)
    return zhy, o_f, o_b, s_fin


def kernel(x_prompt, x_sample, state_gdn, c, c_ctx, ada_w, ada_b, norm_pre, norm_post, ffn_wg, ffn_wu, ffn_wd,
           w_in, w_out, hy_conv_w, hy_f_w1, hy_f_b1, hy_f_w2, hy_f_b2, hy_f_w3, hy_decay, hy_bias, hy_out_norm,
           gdn_conv_w, gdn_a_log, gdn_dt_bias, gdn_o_norm):
    nb, seq, d = x_prompt.shape
    db, dseq, _ = x_sample.shape
    depth = ada_w.shape[0]
    c_hy = hy_decay.shape[-1]
    d_gdn = gdn_conv_w.shape[-1] // 3
    heads = GDN_HEADS
    dk = d_gdn // heads
    n_ctx_rows = nb * seq
    n_lat_rows = db * dseq
    pos = tuple(jnp.asarray(t).astype(x_sample.dtype) for t in _grid_pos_tables_np(dseq, d))
    n_mod_rows = -(-(1 + db) // SUBLANES) * SUBLANES
    cvec = jnp.concatenate([c_ctx[None], c, jnp.zeros((n_mod_rows - 1 - db, d), F32)], axis=0)
    wg, wu, wd = ffn_wg.astype(BF16), ffn_wu.astype(BF16), ffn_wd.astype(BF16)
    n_main_cols = 3 * c_hy + 4 * d_gdn
    n_tail = w_in.shape[-1] - n_main_cols
    w_in_t = jnp.swapaxes(w_in, 1, 2).astype(BF16)

    xs = [x_prompt.reshape(n_ctx_rows, d), x_sample.reshape(n_lat_rows, d)]
    groups = [dict(g0=0, rows_per_req=n_ctx_rows), dict(g0=1, rows_per_req=dseq)]
    shapes = [(nb, seq), (db, dseq)]
    ctx_states = []
    for l in range(depth):
        mod = _mod_call(cvec, ada_w[l], ada_b[l]).reshape(n_mod_rows, N_MOD, d)
        w_tail_t = jnp.pad(w_in_t[l, n_main_cols:], ((0, LANES - n_tail), (0, 0)))
        w_out_l = w_out[l].astype(BF16)
        hy_f = (hy_f_w1[l], hy_f_b1[l], hy_f_w2[l], hy_f_b2[l], hy_f_w3[l])
        states = [None, state_gdn[:, l].astype(F32)]
        for gi, (grp, (n_seq, n_tok)) in enumerate(zip(groups, shapes)):
            x = _ffn_call(xs[gi], mod, norm_pre[l, 0], norm_post[l, 0], wg, wu, wd, (l, 0), mi=0,
                          pos=pos if (gi == 1 and l == 0) else None, **grp)
            proj, proj_ab = _inproj_call(x, mod, norm_pre[l, 1], w_in_t, l, w_tail_t, mi=3,
                                         n_main_cols=n_main_cols, **grp)
            zhy, o_f, o_b, s_fin = _mixer(
                proj, proj_ab, states[gi], n_seq=n_seq, n_tok=n_tok, hy_conv_w=hy_conv_w[l], hy_f=hy_f,
                hy_decay=hy_decay[l], hy_bias=hy_bias[l], gdn_conv_w=gdn_conv_w[l], gdn_a_log=gdn_a_log[l],
                gdn_dt_bias=gdn_dt_bias[l], c_hy=c_hy, d_gdn=d_gdn)
            if gi == 0:
                ctx_states.append(s_fin)
            x = _outproj_call(x, mod, zhy, o_f, o_b, proj, hy_out_norm[l], gdn_o_norm[l], norm_post[l, 1], w_out_l,
                              mi=5, zg_col_blk=(3 * c_hy + 3 * d_gdn) // d_gdn, heads=heads, dk=dk, **grp)
            xs[gi] = _ffn_call(x, mod, norm_pre[l, 2], norm_post[l, 2], wg, wu, wd, (l, 1), mi=6, **grp)

    new_state = ctx_states[0][:, None] if depth == 1 else jnp.stack(ctx_states, axis=1)
    return xs[0].reshape(nb, seq, d), xs[1].reshape(db, dseq, d), new_state
```
